```python
import math
import jax, jax.numpy as jnp
from jax import lax
import numpy as np

D_MODEL = 2048
BATCH = 8
SEQ = 2048
DEPTH = 1

N_DIFF_HEADS = 8
DIFF_HEAD_DIM = 64
DIFF_V_DIM = 2 * DIFF_HEAD_DIM
DIFF_WIDTH = N_DIFF_HEADS * DIFF_V_DIM
N_RET_HEADS = 8
RET_QK_DIM = 64
RET_V_DIM = 128
RET_WIDTH = N_RET_HEADS * RET_V_DIM
MIX_WIDTH = DIFF_WIDTH + RET_WIDTH
COL_SPLITS = (
    N_DIFF_HEADS * 2 * DIFF_HEAD_DIM,
    N_DIFF_HEADS * 2 * DIFF_HEAD_DIM,
    DIFF_WIDTH,
    N_RET_HEADS * RET_QK_DIM,
    N_RET_HEADS * RET_QK_DIM,
    RET_WIDTH,
    RET_WIDTH,
)
IN_COLS = sum(COL_SPLITS)
Q_BLOCK = 128
RET_CHUNK = 128
ROPE_BASE = 10000.0
N_BUCKETS = 32
MAX_DISTANCE = 128
N_GROUPS = 4
EXPERTS_PER_GROUP = 8
N_EXPERTS = N_GROUPS * EXPERTS_PER_GROUP
TOP_K_INNER = 2
D_FF_EXPERT = 1024
EPS = 1e-6

kernel_name = "hymba_diffattn_retnet_hmoe_block"


def rms_norm(x, g):
    xf = x.astype(jnp.float32)
    y = xf * lax.rsqrt(jnp.mean(xf * xf, axis=-1, keepdims=True) + EPS)
    return (y * g.astype(jnp.float32)).astype(x.dtype)


def t5_bucket(dist):
    max_exact = N_BUCKETS // 2
    d = jnp.maximum(dist, 0)
    log_ratio = jnp.log(jnp.maximum(d, 1).astype(jnp.float32) / max_exact) / math.log(MAX_DISTANCE / max_exact)
    large = max_exact + (log_ratio * (N_BUCKETS - max_exact)).astype(jnp.int32)
    large = jnp.minimum(large, N_BUCKETS - 1)
    return jnp.where(d < max_exact, d, large)


def rotary(x, pos):
    d = x.shape[-1]
    inv_freq = ROPE_BASE ** (-jnp.arange(0, d, 2, dtype=jnp.float32) / d)
    ang = pos.astype(jnp.float32)[:, None] * inv_freq[None, :]
    cos = jnp.cos(ang)[None, :, None, :].astype(x.dtype)
    sin = jnp.sin(ang)[None, :, None, :].astype(x.dtype)
    x1, x2 = x[..., : d // 2], x[..., d // 2:]
    return jnp.concatenate([x1 * cos - x2 * sin, x1 * sin + x2 * cos], axis=-1)


def diff_attention(q, k, v, rel_bias, lam):
    S = q.shape[1]
    scale = DIFF_HEAD_DIM ** -0.5
    outs = []
    for blk in range(S // Q_BLOCK):
        s0, s1 = blk * Q_BLOCK, (blk + 1) * Q_BLOCK
        qb = q[:, s0:s1]
        kb, vb = k[:, :s1], v[:, :s1]
        logits = jnp.einsum('bqhcd,bkhcd->bhcqk', qb, kb).astype(jnp.float32) * scale
        dist = jnp.arange(s0, s1)[:, None] - jnp.arange(s1)[None, :]
        bias = jnp.transpose(rel_bias[t5_bucket(dist)], (2, 0, 1)).astype(jnp.float32)
        logits = logits + bias[None, :, None]
        logits = jnp.where((dist >= 0)[None, None, None], logits, -jnp.inf)
        p = jax.nn.softmax(logits, axis=-1)
        attn = (p[:, :, 0] - lam * p[:, :, 1]).astype(v.dtype)
        outs.append(jnp.einsum('bhqk,bkhd->bqhd', attn, vb))
    return jnp.concatenate(outs, axis=1)


def retention(q, k, v):
    B, S, H, dk = q.shape
    dv = v.shape[-1]
    C = RET_CHUNK
    NC = S // C
    log_g = jnp.log(1.0 - jnp.exp2(-5.0 - jnp.arange(H, dtype=jnp.float32)))
    n = jnp.arange(C, dtype=jnp.float32)
    diff = n[:, None] - n[None, :]
    inner_decay = jnp.where(diff[None] >= 0, jnp.exp(jnp.maximum(diff, 0.0)[None] * log_g[:, None, None]), 0.0).astype(q.dtype)
    cross_decay = jnp.exp((n[None] + 1.0) * log_g[:, None]).astype(q.dtype)
    key_decay = jnp.exp((C - 1.0 - n[None]) * log_g[:, None]).astype(q.dtype)
    chunk_decay = jnp.exp(C * log_g).astype(q.dtype)

    def to_chunks(t):
        return jnp.transpose(t.reshape(B, NC, C, H, t.shape[-1]), (1, 0, 3, 2, 4))

    def step(state, inp):
        qc, kc, vc = inp
        scores = jnp.einsum('bhnd,bhmd->bhnm', qc, kc) * inner_decay[None]
        inner = jnp.einsum('bhnm,bhme->bhne', scores, vc)
        cross = jnp.einsum('bhnd,bhde->bhne', qc, state) * cross_decay[None, :, :, None]
        new_state = state * chunk_decay[None, :, None, None] + jnp.einsum('bhmd,bhme->bhde', kc * key_decay[None, :, :, None], vc)
        return new_state, inner + cross

    state0 = jnp.zeros((B, H, dk, dv), q.dtype)
    _, ys = lax.scan(step, state0, (to_chunks(q), to_chunks(k), to_chunks(v)))
    return jnp.transpose(ys, (1, 0, 3, 2, 4)).reshape(B, S, H, dv)


def hierarchical_moe(h, w_group_router, b_group, w_inner_router, b_inner, w_gate, w_up, w_down):
    N = h.shape[0]
    group_logits = (h @ w_group_router).astype(jnp.float32) + b_group.astype(jnp.float32)
    group_probs = jax.nn.softmax(group_logits, axis=-1)
    g_idx = jnp.argmax(group_logits, axis=-1)
    p_group = jnp.take_along_axis(group_probs, g_idx[:, None], axis=1)
    inner_all = jnp.einsum('nd,gde->nge', h, w_inner_router).astype(jnp.float32)
    inner = jnp.take_along_axis(inner_all, g_idx[:, None, None], axis=1)[:, 0] + b_inner[g_idx].astype(jnp.float32)
    top_vals, top_idx = lax.top_k(inner, TOP_K_INNER)
    gates = p_group * jax.nn.softmax(top_vals, axis=-1)
    expert_id = (g_idx[:, None] * EXPERTS_PER_GROUP + top_idx).reshape(-1)
    order = jnp.argsort(expert_id)
    token = order // TOP_K_INNER
    xs = h[token]
    sizes = jnp.bincount(expert_id, length=N_EXPERTS).astype(jnp.int32)
    a = lax.ragged_dot(xs, w_gate, sizes)
    b = lax.ragged_dot(xs, w_up, sizes)
    ys = lax.ragged_dot(jax.nn.silu(a) * b, w_down, sizes)
    ys = ys * gates.reshape(-1)[order][:, None].astype(ys.dtype)
    return jnp.zeros((N, h.shape[1]), ys.dtype).at[token].add(ys)


def setup_inputs(seed: int = 0) -> dict:
    key = jax.random.key(seed)
    ks = jax.random.split(key, 20)
    f32 = jnp.float32
    nrm = lambda k, shape, s: jax.random.normal(k, shape, f32) * s
    return {
        "x": nrm(ks[0], (BATCH, SEQ, D_MODEL), 1.0),
        "rel_bias": nrm(ks[1], (N_BUCKETS, N_DIFF_HEADS), 0.5),
        "attn_norm_g": 1.0 + nrm(ks[2], (DEPTH, D_MODEL), 0.02),
        "w_in": nrm(ks[3], (DEPTH, D_MODEL, IN_COLS), D_MODEL ** -0.5),
        "lambda_q1": nrm(ks[4], (DEPTH, DIFF_HEAD_DIM), 0.1),
        "lambda_k1": nrm(ks[5], (DEPTH, DIFF_HEAD_DIM), 0.1),
        "lambda_q2": nrm(ks[6], (DEPTH, DIFF_HEAD_DIM), 0.1),
        "lambda_k2": nrm(ks[7], (DEPTH, DIFF_HEAD_DIM), 0.1),
        "diff_subln_g": 1.0 + nrm(ks[8], (DEPTH, DIFF_V_DIM), 0.02),
        "ret_gn_g": 1.0 + nrm(ks[9], (DEPTH, RET_WIDTH), 0.02),
        "w_out": nrm(ks[10], (DEPTH, MIX_WIDTH, D_MODEL), MIX_WIDTH ** -0.5),
        "ffn_norm_g": 1.0 + nrm(ks[11], (DEPTH, D_MODEL), 0.02),
        "w_group_router": nrm(ks[12], (DEPTH, D_MODEL, N_GROUPS), D_MODEL ** -0.5),
        "b_group": nrm(ks[13], (DEPTH, N_GROUPS), 0.01),
        "w_inner_router": nrm(ks[14], (DEPTH, N_GROUPS, D_MODEL, EXPERTS_PER_GROUP), D_MODEL ** -0.5),
        "b_inner": nrm(ks[15], (DEPTH, N_GROUPS, EXPERTS_PER_GROUP), 0.01),
        "w_gate_exp": nrm(ks[16], (DEPTH, N_EXPERTS, D_MODEL, D_FF_EXPERT), D_MODEL ** -0.5),
        "w_up_exp": nrm(ks[17], (DEPTH, N_EXPERTS, D_MODEL, D_FF_EXPERT), D_MODEL ** -0.5),
        "w_down_exp": nrm(ks[18], (DEPTH, N_EXPERTS, D_FF_EXPERT, D_MODEL), D_FF_EXPERT ** -0.5),
        "final_g": 1.0 + nrm(ks[19], (D_MODEL,), 0.02),
    }


def reference(x, rel_bias, attn_norm_g, w_in, lambda_q1, lambda_k1, lambda_q2, lambda_k2,
              diff_subln_g, ret_gn_g, w_out, ffn_norm_g, w_group_router, b_group,
              w_inner_router, b_inner, w_gate_exp, w_up_exp, w_down_exp, final_g):
    B, S, D = x.shape
    pos = jnp.arange(S)
    for l in range(DEPTH):
        h = rms_norm(x, attn_norm_g[l])
        proj = h @ w_in[l]
        cuts = list(np.cumsum(COL_SPLITS)[:-1])
        dq, dk_, dv_, rq, rk, rv, rg = jnp.split(proj, cuts, axis=-1)

        lam_init = 0.8 - 0.6 * math.exp(-0.3 * l)
        lam = (jnp.exp(jnp.sum(lambda_q1[l] * lambda_k1[l]).astype(jnp.float32))
               - jnp.exp(jnp.sum(lambda_q2[l] * lambda_k2[l]).astype(jnp.float32)) + lam_init)
        dq = dq.reshape(B, S, N_DIFF_HEADS, 2, DIFF_HEAD_DIM)
        dk_ = dk_.reshape(B, S, N_DIFF_HEADS, 2, DIFF_HEAD_DIM)
        dv_ = dv_.reshape(B, S, N_DIFF_HEADS, DIFF_V_DIM)
        a_out = diff_attention(dq, dk_, dv_, rel_bias, lam)
        a_out = rms_norm(a_out, diff_subln_g[l]) * (1.0 - lam_init)
        a_out = a_out.reshape(B, S, DIFF_WIDTH)

        rq = rotary(rq.reshape(B, S, N_RET_HEADS, RET_QK_DIM), pos)
        rk = rotary(rk.reshape(B, S, N_RET_HEADS, RET_QK_DIM), pos) * (RET_QK_DIM ** -0.5)
        rv = rv.reshape(B, S, N_RET_HEADS, RET_V_DIM)
        r = retention(rq, rk, rv).astype(jnp.float32)
        mu = jnp.mean(r, axis=-1, keepdims=True)
        var = jnp.mean(jnp.square(r - mu), axis=-1, keepdims=True)
        r = ((r - mu) * lax.rsqrt(var + EPS)).reshape(B, S, RET_WIDTH) * ret_gn_g[l].astype(jnp.float32)
        r_out = (jax.nn.silu(rg) * r.astype(x.dtype))

        mixed = jnp.concatenate([a_out.astype(x.dtype), r_out], axis=-1)
        x = x + mixed @ w_out[l]

        h2 = rms_norm(x, ffn_norm_g[l]).reshape(B * S, D)
        y = hierarchical_moe(h2, w_group_router[l], b_group[l], w_inner_router[l], b_inner[l],
                             w_gate_exp[l], w_up_exp[l], w_down_exp[l])
        x = x + y.reshape(B, S, D).astype(x.dtype)
    return rms_norm(x, final_g)
```

```python
import functools
import math

import numpy as np
import jax
import jax.numpy as jnp
from jax import lax
from jax.experimental import pallas as pl
from jax.experimental.pallas import tpu as pltpu

F32 = jnp.float32
BF16 = jnp.bfloat16

EPS = 1e-6
LANES = 128
N_DIFF_HEADS = 8
DIFF_HALF = 64
N_RET_HEADS = 8
RET_QK = 64
RET_CHUNK = 128
ROPE_BASE = 10000.0
N_BUCKETS = 32
MAX_DISTANCE = 128
N_GROUPS = 4
EXPERTS_PER_GROUP = 8
N_EXPERTS = N_GROUPS * EXPERTS_PER_GROUP
MASK_VALUE = -1e30
VMEM_LIMIT = 56 * 1024 * 1024

COL_DQ, COL_DK, COL_DV = 0, 8, 16
COL_RQ, COL_RK, COL_RV, COL_RG = 24, 28, 32, 40
IN_COLS = 48 * LANES


def _params(sem, vmem=VMEM_LIMIT):
    return pltpu.CompilerParams(dimension_semantics=sem, vmem_limit_bytes=vmem)


def _inproj_body(x_ref, g_ref, w_ref, o_ref, h_ref, *, row_chunk):
    tm = x_ref.shape[0]

    @pl.when(pl.program_id(1) == 0)
    def _():
        g = g_ref[...]

        def chunk(c, carry):
            r = pl.multiple_of(c * row_chunk, row_chunk)
            x = x_ref[pl.ds(r, row_chunk), :]
            ms = jnp.mean(x * x, axis=-1, keepdims=True)
            h_ref[pl.ds(r, row_chunk), :] = ((x * lax.rsqrt(ms + EPS)) * g).astype(BF16)
            return carry

        lax.fori_loop(0, tm // row_chunk, chunk, 0)

    o_ref[...] = jnp.dot(h_ref[...], w_ref[...], preferred_element_type=F32).astype(o_ref.dtype)


def _inproj(x2d, g, w_bf16, *, tm, tn):
    n, d = x2d.shape
    cols = w_bf16.shape[1]
    return pl.pallas_call(
        functools.partial(_inproj_body, row_chunk=128),
        grid=(n // tm, cols // tn),
        in_specs=[
            pl.BlockSpec((tm, d), lambda i, j: (i, 0)),
            pl.BlockSpec((1, d), lambda i, j: (0, 0)),
            pl.BlockSpec((d, tn), lambda i, j: (0, j)),
        ],
        out_specs=pl.BlockSpec((tm, tn), lambda i, j: (i, j)),
        out_shape=jax.ShapeDtypeStruct((n, cols), BF16),
        scratch_shapes=[pltpu.VMEM((tm, d), BF16)],
        compiler_params=_params(("arbitrary", "arbitrary")),
        name="inproj",
    )(x2d, g, w_bf16)


def _t5_bucket_table(n):
    d = np.arange(n)
    max_exact = N_BUCKETS // 2
    log_ratio = np.log(np.maximum(d, 1).astype(np.float64) / max_exact) / math.log(MAX_DISTANCE / max_exact)
    large = np.minimum(max_exact + (log_ratio * (N_BUCKETS - max_exact)).astype(np.int64), N_BUCKETS - 1)
    return np.where(d < max_exact, d, large).astype(np.int32)


def _bucket_tiles(t):
    table = _t5_bucket_table(2 * t)
    k = np.arange(t)[:, None]
    q = np.arange(t)[None, :]
    d0 = q - k
    tile0 = np.where(d0 >= 0, table[np.maximum(d0, 0)], -1)
    tile1 = table[t + q - k]
    return np.stack([tile0, tile1]).astype(np.int32)


def _bias_body(rb_ref, bkt_ref, o_ref):
    h = pl.program_id(0)
    bkt = bkt_ref[...]
    far = rb_ref[N_BUCKETS - 1, h]
    acc = jnp.zeros(bkt.shape, F32)
    for b in range(N_BUCKETS - 1):
        acc = jnp.where(bkt == b, rb_ref[b, h] - far, acc)
    o_ref[0] = jnp.where(bkt < 0, MASK_VALUE, acc)


def _bias_tiles(rel_bias, t):
    bkt = jnp.asarray(_bucket_tiles(t))
    return pl.pallas_call(
        _bias_body,
        grid=(N_DIFF_HEADS,),
        in_specs=[
            pl.BlockSpec(memory_space=pltpu.SMEM),
            pl.BlockSpec((2, t, t), lambda h: (0, 0, 0)),
        ],
        out_specs=pl.BlockSpec((1, 2, t, t), lambda h: (h, 0, 0, 0)),
        out_shape=jax.ShapeDtypeStruct((N_DIFF_HEADS, 2, t, t), F32),
        compiler_params=_params(("arbitrary",)),
        name="bias_tiles",
    )(rel_bias, bkt)


_NT = (((1,), (1,)), ((), ()))
_TN = (((0,), (0,)), ((), ()))


def _diffattn_body(q_ref, k_ref, v_ref, bias_ref, lq1_ref, lk1_ref, lq2_ref, lk2_ref, g_ref, o_ref,
                   qm_ref, m_ref, l_ref, acc_ref, *, t, lam_init):
    qi = pl.program_id(2)

    q = q_ref[0] * jnp.asarray(DIFF_HALF ** -0.5, BF16)
    lane = lax.broadcasted_iota(jnp.int32, q.shape, 1)
    zero = jnp.zeros_like(q)
    qm_ref[0] = jnp.where(lane < DIFF_HALF, q, zero)
    qm_ref[1] = jnp.where(lane >= DIFF_HALF, q, zero)
    m_ref[...] = jnp.full(m_ref.shape, MASK_VALUE, F32)
    l_ref[...] = jnp.zeros(l_ref.shape, F32)
    acc_ref[...] = jnp.zeros(acc_ref.shape, F32)

    def step(j, bias):
        r = pl.multiple_of(j * t, t)
        kb = k_ref[0, pl.ds(r, t), :]
        vb = v_ref[0, pl.ds(r, t), :]
        for c in range(2):
            s = lax.dot_general(kb, qm_ref[c], _NT, preferred_element_type=F32)
            if bias is not None:
                s = s + bias
            m_old = m_ref[c]
            m_new = jnp.maximum(m_old, jnp.max(s, axis=0, keepdims=True))
            alpha = jnp.exp(m_old - m_new)
            p = jnp.exp(s - m_new)
            l_ref[c] = alpha * l_ref[c] + jnp.sum(p, axis=0, keepdims=True)
            pv = lax.dot_general(vb, p.astype(BF16), _TN, preferred_element_type=F32)
            acc_ref[c] = alpha * acc_ref[c] + pv
            m_ref[c] = m_new

    def far_step(j, carry):
        step(j, None)
        return carry

    lax.fori_loop(0, qi - 1, far_step, 0)

    @pl.when(qi >= 1)
    def _():
        step(qi - 1, bias_ref[0, 1])

    step(qi, bias_ref[0, 0])

    lam = (jnp.exp(jnp.sum(lq1_ref[...] * lk1_ref[...], axis=-1, keepdims=True))
           - jnp.exp(jnp.sum(lq2_ref[...] * lk2_ref[...], axis=-1, keepdims=True)) + lam_init)
    o = acc_ref[0] * (1.0 / l_ref[0]) - lam * (acc_ref[1] * (1.0 / l_ref[1]))
    ot = o.T
    ms = jnp.mean(ot * ot, axis=-1, keepdims=True)
    o_ref[0] = (((ot * lax.rsqrt(ms + EPS)) * g_ref[...]) * (1.0 - lam_init)).astype(o_ref.dtype)


def _diff_attention(proj3, bias, lq1, lk1, lq2, lk2, subln_g, *, t, lam_init):
    b, s, _ = proj3.shape
    vec = pl.BlockSpec((1, DIFF_HALF), lambda bi, h, qi: (0, 0))
    return pl.pallas_call(
        functools.partial(_diffattn_body, t=t, lam_init=lam_init),
        grid=(b, N_DIFF_HEADS, s // t),
        in_specs=[
            pl.BlockSpec((1, t, LANES), lambda bi, h, qi: (bi, qi, COL_DQ + h)),
            pl.BlockSpec((1, s, LANES), lambda bi, h, qi: (bi, 0, COL_DK + h)),
            pl.BlockSpec((1, s, LANES), lambda bi, h, qi: (bi, 0, COL_DV + h)),
            pl.BlockSpec((1, 2, t, t), lambda bi, h, qi: (h, 0, 0, 0)),
            vec, vec, vec, vec,
            pl.BlockSpec((1, LANES), lambda bi, h, qi: (0, 0)),
        ],
        out_specs=pl.BlockSpec((1, t, LANES), lambda bi, h, qi: (bi, qi, h)),
        out_shape=jax.ShapeDtypeStruct((b, s, N_DIFF_HEADS * LANES), BF16),
        scratch_shapes=[
            pltpu.VMEM((2, t, LANES), BF16),
            pltpu.VMEM((2, 1, t), F32),
            pltpu.VMEM((2, 1, t), F32),
            pltpu.VMEM((2, LANES, t), F32),
        ],
        compiler_params=_params(("arbitrary", "arbitrary", "arbitrary")),
        name="diff_attention",
    )(proj3, proj3, proj3, bias, lq1, lk1, lq2, lk2, subln_g)


def _retention_tables(s):
    c = RET_CHUNK
    half = RET_QK // 2
    inv_freq = ROPE_BASE ** (-jnp.arange(0, RET_QK, 2, dtype=F32) / RET_QK)
    ang = jnp.arange(s, dtype=F32)[:, None] * inv_freq[None, :]
    cos, sin = jnp.cos(ang), jnp.sin(ang)
    cos_t = jnp.tile(cos, (1, LANES // half))
    sin_t = jnp.tile(jnp.concatenate([-sin, sin], axis=1), (1, LANES // RET_QK))
    log_g = jnp.log(1.0 - jnp.exp2(-5.0 - jnp.arange(N_RET_HEADS, dtype=F32)))
    n = jnp.arange(c, dtype=F32)
    diff = n[:, None] - n[None, :]
    inner = jnp.where(diff[None] >= 0, jnp.exp(jnp.maximum(diff, 0.0)[None] * log_g[:, None, None]), 0.0)
    cross = jnp.exp((n[None] + 1.0) * log_g[:, None])
    key = jnp.exp((c - 1.0 - n[None]) * log_g[:, None])
    chunk = jnp.exp(c * log_g)
    bc = lambda a: jnp.broadcast_to(a[..., None], a.shape + (LANES,)).astype(F32)
    return cos_t, sin_t, inner.astype(F32), bc(cross), bc(key), bc(chunk[:, None])


def _retention_body(rq_ref, rk_ref, rv_ref, rg_ref, cos_ref, sin_ref, inner_ref, cross_ref, key_ref,
                    chunk_ref, gn_ref, o_ref, state_ref):
    c = RET_CHUNK
    nc = rq_ref.shape[1] // c
    state_ref[...] = jnp.zeros(state_ref.shape, F32)
    lane = lax.broadcasted_iota(jnp.int32, (c, LANES), 1)
    first_half = (lane % RET_QK) < (RET_QK // 2)
    head_mask = [lane < RET_QK, lane >= RET_QK]

    def rotary(x, cos, sin):
        rot = jnp.where(first_half, pltpu.roll(x, LANES - RET_QK // 2, 1), pltpu.roll(x, RET_QK // 2, 1))
        return x * cos + rot * sin

    def chunk_step(ci, carry):
        r = pl.multiple_of(ci * c, c)
        cos = cos_ref[pl.ds(r, c), :]
        sin = sin_ref[pl.ds(r, c), :]
        q = rotary(rq_ref[0, pl.ds(r, c), :].astype(F32), cos, sin)
        k = rotary(rk_ref[0, pl.ds(r, c), :].astype(F32), cos, sin) * (RET_QK ** -0.5)
        kb = k.astype(BF16)
        for hh in range(2):
            qh = jnp.where(head_mask[hh], q, 0.0).astype(BF16)
            vh = rv_ref[0, pl.ds(r, c), hh * LANES:(hh + 1) * LANES]
            scores = lax.dot_general(qh, kb, _NT, preferred_element_type=F32) * inner_ref[hh]
            inner = jnp.dot(scores.astype(BF16), vh, preferred_element_type=F32)
            state = state_ref[hh]
            cross = jnp.dot(qh, state.astype(BF16), preferred_element_type=F32) * cross_ref[hh]
            kd = (k * key_ref[hh]).astype(BF16)
            state_ref[hh] = state * chunk_ref[hh] + lax.dot_general(kd, vh, _TN, preferred_element_type=F32)
            y = inner + cross
            mu = jnp.mean(y, axis=-1, keepdims=True)
            yc = y - mu
            var = jnp.mean(yc * yc, axis=-1, keepdims=True)
            yn = (yc * lax.rsqrt(var + EPS)) * gn_ref[:, hh * LANES:(hh + 1) * LANES]
            gate = rg_ref[0, pl.ds(r, c), hh * LANES:(hh + 1) * LANES].astype(F32)
            silu = gate * (1.0 / (1.0 + jnp.exp(-gate)))
            o_ref[0, pl.ds(r, c), hh * LANES:(hh + 1) * LANES] = (silu * yn).astype(o_ref.dtype)
        return carry

    lax.fori_loop(0, nc, chunk_step, 0)


def _retention(proj3, tables, gn_g):
    b, s, _ = proj3.shape
    cos_t, sin_t, inner, cross, key, chunk = tables
    c = RET_CHUNK
    pairs = N_RET_HEADS // 2
    pos = pl.BlockSpec((s, LANES), lambda bi, hp: (0, 0))
    return pl.pallas_call(
        _retention_body,
        grid=(b, pairs),
        in_specs=[
            pl.BlockSpec((1, s, LANES), lambda bi, hp: (bi, 0, COL_RQ + hp)),
            pl.BlockSpec((1, s, LANES), lambda bi, hp: (bi, 0, COL_RK + hp)),
            pl.BlockSpec((1, s, 2 * LANES), lambda bi, hp: (bi, 0, COL_RV // 2 + hp)),
            pl.BlockSpec((1, s, 2 * LANES), lambda bi, hp: (bi, 0, COL_RG // 2 + hp)),
            pos, pos,
            pl.BlockSpec((2, c, c), lambda bi, hp: (hp, 0, 0)),
            pl.BlockSpec((2, c, LANES), lambda bi, hp: (hp, 0, 0)),
            pl.BlockSpec((2, c, LANES), lambda bi, hp: (hp, 0, 0)),
            pl.BlockSpec((2, 1, LANES), lambda bi, hp: (hp, 0, 0)),
            pl.BlockSpec((1, 2 * LANES), lambda bi, hp: (0, hp)),
        ],
        out_specs=pl.BlockSpec((1, s, 2 * LANES), lambda bi, hp: (bi, 0, hp)),
        out_shape=jax.ShapeDtypeStruct((b, s, N_RET_HEADS * LANES), BF16),
        scratch_shapes=[pltpu.VMEM((2, LANES, LANES), F32)],
        compiler_params=_params(("arbitrary", "arbitrary")),
        name="retention",
    )(proj3, proj3, proj3, proj3, cos_t, sin_t, inner, cross, key, chunk, gn_g)


def _route(logits):
    lane = lax.broadcasted_iota(jnp.int32, logits.shape, 1).astype(F32)
    big = float(LANES)
    is_group = lane < N_GROUPS
    gl = jnp.where(is_group, logits, -jnp.inf)
    gmax = jnp.max(gl, axis=-1, keepdims=True)
    gidx = jnp.min(jnp.where(gl == gmax, lane, big), axis=-1, keepdims=True)
    p_group = 1.0 / jnp.sum(jnp.where(is_group, jnp.exp(gl - gmax), 0.0), axis=-1, keepdims=True)
    lo = N_GROUPS + EXPERTS_PER_GROUP * gidx
    il = jnp.where(lane >= lo, jnp.where(lane < lo + EXPERTS_PER_GROUP, logits, -jnp.inf), -jnp.inf)
    v1 = jnp.max(il, axis=-1, keepdims=True)
    i1 = jnp.min(jnp.where(il == v1, lane, big), axis=-1, keepdims=True)
    il2 = jnp.where(lane == i1, -jnp.inf, il)
    v2 = jnp.max(il2, axis=-1, keepdims=True)
    i2 = jnp.min(jnp.where(il2 == v2, lane, big), axis=-1, keepdims=True)
    e2 = jnp.exp(v2 - v1)
    inv = 1.0 / (1.0 + e2)
    g1 = p_group * inv
    g2 = p_group * (e2 * inv)
    out = jnp.where(lane == 0.0, i1 - N_GROUPS,
                    jnp.where(lane == 1.0, i2 - N_GROUPS,
                              jnp.where(lane == 2.0, g1, jnp.where(lane == 3.0, g2, 0.0))))
    return out


def _outproj_body(a_ref, r_ref, x_ref, w_ref, g_ref, wr_ref, br_ref, x1_ref, h2_ref, route_ref):
    half = a_ref.shape[1]
    acc = jnp.dot(a_ref[...], w_ref[:half, :], preferred_element_type=F32)
    acc = acc + jnp.dot(r_ref[...], w_ref[half:, :], preferred_element_type=F32)
    x1 = x_ref[...] + acc
    x1_ref[...] = x1
    ms = jnp.mean(x1 * x1, axis=-1, keepdims=True)
    h2 = (x1 * lax.rsqrt(ms + EPS)) * g_ref[...]
    h2_ref[...] = h2
    logits = jnp.dot(h2.astype(BF16), wr_ref[...], preferred_element_type=F32) + br_ref[...]
    route_ref[...] = _route(logits)


def _outproj_router(a2d, r2d, x2d, w_out_bf16, g, wr_bf16, br, *, tm):
    n, d = x2d.shape
    half = a2d.shape[1]
    row = lambda width: pl.BlockSpec((tm, width), lambda i: (i, 0))
    full = lambda shape: pl.BlockSpec(shape, lambda i: (0, 0))
    return pl.pallas_call(
        _outproj_body,
        grid=(n // tm,),
        in_specs=[row(half), row(half), row(d), full((2 * half, d)), full((1, d)), full((d, LANES)),
                  full((1, LANES))],
        out_specs=[row(d), row(d), row(LANES)],
        out_shape=[jax.ShapeDtypeStruct((n, d), F32), jax.ShapeDtypeStruct((n, d), F32),
                   jax.ShapeDtypeStruct((n, LANES), F32)],
        compiler_params=_params(("arbitrary",)),
        name="outproj_router",
    )(a2d, r2d, x2d, w_out_bf16, g, wr_bf16, br)


def _moe_plan(expert_id, tm):
    n_slots = expert_id.shape[0]
    n_tiles = n_slots // tm
    max_visits = n_tiles + N_EXPERTS - 1
    order = jnp.argsort(expert_id).astype(jnp.int32)
    sizes = jnp.zeros((N_EXPERTS,), jnp.int32).at[expert_id].add(1)
    ends = jnp.cumsum(sizes)
    starts = ends - sizes
    first_tile = starts // tm
    last_tile = jnp.maximum(ends - 1, 0) // tm
    visits = jnp.where(sizes > 0, last_tile - first_tile + 1, 0)
    v_end = jnp.cumsum(visits)
    v_start = v_end - visits
    n_visits = v_end[-1]
    v = jnp.arange(max_visits, dtype=jnp.int32)
    vc = jnp.minimum(v, n_visits - 1)
    v_expert = jnp.searchsorted(v_end, vc, side="right").astype(jnp.int32)
    v_tile = (first_tile[v_expert] + (vc - v_start[v_expert])).astype(jnp.int32)
    step = jnp.arange(2 * max_visits, dtype=jnp.int32)
    sv = jnp.minimum(step // 2, n_visits - 1)
    fs = jnp.where(step // 2 < n_visits, step % 2, 1)
    s_f = jnp.where(sv % 2 == 0, fs, 1 - fs).astype(jnp.int32)
    s_e = v_expert[sv]
    idx = jnp.stack([(order // 2).reshape(n_tiles, tm), order.reshape(n_tiles, tm)], axis=1)
    return dict(v_tile=v_tile, v_expert=v_expert, n_visits=n_visits.reshape(1).astype(jnp.int32),
                starts=starts.astype(jnp.int32), ends=ends.astype(jnp.int32), s_e=s_e, s_f=s_f, idx=idx,
                max_visits=max_visits)


def _moe_body(vt_ref, ve_ref, nv_ref, st_ref, en_ref, se_ref, sf_ref,
              idx_hbm, h2_hbm, wg_ref, wu_ref, wd_ref, ys_hbm,
              idx_smem, xbuf, obuf, sem_idx, sem_rows):
    del se_ref, sf_ref
    v = pl.program_id(0)
    fs = pl.program_id(1)
    tm = xbuf.shape[0]
    nv = nv_ref[0]
    t = vt_ref[v]
    e = ve_ref[v]
    valid = v < nv
    prev_t = vt_ref[jnp.maximum(v - 1, 0)]
    next_t = vt_ref[jnp.minimum(v + 1, vt_ref.shape[0] - 1)]
    first_of_tile = jnp.logical_or(v == 0, prev_t != t)
    last_of_tile = jnp.logical_or(v == nv - 1, next_t != t)
    opens = jnp.logical_and(valid, jnp.logical_and(first_of_tile, fs == 0))
    closes = jnp.logical_and(valid, jnp.logical_and(last_of_tile, fs == 1))

    def row_copy_in(r):
        tok = idx_smem[0, r]
        return pltpu.make_async_copy(h2_hbm.at[pl.ds(tok, 1)], xbuf.at[pl.ds(r, 1)], sem_rows)

    def row_copy_out(r):
        slot = idx_smem[1, r]
        return pltpu.make_async_copy(obuf.at[pl.ds(r, 1)], ys_hbm.at[pl.ds(slot, 1)], sem_rows)

    def for_rows(fn):
        def body(r, carry):
            fn(r)
            return carry
        lax.fori_loop(0, tm, body, 0, unroll=8)

    @pl.when(opens)
    def _():
        cp = pltpu.make_async_copy(idx_hbm.at[t], idx_smem, sem_idx)
        cp.start()
        cp.wait()
        for_rows(lambda r: row_copy_in(r).start())
        for_rows(lambda r: row_copy_in(r).wait())

    @pl.when(valid)
    def _():
        x = xbuf[...].astype(BF16)
        a = jnp.dot(x, wg_ref[0].astype(BF16), preferred_element_type=F32)
        b = jnp.dot(x, wu_ref[0].astype(BF16), preferred_element_type=F32)
        hmid = ((a * (1.0 / (1.0 + jnp.exp(-a)))) * b).astype(BF16)
        contrib = jnp.dot(hmid, wd_ref[0].astype(BF16), preferred_element_type=F32)
        row = t * tm + lax.broadcasted_iota(jnp.int32, (tm, 1), 0)
        mine = jnp.logical_and(row >= st_ref[e], row < en_ref[e])
        contrib = jnp.where(mine, contrib, 0.0)

        @pl.when(opens)
        def _():
            obuf[...] = contrib

        @pl.when(jnp.logical_not(opens))
        def _():
            obuf[...] += contrib

    @pl.when(closes)
    def _():
        for_rows(lambda r: row_copy_out(r).start())
        for_rows(lambda r: row_copy_out(r).wait())


def _moe_ffn(h2, plan, w_gate, w_up, w_down, *, tm, n_f):
    n, d = h2.shape
    n_slots = 2 * n
    d_ff = w_gate.shape[2]
    fh = d_ff // n_f
    max_visits = plan["max_visits"]
    grid_spec = pltpu.PrefetchScalarGridSpec(
        num_scalar_prefetch=7,
        grid=(max_visits, n_f),
        in_specs=[
            pl.BlockSpec(memory_space=pl.ANY),
            pl.BlockSpec(memory_space=pl.ANY),
            pl.BlockSpec((1, d, fh), lambda v, f, vt, ve, nv, st, en, se, sf: (se[2 * v + f], 0, sf[2 * v + f])),
            pl.BlockSpec((1, d, fh), lambda v, f, vt, ve, nv, st, en, se, sf: (se[2 * v + f], 0, sf[2 * v + f])),
            pl.BlockSpec((1, fh, d), lambda v, f, vt, ve, nv, st, en, se, sf: (se[2 * v + f], sf[2 * v + f], 0)),
        ],
        out_specs=pl.BlockSpec(memory_space=pl.ANY),
        scratch_shapes=[
            pltpu.SMEM((2, tm), jnp.int32),
            pltpu.VMEM((tm, d), F32),
            pltpu.VMEM((tm, d), F32),
            pltpu.SemaphoreType.DMA(()),
            pltpu.SemaphoreType.DMA(()),
        ],
    )
    return pl.pallas_call(
        _moe_body,
        grid_spec=grid_spec,
        out_shape=jax.ShapeDtypeStruct((n_slots, d), F32),
        compiler_params=_params(("arbitrary", "arbitrary")),
        name="moe_ffn",
    )(plan["v_tile"], plan["v_expert"], plan["n_visits"], plan["starts"], plan["ends"], plan["s_e"], plan["s_f"],
      plan["idx"], h2, w_gate, w_up, w_down)


def _final_body(x1_ref, ys_ref, route_ref, g_ref, o_ref):
    d = x1_ref.shape[1]
    route = route_ref[...]
    y = ys_ref[:, :d] * route[:, 2:3] + ys_ref[:, d:] * route[:, 3:4]
    x2 = x1_ref[...] + y
    ms = jnp.mean(x2 * x2, axis=-1, keepdims=True)
    o_ref[...] = ((x2 * lax.rsqrt(ms + EPS)) * g_ref[...]).astype(o_ref.dtype)


def _final(x1, ys2, route, g, *, tm):
    n, d = x1.shape
    return pl.pallas_call(
        _final_body,
        grid=(n // tm,),
        in_specs=[
            pl.BlockSpec((tm, d), lambda i: (i, 0)),
            pl.BlockSpec((tm, 2 * d), lambda i: (i, 0)),
            pl.BlockSpec((tm, LANES), lambda i: (i, 0)),
            pl.BlockSpec((1, d), lambda i: (0, 0)),
        ],
        out_specs=pl.BlockSpec((tm, d), lambda i: (i, 0)),
        out_shape=jax.ShapeDtypeStruct((n, d), F32),
        compiler_params=_params(("arbitrary",)),
        name="final_norm",
    )(x1, ys2, route, g)


def _layer(x, rel_bias, attn_norm_g, w_in, lq1, lk1, lq2, lk2, diff_subln_g, ret_gn_g, w_out, ffn_norm_g,
           w_group_router, b_group, w_inner_router, b_inner, w_gate_exp, w_up_exp, w_down_exp, *, layer,
           attn_tile, row_tile, moe_tile):
    b, s, d = x.shape
    n = b * s
    x2d = x.reshape(n, d)
    lam_init = 0.8 - 0.6 * math.exp(-0.3 * layer)

    proj = _inproj(x2d, attn_norm_g.reshape(1, d), w_in.astype(BF16), tm=min(2 * row_tile, n), tn=8 * LANES)
    proj3 = proj.reshape(b, s, IN_COLS)

    bias = _bias_tiles(rel_bias, attn_tile)
    vec = lambda a: a.reshape(1, -1)
    a_out = _diff_attention(proj3, bias, vec(lq1), vec(lk1), vec(lq2), vec(lk2), vec(diff_subln_g),
                            t=attn_tile, lam_init=lam_init)
    r_out = _retention(proj3, _retention_tables(s), vec(ret_gn_g))

    pad = LANES - N_GROUPS - N_EXPERTS
    wr = jnp.concatenate([w_group_router, jnp.transpose(w_inner_router, (1, 0, 2)).reshape(d, N_EXPERTS),
                          jnp.zeros((d, pad), F32)], axis=1).astype(BF16)
    br = jnp.concatenate([b_group, b_inner.reshape(-1), jnp.zeros((pad,), F32)]).reshape(1, LANES)
    x1, h2, route = _outproj_router(a_out.reshape(n, -1), r_out.reshape(n, -1), x2d, w_out.astype(BF16),
                                    vec(ffn_norm_g), wr, br, tm=row_tile)

    expert_id = route[:, :2].astype(jnp.int32).reshape(-1)
    plan = _moe_plan(expert_id, moe_tile)
    ys = _moe_ffn(h2, plan, w_gate_exp, w_up_exp, w_down_exp, tm=moe_tile, n_f=2)
    return x1, ys.reshape(n, 2 * d), route


def kernel(x, rel_bias, attn_norm_g, w_in, lambda_q1, lambda_k1, lambda_q2, lambda_k2, diff_subln_g, ret_gn_g,
           w_out, ffn_norm_g, w_group_router, b_group, w_inner_router, b_inner, w_gate_exp, w_up_exp, w_down_exp,
           final_g):
    b, s, d = x.shape
    depth = w_in.shape[0]
    assert depth == 1, "the final norm is fused into the single layer's combine step"
    x1, ys2, route = _layer(x, rel_bias, attn_norm_g[0], w_in[0], lambda_q1[0], lambda_k1[0], lambda_q2[0],
                            lambda_k2[0], diff_subln_g[0], ret_gn_g[0], w_out[0], ffn_norm_g[0],
                            w_group_router[0], b_group[0], w_inner_router[0], b_inner[0], w_gate_exp[0],
                            w_up_exp[0], w_down_exp[0], layer=0, attn_tile=min(512, s), row_tile=512,
                            moe_tile=512)
    out = _final(x1, ys2, route, final_g.reshape(1, d), tm=256)
    return out.reshape(b, s, d)
```

```python
import functools
import math

import numpy as np
import jax
import jax.numpy as jnp
from jax import lax
from jax.experimental import pallas as pl
from jax.experimental.pallas import tpu as pltpu

F32 = jnp.float32
BF16 = jnp.bfloat16

EPS = 1e-6
LANES = 128
N_DIFF_HEADS = 8
DIFF_HALF = 64
N_RET_HEADS = 8
RET_QK = 64
RET_CHUNK = 128
ROPE_BASE = 10000.0
N_BUCKETS = 32
MAX_DISTANCE = 128
N_GROUPS = 4
EXPERTS_PER_GROUP = 8
N_EXPERTS = N_GROUPS * EXPERTS_PER_GROUP
MASK_VALUE = -1e30
VMEM_LIMIT = 56 * 1024 * 1024
MOE_VMEM_LIMIT = 60 * 1024 * 1024

COL_DQ, COL_DK, COL_DV = 0, 8, 16
COL_RQ, COL_RK, COL_RV, COL_RG = 24, 28, 32, 40
IN_COLS = 48 * LANES


def _params(sem, vmem=VMEM_LIMIT):
    return pltpu.CompilerParams(dimension_semantics=sem, vmem_limit_bytes=vmem)


def _inproj_body(x_ref, g_ref, w_ref, o_ref, h_ref, *, row_chunk):
    tm = x_ref.shape[0]

    @pl.when(pl.program_id(1) == 0)
    def _():
        g = g_ref[...]

        def chunk(c, carry):
            r = pl.multiple_of(c * row_chunk, row_chunk)
            x = x_ref[pl.ds(r, row_chunk), :]
            ms = jnp.mean(x * x, axis=-1, keepdims=True)
            h_ref[pl.ds(r, row_chunk), :] = ((x * lax.rsqrt(ms + EPS)) * g).astype(BF16)
            return carry

        lax.fori_loop(0, tm // row_chunk, chunk, 0)

    o_ref[...] = jnp.dot(h_ref[...], w_ref[...], preferred_element_type=F32).astype(o_ref.dtype)


def _inproj(x2d, g, w_bf16, *, tm, tn):
    n, d = x2d.shape
    cols = w_bf16.shape[1]
    return pl.pallas_call(
        functools.partial(_inproj_body, row_chunk=128),
        grid=(n // tm, cols // tn),
        in_specs=[
            pl.BlockSpec((tm, d), lambda i, j: (i, 0)),
            pl.BlockSpec((1, d), lambda i, j: (0, 0)),
            pl.BlockSpec((d, tn), lambda i, j: (0, j)),
        ],
        out_specs=pl.BlockSpec((tm, tn), lambda i, j: (i, j)),
        out_shape=jax.ShapeDtypeStruct((n, cols), BF16),
        scratch_shapes=[pltpu.VMEM((tm, d), BF16)],
        compiler_params=_params(("arbitrary", "arbitrary")),
        name="inproj",
    )(x2d, g, w_bf16)


def _t5_bucket_table(n):
    d = np.arange(n)
    max_exact = N_BUCKETS // 2
    log_ratio = np.log(np.maximum(d, 1).astype(np.float64) / max_exact) / math.log(MAX_DISTANCE / max_exact)
    large = np.minimum(max_exact + (log_ratio * (N_BUCKETS - max_exact)).astype(np.int64), N_BUCKETS - 1)
    return np.where(d < max_exact, d, large).astype(np.int32)


def _bucket_tiles(t):
    table = _t5_bucket_table(2 * t)
    k = np.arange(t)[:, None]
    q = np.arange(t)[None, :]
    d0 = q - k
    tile0 = np.where(d0 >= 0, table[np.maximum(d0, 0)], -1)
    tile1 = table[t + q - k]
    return np.stack([tile0, tile1]).astype(np.int32)


def _bias_body(rb_ref, bkt_ref, o_ref):
    h = pl.program_id(0)
    bkt = bkt_ref[...]
    far = rb_ref[N_BUCKETS - 1, h]
    acc = jnp.zeros(bkt.shape, F32)
    for b in range(N_BUCKETS - 1):
        acc = jnp.where(bkt == b, rb_ref[b, h] - far, acc)
    o_ref[0] = jnp.where(bkt < 0, MASK_VALUE, acc)


def _bias_tiles(rel_bias, t):
    bkt = jnp.asarray(_bucket_tiles(t))
    return pl.pallas_call(
        _bias_body,
        grid=(N_DIFF_HEADS,),
        in_specs=[
            pl.BlockSpec(memory_space=pltpu.SMEM),
            pl.BlockSpec((2, t, t), lambda h: (0, 0, 0)),
        ],
        out_specs=pl.BlockSpec((1, 2, t, t), lambda h: (h, 0, 0, 0)),
        out_shape=jax.ShapeDtypeStruct((N_DIFF_HEADS, 2, t, t), F32),
        compiler_params=_params(("arbitrary",)),
        name="bias_tiles",
    )(rel_bias, bkt)


_NT = (((1,), (1,)), ((), ()))
_TN = (((0,), (0,)), ((), ()))


def _diffattn_body(q_ref, k_ref, v_ref, bias_ref, lq1_ref, lk1_ref, lq2_ref, lk2_ref, g_ref, o_ref,
                   qm_ref, m_ref, l_ref, acc_ref, *, t, lam_init):
    qi = pl.program_id(2)

    q = q_ref[0] * jnp.asarray(DIFF_HALF ** -0.5, BF16)
    lane = lax.broadcasted_iota(jnp.int32, q.shape, 1)
    zero = jnp.zeros_like(q)
    qm_ref[0] = jnp.where(lane < DIFF_HALF, q, zero)
    qm_ref[1] = jnp.where(lane >= DIFF_HALF, q, zero)
    m_ref[...] = jnp.full(m_ref.shape, MASK_VALUE, F32)
    l_ref[...] = jnp.zeros(l_ref.shape, F32)
    acc_ref[...] = jnp.zeros(acc_ref.shape, F32)

    def step(j, bias):
        r = pl.multiple_of(j * t, t)
        kb = k_ref[0, pl.ds(r, t), :]
        vb = v_ref[0, pl.ds(r, t), :]
        for c in range(2):
            s = lax.dot_general(kb, qm_ref[c], _NT, preferred_element_type=F32)
            if bias is not None:
                s = s + bias
            m_old = m_ref[c]
            m_new = jnp.maximum(m_old, jnp.max(s, axis=0, keepdims=True))
            alpha = jnp.exp(m_old - m_new)
            p = jnp.exp(s - m_new)
            l_ref[c] = alpha * l_ref[c] + jnp.sum(p, axis=0, keepdims=True)
            pv = lax.dot_general(vb, p.astype(BF16), _TN, preferred_element_type=F32)
            acc_ref[c] = alpha * acc_ref[c] + pv
            m_ref[c] = m_new

    def far_step(j, carry):
        step(j, None)
        return carry

    lax.fori_loop(0, qi - 1, far_step, 0)

    @pl.when(qi >= 1)
    def _():
        step(qi - 1, bias_ref[0, 1])

    step(qi, bias_ref[0, 0])

    lam = (jnp.exp(jnp.sum(lq1_ref[...] * lk1_ref[...], axis=-1, keepdims=True))
           - jnp.exp(jnp.sum(lq2_ref[...] * lk2_ref[...], axis=-1, keepdims=True)) + lam_init)
    o = acc_ref[0] * (1.0 / l_ref[0]) - lam * (acc_ref[1] * (1.0 / l_ref[1]))
    ot = o.T
    ms = jnp.mean(ot * ot, axis=-1, keepdims=True)
    o_ref[0] = (((ot * lax.rsqrt(ms + EPS)) * g_ref[...]) * (1.0 - lam_init)).astype(o_ref.dtype)


def _diff_attention(proj3, bias, lq1, lk1, lq2, lk2, subln_g, *, t, lam_init):
    b, s, _ = proj3.shape
    vec = pl.BlockSpec((1, DIFF_HALF), lambda bi, h, qi: (0, 0))
    return pl.pallas_call(
        functools.partial(_diffattn_body, t=t, lam_init=lam_init),
        grid=(b, N_DIFF_HEADS, s // t),
        in_specs=[
            pl.BlockSpec((1, t, LANES), lambda bi, h, qi: (bi, qi, COL_DQ + h)),
            pl.BlockSpec((1, s, LANES), lambda bi, h, qi: (bi, 0, COL_DK + h)),
            pl.BlockSpec((1, s, LANES), lambda bi, h, qi: (bi, 0, COL_DV + h)),
            pl.BlockSpec((1, 2, t, t), lambda bi, h, qi: (h, 0, 0, 0)),
            vec, vec, vec, vec,
            pl.BlockSpec((1, LANES), lambda bi, h, qi: (0, 0)),
        ],
        out_specs=pl.BlockSpec((1, t, LANES), lambda bi, h, qi: (bi, qi, h)),
        out_shape=jax.ShapeDtypeStruct((b, s, N_DIFF_HEADS * LANES), BF16),
        scratch_shapes=[
            pltpu.VMEM((2, t, LANES), BF16),
            pltpu.VMEM((2, 1, t), F32),
            pltpu.VMEM((2, 1, t), F32),
            pltpu.VMEM((2, LANES, t), F32),
        ],
        compiler_params=_params(("arbitrary", "arbitrary", "arbitrary")),
        name="diff_attention",
    )(proj3, proj3, proj3, bias, lq1, lk1, lq2, lk2, subln_g)


def _retention_tables(s):
    c = RET_CHUNK
    half = RET_QK // 2
    inv_freq = ROPE_BASE ** (-jnp.arange(0, RET_QK, 2, dtype=F32) / RET_QK)
    ang = jnp.arange(s, dtype=F32)[:, None] * inv_freq[None, :]
    cos, sin = jnp.cos(ang), jnp.sin(ang)
    cos_t = jnp.tile(cos, (1, LANES // half))
    sin_t = jnp.tile(jnp.concatenate([-sin, sin], axis=1), (1, LANES // RET_QK))
    log_g = jnp.log(1.0 - jnp.exp2(-5.0 - jnp.arange(N_RET_HEADS, dtype=F32)))
    n = jnp.arange(c, dtype=F32)
    diff = n[:, None] - n[None, :]
    inner = jnp.where(diff[None] >= 0, jnp.exp(jnp.maximum(diff, 0.0)[None] * log_g[:, None, None]), 0.0)
    cross = jnp.exp((n[None] + 1.0) * log_g[:, None])
    key = jnp.exp((c - 1.0 - n[None]) * log_g[:, None])
    chunk = jnp.exp(c * log_g)
    bc = lambda a: jnp.broadcast_to(a[..., None], a.shape + (LANES,)).astype(F32)
    return cos_t, sin_t, inner.astype(F32), bc(cross), bc(key), bc(chunk[:, None])


def _retention_body(rq_ref, rk_ref, rv_ref, rg_ref, cos_ref, sin_ref, inner_ref, cross_ref, key_ref,
                    chunk_ref, gn_ref, o_ref, state_ref):
    c = RET_CHUNK
    nc = rq_ref.shape[1] // c
    state_ref[...] = jnp.zeros(state_ref.shape, F32)
    lane = lax.broadcasted_iota(jnp.int32, (c, LANES), 1)
    first_half = (lane % RET_QK) < (RET_QK // 2)
    head_mask = [lane < RET_QK, lane >= RET_QK]

    def rotary(x, cos, sin):
        rot = jnp.where(first_half, pltpu.roll(x, LANES - RET_QK // 2, 1), pltpu.roll(x, RET_QK // 2, 1))
        return x * cos + rot * sin

    def chunk_step(ci, carry):
        r = pl.multiple_of(ci * c, c)
        cos = cos_ref[pl.ds(r, c), :]
        sin = sin_ref[pl.ds(r, c), :]
        q = rotary(rq_ref[0, pl.ds(r, c), :].astype(F32), cos, sin)
        k = rotary(rk_ref[0, pl.ds(r, c), :].astype(F32), cos, sin) * (RET_QK ** -0.5)
        kb = k.astype(BF16)
        for hh in range(2):
            qh = jnp.where(head_mask[hh], q, 0.0).astype(BF16)
            vh = rv_ref[0, pl.ds(r, c), hh * LANES:(hh + 1) * LANES]
            scores = lax.dot_general(qh, kb, _NT, preferred_element_type=F32) * inner_ref[hh]
            inner = jnp.dot(scores.astype(BF16), vh, preferred_element_type=F32)
            state = state_ref[hh]
            cross = jnp.dot(qh, state.astype(BF16), preferred_element_type=F32) * cross_ref[hh]
            kd = (k * key_ref[hh]).astype(BF16)
            state_ref[hh] = state * chunk_ref[hh] + lax.dot_general(kd, vh, _TN, preferred_element_type=F32)
            y = inner + cross
            mu = jnp.mean(y, axis=-1, keepdims=True)
            yc = y - mu
            var = jnp.mean(yc * yc, axis=-1, keepdims=True)
            yn = (yc * lax.rsqrt(var + EPS)) * gn_ref[:, hh * LANES:(hh + 1) * LANES]
            gate = rg_ref[0, pl.ds(r, c), hh * LANES:(hh + 1) * LANES].astype(F32)
            silu = gate * (1.0 / (1.0 + jnp.exp(-gate)))
            o_ref[0, pl.ds(r, c), hh * LANES:(hh + 1) * LANES] = (silu * yn).astype(o_ref.dtype)
        return carry

    lax.fori_loop(0, nc, chunk_step, 0)


def _retention(proj3, tables, gn_g):
    b, s, _ = proj3.shape
    cos_t, sin_t, inner, cross, key, chunk = tables
    c = RET_CHUNK
    pairs = N_RET_HEADS // 2
    pos = pl.BlockSpec((s, LANES), lambda bi, hp: (0, 0))
    return pl.pallas_call(
        _retention_body,
        grid=(b, pairs),
        in_specs=[
            pl.BlockSpec((1, s, LANES), lambda bi, hp: (bi, 0, COL_RQ + hp)),
            pl.BlockSpec((1, s, LANES), lambda bi, hp: (bi, 0, COL_RK + hp)),
            pl.BlockSpec((1, s, 2 * LANES), lambda bi, hp: (bi, 0, COL_RV // 2 + hp)),
            pl.BlockSpec((1, s, 2 * LANES), lambda bi, hp: (bi, 0, COL_RG // 2 + hp)),
            pos, pos,
            pl.BlockSpec((2, c, c), lambda bi, hp: (hp, 0, 0)),
            pl.BlockSpec((2, c, LANES), lambda bi, hp: (hp, 0, 0)),
            pl.BlockSpec((2, c, LANES), lambda bi, hp: (hp, 0, 0)),
            pl.BlockSpec((2, 1, LANES), lambda bi, hp: (hp, 0, 0)),
            pl.BlockSpec((1, 2 * LANES), lambda bi, hp: (0, hp)),
        ],
        out_specs=pl.BlockSpec((1, s, 2 * LANES), lambda bi, hp: (bi, 0, hp)),
        out_shape=jax.ShapeDtypeStruct((b, s, N_RET_HEADS * LANES), BF16),
        scratch_shapes=[pltpu.VMEM((2, LANES, LANES), F32)],
        compiler_params=_params(("arbitrary", "arbitrary")),
        name="retention",
    )(proj3, proj3, proj3, proj3, cos_t, sin_t, inner, cross, key, chunk, gn_g)


def _route(logits):
    lane = lax.broadcasted_iota(jnp.int32, logits.shape, 1).astype(F32)
    big = float(LANES)
    is_group = lane < N_GROUPS
    gl = jnp.where(is_group, logits, -jnp.inf)
    gmax = jnp.max(gl, axis=-1, keepdims=True)
    gidx = jnp.min(jnp.where(gl == gmax, lane, big), axis=-1, keepdims=True)
    p_group = 1.0 / jnp.sum(jnp.where(is_group, jnp.exp(gl - gmax), 0.0), axis=-1, keepdims=True)
    lo = N_GROUPS + EXPERTS_PER_GROUP * gidx
    il = jnp.where(lane >= lo, jnp.where(lane < lo + EXPERTS_PER_GROUP, logits, -jnp.inf), -jnp.inf)
    v1 = jnp.max(il, axis=-1, keepdims=True)
    i1 = jnp.min(jnp.where(il == v1, lane, big), axis=-1, keepdims=True)
    il2 = jnp.where(lane == i1, -jnp.inf, il)
    v2 = jnp.max(il2, axis=-1, keepdims=True)
    i2 = jnp.min(jnp.where(il2 == v2, lane, big), axis=-1, keepdims=True)
    e2 = jnp.exp(v2 - v1)
    inv = 1.0 / (1.0 + e2)
    g1 = p_group * inv
    g2 = p_group * (e2 * inv)
    out = jnp.where(lane == 0.0, i1 - N_GROUPS,
                    jnp.where(lane == 1.0, i2 - N_GROUPS,
                              jnp.where(lane == 2.0, g1, jnp.where(lane == 3.0, g2, 0.0))))
    return out


def _outproj_body(a_ref, r_ref, x_ref, w_ref, g_ref, wr_ref, br_ref, x1_ref, h2_ref, route_ref):
    half = a_ref.shape[1]
    acc = jnp.dot(a_ref[...], w_ref[:half, :], preferred_element_type=F32)
    acc = acc + jnp.dot(r_ref[...], w_ref[half:, :], preferred_element_type=F32)
    x1 = x_ref[...] + acc
    x1_ref[...] = x1
    ms = jnp.mean(x1 * x1, axis=-1, keepdims=True)
    h2 = (x1 * lax.rsqrt(ms + EPS)) * g_ref[...]
    h2_ref[...] = h2
    logits = jnp.dot(h2.astype(BF16), wr_ref[...], preferred_element_type=F32) + br_ref[...]
    route_ref[...] = _route(logits)


def _outproj_router(a2d, r2d, x2d, w_out_bf16, g, wr_bf16, br, *, tm):
    n, d = x2d.shape
    half = a2d.shape[1]
    row = lambda width: pl.BlockSpec((tm, width), lambda i: (i, 0))
    full = lambda shape: pl.BlockSpec(shape, lambda i: (0, 0))
    return pl.pallas_call(
        _outproj_body,
        grid=(n // tm,),
        in_specs=[row(half), row(half), row(d), full((2 * half, d)), full((1, d)), full((d, LANES)),
                  full((1, LANES))],
        out_specs=[row(d), row(d), row(LANES)],
        out_shape=[jax.ShapeDtypeStruct((n, d), F32), jax.ShapeDtypeStruct((n, d), F32),
                   jax.ShapeDtypeStruct((n, LANES), F32)],
        compiler_params=_params(("arbitrary",)),
        name="outproj_router",
    )(a2d, r2d, x2d, w_out_bf16, g, wr_bf16, br)


def _moe_plan(expert_id, tm):
    n_slots = expert_id.shape[0]
    n_tiles = n_slots // tm
    max_visits = n_tiles + N_EXPERTS - 1
    order = jnp.argsort(expert_id).astype(jnp.int32)
    sizes = jnp.zeros((N_EXPERTS,), jnp.int32).at[expert_id].add(1)
    ends = jnp.cumsum(sizes)
    starts = ends - sizes
    first_tile = starts // tm
    last_tile = jnp.maximum(ends - 1, 0) // tm
    visits = jnp.where(sizes > 0, last_tile - first_tile + 1, 0)
    v_end = jnp.cumsum(visits)
    v_start = v_end - visits
    n_visits = v_end[-1]
    v = jnp.arange(max_visits, dtype=jnp.int32)
    vc = jnp.minimum(v, n_visits - 1)
    v_expert = jnp.searchsorted(v_end, vc, side="right").astype(jnp.int32)
    v_tile = (first_tile[v_expert] + (vc - v_start[v_expert])).astype(jnp.int32)
    eid = jnp.arange(N_EXPERTS, dtype=jnp.int32)
    cand = jnp.where((sizes[None, :] > 0) & (eid[None, :] > eid[:, None]), eid[None, :], N_EXPERTS)
    nxt = jnp.min(cand, axis=1)
    next_expert = jnp.where(nxt < N_EXPERTS, nxt, -1).astype(jnp.int32)
    return dict(v_tile=v_tile, v_expert=v_expert, n_visits=n_visits.reshape(1).astype(jnp.int32),
                starts=starts.astype(jnp.int32), ends=ends.astype(jnp.int32), next_expert=next_expert,
                order=order.reshape(n_tiles, tm), max_visits=max_visits)


def _moe_body(vt_ref, ve_ref, nv_ref, st_ref, en_ref, nx_ref,
              order_hbm, h2_hbm, wg_hbm, wu_hbm, wd_hbm, ys_hbm,
              idx_smem, xbuf, obuf, wg_st, wu_st, wd_st, wg_rs, wu_rs, wd_rs,
              isem, gsem, ssem, wsem, *, n_tokens, n_tiles, cast_rows):
    v = pl.program_id(0)
    tm = xbuf.shape[1]
    nv = nv_ref[0]
    t = vt_ref[v]
    e = ve_ref[v]
    valid = v < nv
    vp = jnp.maximum(v - 1, 0)
    vn = jnp.minimum(v + 1, vt_ref.shape[0] - 1)
    first_of_tile = jnp.logical_or(v == 0, vt_ref[vp] != t)
    last_of_tile = jnp.logical_or(v == nv - 1, vt_ref[vn] != t)
    first_of_expert = jnp.logical_or(v == 0, ve_ref[vp] != e)

    def idx_copy(tile):
        slot = lax.rem(tile, 3)
        return pltpu.make_async_copy(order_hbm.at[tile], idx_smem.at[slot], isem.at[slot])

    def weight_copies(ex):
        return (pltpu.make_async_copy(wg_hbm.at[ex], wg_st, wsem.at[0]),
                pltpu.make_async_copy(wu_hbm.at[ex], wu_st, wsem.at[1]),
                pltpu.make_async_copy(wd_hbm.at[ex], wd_st, wsem.at[2]))

    def issue_gather(tile):
        islot = lax.rem(tile, 3)
        xslot = lax.rem(tile, 2)

        def body(r, carry):
            tok = lax.shift_right_logical(idx_smem[islot, r], 1)
            pltpu.make_async_copy(h2_hbm.at[pl.ds(tok, 1)], xbuf.at[xslot, pl.ds(r, 1)], gsem.at[xslot]).start()
            return carry

        lax.fori_loop(0, tm, body, 0, unroll=8)

    def wait_gather(tile):
        xslot = lax.rem(tile, 2)
        pltpu.make_async_copy(h2_hbm.at[pl.ds(0, tm)], xbuf.at[xslot], gsem.at[xslot]).wait()

    def issue_scatter(tile):
        islot = lax.rem(tile, 3)
        oslot = lax.rem(tile, 2)

        def body(r, carry):
            s = idx_smem[islot, r]
            dst = (s & 1) * n_tokens + lax.shift_right_logical(s, 1)
            pltpu.make_async_copy(obuf.at[oslot, pl.ds(r, 1)], ys_hbm.at[pl.ds(dst, 1)], ssem.at[oslot]).start()
            return carry

        lax.fori_loop(0, tm, body, 0, unroll=8)

    def wait_scatter(oslot):
        pltpu.make_async_copy(obuf.at[oslot], ys_hbm.at[pl.ds(0, tm)], ssem.at[oslot]).wait()

    @pl.when(jnp.logical_and(valid, first_of_tile))
    def _():
        @pl.when(t == 0)
        def _():
            first = idx_copy(t)
            first.start()
            first.wait()
            issue_gather(t)
            idx_copy(t + 1).start()

        @pl.when(t + 1 < n_tiles)
        def _():
            idx_copy(t + 1).wait()
            issue_gather(t + 1)

        @pl.when(t + 2 < n_tiles)
        def _():
            idx_copy(t + 2).start()

        @pl.when(t >= 2)
        def _():
            wait_scatter(lax.rem(t, 2))

    @pl.when(jnp.logical_and(valid, first_of_expert))
    def _():
        @pl.when(v == 0)
        def _():
            for cp in weight_copies(e):
                cp.start()

        for cp in weight_copies(e):
            cp.wait()

        def cast_up(i, carry):
            r = pl.multiple_of(i * cast_rows, cast_rows)
            wg_rs[pl.ds(r, cast_rows), :] = wg_st[pl.ds(r, cast_rows), :].astype(BF16)
            wu_rs[pl.ds(r, cast_rows), :] = wu_st[pl.ds(r, cast_rows), :].astype(BF16)
            return carry

        def cast_down(i, carry):
            r = pl.multiple_of(i * cast_rows, cast_rows)
            wd_rs[pl.ds(r, cast_rows), :] = wd_st[pl.ds(r, cast_rows), :].astype(BF16)
            return carry

        lax.fori_loop(0, wg_st.shape[0] // cast_rows, cast_up, 0)
        lax.fori_loop(0, wd_st.shape[0] // cast_rows, cast_down, 0)
        nxt = nx_ref[e]

        @pl.when(nxt >= 0)
        def _():
            for cp in weight_copies(nxt):
                cp.start()

    @pl.when(jnp.logical_and(valid, first_of_tile))
    def _():
        wait_gather(t)

    @pl.when(valid)
    def _():
        slot = lax.rem(t, 2)
        x = xbuf[slot].astype(BF16)
        a = jnp.dot(x, wg_rs[...], preferred_element_type=F32)
        b = jnp.dot(x, wu_rs[...], preferred_element_type=F32)
        row = t * tm + lax.broadcasted_iota(jnp.int32, (tm, 1), 0)
        mine = jnp.logical_and(row >= st_ref[e], row < en_ref[e])
        hmid = jnp.where(mine, (a * (1.0 / (1.0 + jnp.exp(-a)))) * b, 0.0).astype(BF16)
        contrib = jnp.dot(hmid, wd_rs[...], preferred_element_type=F32)

        @pl.when(first_of_tile)
        def _():
            obuf[slot] = contrib

        @pl.when(jnp.logical_not(first_of_tile))
        def _():
            obuf[slot] += contrib

    @pl.when(jnp.logical_and(valid, last_of_tile))
    def _():
        issue_scatter(t)

    @pl.when(v == nv - 1)
    def _():
        wait_scatter(0)
        wait_scatter(1)


def _moe_ffn(h2, plan, w_gate, w_up, w_down, *, tm):
    n, d = h2.shape
    n_slots = 2 * n
    n_tiles = n_slots // tm
    assert n_tiles >= 2 and n_slots % tm == 0
    d_ff = w_gate.shape[2]
    any_spec = pl.BlockSpec(memory_space=pl.ANY)
    grid_spec = pltpu.PrefetchScalarGridSpec(
        num_scalar_prefetch=6,
        grid=(plan["max_visits"],),
        in_specs=[any_spec] * 5,
        out_specs=any_spec,
        scratch_shapes=[
            pltpu.SMEM((3, tm), jnp.int32),
            pltpu.VMEM((2, tm, d), F32),
            pltpu.VMEM((2, tm, d), F32),
            pltpu.VMEM((d, d_ff), F32),
            pltpu.VMEM((d, d_ff), F32),
            pltpu.VMEM((d_ff, d), F32),
            pltpu.VMEM((d, d_ff), BF16),
            pltpu.VMEM((d, d_ff), BF16),
            pltpu.VMEM((d_ff, d), BF16),
            pltpu.SemaphoreType.DMA((3,)),
            pltpu.SemaphoreType.DMA((2,)),
            pltpu.SemaphoreType.DMA((2,)),
            pltpu.SemaphoreType.DMA((3,)),
        ],
    )
    return pl.pallas_call(
        functools.partial(_moe_body, n_tokens=n, n_tiles=n_tiles, cast_rows=128),
        grid_spec=grid_spec,
        out_shape=jax.ShapeDtypeStruct((n_slots, d), F32),
        compiler_params=_params(("arbitrary",), vmem=MOE_VMEM_LIMIT),
        name="moe_ffn",
    )(plan["v_tile"], plan["v_expert"], plan["n_visits"], plan["starts"], plan["ends"], plan["next_expert"],
      plan["order"], h2, w_gate, w_up, w_down)


def _final_body(x1_ref, ys0_ref, ys1_ref, route_ref, g_ref, o_ref):
    route = route_ref[...]
    y = ys0_ref[...] * route[:, 2:3] + ys1_ref[...] * route[:, 3:4]
    x2 = x1_ref[...] + y
    ms = jnp.mean(x2 * x2, axis=-1, keepdims=True)
    o_ref[...] = ((x2 * lax.rsqrt(ms + EPS)) * g_ref[...]).astype(o_ref.dtype)


def _final(x1, ys, route, g, *, tm):
    n, d = x1.shape
    nb = n // tm
    return pl.pallas_call(
        _final_body,
        grid=(nb,),
        in_specs=[
            pl.BlockSpec((tm, d), lambda i: (i, 0)),
            pl.BlockSpec((tm, d), lambda i: (i, 0)),
            pl.BlockSpec((tm, d), lambda i: (nb + i, 0)),
            pl.BlockSpec((tm, LANES), lambda i: (i, 0)),
            pl.BlockSpec((1, d), lambda i: (0, 0)),
        ],
        out_specs=pl.BlockSpec((tm, d), lambda i: (i, 0)),
        out_shape=jax.ShapeDtypeStruct((n, d), F32),
        compiler_params=_params(("arbitrary",)),
        name="final_norm",
    )(x1, ys, ys, route, g)


def _layer(x, rel_bias, attn_norm_g, w_in, lq1, lk1, lq2, lk2, diff_subln_g, ret_gn_g, w_out, ffn_norm_g,
           w_group_router, b_group, w_inner_router, b_inner, w_gate_exp, w_up_exp, w_down_exp, *, layer,
           attn_tile, row_tile, moe_tile):
    b, s, d = x.shape
    n = b * s
    x2d = x.reshape(n, d)
    lam_init = 0.8 - 0.6 * math.exp(-0.3 * layer)

    proj = _inproj(x2d, attn_norm_g.reshape(1, d), w_in.astype(BF16), tm=min(2 * row_tile, n), tn=8 * LANES)
    proj3 = proj.reshape(b, s, IN_COLS)

    bias = _bias_tiles(rel_bias, attn_tile)
    vec = lambda a: a.reshape(1, -1)
    a_out = _diff_attention(proj3, bias, vec(lq1), vec(lk1), vec(lq2), vec(lk2), vec(diff_subln_g),
                            t=attn_tile, lam_init=lam_init)
    r_out = _retention(proj3, _retention_tables(s), vec(ret_gn_g))

    pad = LANES - N_GROUPS - N_EXPERTS
    wr = jnp.concatenate([w_group_router, jnp.transpose(w_inner_router, (1, 0, 2)).reshape(d, N_EXPERTS),
                          jnp.zeros((d, pad), F32)], axis=1).astype(BF16)
    br = jnp.concatenate([b_group, b_inner.reshape(-1), jnp.zeros((pad,), F32)]).reshape(1, LANES)
    x1, h2, route = _outproj_router(a_out.reshape(n, -1), r_out.reshape(n, -1), x2d, w_out.astype(BF16),
                                    vec(ffn_norm_g), wr, br, tm=row_tile)

    expert_id = route[:, :2].astype(jnp.int32).reshape(-1)
    plan = _moe_plan(expert_id, moe_tile)
    ys = _moe_ffn(h2, plan, w_gate_exp, w_up_exp, w_down_exp, tm=moe_tile)
    return x1, ys, route


def kernel(x, rel_bias, attn_norm_g, w_in, lambda_q1, lambda_k1, lambda_q2, lambda_k2, diff_subln_g, ret_gn_g,
           w_out, ffn_norm_g, w_group_router, b_group, w_inner_router, b_inner, w_gate_exp, w_up_exp, w_down_exp,
           final_g):
    b, s, d = x.shape
    depth = w_in.shape[0]
    assert depth == 1, "the final norm is fused into the single layer's combine step"
    first = lambda a: a.reshape(a.shape[1:])
    x1, ys, route = _layer(x, rel_bias, attn_norm_g[0], w_in[0], lambda_q1[0], lambda_k1[0], lambda_q2[0],
                           lambda_k2[0], diff_subln_g[0], ret_gn_g[0], w_out[0], ffn_norm_g[0],
                           w_group_router[0], b_group[0], w_inner_router[0], b_inner[0], first(w_gate_exp),
                           first(w_up_exp), first(w_down_exp), layer=0, attn_tile=min(512, s), row_tile=512,
                           moe_tile=256)
    out = _final(x1, ys, route, final_g.reshape(1, d), tm=256)
    return out.reshape(b, s, d)
```

```python
import functools
import math

import numpy as np
import jax
import jax.numpy as jnp
from jax import lax
from jax.experimental import pallas as pl
from jax.experimental.pallas import tpu as pltpu

F32 = jnp.float32
BF16 = jnp.bfloat16

EPS = 1e-6
LANES = 128
N_DIFF_HEADS = 8
DIFF_HALF = 64
N_RET_HEADS = 8
RET_QK = 64
RET_CHUNK = 128
ROPE_BASE = 10000.0
N_BUCKETS = 32
MAX_DISTANCE = 128
N_GROUPS = 4
EXPERTS_PER_GROUP = 8
N_EXPERTS = N_GROUPS * EXPERTS_PER_GROUP
MASK_VALUE = -1e30
VMEM_LIMIT = 56 * 1024 * 1024
MOE_VMEM_LIMIT = 60 * 1024 * 1024

COL_DQ, COL_DK, COL_DV = 0, 8, 16
COL_RQ, COL_RK, COL_RV, COL_RG = 24, 28, 32, 40
IN_COLS = 48 * LANES


def _params(sem, vmem=VMEM_LIMIT):
    return pltpu.CompilerParams(dimension_semantics=sem, vmem_limit_bytes=vmem)


def _inproj_body(x_ref, g_ref, w_ref, o_ref, h_ref, *, row_chunk):
    tm = x_ref.shape[0]

    @pl.when(pl.program_id(1) == 0)
    def _():
        g = g_ref[...]

        def chunk(c, carry):
            r = pl.multiple_of(c * row_chunk, row_chunk)
            x = x_ref[pl.ds(r, row_chunk), :]
            ms = jnp.mean(x * x, axis=-1, keepdims=True)
            h_ref[pl.ds(r, row_chunk), :] = ((x * lax.rsqrt(ms + EPS)) * g).astype(BF16)
            return carry

        lax.fori_loop(0, tm // row_chunk, chunk, 0)

    o_ref[...] = jnp.dot(h_ref[...], w_ref[...], preferred_element_type=F32).astype(o_ref.dtype)


def _inproj(x2d, g, w_bf16, *, tm, tn):
    n, d = x2d.shape
    cols = w_bf16.shape[1]
    return pl.pallas_call(
        functools.partial(_inproj_body, row_chunk=128),
        grid=(n // tm, cols // tn),
        in_specs=[
            pl.BlockSpec((tm, d), lambda i, j: (i, 0)),
            pl.BlockSpec((1, d), lambda i, j: (0, 0)),
            pl.BlockSpec((d, tn), lambda i, j: (0, j)),
        ],
        out_specs=pl.BlockSpec((tm, tn), lambda i, j: (i, j)),
        out_shape=jax.ShapeDtypeStruct((n, cols), BF16),
        scratch_shapes=[pltpu.VMEM((tm, d), BF16)],
        compiler_params=_params(("arbitrary", "arbitrary")),
        name="inproj",
    )(x2d, g, w_bf16)


def _t5_bucket_table(n):
    d = np.arange(n)
    max_exact = N_BUCKETS // 2
    log_ratio = np.log(np.maximum(d, 1).astype(np.float64) / max_exact) / math.log(MAX_DISTANCE / max_exact)
    large = np.minimum(max_exact + (log_ratio * (N_BUCKETS - max_exact)).astype(np.int64), N_BUCKETS - 1)
    return np.where(d < max_exact, d, large).astype(np.int32)


def _bucket_tiles(t):
    table = _t5_bucket_table(2 * t)
    k = np.arange(t)[:, None]
    q = np.arange(t)[None, :]
    d0 = q - k
    tile0 = np.where(d0 >= 0, table[np.maximum(d0, 0)], -1)
    tile1 = table[t + q - k]
    return np.stack([tile0, tile1]).astype(np.int32)


def _bias_body(rb_ref, bkt_ref, o_ref):
    h = pl.program_id(0)
    bkt = bkt_ref[...]
    far = rb_ref[N_BUCKETS - 1, h]
    acc = jnp.zeros(bkt.shape, F32)
    for b in range(N_BUCKETS - 1):
        acc = jnp.where(bkt == b, rb_ref[b, h] - far, acc)
    o_ref[0] = jnp.where(bkt < 0, MASK_VALUE, acc)


def _bias_tiles(rel_bias, t):
    bkt = jnp.asarray(_bucket_tiles(t))
    return pl.pallas_call(
        _bias_body,
        grid=(N_DIFF_HEADS,),
        in_specs=[
            pl.BlockSpec(memory_space=pltpu.SMEM),
            pl.BlockSpec((2, t, t), lambda h: (0, 0, 0)),
        ],
        out_specs=pl.BlockSpec((1, 2, t, t), lambda h: (h, 0, 0, 0)),
        out_shape=jax.ShapeDtypeStruct((N_DIFF_HEADS, 2, t, t), F32),
        compiler_params=_params(("arbitrary",)),
        name="bias_tiles",
    )(rel_bias, bkt)


_NT = (((1,), (1,)), ((), ()))
_TN = (((0,), (0,)), ((), ()))


def _diffattn_body(q_ref, k_ref, v_ref, bias_ref, lq1_ref, lk1_ref, lq2_ref, lk2_ref, g_ref, o_ref,
                   qm_ref, m_ref, l_ref, acc_ref, *, t, lam_init):
    qi = pl.program_id(2)

    q = q_ref[0] * jnp.asarray(DIFF_HALF ** -0.5, BF16)
    lane = lax.broadcasted_iota(jnp.int32, q.shape, 1)
    zero = jnp.zeros_like(q)
    qm_ref[0] = jnp.where(lane < DIFF_HALF, q, zero)
    qm_ref[1] = jnp.where(lane >= DIFF_HALF, q, zero)
    m_ref[...] = jnp.full(m_ref.shape, MASK_VALUE, F32)
    l_ref[...] = jnp.zeros(l_ref.shape, F32)
    acc_ref[...] = jnp.zeros(acc_ref.shape, F32)

    def step(j, bias):
        r = pl.multiple_of(j * t, t)
        kb = k_ref[0, pl.ds(r, t), :]
        vb = v_ref[0, pl.ds(r, t), :]
        for c in range(2):
            s = lax.dot_general(kb, qm_ref[c], _NT, preferred_element_type=F32)
            if bias is not None:
                s = s + bias
            m_old = m_ref[c]
            m_new = jnp.maximum(m_old, jnp.max(s, axis=0, keepdims=True))
            alpha = jnp.exp(m_old - m_new)
            p = jnp.exp(s - m_new)
            l_ref[c] = alpha * l_ref[c] + jnp.sum(p, axis=0, keepdims=True)
            pv = lax.dot_general(vb, p.astype(BF16), _TN, preferred_element_type=F32)
            acc_ref[c] = alpha * acc_ref[c] + pv
            m_ref[c] = m_new

    def far_step(j, carry):
        step(j, None)
        return carry

    lax.fori_loop(0, qi - 1, far_step, 0)

    @pl.when(qi >= 1)
    def _():
        step(qi - 1, bias_ref[0, 1])

    step(qi, bias_ref[0, 0])

    lam = (jnp.exp(jnp.sum(lq1_ref[...] * lk1_ref[...], axis=-1, keepdims=True))
           - jnp.exp(jnp.sum(lq2_ref[...] * lk2_ref[...], axis=-1, keepdims=True)) + lam_init)
    o = acc_ref[0] * (1.0 / l_ref[0]) - lam * (acc_ref[1] * (1.0 / l_ref[1]))
    ot = o.T
    ms = jnp.mean(ot * ot, axis=-1, keepdims=True)
    o_ref[0] = (((ot * lax.rsqrt(ms + EPS)) * g_ref[...]) * (1.0 - lam_init)).astype(o_ref.dtype)


def _diff_attention(proj3, bias, lq1, lk1, lq2, lk2, subln_g, *, t, lam_init):
    b, s, _ = proj3.shape
    vec = pl.BlockSpec((1, DIFF_HALF), lambda bi, h, qi: (0, 0))
    return pl.pallas_call(
        functools.partial(_diffattn_body, t=t, lam_init=lam_init),
        grid=(b, N_DIFF_HEADS, s // t),
        in_specs=[
            pl.BlockSpec((1, t, LANES), lambda bi, h, qi: (bi, qi, COL_DQ + h)),
            pl.BlockSpec((1, s, LANES), lambda bi, h, qi: (bi, 0, COL_DK + h)),
            pl.BlockSpec((1, s, LANES), lambda bi, h, qi: (bi, 0, COL_DV + h)),
            pl.BlockSpec((1, 2, t, t), lambda bi, h, qi: (h, 0, 0, 0)),
            vec, vec, vec, vec,
            pl.BlockSpec((1, LANES), lambda bi, h, qi: (0, 0)),
        ],
        out_specs=pl.BlockSpec((1, t, LANES), lambda bi, h, qi: (bi, qi, h)),
        out_shape=jax.ShapeDtypeStruct((b, s, N_DIFF_HEADS * LANES), BF16),
        scratch_shapes=[
            pltpu.VMEM((2, t, LANES), BF16),
            pltpu.VMEM((2, 1, t), F32),
            pltpu.VMEM((2, 1, t), F32),
            pltpu.VMEM((2, LANES, t), F32),
        ],
        compiler_params=_params(("arbitrary", "arbitrary", "arbitrary")),
        name="diff_attention",
    )(proj3, proj3, proj3, bias, lq1, lk1, lq2, lk2, subln_g)


def _retention_tables(s):
    c = RET_CHUNK
    half = RET_QK // 2
    inv_freq = ROPE_BASE ** (-jnp.arange(0, RET_QK, 2, dtype=F32) / RET_QK)
    ang = jnp.arange(s, dtype=F32)[:, None] * inv_freq[None, :]
    cos, sin = jnp.cos(ang), jnp.sin(ang)
    cos_t = jnp.tile(cos, (1, LANES // half))
    sin_t = jnp.tile(jnp.concatenate([-sin, sin], axis=1), (1, LANES // RET_QK))
    log_g = jnp.log(1.0 - jnp.exp2(-5.0 - jnp.arange(N_RET_HEADS, dtype=F32)))
    n = jnp.arange(c, dtype=F32)
    diff = n[:, None] - n[None, :]
    inner = jnp.where(diff[None] >= 0, jnp.exp(jnp.maximum(diff, 0.0)[None] * log_g[:, None, None]), 0.0)
    cross = jnp.exp((n[None] + 1.0) * log_g[:, None])
    key = jnp.exp((c - 1.0 - n[None]) * log_g[:, None])
    chunk = jnp.exp(c * log_g)
    bc = lambda a: jnp.broadcast_to(a[..., None], a.shape + (LANES,)).astype(F32)
    return cos_t, sin_t, inner.astype(F32), bc(cross), bc(key), bc(chunk[:, None])


def _retention_body(rq_ref, rk_ref, rv_ref, rg_ref, cos_ref, sin_ref, inner_ref, cross_ref, key_ref,
                    chunk_ref, gn_ref, o_ref, state_ref):
    c = RET_CHUNK
    nc = rq_ref.shape[1] // c
    state_ref[...] = jnp.zeros(state_ref.shape, F32)
    lane = lax.broadcasted_iota(jnp.int32, (c, LANES), 1)
    first_half = (lane % RET_QK) < (RET_QK // 2)
    head_mask = [lane < RET_QK, lane >= RET_QK]

    def rotary(x, cos, sin):
        rot = jnp.where(first_half, pltpu.roll(x, LANES - RET_QK // 2, 1), pltpu.roll(x, RET_QK // 2, 1))
        return x * cos + rot * sin

    def chunk_step(ci, carry):
        r = pl.multiple_of(ci * c, c)
        cos = cos_ref[pl.ds(r, c), :]
        sin = sin_ref[pl.ds(r, c), :]
        q = rotary(rq_ref[0, pl.ds(r, c), :].astype(F32), cos, sin)
        k = rotary(rk_ref[0, pl.ds(r, c), :].astype(F32), cos, sin) * (RET_QK ** -0.5)
        kb = k.astype(BF16)
        for hh in range(2):
            qh = jnp.where(head_mask[hh], q, 0.0).astype(BF16)
            vh = rv_ref[0, pl.ds(r, c), hh * LANES:(hh + 1) * LANES]
            scores = lax.dot_general(qh, kb, _NT, preferred_element_type=F32) * inner_ref[hh]
            inner = jnp.dot(scores.astype(BF16), vh, preferred_element_type=F32)
            state = state_ref[hh]
            cross = jnp.dot(qh, state.astype(BF16), preferred_element_type=F32) * cross_ref[hh]
            kd = (k * key_ref[hh]).astype(BF16)
            state_ref[hh] = state * chunk_ref[hh] + lax.dot_general(kd, vh, _TN, preferred_element_type=F32)
            y = inner + cross
            mu = jnp.mean(y, axis=-1, keepdims=True)
            yc = y - mu
            var = jnp.mean(yc * yc, axis=-1, keepdims=True)
            yn = (yc * lax.rsqrt(var + EPS)) * gn_ref[:, hh * LANES:(hh + 1) * LANES]
            gate = rg_ref[0, pl.ds(r, c), hh * LANES:(hh + 1) * LANES].astype(F32)
            silu = gate * (1.0 / (1.0 + jnp.exp(-gate)))
            o_ref[0, pl.ds(r, c), hh * LANES:(hh + 1) * LANES] = (silu * yn).astype(o_ref.dtype)
        return carry

    lax.fori_loop(0, nc, chunk_step, 0)


def _retention(proj3, tables, gn_g):
    b, s, _ = proj3.shape
    cos_t, sin_t, inner, cross, key, chunk = tables
    c = RET_CHUNK
    pairs = N_RET_HEADS // 2
    pos = pl.BlockSpec((s, LANES), lambda bi, hp: (0, 0))
    return pl.pallas_call(
        _retention_body,
        grid=(b, pairs),
        in_specs=[
            pl.BlockSpec((1, s, LANES), lambda bi, hp: (bi, 0, COL_RQ + hp)),
            pl.BlockSpec((1, s, LANES), lambda bi, hp: (bi, 0, COL_RK + hp)),
            pl.BlockSpec((1, s, 2 * LANES), lambda bi, hp: (bi, 0, COL_RV // 2 + hp)),
            pl.BlockSpec((1, s, 2 * LANES), lambda bi, hp: (bi, 0, COL_RG // 2 + hp)),
            pos, pos,
            pl.BlockSpec((2, c, c), lambda bi, hp: (hp, 0, 0)),
            pl.BlockSpec((2, c, LANES), lambda bi, hp: (hp, 0, 0)),
            pl.BlockSpec((2, c, LANES), lambda bi, hp: (hp, 0, 0)),
            pl.BlockSpec((2, 1, LANES), lambda bi, hp: (hp, 0, 0)),
            pl.BlockSpec((1, 2 * LANES), lambda bi, hp: (0, hp)),
        ],
        out_specs=pl.BlockSpec((1, s, 2 * LANES), lambda bi, hp: (bi, 0, hp)),
        out_shape=jax.ShapeDtypeStruct((b, s, N_RET_HEADS * LANES), BF16),
        scratch_shapes=[pltpu.VMEM((2, LANES, LANES), F32)],
        compiler_params=_params(("arbitrary", "arbitrary")),
        name="retention",
    )(proj3, proj3, proj3, proj3, cos_t, sin_t, inner, cross, key, chunk, gn_g)


def _route(logits):
    lane = lax.broadcasted_iota(jnp.int32, logits.shape, 1).astype(F32)
    big = float(LANES)
    is_group = lane < N_GROUPS
    gl = jnp.where(is_group, logits, -jnp.inf)
    gmax = jnp.max(gl, axis=-1, keepdims=True)
    gidx = jnp.min(jnp.where(gl == gmax, lane, big), axis=-1, keepdims=True)
    p_group = 1.0 / jnp.sum(jnp.where(is_group, jnp.exp(gl - gmax), 0.0), axis=-1, keepdims=True)
    lo = N_GROUPS + EXPERTS_PER_GROUP * gidx
    il = jnp.where(lane >= lo, jnp.where(lane < lo + EXPERTS_PER_GROUP, logits, -jnp.inf), -jnp.inf)
    v1 = jnp.max(il, axis=-1, keepdims=True)
    i1 = jnp.min(jnp.where(il == v1, lane, big), axis=-1, keepdims=True)
    il2 = jnp.where(lane == i1, -jnp.inf, il)
    v2 = jnp.max(il2, axis=-1, keepdims=True)
    i2 = jnp.min(jnp.where(il2 == v2, lane, big), axis=-1, keepdims=True)
    e2 = jnp.exp(v2 - v1)
    inv = 1.0 / (1.0 + e2)
    g1 = p_group * inv
    g2 = p_group * (e2 * inv)
    out = jnp.where(lane == 0.0, i1 - N_GROUPS,
                    jnp.where(lane == 1.0, i2 - N_GROUPS,
                              jnp.where(lane == 2.0, g1, jnp.where(lane == 3.0, g2, 0.0))))
    return out


def _outproj_body(a_ref, r_ref, x_ref, w_ref, g_ref, wr_ref, br_ref, x1_ref, h2_ref, route_ref):
    half = a_ref.shape[1]
    acc = jnp.dot(a_ref[...], w_ref[:half, :], preferred_element_type=F32)
    acc = acc + jnp.dot(r_ref[...], w_ref[half:, :], preferred_element_type=F32)
    x1 = x_ref[...] + acc
    x1_ref[...] = x1
    ms = jnp.mean(x1 * x1, axis=-1, keepdims=True)
    h2 = (x1 * lax.rsqrt(ms + EPS)) * g_ref[...]
    h2_ref[...] = h2
    logits = jnp.dot(h2.astype(BF16), wr_ref[...], preferred_element_type=F32) + br_ref[...]
    route_ref[...] = _route(logits)


def _outproj_router(a2d, r2d, x2d, w_out_bf16, g, wr_bf16, br, *, tm):
    n, d = x2d.shape
    half = a2d.shape[1]
    row = lambda width: pl.BlockSpec((tm, width), lambda i: (i, 0))
    full = lambda shape: pl.BlockSpec(shape, lambda i: (0, 0))
    return pl.pallas_call(
        _outproj_body,
        grid=(n // tm,),
        in_specs=[row(half), row(half), row(d), full((2 * half, d)), full((1, d)), full((d, LANES)),
                  full((1, LANES))],
        out_specs=[row(d), row(d), row(LANES)],
        out_shape=[jax.ShapeDtypeStruct((n, d), F32), jax.ShapeDtypeStruct((n, d), F32),
                   jax.ShapeDtypeStruct((n, LANES), F32)],
        compiler_params=_params(("arbitrary",)),
        name="outproj_router",
    )(a2d, r2d, x2d, w_out_bf16, g, wr_bf16, br)


def _moe_plan(expert_id, tm):
    n_slots = expert_id.shape[0]
    n_tiles = n_slots // tm
    order = jnp.argsort(expert_id).astype(jnp.int32)
    sizes = jnp.zeros((N_EXPERTS,), jnp.int32).at[expert_id].add(1)
    ends = jnp.cumsum(sizes)
    starts = ends - sizes
    tile_row0 = jnp.arange(n_tiles, dtype=jnp.int32) * tm
    first_e = jnp.searchsorted(ends, tile_row0, side="right").astype(jnp.int32)
    last_e = jnp.searchsorted(ends, tile_row0 + (tm - 1), side="right").astype(jnp.int32)
    owners_upto = jnp.cumsum((sizes > 0).astype(jnp.int32))
    n_extra = (owners_upto[last_e] - owners_upto[first_e]).astype(jnp.int32)
    prev_last = jnp.concatenate([jnp.full((1,), -1, jnp.int32), last_e[:-1]])
    switch = (prev_last != first_e).astype(jnp.int32)
    eid = jnp.arange(N_EXPERTS, dtype=jnp.int32)
    cand = jnp.where((sizes[None, :] > 0) & (eid[None, :] > eid[:, None]), eid[None, :], N_EXPERTS)
    nxt = jnp.min(cand, axis=1)
    next_expert = jnp.where(nxt < N_EXPERTS, nxt, -1).astype(jnp.int32)
    spare = 2 * (n_slots + jnp.arange(tm, dtype=jnp.int32))
    codes = jnp.concatenate([order.reshape(n_tiles, tm), spare[None, :]], axis=0).reshape(n_tiles + 1, 1, tm)
    return dict(first_e=first_e, n_extra=n_extra, switch=switch, starts=starts.astype(jnp.int32),
                ends=ends.astype(jnp.int32), next_expert=next_expert, codes=codes)


def _moe_body(fe_ref, nextra_ref, sw_ref, st_ref, en_ref, nx_ref,
              codes_hbm, h2_hbm, wg_hbm, wu_hbm, wd_hbm, ys_hbm,
              gidx, sidx, xbuf, obuf, wg_st, wu_st, wd_st, wg_rs, wu_rs, wd_rs,
              gisem, sisem, gsem, ssem, wsem, *, n_tokens, n_tiles, cast_rows, ff_chunks):
    t = pl.program_id(0)
    tm = xbuf.shape[1]
    last = n_tiles - 1
    spare_codes = n_tiles

    def gidx_copy(tile, slot):
        return pltpu.make_async_copy(codes_hbm.at[tile], gidx.at[slot], gisem.at[slot])

    def sidx_copy(tile, slot):
        return pltpu.make_async_copy(codes_hbm.at[tile], sidx.at[slot], sisem.at[slot])

    def weight_copies(ex):
        return (pltpu.make_async_copy(wg_hbm.at[ex], wg_st, wsem.at[0]),
                pltpu.make_async_copy(wu_hbm.at[ex], wu_st, wsem.at[1]),
                pltpu.make_async_copy(wd_hbm.at[ex], wd_st, wsem.at[2]))

    def gather_row(slot, r):
        tok = lax.shift_right_logical(gidx[slot, 0, r], 1)
        return pltpu.make_async_copy(h2_hbm.at[pl.ds(tok, 1)], xbuf.at[slot, pl.ds(r, 1)], gsem.at[slot])

    def scatter_row(slot, r):
        code = sidx[slot, 0, r]
        dst = (code & 1) * n_tokens + lax.shift_right_logical(code, 1)
        return pltpu.make_async_copy(obuf.at[slot, pl.ds(r, 1)], ys_hbm.at[pl.ds(dst, 1)], ssem.at[slot])

    def wait_gather(slot):
        pltpu.make_async_copy(h2_hbm.at[pl.ds(0, tm)], xbuf.at[slot], gsem.at[slot]).wait()

    def wait_scatter(slot):
        pltpu.make_async_copy(obuf.at[slot], ys_hbm.at[pl.ds(0, tm)], ssem.at[slot]).wait()

    def for_rows(fn):
        def body(r, carry):
            fn(r)
            return carry
        lax.fori_loop(0, tm, body, 0, unroll=8)

    def take_over(ex, first):
        @pl.when(first)
        def _():
            for cp in weight_copies(ex):
                cp.start()

        for cp in weight_copies(ex):
            cp.wait()

        def cast_up(i, carry):
            r = pl.multiple_of(i * cast_rows, cast_rows)
            wg_rs[pl.ds(r, cast_rows), :] = wg_st[pl.ds(r, cast_rows), :].astype(BF16)
            wu_rs[pl.ds(r, cast_rows), :] = wu_st[pl.ds(r, cast_rows), :].astype(BF16)
            return carry

        def cast_down(i, carry):
            r = pl.multiple_of(i * cast_rows, cast_rows)
            wd_rs[pl.ds(r, cast_rows), :] = wd_st[pl.ds(r, cast_rows), :].astype(BF16)
            return carry

        lax.fori_loop(0, wg_st.shape[0] // cast_rows, cast_up, 0)
        lax.fori_loop(0, wd_st.shape[0] // cast_rows, cast_down, 0)
        nxt = nx_ref[ex]

        @pl.when(nxt >= 0)
        def _():
            for cp in weight_copies(nxt):
                cp.start()

    def ffn(slot, ex, accumulate, row_dmas=None):
        x = xbuf[slot].astype(BF16)
        row = t * tm + lax.broadcasted_iota(jnp.int32, (tm, 1), 0)
        mine = jnp.logical_and(row >= st_ref[ex], row < en_ref[ex])
        d_ff = wg_rs.shape[1]
        n_chunks = ff_chunks if row_dmas is not None else 1
        fc = d_ff // n_chunks
        rows = tm // n_chunks
        contrib = None
        for c in range(n_chunks):
            cols = slice(c * fc, (c + 1) * fc)
            a = jnp.dot(x, wg_rs[:, cols], preferred_element_type=F32)
            b = jnp.dot(x, wu_rs[:, cols], preferred_element_type=F32)
            hmid = jnp.where(mine, (a * (1.0 / (1.0 + jnp.exp(-a)))) * b, 0.0).astype(BF16)
            part = jnp.dot(hmid, wd_rs[cols, :], preferred_element_type=F32)
            contrib = part if contrib is None else contrib + part
            if row_dmas is not None:
                row_dmas(c * rows, (c + 1) * rows)
        if accumulate:
            obuf[slot] += contrib
        else:
            obuf[slot] = contrib

    e0 = fe_ref[t]

    def step(p):
        q = 1 - p
        if p == 0:
            @pl.when(t == 0)
            def _():
                copies = (gidx_copy(0, 0), gidx_copy(min(1, last), 1), sidx_copy(spare_codes, 1))
                for cp in copies:
                    cp.start()
                for cp in copies:
                    cp.wait()
                obuf[1] = jnp.zeros(obuf.shape[1:], F32)
                for_rows(lambda r: gather_row(0, r).start())

        gidx_copy(jnp.minimum(t + 2, last), p).start()
        sidx_copy(t, p).start()

        @pl.when(t >= 1)
        def _():
            gidx_copy(0, q).wait()
            sidx_copy(0, q).wait()
            wait_scatter(p)

        wait_gather(p)

        @pl.when(sw_ref[t] == 1)
        def _():
            take_over(e0, t == 0)

        def row_dmas(lo, hi):
            for r in range(lo, hi):
                scatter_row(q, r).start()
                gather_row(q, r).start()

        ffn(p, e0, accumulate=False, row_dmas=row_dmas)

    for parity in range(2):
        pl.when(lax.rem(t, 2) == parity)(functools.partial(step, parity))

    slot = lax.rem(t, 2)

    def further_owner(_, ex):
        nxt = nx_ref[ex]
        take_over(nxt, False)
        ffn(slot, nxt, accumulate=True)
        return nxt

    lax.fori_loop(0, nextra_ref[t], further_owner, e0)

    @pl.when(t == last)
    def _():
        gidx_copy(0, slot).wait()
        sidx_copy(0, slot).wait()
        for_rows(lambda r: scatter_row(slot, r).start())
        wait_gather(1 - slot)
        wait_scatter(1 - slot)
        wait_scatter(slot)


def _moe_ffn(h2, plan, w_gate, w_up, w_down, *, tm):
    n, d = h2.shape
    n_slots = 2 * n
    n_tiles = n_slots // tm
    assert n_tiles >= 2 and n_slots % tm == 0
    d_ff = w_gate.shape[2]
    any_spec = pl.BlockSpec(memory_space=pl.ANY)
    grid_spec = pltpu.PrefetchScalarGridSpec(
        num_scalar_prefetch=6,
        grid=(n_tiles,),
        in_specs=[any_spec] * 5,
        out_specs=any_spec,
        scratch_shapes=[
            pltpu.SMEM((2, 1, tm), jnp.int32),
            pltpu.SMEM((2, 1, tm), jnp.int32),
            pltpu.VMEM((2, tm, d), F32),
            pltpu.VMEM((2, tm, d), F32),
            pltpu.VMEM((d, d_ff), F32),
            pltpu.VMEM((d, d_ff), F32),
            pltpu.VMEM((d_ff, d), F32),
            pltpu.VMEM((d, d_ff), BF16),
            pltpu.VMEM((d, d_ff), BF16),
            pltpu.VMEM((d_ff, d), BF16),
            pltpu.SemaphoreType.DMA((2,)),
            pltpu.SemaphoreType.DMA((2,)),
            pltpu.SemaphoreType.DMA((2,)),
            pltpu.SemaphoreType.DMA((2,)),
            pltpu.SemaphoreType.DMA((3,)),
        ],
    )
    return pl.pallas_call(
        functools.partial(_moe_body, n_tokens=n, n_tiles=n_tiles, cast_rows=128,
                          ff_chunks=d_ff // (2 * LANES)),
        grid_spec=grid_spec,
        out_shape=jax.ShapeDtypeStruct((n_slots + tm, d), F32),
        compiler_params=_params(("arbitrary",), vmem=MOE_VMEM_LIMIT),
        name="moe_ffn",
    )(plan["first_e"], plan["n_extra"], plan["switch"], plan["starts"], plan["ends"], plan["next_expert"],
      plan["codes"], h2, w_gate, w_up, w_down)


def _final_body(x1_ref, ys0_ref, ys1_ref, route_ref, g_ref, o_ref):
    route = route_ref[...]
    y = ys0_ref[...] * route[:, 2:3] + ys1_ref[...] * route[:, 3:4]
    x2 = x1_ref[...] + y
    ms = jnp.mean(x2 * x2, axis=-1, keepdims=True)
    o_ref[...] = ((x2 * lax.rsqrt(ms + EPS)) * g_ref[...]).astype(o_ref.dtype)


def _final(x1, ys, route, g, *, tm):
    n, d = x1.shape
    nb = n // tm
    return pl.pallas_call(
        _final_body,
        grid=(nb,),
        in_specs=[
            pl.BlockSpec((tm, d), lambda i: (i, 0)),
            pl.BlockSpec((tm, d), lambda i: (i, 0)),
            pl.BlockSpec((tm, d), lambda i: (nb + i, 0)),
            pl.BlockSpec((tm, LANES), lambda i: (i, 0)),
            pl.BlockSpec((1, d), lambda i: (0, 0)),
        ],
        out_specs=pl.BlockSpec((tm, d), lambda i: (i, 0)),
        out_shape=jax.ShapeDtypeStruct((n, d), F32),
        compiler_params=_params(("arbitrary",)),
        name="final_norm",
    )(x1, ys, ys, route, g)


def _layer(x, rel_bias, attn_norm_g, w_in, lq1, lk1, lq2, lk2, diff_subln_g, ret_gn_g, w_out, ffn_norm_g,
           w_group_router, b_group, w_inner_router, b_inner, w_gate_exp, w_up_exp, w_down_exp, *, layer,
           attn_tile, row_tile, moe_tile):
    b, s, d = x.shape
    n = b * s
    x2d = x.reshape(n, d)
    lam_init = 0.8 - 0.6 * math.exp(-0.3 * layer)

    proj = _inproj(x2d, attn_norm_g.reshape(1, d), w_in.astype(BF16), tm=min(2 * row_tile, n), tn=8 * LANES)
    proj3 = proj.reshape(b, s, IN_COLS)

    bias = _bias_tiles(rel_bias, attn_tile)
    vec = lambda a: a.reshape(1, -1)
    a_out = _diff_attention(proj3, bias, vec(lq1), vec(lk1), vec(lq2), vec(lk2), vec(diff_subln_g),
                            t=attn_tile, lam_init=lam_init)
    r_out = _retention(proj3, _retention_tables(s), vec(ret_gn_g))

    pad = LANES - N_GROUPS - N_EXPERTS
    wr = jnp.concatenate([w_group_router, jnp.transpose(w_inner_router, (1, 0, 2)).reshape(d, N_EXPERTS),
                          jnp.zeros((d, pad), F32)], axis=1).astype(BF16)
    br = jnp.concatenate([b_group, b_inner.reshape(-1), jnp.zeros((pad,), F32)]).reshape(1, LANES)
    x1, h2, route = _outproj_router(a_out.reshape(n, -1), r_out.reshape(n, -1), x2d, w_out.astype(BF16),
                                    vec(ffn_norm_g), wr, br, tm=row_tile)

    expert_id = route[:, :2].astype(jnp.int32).reshape(-1)
    plan = _moe_plan(expert_id, moe_tile)
    ys = _moe_ffn(h2, plan, w_gate_exp, w_up_exp, w_down_exp, tm=moe_tile)
    return x1, ys, route


def kernel(x, rel_bias, attn_norm_g, w_in, lambda_q1, lambda_k1, lambda_q2, lambda_k2, diff_subln_g, ret_gn_g,
           w_out, ffn_norm_g, w_group_router, b_group, w_inner_router, b_inner, w_gate_exp, w_up_exp, w_down_exp,
           final_g):
    b, s, d = x.shape
    depth = w_in.shape[0]
    assert depth == 1, "the final norm is fused into the single layer's combine step"
    first = lambda a: a.reshape(a.shape[1:])
    x1, ys, route = _layer(x, rel_bias, attn_norm_g[0], w_in[0], lambda_q1[0], lambda_k1[0], lambda_q2[0],
                           lambda_k2[0], diff_subln_g[0], ret_gn_g[0], w_out[0], ffn_norm_g[0],
                           w_group_router[0], b_group[0], w_inner_router[0], b_inner[0], first(w_gate_exp),
                           first(w_up_exp), first(w_down_exp), layer=0, attn_tile=min(512, s), row_tile=512,
                           moe_tile=256)
    out = _final(x1, ys, route, final_g.reshape(1, d), tm=256)
    return out.reshape(b, s, d)
```

```python
import functools
import math

import numpy as np
import jax
import jax.numpy as jnp
from jax import lax
from jax.experimental import pallas as pl
from jax.experimental.pallas import tpu as pltpu

F32 = jnp.float32
BF16 = jnp.bfloat16

EPS = 1e-6
LANES = 128
N_DIFF_HEADS = 8
DIFF_HALF = 64
N_RET_HEADS = 8
RET_QK = 64
RET_CHUNK = 128
ROPE_BASE = 10000.0
N_BUCKETS = 32
MAX_DISTANCE = 128
N_GROUPS = 4
EXPERTS_PER_GROUP = 8
N_EXPERTS = N_GROUPS * EXPERTS_PER_GROUP
MASK_VALUE = -1e30
LOG2E = math.log2(math.e)
VMEM_LIMIT = 56 * 1024 * 1024
MOE_VMEM_LIMIT = 60 * 1024 * 1024

COL_DQ, COL_DK, COL_DV = 0, 8, 16
COL_RQ, COL_RK, COL_RV, COL_RG = 24, 28, 32, 40
IN_COLS = 48 * LANES


def _params(sem, vmem=VMEM_LIMIT):
    return pltpu.CompilerParams(dimension_semantics=sem, vmem_limit_bytes=vmem)


def _inproj_body(x_ref, g_ref, w_ref, cs_ref, o_ref, h_ref, *, row_chunk):
    tm = x_ref.shape[0]

    @pl.when(pl.program_id(1) == 0)
    def _():
        g = g_ref[...]

        def chunk(c, carry):
            r = pl.multiple_of(c * row_chunk, row_chunk)
            x = x_ref[pl.ds(r, row_chunk), :]
            ms = jnp.mean(x * x, axis=-1, keepdims=True)
            h_ref[pl.ds(r, row_chunk), :] = ((x * lax.rsqrt(ms + EPS)) * g).astype(BF16)
            return carry

        lax.fori_loop(0, tm // row_chunk, chunk, 0)

    acc = jnp.dot(h_ref[...], w_ref[...], preferred_element_type=F32)
    o_ref[...] = (acc * cs_ref[...]).astype(o_ref.dtype)


def _inproj(x2d, g, w_bf16, col_scale, *, tm, tn):
    n, d = x2d.shape
    cols = w_bf16.shape[1]
    return pl.pallas_call(
        functools.partial(_inproj_body, row_chunk=128),
        grid=(n // tm, cols // tn),
        in_specs=[
            pl.BlockSpec((tm, d), lambda i, j: (i, 0)),
            pl.BlockSpec((1, d), lambda i, j: (0, 0)),
            pl.BlockSpec((d, tn), lambda i, j: (0, j)),
            pl.BlockSpec((1, tn), lambda i, j: (0, j)),
        ],
        out_specs=pl.BlockSpec((tm, tn), lambda i, j: (i, j)),
        out_shape=jax.ShapeDtypeStruct((n, cols), BF16),
        scratch_shapes=[pltpu.VMEM((tm, d), BF16)],
        compiler_params=_params(("arbitrary", "arbitrary")),
        name="inproj",
    )(x2d, g, w_bf16, col_scale)


def _t5_bucket_table(n):
    d = np.arange(n)
    max_exact = N_BUCKETS // 2
    log_ratio = np.log(np.maximum(d, 1).astype(np.float64) / max_exact) / math.log(MAX_DISTANCE / max_exact)
    large = np.minimum(max_exact + (log_ratio * (N_BUCKETS - max_exact)).astype(np.int64), N_BUCKETS - 1)
    return np.where(d < max_exact, d, large).astype(np.int32)


def _bucket_tiles(t):
    table = _t5_bucket_table(2 * t)
    k = np.arange(t)[:, None]
    q = np.arange(t)[None, :]
    d0 = q - k
    tile0 = np.where(d0 >= 0, table[np.maximum(d0, 0)], -1)
    tile1 = table[t + q - k]
    return np.stack([tile0, tile1]).astype(np.int32)


def _bias_body(rb_ref, bkt_ref, o_ref):
    h = pl.program_id(0)
    bkt = bkt_ref[...]
    far = rb_ref[N_BUCKETS - 1, h]
    acc = jnp.zeros(bkt.shape, F32)
    for b in range(N_BUCKETS - 1):
        acc = jnp.where(bkt == b, (rb_ref[b, h] - far) * LOG2E, acc)
    o_ref[0] = jnp.where(bkt < 0, MASK_VALUE, acc)


def _bias_tiles(rel_bias, t):
    bkt = jnp.asarray(_bucket_tiles(t))
    return pl.pallas_call(
        _bias_body,
        grid=(N_DIFF_HEADS,),
        in_specs=[
            pl.BlockSpec(memory_space=pltpu.SMEM),
            pl.BlockSpec((2, t, t), lambda h: (0, 0, 0)),
        ],
        out_specs=pl.BlockSpec((1, 2, t, t), lambda h: (h, 0, 0, 0)),
        out_shape=jax.ShapeDtypeStruct((N_DIFF_HEADS, 2, t, t), F32),
        compiler_params=_params(("arbitrary",)),
        name="bias_tiles",
    )(rel_bias, bkt)


_NT = (((1,), (1,)), ((), ()))
_TN = (((0,), (0,)), ((), ()))


def _diffattn_body(q_ref, k_ref, v_ref, bias_ref, lq1_ref, lk1_ref, lq2_ref, lk2_ref, g_ref, o_ref, *, t, lam_init):
    nq = k_ref.shape[1] // t
    lane = lax.broadcasted_iota(jnp.int32, (t, LANES), 1)
    lam = (jnp.exp(jnp.sum(lq1_ref[...] * lk1_ref[...], axis=-1, keepdims=True))
           - jnp.exp(jnp.sum(lq2_ref[...] * lk2_ref[...], axis=-1, keepdims=True)) + lam_init)
    state = {}

    def scores(qi, j):
        if j == 0:
            q = q_ref[0, qi * t:(qi + 1) * t, :]
            zero = jnp.zeros_like(q)
            state[qi] = dict(qm=(jnp.where(lane < DIFF_HALF, q, zero), jnp.where(lane >= DIFF_HALF, q, zero)))
        kb = k_ref[0, j * t:(j + 1) * t, :]
        out = []
        for c in range(2):
            s = lax.dot_general(kb, state[qi]["qm"][c], _NT, preferred_element_type=F32)
            if j == qi:
                s = s + bias_ref[0, 0]
            elif j == qi - 1:
                s = s + bias_ref[0, 1]
            out.append(s)
        return out

    def softmax_step(qi, j, ss):
        st = state[qi]
        ps, alphas = [], []
        for c in range(2):
            s = ss[c]
            smax = jnp.max(s, axis=0, keepdims=True)
            if j == 0:
                m_new, alpha = smax, None
            else:
                m_new = jnp.maximum(st["m"][c], smax)
                alpha = jnp.exp2(st["m"][c] - m_new)
            p = jnp.exp2(s - m_new)
            psum = jnp.sum(p, axis=0, keepdims=True)
            st.setdefault("m", [None, None])[c] = m_new
            st.setdefault("l", [None, None])[c] = psum if j == 0 else alpha * st["l"][c] + psum
            ps.append(p.astype(BF16))
            alphas.append(alpha)
        return ps, alphas

    def values_step(qi, j, ps, alphas):
        st = state[qi]
        vb = v_ref[0, j * t:(j + 1) * t, :]
        for c in range(2):
            pv = lax.dot_general(vb, ps[c], _TN, preferred_element_type=F32)
            st.setdefault("acc", [None, None])[c] = pv if j == 0 else alphas[c] * st["acc"][c] + pv
        if j == qi:
            o = st["acc"][0] * (1.0 / st["l"][0]) - lam * (st["acc"][1] * (1.0 / st["l"][1]))
            ot = o.T
            ms = jnp.mean(ot * ot, axis=-1, keepdims=True)
            o_ref[0, qi * t:(qi + 1) * t, :] = (
                ((ot * lax.rsqrt(ms + EPS)) * g_ref[...]) * (1.0 - lam_init)).astype(o_ref.dtype)
            del state[qi]

    steps = [(qi, j) for qi in range(nq) for j in range(qi + 1)]
    ss = scores(*steps[0])
    for n, (qi, j) in enumerate(steps):
        nxt = scores(*steps[n + 1]) if n + 1 < len(steps) else None
        ps, alphas = softmax_step(qi, j, ss)
        values_step(qi, j, ps, alphas)
        ss = nxt


def _diff_attention(proj3, bias, lq1, lk1, lq2, lk2, subln_g, *, t, lam_init):
    b, s, _ = proj3.shape
    vec = pl.BlockSpec((1, DIFF_HALF), lambda bi, h: (0, 0))
    head = lambda col: pl.BlockSpec((1, s, LANES), lambda bi, h: (bi, 0, col + h))
    return pl.pallas_call(
        functools.partial(_diffattn_body, t=t, lam_init=lam_init),
        grid=(b, N_DIFF_HEADS),
        in_specs=[
            head(COL_DQ), head(COL_DK), head(COL_DV),
            pl.BlockSpec((1, 2, t, t), lambda bi, h: (h, 0, 0, 0)),
            vec, vec, vec, vec,
            pl.BlockSpec((1, LANES), lambda bi, h: (0, 0)),
        ],
        out_specs=pl.BlockSpec((1, s, LANES), lambda bi, h: (bi, 0, h)),
        out_shape=jax.ShapeDtypeStruct((b, s, N_DIFF_HEADS * LANES), BF16),
        compiler_params=_params(("arbitrary", "arbitrary")),
        name="diff_attention",
    )(proj3, proj3, proj3, bias, lq1, lk1, lq2, lk2, subln_g)


def _retention_tables(s):
    c = RET_CHUNK
    half = RET_QK // 2
    inv_freq = ROPE_BASE ** (-jnp.arange(0, RET_QK, 2, dtype=F32) / RET_QK)
    ang = jnp.arange(s, dtype=F32)[:, None] * inv_freq[None, :]
    cos, sin = jnp.cos(ang), jnp.sin(ang)
    cos_t = jnp.tile(cos, (1, LANES // half))
    sin_t = jnp.tile(jnp.concatenate([-sin, sin], axis=1), (1, LANES // RET_QK))
    log_g = jnp.log(1.0 - jnp.exp2(-5.0 - jnp.arange(N_RET_HEADS, dtype=F32)))
    n = jnp.arange(c, dtype=F32)
    diff = n[:, None] - n[None, :]
    inner = jnp.where(diff[None] >= 0, jnp.exp(jnp.maximum(diff, 0.0)[None] * log_g[:, None, None]), 0.0)
    cross = jnp.exp((n[None] + 1.0) * log_g[:, None])
    key = jnp.exp((c - 1.0 - n[None]) * log_g[:, None])
    chunk = jnp.exp(c * log_g)
    bc = lambda a: jnp.broadcast_to(a[..., None], a.shape + (LANES,)).astype(F32)
    return cos_t, sin_t, inner.astype(F32), bc(cross), bc(key), bc(chunk[:, None])


def _retention_body(rq_ref, rk_ref, rv_ref, rg_ref, cos_ref, sin_ref, inner_ref, cross_ref, key_ref,
                    chunk_ref, gn_ref, o_ref, state_ref):
    c = RET_CHUNK
    nc = rq_ref.shape[1] // c
    nh = N_RET_HEADS
    state_ref[...] = jnp.zeros(state_ref.shape, F32)
    lane = lax.broadcasted_iota(jnp.int32, (c, LANES), 1)
    first_half = (lane % RET_QK) < (RET_QK // 2)
    head_mask = [lane < RET_QK, lane >= RET_QK]
    slab = lambda i: slice(i * LANES, (i + 1) * LANES)

    def rotary(x, cos, sin):
        rot = jnp.where(first_half, pltpu.roll(x, LANES - RET_QK // 2, 1), pltpu.roll(x, RET_QK // 2, 1))
        return x * cos + rot * sin

    def chunk_step(ci, carry):
        r = pl.multiple_of(ci * c, c)
        rows = pl.ds(r, c)
        cos = cos_ref[rows, :]
        sin = sin_ref[rows, :]
        q_pair, k_pair, kb_pair = [], [], []
        for pr in range(nh // 2):
            q_pair.append(rotary(rq_ref[0, rows, slab(pr)].astype(F32), cos, sin))
            k = rotary(rk_ref[0, rows, slab(pr)].astype(F32), cos, sin) * (RET_QK ** -0.5)
            k_pair.append(k)
            kb_pair.append(k.astype(BF16))
        qh = [jnp.where(head_mask[h % 2], q_pair[h // 2], 0.0).astype(BF16) for h in range(nh)]
        kd = [(k_pair[h // 2] * key_ref[h]).astype(BF16) for h in range(nh)]
        vh = [rv_ref[0, rows, slab(h)] for h in range(nh)]
        scores = [lax.dot_general(qh[h], kb_pair[h // 2], _NT, preferred_element_type=F32) for h in range(nh)]
        cross = [jnp.dot(qh[h], state_ref[h].astype(BF16), preferred_element_type=F32) for h in range(nh)]
        kv = [lax.dot_general(kd[h], vh[h], _TN, preferred_element_type=F32) for h in range(nh)]
        decayed = [(scores[h] * inner_ref[h]).astype(BF16) for h in range(nh)]
        for h in range(nh):
            state_ref[h] = state_ref[h] * chunk_ref[h] + kv[h]
        inner = [jnp.dot(decayed[h], vh[h], preferred_element_type=F32) for h in range(nh)]
        for h in range(nh):
            y = inner[h] + cross[h] * cross_ref[h]
            mu = jnp.mean(y, axis=-1, keepdims=True)
            yc = y - mu
            var = jnp.mean(yc * yc, axis=-1, keepdims=True)
            yn = (yc * lax.rsqrt(var + EPS)) * gn_ref[:, slab(h)]
            gate = rg_ref[0, rows, slab(h)].astype(F32)
            silu = gate * (1.0 / (1.0 + jnp.exp(-gate)))
            o_ref[0, rows, slab(h)] = (silu * yn).astype(o_ref.dtype)
        return carry

    lax.fori_loop(0, nc, chunk_step, 0)


def _retention(proj3, tables, gn_g):
    b, s, _ = proj3.shape
    cos_t, sin_t, inner, cross, key, chunk = tables
    c = RET_CHUNK
    nh = N_RET_HEADS
    qk_w = nh * RET_QK
    v_w = nh * LANES
    full = lambda shape: pl.BlockSpec(shape, lambda bi: (0,) * len(shape))
    return pl.pallas_call(
        _retention_body,
        grid=(b,),
        in_specs=[
            pl.BlockSpec((1, s, qk_w), lambda bi: (bi, 0, COL_RQ * LANES // qk_w)),
            pl.BlockSpec((1, s, qk_w), lambda bi: (bi, 0, COL_RK * LANES // qk_w)),
            pl.BlockSpec((1, s, v_w), lambda bi: (bi, 0, COL_RV * LANES // v_w)),
            pl.BlockSpec((1, s, v_w), lambda bi: (bi, 0, COL_RG * LANES // v_w)),
            full((s, LANES)), full((s, LANES)),
            full((nh, c, c)), full((nh, c, LANES)), full((nh, c, LANES)), full((nh, 1, LANES)),
            full((1, v_w)),
        ],
        out_specs=pl.BlockSpec((1, s, v_w), lambda bi: (bi, 0, 0)),
        out_shape=jax.ShapeDtypeStruct((b, s, v_w), BF16),
        scratch_shapes=[pltpu.VMEM((nh, LANES, LANES), F32)],
        compiler_params=_params(("arbitrary",)),
        name="retention",
    )(proj3, proj3, proj3, proj3, cos_t, sin_t, inner, cross, key, chunk, gn_g)


def _route(logits):
    lane = lax.broadcasted_iota(jnp.int32, logits.shape, 1).astype(F32)
    big = float(LANES)
    is_group = lane < N_GROUPS
    gl = jnp.where(is_group, logits, -jnp.inf)
    gmax = jnp.max(gl, axis=-1, keepdims=True)
    gidx = jnp.min(jnp.where(gl == gmax, lane, big), axis=-1, keepdims=True)
    p_group = 1.0 / jnp.sum(jnp.where(is_group, jnp.exp(gl - gmax), 0.0), axis=-1, keepdims=True)
    lo = N_GROUPS + EXPERTS_PER_GROUP * gidx
    il = jnp.where(lane >= lo, jnp.where(lane < lo + EXPERTS_PER_GROUP, logits, -jnp.inf), -jnp.inf)
    v1 = jnp.max(il, axis=-1, keepdims=True)
    i1 = jnp.min(jnp.where(il == v1, lane, big), axis=-1, keepdims=True)
    il2 = jnp.where(lane == i1, -jnp.inf, il)
    v2 = jnp.max(il2, axis=-1, keepdims=True)
    i2 = jnp.min(jnp.where(il2 == v2, lane, big), axis=-1, keepdims=True)
    e2 = jnp.exp(v2 - v1)
    inv = 1.0 / (1.0 + e2)
    g1 = p_group * inv
    g2 = p_group * (e2 * inv)
    out = jnp.where(lane == 0.0, i1 - N_GROUPS,
                    jnp.where(lane == 1.0, i2 - N_GROUPS,
                              jnp.where(lane == 2.0, g1, jnp.where(lane == 3.0, g2, 0.0))))
    return out


def _outproj_body(a_ref, r_ref, x_ref, w_ref, g_ref, wr_ref, br_ref, x1_ref, h2_ref, route_ref):
    half = a_ref.shape[1]
    acc = jnp.dot(a_ref[...], w_ref[:half, :], preferred_element_type=F32)
    acc = acc + jnp.dot(r_ref[...], w_ref[half:, :], preferred_element_type=F32)
    x1 = x_ref[...] + acc
    x1_ref[...] = x1
    ms = jnp.mean(x1 * x1, axis=-1, keepdims=True)
    h2 = (x1 * lax.rsqrt(ms + EPS)) * g_ref[...]
    h2_ref[...] = h2
    logits = jnp.dot(h2.astype(BF16), wr_ref[...], preferred_element_type=F32) + br_ref[...]
    route_ref[...] = _route(logits)


def _outproj_router(a2d, r2d, x2d, w_out_bf16, g, wr_bf16, br, *, tm):
    n, d = x2d.shape
    half = a2d.shape[1]
    row = lambda width: pl.BlockSpec((tm, width), lambda i: (i, 0))
    full = lambda shape: pl.BlockSpec(shape, lambda i: (0, 0))
    return pl.pallas_call(
        _outproj_body,
        grid=(n // tm,),
        in_specs=[row(half), row(half), row(d), full((2 * half, d)), full((1, d)), full((d, LANES)),
                  full((1, LANES))],
        out_specs=[row(d), row(d), row(LANES)],
        out_shape=[jax.ShapeDtypeStruct((n, d), F32), jax.ShapeDtypeStruct((n, d), F32),
                   jax.ShapeDtypeStruct((n, LANES), F32)],
        compiler_params=_params(("arbitrary",)),
        name="outproj_router",
    )(a2d, r2d, x2d, w_out_bf16, g, wr_bf16, br)


def _moe_plan(expert_id, tm):
    n_slots = expert_id.shape[0]
    n_tiles = n_slots // tm
    order = jnp.argsort(expert_id).astype(jnp.int32)
    sizes = jnp.zeros((N_EXPERTS,), jnp.int32).at[expert_id].add(1)
    ends = jnp.cumsum(sizes)
    starts = ends - sizes
    tile_row0 = jnp.arange(n_tiles, dtype=jnp.int32) * tm
    first_e = jnp.searchsorted(ends, tile_row0, side="right").astype(jnp.int32)
    last_e = jnp.searchsorted(ends, tile_row0 + (tm - 1), side="right").astype(jnp.int32)
    owners_upto = jnp.cumsum((sizes > 0).astype(jnp.int32))
    n_extra = (owners_upto[last_e] - owners_upto[first_e]).astype(jnp.int32)
    prev_last = jnp.concatenate([jnp.full((1,), -1, jnp.int32), last_e[:-1]])
    switch = (prev_last != first_e).astype(jnp.int32)
    eid = jnp.arange(N_EXPERTS, dtype=jnp.int32)
    cand = jnp.where((sizes[None, :] > 0) & (eid[None, :] > eid[:, None]), eid[None, :], N_EXPERTS)
    nxt = jnp.min(cand, axis=1)
    next_expert = jnp.where(nxt < N_EXPERTS, nxt, -1).astype(jnp.int32)
    spare = 2 * (n_slots + jnp.arange(tm, dtype=jnp.int32))
    codes = jnp.concatenate([order.reshape(n_tiles, tm), spare[None, :]], axis=0).reshape(n_tiles + 1, 1, tm)
    return dict(first_e=first_e, n_extra=n_extra, switch=switch, starts=starts.astype(jnp.int32),
                ends=ends.astype(jnp.int32), next_expert=next_expert, codes=codes)


def _moe_body(fe_ref, nextra_ref, sw_ref, st_ref, en_ref, nx_ref,
              codes_hbm, h2_hbm, wg_hbm, wu_hbm, wd_hbm, ys_hbm,
              gidx, sidx, xbuf, obuf, wg_st, wu_st, wd_st, wg_rs, wu_rs, wd_rs,
              gisem, sisem, gsem, ssem, wsem, *, n_tokens, n_tiles, cast_rows, ff_chunks):
    t = pl.program_id(0)
    tm = xbuf.shape[1]
    last = n_tiles - 1
    spare_codes = n_tiles

    def gidx_copy(tile, slot):
        return pltpu.make_async_copy(codes_hbm.at[tile], gidx.at[slot], gisem.at[slot])

    def sidx_copy(tile, slot):
        return pltpu.make_async_copy(codes_hbm.at[tile], sidx.at[slot], sisem.at[slot])

    def weight_copies(ex):
        return (pltpu.make_async_copy(wg_hbm.at[ex], wg_st, wsem.at[0]),
                pltpu.make_async_copy(wu_hbm.at[ex], wu_st, wsem.at[1]),
                pltpu.make_async_copy(wd_hbm.at[ex], wd_st, wsem.at[2]))

    def gather_row(slot, r):
        tok = lax.shift_right_logical(gidx[slot, 0, r], 1)
        return pltpu.make_async_copy(h2_hbm.at[pl.ds(tok, 1)], xbuf.at[slot, pl.ds(r, 1)], gsem.at[slot])

    def scatter_row(slot, r):
        code = sidx[slot, 0, r]
        dst = (code & 1) * n_tokens + lax.shift_right_logical(code, 1)
        return pltpu.make_async_copy(obuf.at[slot, pl.ds(r, 1)], ys_hbm.at[pl.ds(dst, 1)], ssem.at[slot])

    def wait_gather(slot):
        pltpu.make_async_copy(h2_hbm.at[pl.ds(0, tm)], xbuf.at[slot], gsem.at[slot]).wait()

    def wait_scatter(slot):
        pltpu.make_async_copy(obuf.at[slot], ys_hbm.at[pl.ds(0, tm)], ssem.at[slot]).wait()

    def for_rows(fn):
        def body(r, carry):
            fn(r)
            return carry
        lax.fori_loop(0, tm, body, 0, unroll=8)

    def take_over(ex, first):
        @pl.when(first)
        def _():
            for cp in weight_copies(ex):
                cp.start()

        for cp in weight_copies(ex):
            cp.wait()

        def cast_up(i, carry):
            r = pl.multiple_of(i * cast_rows, cast_rows)
            wg_rs[pl.ds(r, cast_rows), :] = wg_st[pl.ds(r, cast_rows), :].astype(BF16)
            wu_rs[pl.ds(r, cast_rows), :] = wu_st[pl.ds(r, cast_rows), :].astype(BF16)
            return carry

        def cast_down(i, carry):
            r = pl.multiple_of(i * cast_rows, cast_rows)
            wd_rs[pl.ds(r, cast_rows), :] = wd_st[pl.ds(r, cast_rows), :].astype(BF16)
            return carry

        lax.fori_loop(0, wg_st.shape[0] // cast_rows, cast_up, 0)
        lax.fori_loop(0, wd_st.shape[0] // cast_rows, cast_down, 0)
        nxt = nx_ref[ex]

        @pl.when(nxt >= 0)
        def _():
            for cp in weight_copies(nxt):
                cp.start()

    def ffn(slot, ex, accumulate, row_dmas=None):
        x = xbuf[slot].astype(BF16)
        row = t * tm + lax.broadcasted_iota(jnp.int32, (tm, 1), 0)
        mine = jnp.logical_and(row >= st_ref[ex], row < en_ref[ex])
        d_ff = wg_rs.shape[1]
        n_chunks = ff_chunks if row_dmas is not None else 1
        fc = d_ff // n_chunks
        rows = tm // n_chunks
        contrib = None
        for c in range(n_chunks):
            cols = slice(c * fc, (c + 1) * fc)
            a = jnp.dot(x, wg_rs[:, cols], preferred_element_type=F32)
            b = jnp.dot(x, wu_rs[:, cols], preferred_element_type=F32)
            hmid = jnp.where(mine, (a * (1.0 / (1.0 + jnp.exp(-a)))) * b, 0.0).astype(BF16)
            part = jnp.dot(hmid, wd_rs[cols, :], preferred_element_type=F32)
            contrib = part if contrib is None else contrib + part
            if row_dmas is not None:
                row_dmas(c * rows, (c + 1) * rows)
        if accumulate:
            obuf[slot] += contrib
        else:
            obuf[slot] = contrib

    e0 = fe_ref[t]

    def step(p):
        q = 1 - p
        if p == 0:
            @pl.when(t == 0)
            def _():
                copies = (gidx_copy(0, 0), gidx_copy(min(1, last), 1), sidx_copy(spare_codes, 1))
                for cp in copies:
                    cp.start()
                for cp in copies:
                    cp.wait()
                obuf[1] = jnp.zeros(obuf.shape[1:], F32)
                for_rows(lambda r: gather_row(0, r).start())

        gidx_copy(jnp.minimum(t + 2, last), p).start()
        sidx_copy(t, p).start()

        @pl.when(t >= 1)
        def _():
            gidx_copy(0, q).wait()
            sidx_copy(0, q).wait()
            wait_scatter(p)

        wait_gather(p)

        @pl.when(sw_ref[t] == 1)
        def _():
            take_over(e0, t == 0)

        def row_dmas(lo, hi):
            for r in range(lo, hi):
                scatter_row(q, r).start()
                gather_row(q, r).start()

        ffn(p, e0, accumulate=False, row_dmas=row_dmas)

    for parity in range(2):
        pl.when(lax.rem(t, 2) == parity)(functools.partial(step, parity))

    slot = lax.rem(t, 2)

    def further_owner(_, ex):
        nxt = nx_ref[ex]
        take_over(nxt, False)
        ffn(slot, nxt, accumulate=True)
        return nxt

    lax.fori_loop(0, nextra_ref[t], further_owner, e0)

    @pl.when(t == last)
    def _():
        gidx_copy(0, slot).wait()
        sidx_copy(0, slot).wait()
        for_rows(lambda r: scatter_row(slot, r).start())
        wait_gather(1 - slot)
        wait_scatter(1 - slot)
        wait_scatter(slot)


def _moe_ffn(h2, plan, w_gate, w_up, w_down, *, tm):
    n, d = h2.shape
    n_slots = 2 * n
    n_tiles = n_slots // tm
    assert n_tiles >= 2 and n_slots % tm == 0
    d_ff = w_gate.shape[2]
    any_spec = pl.BlockSpec(memory_space=pl.ANY)
    grid_spec = pltpu.PrefetchScalarGridSpec(
        num_scalar_prefetch=6,
        grid=(n_tiles,),
        in_specs=[any_spec] * 5,
        out_specs=any_spec,
        scratch_shapes=[
            pltpu.SMEM((2, 1, tm), jnp.int32),
            pltpu.SMEM((2, 1, tm), jnp.int32),
            pltpu.VMEM((2, tm, d), F32),
            pltpu.VMEM((2, tm, d), F32),
            pltpu.VMEM((d, d_ff), F32),
            pltpu.VMEM((d, d_ff), F32),
            pltpu.VMEM((d_ff, d), F32),
            pltpu.VMEM((d, d_ff), BF16),
            pltpu.VMEM((d, d_ff), BF16),
            pltpu.VMEM((d_ff, d), BF16),
            pltpu.SemaphoreType.DMA((2,)),
            pltpu.SemaphoreType.DMA((2,)),
            pltpu.SemaphoreType.DMA((2,)),
            pltpu.SemaphoreType.DMA((2,)),
            pltpu.SemaphoreType.DMA((3,)),
        ],
    )
    return pl.pallas_call(
        functools.partial(_moe_body, n_tokens=n, n_tiles=n_tiles, cast_rows=128,
                          ff_chunks=d_ff // (2 * LANES)),
        grid_spec=grid_spec,
        out_shape=jax.ShapeDtypeStruct((n_slots + tm, d), F32),
        compiler_params=_params(("arbitrary",), vmem=MOE_VMEM_LIMIT),
        name="moe_ffn",
    )(plan["first_e"], plan["n_extra"], plan["switch"], plan["starts"], plan["ends"], plan["next_expert"],
      plan["codes"], h2, w_gate, w_up, w_down)


def _final_body(x1_ref, ys0_ref, ys1_ref, route_ref, g_ref, o_ref):
    route = route_ref[...]
    y = ys0_ref[...] * route[:, 2:3] + ys1_ref[...] * route[:, 3:4]
    x2 = x1_ref[...] + y
    ms = jnp.mean(x2 * x2, axis=-1, keepdims=True)
    o_ref[...] = ((x2 * lax.rsqrt(ms + EPS)) * g_ref[...]).astype(o_ref.dtype)


def _final(x1, ys, route, g, *, tm):
    n, d = x1.shape
    nb = n // tm
    return pl.pallas_call(
        _final_body,
        grid=(nb,),
        in_specs=[
            pl.BlockSpec((tm, d), lambda i: (i, 0)),
            pl.BlockSpec((tm, d), lambda i: (i, 0)),
            pl.BlockSpec((tm, d), lambda i: (nb + i, 0)),
            pl.BlockSpec((tm, LANES), lambda i: (i, 0)),
            pl.BlockSpec((1, d), lambda i: (0, 0)),
        ],
        out_specs=pl.BlockSpec((tm, d), lambda i: (i, 0)),
        out_shape=jax.ShapeDtypeStruct((n, d), F32),
        compiler_params=_params(("arbitrary",)),
        name="final_norm",
    )(x1, ys, ys, route, g)


def _layer(x, rel_bias, attn_norm_g, w_in, lq1, lk1, lq2, lk2, diff_subln_g, ret_gn_g, w_out, ffn_norm_g,
           w_group_router, b_group, w_inner_router, b_inner, w_gate_exp, w_up_exp, w_down_exp, *, layer,
           attn_tile, row_tile, moe_tile):
    b, s, d = x.shape
    n = b * s
    x2d = x.reshape(n, d)
    lam_init = 0.8 - 0.6 * math.exp(-0.3 * layer)

    dq_cols = (COL_DK - COL_DQ) * LANES
    col_scale = jnp.concatenate([jnp.full((1, dq_cols), DIFF_HALF ** -0.5 * LOG2E, F32),
                                 jnp.ones((1, IN_COLS - dq_cols), F32)], axis=1)
    proj = _inproj(x2d, attn_norm_g.reshape(1, d), w_in.astype(BF16), col_scale, tm=min(2 * row_tile, n),
                   tn=8 * LANES)
    proj3 = proj.reshape(b, s, IN_COLS)

    bias = _bias_tiles(rel_bias, attn_tile)
    vec = lambda a: a.reshape(1, -1)
    a_out = _diff_attention(proj3, bias, vec(lq1), vec(lk1), vec(lq2), vec(lk2), vec(diff_subln_g),
                            t=attn_tile, lam_init=lam_init)
    r_out = _retention(proj3, _retention_tables(s), vec(ret_gn_g))

    pad = LANES - N_GROUPS - N_EXPERTS
    wr = jnp.concatenate([w_group_router, jnp.transpose(w_inner_router, (1, 0, 2)).reshape(d, N_EXPERTS),
                          jnp.zeros((d, pad), F32)], axis=1).astype(BF16)
    br = jnp.concatenate([b_group, b_inner.reshape(-1), jnp.zeros((pad,), F32)]).reshape(1, LANES)
    x1, h2, route = _outproj_router(a_out.reshape(n, -1), r_out.reshape(n, -1), x2d, w_out.astype(BF16),
                                    vec(ffn_norm_g), wr, br, tm=row_tile)

    expert_id = route[:, :2].astype(jnp.int32).reshape(-1)
    plan = _moe_plan(expert_id, moe_tile)
    ys = _moe_ffn(h2, plan, w_gate_exp, w_up_exp, w_down_exp, tm=moe_tile)
    return x1, ys, route


def kernel(x, rel_bias, attn_norm_g, w_in, lambda_q1, lambda_k1, lambda_q2, lambda_k2, diff_subln_g, ret_gn_g,
           w_out, ffn_norm_g, w_group_router, b_group, w_inner_router, b_inner, w_gate_exp, w_up_exp, w_down_exp,
           final_g):
    b, s, d = x.shape
    depth = w_in.shape[0]
    assert depth == 1, "the final norm is fused into the single layer's combine step"
    first = lambda a: a.reshape(a.shape[1:])
    x1, ys, route = _layer(x, rel_bias, attn_norm_g[0], w_in[0], lambda_q1[0], lambda_k1[0], lambda_q2[0],
                           lambda_k2[0], diff_subln_g[0], ret_gn_g[0], w_out[0], ffn_norm_g[0],
                           w_group_router[0], b_group[0], w_inner_router[0], b_inner[0], first(w_gate_exp),
                           first(w_up_exp), first(w_down_exp), layer=0, attn_tile=min(512, s), row_tile=512,
                           moe_tile=256)
    out = _final(x1, ys, route, final_g.reshape(1, d), tm=256)
    return out.reshape(b, s, d)
```

```python
import functools
import math

import numpy as np
import jax
import jax.numpy as jnp
from jax import lax
from jax.experimental import pallas as pl
from jax.experimental.pallas import tpu as pltpu

F32 = jnp.float32
BF16 = jnp.bfloat16

EPS = 1e-6
LANES = 128
N_DIFF_HEADS = 8
DIFF_HALF = 64
N_RET_HEADS = 8
RET_QK = 64
RET_CHUNK = 128
ROPE_BASE = 10000.0
N_BUCKETS = 32
MAX_DISTANCE = 128
N_GROUPS = 4
EXPERTS_PER_GROUP = 8
N_EXPERTS = N_GROUPS * EXPERTS_PER_GROUP
MASK_VALUE = -1e30
LOG2E = math.log2(math.e)
VMEM_LIMIT = 56 * 1024 * 1024
MOE_VMEM_LIMIT = 60 * 1024 * 1024

COL_DQ, COL_DK, COL_DV = 0, 8, 16
COL_RQ, COL_RK, COL_RV, COL_RG = 24, 28, 32, 40
IN_COLS = 48 * LANES


def _params(sem, vmem=VMEM_LIMIT):
    return pltpu.CompilerParams(dimension_semantics=sem, vmem_limit_bytes=vmem)


def _pack_halves(x):
    hd = x.shape[1] // 2
    bits = lambda h: lax.bitcast_convert_type(h.astype(BF16).astype(F32), jnp.uint32)
    return lax.shift_right_logical(bits(x[:, :hd]), jnp.uint32(16)) | bits(x[:, hd:])


def _unpack_halves(w):
    return (lax.bitcast_convert_type(lax.shift_left(w, jnp.uint32(16)), F32),
            lax.bitcast_convert_type(w & jnp.uint32(0xFFFF0000), F32))


def _inproj_body(x_ref, g_ref, w_ref, cs_ref, o_ref, h_ref, *, row_chunk):
    tm = x_ref.shape[0]

    @pl.when(pl.program_id(1) == 0)
    def _():
        g = g_ref[...]

        def chunk(c, carry):
            r = pl.multiple_of(c * row_chunk, row_chunk)
            x = x_ref[pl.ds(r, row_chunk), :]
            ms = jnp.mean(x * x, axis=-1, keepdims=True)
            h_ref[pl.ds(r, row_chunk), :] = ((x * lax.rsqrt(ms + EPS)) * g).astype(BF16)
            return carry

        lax.fori_loop(0, tm // row_chunk, chunk, 0)

    acc = jnp.dot(h_ref[...], w_ref[...].astype(BF16), preferred_element_type=F32)
    o_ref[...] = (acc * cs_ref[...]).astype(o_ref.dtype)


def _inproj(x2d, g, w, col_scale, *, tm, tn):
    n, d = x2d.shape
    cols = w.shape[1]
    return pl.pallas_call(
        functools.partial(_inproj_body, row_chunk=128),
        grid=(n // tm, cols // tn),
        in_specs=[
            pl.BlockSpec((tm, d), lambda i, j: (i, 0)),
            pl.BlockSpec((1, d), lambda i, j: (0, 0)),
            pl.BlockSpec((d, tn), lambda i, j: (0, j)),
            pl.BlockSpec((1, tn), lambda i, j: (0, j)),
        ],
        out_specs=pl.BlockSpec((tm, tn), lambda i, j: (i, j)),
        out_shape=jax.ShapeDtypeStruct((n, cols), BF16),
        scratch_shapes=[pltpu.VMEM((tm, d), BF16)],
        compiler_params=_params(("arbitrary", "arbitrary")),
        name="inproj",
    )(x2d, g, w, col_scale)


def _t5_bucket_table(n):
    d = np.arange(n)
    max_exact = N_BUCKETS // 2
    log_ratio = np.log(np.maximum(d, 1).astype(np.float64) / max_exact) / math.log(MAX_DISTANCE / max_exact)
    large = np.minimum(max_exact + (log_ratio * (N_BUCKETS - max_exact)).astype(np.int64), N_BUCKETS - 1)
    return np.where(d < max_exact, d, large).astype(np.int32)


def _bucket_tiles(t):
    table = _t5_bucket_table(2 * t)
    k = np.arange(t)[:, None]
    q = np.arange(t)[None, :]
    d0 = q - k
    tile0 = np.where(d0 >= 0, table[np.maximum(d0, 0)], -1)
    tile1 = table[t + q - k]
    return np.stack([tile0, tile1]).astype(np.int32)


def _bias_body(rb_ref, bkt_ref, o_ref):
    h = pl.program_id(0)
    bkt = bkt_ref[...]
    far = rb_ref[N_BUCKETS - 1, h]
    acc = jnp.zeros(bkt.shape, F32)
    for b in range(N_BUCKETS - 1):
        acc = jnp.where(bkt == b, (rb_ref[b, h] - far) * LOG2E, acc)
    o_ref[0] = jnp.where(bkt < 0, MASK_VALUE, acc)


def _bias_tiles(rel_bias, t):
    bkt = jnp.asarray(_bucket_tiles(t))
    return pl.pallas_call(
        _bias_body,
        grid=(N_DIFF_HEADS,),
        in_specs=[
            pl.BlockSpec(memory_space=pltpu.SMEM),
            pl.BlockSpec((2, t, t), lambda h: (0, 0, 0)),
        ],
        out_specs=pl.BlockSpec((1, 2, t, t), lambda h: (h, 0, 0, 0)),
        out_shape=jax.ShapeDtypeStruct((N_DIFF_HEADS, 2, t, t), F32),
        compiler_params=_params(("arbitrary",)),
        name="bias_tiles",
    )(rel_bias, bkt)


_NT = (((1,), (1,)), ((), ()))
_TN = (((0,), (0,)), ((), ()))


def _diffattn_body(q_ref, k_ref, v_ref, bias_ref, lq1_ref, lk1_ref, lq2_ref, lk2_ref, g_ref, o_ref, *, t, lam_init):
    nq = k_ref.shape[1] // t
    lane = lax.broadcasted_iota(jnp.int32, (t, LANES), 1)
    lam = (jnp.exp(jnp.sum(lq1_ref[...] * lk1_ref[...], axis=-1, keepdims=True))
           - jnp.exp(jnp.sum(lq2_ref[...] * lk2_ref[...], axis=-1, keepdims=True)) + lam_init)
    state = {}

    def scores(qi, j):
        if j == 0:
            q = q_ref[0, qi * t:(qi + 1) * t, :]
            zero = jnp.zeros_like(q)
            state[qi] = dict(qm=(jnp.where(lane < DIFF_HALF, q, zero), jnp.where(lane >= DIFF_HALF, q, zero)))
        kb = k_ref[0, j * t:(j + 1) * t, :]
        out = []
        for c in range(2):
            s = lax.dot_general(kb, state[qi]["qm"][c], _NT, preferred_element_type=F32)
            if j == qi:
                s = s + bias_ref[0, 0]
            elif j == qi - 1:
                s = s + bias_ref[0, 1]
            out.append(s)
        return out

    def softmax_step(qi, j, ss):
        st = state[qi]
        ps, alphas = [], []
        for c in range(2):
            s = ss[c]
            smax = jnp.max(s, axis=0, keepdims=True)
            if j == 0:
                m_new, alpha = smax, None
            else:
                m_new = jnp.maximum(st["m"][c], smax)
                alpha = jnp.exp2(st["m"][c] - m_new)
            p = jnp.exp2(s - m_new)
            psum = jnp.sum(p, axis=0, keepdims=True)
            st.setdefault("m", [None, None])[c] = m_new
            st.setdefault("l", [None, None])[c] = psum if j == 0 else alpha * st["l"][c] + psum
            ps.append(p.astype(BF16))
            alphas.append(alpha)
        return ps, alphas

    def values_step(qi, j, ps, alphas):
        st = state[qi]
        vb = v_ref[0, j * t:(j + 1) * t, :]
        for c in range(2):
            pv = lax.dot_general(vb, ps[c], _TN, preferred_element_type=F32)
            st.setdefault("acc", [None, None])[c] = pv if j == 0 else alphas[c] * st["acc"][c] + pv
        if j == qi:
            o = st["acc"][0] * (1.0 / st["l"][0]) - lam * (st["acc"][1] * (1.0 / st["l"][1]))
            ot = o.T
            ms = jnp.mean(ot * ot, axis=-1, keepdims=True)
            o_ref[0, qi * t:(qi + 1) * t, :] = (
                ((ot * lax.rsqrt(ms + EPS)) * g_ref[...]) * (1.0 - lam_init)).astype(o_ref.dtype)
            del state[qi]

    steps = [(qi, j) for qi in range(nq) for j in range(qi + 1)]
    ss = scores(*steps[0])
    for n, (qi, j) in enumerate(steps):
        nxt = scores(*steps[n + 1]) if n + 1 < len(steps) else None
        ps, alphas = softmax_step(qi, j, ss)
        values_step(qi, j, ps, alphas)
        ss = nxt


def _diff_attention(proj3, bias, lq1, lk1, lq2, lk2, subln_g, *, t, lam_init):
    b, s, _ = proj3.shape
    vec = pl.BlockSpec((1, DIFF_HALF), lambda bi, h: (0, 0))
    head = lambda col: pl.BlockSpec((1, s, LANES), lambda bi, h: (bi, 0, col + h))
    return pl.pallas_call(
        functools.partial(_diffattn_body, t=t, lam_init=lam_init),
        grid=(b, N_DIFF_HEADS),
        in_specs=[
            head(COL_DQ), head(COL_DK), head(COL_DV),
            pl.BlockSpec((1, 2, t, t), lambda bi, h: (h, 0, 0, 0)),
            vec, vec, vec, vec,
            pl.BlockSpec((1, LANES), lambda bi, h: (0, 0)),
        ],
        out_specs=pl.BlockSpec((1, s, LANES), lambda bi, h: (bi, 0, h)),
        out_shape=jax.ShapeDtypeStruct((b, s, N_DIFF_HEADS * LANES), BF16),
        compiler_params=_params(("arbitrary", "arbitrary")),
        name="diff_attention",
    )(proj3, proj3, proj3, bias, lq1, lk1, lq2, lk2, subln_g)


def _retention_tables(s):
    c = RET_CHUNK
    half = RET_QK // 2
    inv_freq = ROPE_BASE ** (-jnp.arange(0, RET_QK, 2, dtype=F32) / RET_QK)
    ang = jnp.arange(s, dtype=F32)[:, None] * inv_freq[None, :]
    cos, sin = jnp.cos(ang), jnp.sin(ang)
    cos_t = jnp.tile(cos, (1, LANES // half))
    sin_t = jnp.tile(jnp.concatenate([-sin, sin], axis=1), (1, LANES // RET_QK))
    log_g = jnp.log(1.0 - jnp.exp2(-5.0 - jnp.arange(N_RET_HEADS, dtype=F32)))
    n = jnp.arange(c, dtype=F32)
    diff = n[:, None] - n[None, :]
    inner = jnp.where(diff[None] >= 0, jnp.exp(jnp.maximum(diff, 0.0)[None] * log_g[:, None, None]), 0.0)
    cross = jnp.exp((n[None] + 1.0) * log_g[:, None])
    key = jnp.exp((c - 1.0 - n[None]) * log_g[:, None])
    chunk = jnp.exp(c * log_g)
    bc = lambda a: jnp.broadcast_to(a[..., None], a.shape + (LANES,)).astype(F32)
    return cos_t, sin_t, inner.astype(F32), bc(cross), bc(key), bc(chunk[:, None])


def _retention_body(rq_ref, rk_ref, rv_ref, rg_ref, cos_ref, sin_ref, inner_ref, cross_ref, key_ref,
                    chunk_ref, gn_ref, o_ref, state_ref):
    c = RET_CHUNK
    nc = rq_ref.shape[1] // c
    nh = N_RET_HEADS
    state_ref[...] = jnp.zeros(state_ref.shape, F32)
    lane = lax.broadcasted_iota(jnp.int32, (c, LANES), 1)
    first_half = (lane % RET_QK) < (RET_QK // 2)
    head_mask = [lane < RET_QK, lane >= RET_QK]
    slab = lambda i: slice(i * LANES, (i + 1) * LANES)

    def rotary(x, cos, sin):
        rot = jnp.where(first_half, pltpu.roll(x, LANES - RET_QK // 2, 1), pltpu.roll(x, RET_QK // 2, 1))
        return x * cos + rot * sin

    def chunk_step(ci, carry):
        r = pl.multiple_of(ci * c, c)
        rows = pl.ds(r, c)
        cos = cos_ref[rows, :]
        sin = sin_ref[rows, :]
        q_pair, k_pair, kb_pair = [], [], []
        for pr in range(nh // 2):
            q_pair.append(rotary(rq_ref[0, rows, slab(pr)].astype(F32), cos, sin))
            k = rotary(rk_ref[0, rows, slab(pr)].astype(F32), cos, sin) * (RET_QK ** -0.5)
            k_pair.append(k)
            kb_pair.append(k.astype(BF16))
        qh = [jnp.where(head_mask[h % 2], q_pair[h // 2], 0.0).astype(BF16) for h in range(nh)]
        kd = [(k_pair[h // 2] * key_ref[h]).astype(BF16) for h in range(nh)]
        vh = [rv_ref[0, rows, slab(h)] for h in range(nh)]
        scores = [lax.dot_general(qh[h], kb_pair[h // 2], _NT, preferred_element_type=F32) for h in range(nh)]
        cross = [jnp.dot(qh[h], state_ref[h].astype(BF16), preferred_element_type=F32) for h in range(nh)]
        kv = [lax.dot_general(kd[h], vh[h], _TN, preferred_element_type=F32) for h in range(nh)]
        decayed = [(scores[h] * inner_ref[h]).astype(BF16) for h in range(nh)]
        for h in range(nh):
            state_ref[h] = state_ref[h] * chunk_ref[h] + kv[h]
        inner = [jnp.dot(decayed[h], vh[h], preferred_element_type=F32) for h in range(nh)]
        for h in range(nh):
            y = inner[h] + cross[h] * cross_ref[h]
            mu = jnp.mean(y, axis=-1, keepdims=True)
            yc = y - mu
            var = jnp.mean(yc * yc, axis=-1, keepdims=True)
            yn = (yc * lax.rsqrt(var + EPS)) * gn_ref[:, slab(h)]
            gate = rg_ref[0, rows, slab(h)].astype(F32)
            silu = gate * (1.0 / (1.0 + jnp.exp(-gate)))
            o_ref[0, rows, slab(h)] = (silu * yn).astype(o_ref.dtype)
        return carry

    lax.fori_loop(0, nc, chunk_step, 0)


def _retention(proj3, tables, gn_g):
    b, s, _ = proj3.shape
    cos_t, sin_t, inner, cross, key, chunk = tables
    c = RET_CHUNK
    nh = N_RET_HEADS
    qk_w = nh * RET_QK
    v_w = nh * LANES
    full = lambda shape: pl.BlockSpec(shape, lambda bi: (0,) * len(shape))
    return pl.pallas_call(
        _retention_body,
        grid=(b,),
        in_specs=[
            pl.BlockSpec((1, s, qk_w), lambda bi: (bi, 0, COL_RQ * LANES // qk_w)),
            pl.BlockSpec((1, s, qk_w), lambda bi: (bi, 0, COL_RK * LANES // qk_w)),
            pl.BlockSpec((1, s, v_w), lambda bi: (bi, 0, COL_RV * LANES // v_w)),
            pl.BlockSpec((1, s, v_w), lambda bi: (bi, 0, COL_RG * LANES // v_w)),
            full((s, LANES)), full((s, LANES)),
            full((nh, c, c)), full((nh, c, LANES)), full((nh, c, LANES)), full((nh, 1, LANES)),
            full((1, v_w)),
        ],
        out_specs=pl.BlockSpec((1, s, v_w), lambda bi: (bi, 0, 0)),
        out_shape=jax.ShapeDtypeStruct((b, s, v_w), BF16),
        scratch_shapes=[pltpu.VMEM((nh, LANES, LANES), F32)],
        compiler_params=_params(("arbitrary",)),
        name="retention",
    )(proj3, proj3, proj3, proj3, cos_t, sin_t, inner, cross, key, chunk, gn_g)


def _route(logits):
    lane = lax.broadcasted_iota(jnp.int32, logits.shape, 1).astype(F32)
    big = float(LANES)
    is_group = lane < N_GROUPS
    gl = jnp.where(is_group, logits, -jnp.inf)
    gmax = jnp.max(gl, axis=-1, keepdims=True)
    gidx = jnp.min(jnp.where(gl == gmax, lane, big), axis=-1, keepdims=True)
    p_group = 1.0 / jnp.sum(jnp.where(is_group, jnp.exp(gl - gmax), 0.0), axis=-1, keepdims=True)
    lo = N_GROUPS + EXPERTS_PER_GROUP * gidx
    il = jnp.where(lane >= lo, jnp.where(lane < lo + EXPERTS_PER_GROUP, logits, -jnp.inf), -jnp.inf)
    v1 = jnp.max(il, axis=-1, keepdims=True)
    i1 = jnp.min(jnp.where(il == v1, lane, big), axis=-1, keepdims=True)
    il2 = jnp.where(lane == i1, -jnp.inf, il)
    v2 = jnp.max(il2, axis=-1, keepdims=True)
    i2 = jnp.min(jnp.where(il2 == v2, lane, big), axis=-1, keepdims=True)
    e2 = jnp.exp(v2 - v1)
    inv = 1.0 / (1.0 + e2)
    g1 = p_group * inv
    g2 = p_group * (e2 * inv)
    out = jnp.where(lane == 0.0, i1 - N_GROUPS,
                    jnp.where(lane == 1.0, i2 - N_GROUPS,
                              jnp.where(lane == 2.0, g1, jnp.where(lane == 3.0, g2, 0.0))))
    return out


def _outproj_body(a_ref, r_ref, x_ref, w_ref, g_ref, wr_ref, br_ref, x1_ref, h2_ref, route_ref, *, sub):
    tm = x_ref.shape[0]
    half = a_ref.shape[1]
    n_sub = tm // sub

    def project(i):
        rows = slice(i * sub, (i + 1) * sub)
        acc = jnp.dot(a_ref[rows, :], w_ref[:half, :], preferred_element_type=F32)
        return acc + jnp.dot(r_ref[rows, :], w_ref[half:, :], preferred_element_type=F32)

    def epilogue(i, acc):
        rows = slice(i * sub, (i + 1) * sub)
        x1 = x_ref[rows, :] + acc
        x1_ref[rows, :] = x1
        ms = jnp.mean(x1 * x1, axis=-1, keepdims=True)
        h2 = (x1 * lax.rsqrt(ms + EPS)) * g_ref[...]
        h2_ref[rows, :] = _pack_halves(h2)
        logits = jnp.dot(h2.astype(BF16), wr_ref[...], preferred_element_type=F32) + br_ref[...]
        route_ref[rows, :] = _route(logits)

    acc = project(0)
    for i in range(n_sub):
        nxt = project(i + 1) if i + 1 < n_sub else None
        epilogue(i, acc)
        acc = nxt


def _outproj_router(a2d, r2d, x2d, w_out_bf16, g, wr_bf16, br, *, tm):
    n, d = x2d.shape
    half = a2d.shape[1]
    row = lambda width: pl.BlockSpec((tm, width), lambda i: (i, 0))
    full = lambda shape: pl.BlockSpec(shape, lambda i: (0, 0))
    return pl.pallas_call(
        functools.partial(_outproj_body, sub=min(tm, 2 * LANES)),
        grid=(n // tm,),
        in_specs=[row(half), row(half), row(d), full((2 * half, d)), full((1, d)), full((d, LANES)),
                  full((1, LANES))],
        out_specs=[row(d), row(d // 2), row(LANES)],
        out_shape=[jax.ShapeDtypeStruct((n, d), F32), jax.ShapeDtypeStruct((n, d // 2), jnp.uint32),
                   jax.ShapeDtypeStruct((n, LANES), F32)],
        compiler_params=_params(("arbitrary",)),
        name="outproj_router",
    )(a2d, r2d, x2d, w_out_bf16, g, wr_bf16, br)


def _moe_plan(expert_id, tm):
    n_slots = expert_id.shape[0]
    n_tiles = n_slots // tm
    order = jnp.argsort(expert_id).astype(jnp.int32)
    sizes = jnp.zeros((N_EXPERTS,), jnp.int32).at[expert_id].add(1)
    ends = jnp.cumsum(sizes)
    starts = ends - sizes
    tile_row0 = jnp.arange(n_tiles, dtype=jnp.int32) * tm
    owner = lambda row: jnp.sum((ends[None, :] <= row[:, None]).astype(jnp.int32), axis=1)
    first_e = owner(tile_row0)
    last_e = owner(tile_row0 + (tm - 1))
    owners_upto = jnp.cumsum((sizes > 0).astype(jnp.int32))
    n_extra = (owners_upto[last_e] - owners_upto[first_e]).astype(jnp.int32)
    prev_last = jnp.concatenate([jnp.full((1,), -1, jnp.int32), last_e[:-1]])
    switch = (prev_last != first_e).astype(jnp.int32)
    eid = jnp.arange(N_EXPERTS, dtype=jnp.int32)
    cand = jnp.where((sizes[None, :] > 0) & (eid[None, :] > eid[:, None]), eid[None, :], N_EXPERTS)
    nxt = jnp.min(cand, axis=1)
    next_expert = jnp.where(nxt < N_EXPERTS, nxt, -1).astype(jnp.int32)
    spare = 2 * (n_slots + jnp.arange(tm, dtype=jnp.int32))
    codes = jnp.concatenate([order.reshape(n_tiles, tm), spare[None, :]], axis=0).reshape(n_tiles + 1, 1, tm)
    return dict(first_e=first_e, n_extra=n_extra, switch=switch, starts=starts.astype(jnp.int32),
                ends=ends.astype(jnp.int32), next_expert=next_expert, codes=codes)


def _moe_body(fe_ref, nextra_ref, sw_ref, st_ref, en_ref, nx_ref,
              codes_hbm, h2_hbm, wg_hbm, wu_hbm, wd_hbm, ys_hbm,
              gidx, sidx, xbuf, obuf, wg_st, wu_st, wd_st, wg_rs, wu_rs, wd_rs,
              gisem, sisem, gsem, ssem, wsem, *, n_tokens, n_tiles, cast_rows, ff_chunks):
    t = pl.program_id(0)
    tm = xbuf.shape[1]
    last = n_tiles - 1
    spare_codes = n_tiles

    def gidx_copy(tile, slot):
        return pltpu.make_async_copy(codes_hbm.at[tile], gidx.at[slot], gisem.at[slot])

    def sidx_copy(tile, slot):
        return pltpu.make_async_copy(codes_hbm.at[tile], sidx.at[slot], sisem.at[slot])

    def weight_copies(ex):
        return (pltpu.make_async_copy(wg_hbm.at[ex], wg_st, wsem.at[0]),
                pltpu.make_async_copy(wu_hbm.at[ex], wu_st, wsem.at[1]),
                pltpu.make_async_copy(wd_hbm.at[ex], wd_st, wsem.at[2]))

    def gather_row(slot, r):
        tok = lax.shift_right_logical(gidx[slot, 0, r], 1)
        return pltpu.make_async_copy(h2_hbm.at[pl.ds(tok, 1)], xbuf.at[slot, pl.ds(r, 1)], gsem.at[slot])

    def scatter_row(slot, r):
        code = sidx[slot, 0, r]
        dst = (code & 1) * n_tokens + lax.shift_right_logical(code, 1)
        return pltpu.make_async_copy(obuf.at[slot, pl.ds(r, 1)], ys_hbm.at[pl.ds(dst, 1)], ssem.at[slot])

    def wait_gather(slot):
        pltpu.make_async_copy(h2_hbm.at[pl.ds(0, tm)], xbuf.at[slot], gsem.at[slot]).wait()

    def wait_scatter(slot):
        pltpu.make_async_copy(obuf.at[slot], ys_hbm.at[pl.ds(0, tm)], ssem.at[slot]).wait()

    def for_rows(fn):
        def body(r, carry):
            fn(r)
            return carry
        lax.fori_loop(0, tm, body, 0, unroll=8)

    def take_over(ex, first):
        @pl.when(first)
        def _():
            for cp in weight_copies(ex):
                cp.start()

        for cp in weight_copies(ex):
            cp.wait()

        def cast_up(i, carry):
            r = pl.multiple_of(i * cast_rows, cast_rows)
            wg_rs[pl.ds(r, cast_rows), :] = wg_st[pl.ds(r, cast_rows), :].astype(BF16)
            wu_rs[pl.ds(r, cast_rows), :] = wu_st[pl.ds(r, cast_rows), :].astype(BF16)
            return carry

        def cast_down(i, carry):
            r = pl.multiple_of(i * cast_rows, cast_rows)
            wd_rs[pl.ds(r, cast_rows), :] = wd_st[pl.ds(r, cast_rows), :].astype(BF16)
            return carry

        lax.fori_loop(0, wg_st.shape[0] // cast_rows, cast_up, 0)
        lax.fori_loop(0, wd_st.shape[0] // cast_rows, cast_down, 0)
        nxt = nx_ref[ex]

        @pl.when(nxt >= 0)
        def _():
            for cp in weight_copies(nxt):
                cp.start()

    def ffn(slot, ex, accumulate, row_dmas=None):
        x_lo, x_hi = (h.astype(BF16) for h in _unpack_halves(xbuf[slot]))
        hd = x_lo.shape[1]
        row = t * tm + lax.broadcasted_iota(jnp.int32, (tm, 1), 0)
        mine = jnp.logical_and(row >= st_ref[ex], row < en_ref[ex])
        d_ff = wg_rs.shape[1]
        n_chunks = ff_chunks if row_dmas is not None else 1
        fc = d_ff // n_chunks
        rows = tm // n_chunks
        contrib = None
        for c in range(n_chunks):
            cols = slice(c * fc, (c + 1) * fc)
            up = lambda w: (jnp.dot(x_lo, w[:hd, cols], preferred_element_type=F32)
                            + jnp.dot(x_hi, w[hd:, cols], preferred_element_type=F32))
            a = up(wg_rs)
            b = up(wu_rs)
            hmid = jnp.where(mine, (a * (1.0 / (1.0 + jnp.exp(-a)))) * b, 0.0).astype(BF16)
            part = jnp.dot(hmid, wd_rs[cols, :], preferred_element_type=F32)
            contrib = part if contrib is None else contrib + part
            if row_dmas is not None:
                row_dmas(c * rows, (c + 1) * rows)
        if accumulate:
            prev_lo, prev_hi = _unpack_halves(obuf[slot])
            contrib = jnp.concatenate([prev_lo, prev_hi], axis=1) + contrib
        obuf[slot] = _pack_halves(contrib)

    e0 = fe_ref[t]

    def step(p):
        q = 1 - p
        if p == 0:
            @pl.when(t == 0)
            def _():
                copies = (gidx_copy(0, 0), gidx_copy(min(1, last), 1), sidx_copy(spare_codes, 1))
                for cp in copies:
                    cp.start()
                for cp in copies:
                    cp.wait()
                obuf[1] = jnp.zeros(obuf.shape[1:], obuf.dtype)
                for_rows(lambda r: gather_row(0, r).start())

        gidx_copy(jnp.minimum(t + 2, last), p).start()
        sidx_copy(t, p).start()

        @pl.when(t >= 1)
        def _():
            gidx_copy(0, q).wait()
            sidx_copy(0, q).wait()
            wait_scatter(p)

        wait_gather(p)

        @pl.when(sw_ref[t] == 1)
        def _():
            take_over(e0, t == 0)

        def row_dmas(lo, hi):
            for r in range(lo, hi):
                scatter_row(q, r).start()
                gather_row(q, r).start()

        ffn(p, e0, accumulate=False, row_dmas=row_dmas)

    for parity in range(2):
        pl.when(lax.rem(t, 2) == parity)(functools.partial(step, parity))

    slot = lax.rem(t, 2)

    def further_owner(_, ex):
        nxt = nx_ref[ex]
        take_over(nxt, False)
        ffn(slot, nxt, accumulate=True)
        return nxt

    lax.fori_loop(0, nextra_ref[t], further_owner, e0)

    @pl.when(t == last)
    def _():
        gidx_copy(0, slot).wait()
        sidx_copy(0, slot).wait()
        for_rows(lambda r: scatter_row(slot, r).start())
        wait_gather(1 - slot)
        wait_scatter(1 - slot)
        wait_scatter(slot)


def _moe_ffn(h2, plan, w_gate, w_up, w_down, *, tm):
    n, hd = h2.shape
    d = 2 * hd
    n_slots = 2 * n
    n_tiles = n_slots // tm
    assert n_tiles >= 2 and n_slots % tm == 0
    d_ff = w_gate.shape[2]
    any_spec = pl.BlockSpec(memory_space=pl.ANY)
    grid_spec = pltpu.PrefetchScalarGridSpec(
        num_scalar_prefetch=6,
        grid=(n_tiles,),
        in_specs=[any_spec] * 5,
        out_specs=any_spec,
        scratch_shapes=[
            pltpu.SMEM((2, 1, tm), jnp.int32),
            pltpu.SMEM((2, 1, tm), jnp.int32),
            pltpu.VMEM((2, tm, hd), jnp.uint32),
            pltpu.VMEM((2, tm, hd), jnp.uint32),
            pltpu.VMEM((d, d_ff), F32),
            pltpu.VMEM((d, d_ff), F32),
            pltpu.VMEM((d_ff, d), F32),
            pltpu.VMEM((d, d_ff), BF16),
            pltpu.VMEM((d, d_ff), BF16),
            pltpu.VMEM((d_ff, d), BF16),
            pltpu.SemaphoreType.DMA((2,)),
            pltpu.SemaphoreType.DMA((2,)),
            pltpu.SemaphoreType.DMA((2,)),
            pltpu.SemaphoreType.DMA((2,)),
            pltpu.SemaphoreType.DMA((3,)),
        ],
    )
    return pl.pallas_call(
        functools.partial(_moe_body, n_tokens=n, n_tiles=n_tiles, cast_rows=128,
                          ff_chunks=d_ff // (2 * LANES)),
        grid_spec=grid_spec,
        out_shape=jax.ShapeDtypeStruct((n_slots + tm, hd), jnp.uint32),
        compiler_params=_params(("arbitrary",), vmem=MOE_VMEM_LIMIT),
        name="moe_ffn",
    )(plan["first_e"], plan["n_extra"], plan["switch"], plan["starts"], plan["ends"], plan["next_expert"],
      plan["codes"], h2, w_gate, w_up, w_down)


def _final_body(x1_ref, ys0_ref, ys1_ref, route_ref, g_ref, o_ref):
    d = x1_ref.shape[1]
    route = route_ref[...]
    g1, g2 = route[:, 2:3], route[:, 3:4]
    ya = _unpack_halves(ys0_ref[...])
    yb = _unpack_halves(ys1_ref[...])
    halves = (slice(0, d // 2), slice(d // 2, d))
    x2 = [x1_ref[:, cols] + (ya[i] * g1 + yb[i] * g2) for i, cols in enumerate(halves)]
    ms = sum(jnp.sum(h * h, axis=-1, keepdims=True) for h in x2) * (1.0 / d)
    inv = lax.rsqrt(ms + EPS)
    for h, cols in zip(x2, halves):
        o_ref[:, cols] = ((h * inv) * g_ref[:, cols]).astype(o_ref.dtype)


def _final(x1, ys, route, g, *, tm):
    n, d = x1.shape
    nb = n // tm
    return pl.pallas_call(
        _final_body,
        grid=(nb,),
        in_specs=[
            pl.BlockSpec((tm, d), lambda i: (i, 0)),
            pl.BlockSpec((tm, d // 2), lambda i: (i, 0)),
            pl.BlockSpec((tm, d // 2), lambda i: (nb + i, 0)),
            pl.BlockSpec((tm, LANES), lambda i: (i, 0)),
            pl.BlockSpec((1, d), lambda i: (0, 0)),
        ],
        out_specs=pl.BlockSpec((tm, d), lambda i: (i, 0)),
        out_shape=jax.ShapeDtypeStruct((n, d), F32),
        compiler_params=_params(("arbitrary",)),
        name="final_norm",
    )(x1, ys, ys, route, g)


def _layer(x, rel_bias, attn_norm_g, w_in, lq1, lk1, lq2, lk2, diff_subln_g, ret_gn_g, w_out, ffn_norm_g,
           w_group_router, b_group, w_inner_router, b_inner, w_gate_exp, w_up_exp, w_down_exp, *, layer,
           attn_tile, row_tile, moe_tile):
    b, s, d = x.shape
    n = b * s
    x2d = x.reshape(n, d)
    lam_init = 0.8 - 0.6 * math.exp(-0.3 * layer)

    dq_cols = (COL_DK - COL_DQ) * LANES
    col_scale = jnp.concatenate([jnp.full((1, dq_cols), DIFF_HALF ** -0.5 * LOG2E, F32),
                                 jnp.ones((1, IN_COLS - dq_cols), F32)], axis=1)
    proj = _inproj(x2d, attn_norm_g.reshape(1, d), w_in, col_scale, tm=min(2 * row_tile, n),
                   tn=8 * LANES)
    proj3 = proj.reshape(b, s, IN_COLS)

    bias = _bias_tiles(rel_bias, attn_tile)
    vec = lambda a: a.reshape(1, -1)
    a_out = _diff_attention(proj3, bias, vec(lq1), vec(lk1), vec(lq2), vec(lk2), vec(diff_subln_g),
                            t=attn_tile, lam_init=lam_init)
    r_out = _retention(proj3, _retention_tables(s), vec(ret_gn_g))

    pad = LANES - N_GROUPS - N_EXPERTS
    wr = jnp.concatenate([w_group_router, jnp.transpose(w_inner_router, (1, 0, 2)).reshape(d, N_EXPERTS),
                          jnp.zeros((d, pad), F32)], axis=1).astype(BF16)
    br = jnp.concatenate([b_group, b_inner.reshape(-1), jnp.zeros((pad,), F32)]).reshape(1, LANES)
    x1, h2, route = _outproj_router(a_out.reshape(n, -1), r_out.reshape(n, -1), x2d, w_out.astype(BF16),
                                    vec(ffn_norm_g), wr, br, tm=row_tile)

    expert_id = route[:, :2].astype(jnp.int32).reshape(-1)
    plan = _moe_plan(expert_id, moe_tile)
    ys = _moe_ffn(h2, plan, w_gate_exp, w_up_exp, w_down_exp, tm=moe_tile)
    return x1, ys, route


def kernel(x, rel_bias, attn_norm_g, w_in, lambda_q1, lambda_k1, lambda_q2, lambda_k2, diff_subln_g, ret_gn_g,
           w_out, ffn_norm_g, w_group_router, b_group, w_inner_router, b_inner, w_gate_exp, w_up_exp, w_down_exp,
           final_g):
    b, s, d = x.shape
    depth = w_in.shape[0]
    assert depth == 1, "the final norm is fused into the single layer's combine step"
    first = lambda a: a.reshape(a.shape[1:])
    x1, ys, route = _layer(x, rel_bias, attn_norm_g[0], w_in[0], lambda_q1[0], lambda_k1[0], lambda_q2[0],
                           lambda_k2[0], diff_subln_g[0], ret_gn_g[0], w_out[0], ffn_norm_g[0],
                           w_group_router[0], b_group[0], w_inner_router[0], b_inner[0], first(w_gate_exp),
                           first(w_up_exp), first(w_down_exp), layer=0, attn_tile=min(512, s), row_tile=512,
                           moe_tile=256)
    out = _final(x1, ys, route, final_g.reshape(1, d), tm=256)
    return out.reshape(b, s, d)
```

```python
import functools
import math

import numpy as np
import jax
import jax.numpy as jnp
from jax import lax
from jax.experimental import pallas as pl
from jax.experimental.pallas import tpu as pltpu

F32 = jnp.float32
BF16 = jnp.bfloat16

EPS = 1e-6
LANES = 128
N_DIFF_HEADS = 8
DIFF_HALF = 64
N_RET_HEADS = 8
RET_QK = 64
RET_CHUNK = 128
ROPE_BASE = 10000.0
N_BUCKETS = 32
MAX_DISTANCE = 128
N_GROUPS = 4
EXPERTS_PER_GROUP = 8
N_EXPERTS = N_GROUPS * EXPERTS_PER_GROUP
MASK_VALUE = -1e30
LOG2E = math.log2(math.e)
VMEM_LIMIT = 56 * 1024 * 1024
MOE_VMEM_LIMIT = 60 * 1024 * 1024

COL_DQ, COL_DK, COL_DV = 0, 8, 16
COL_RQ, COL_RK, COL_RV, COL_RG = 24, 28, 32, 40
IN_COLS = 48 * LANES


def _params(sem, vmem=VMEM_LIMIT):
    return pltpu.CompilerParams(dimension_semantics=sem, vmem_limit_bytes=vmem)


def _pack_halves(x):
    hd = x.shape[1] // 2
    bits = lambda h: lax.bitcast_convert_type(h.astype(BF16).astype(F32), jnp.uint32)
    return lax.shift_right_logical(bits(x[:, :hd]), jnp.uint32(16)) | bits(x[:, hd:])


def _unpack_halves(w):
    return (lax.bitcast_convert_type(lax.shift_left(w, jnp.uint32(16)), F32),
            lax.bitcast_convert_type(w & jnp.uint32(0xFFFF0000), F32))


def _inproj_body(x_ref, g_ref, w_ref, cs_ref, o_ref, h_ref, *, row_chunk):
    tm = x_ref.shape[0]

    @pl.when(pl.program_id(1) == 0)
    def _():
        g = g_ref[...]

        def chunk(c, carry):
            r = pl.multiple_of(c * row_chunk, row_chunk)
            x = x_ref[pl.ds(r, row_chunk), :]
            ms = jnp.mean(x * x, axis=-1, keepdims=True)
            h_ref[pl.ds(r, row_chunk), :] = ((x * lax.rsqrt(ms + EPS)) * g).astype(BF16)
            return carry

        lax.fori_loop(0, tm // row_chunk, chunk, 0)

    acc = jnp.dot(h_ref[...], w_ref[...], preferred_element_type=F32)
    o_ref[...] = (acc * cs_ref[...]).astype(o_ref.dtype)


def _inproj(x2d, g, w, col_scale, *, tm, tn):
    n, d = x2d.shape
    cols = w.shape[1]
    return pl.pallas_call(
        functools.partial(_inproj_body, row_chunk=128),
        grid=(n // tm, cols // tn),
        in_specs=[
            pl.BlockSpec((tm, d), lambda i, j: (i, 0)),
            pl.BlockSpec((1, d), lambda i, j: (0, 0)),
            pl.BlockSpec((d, tn), lambda i, j: (0, j)),
            pl.BlockSpec((1, tn), lambda i, j: (0, j)),
        ],
        out_specs=pl.BlockSpec((tm, tn), lambda i, j: (i, j)),
        out_shape=jax.ShapeDtypeStruct((n, cols), BF16),
        scratch_shapes=[pltpu.VMEM((tm, d), BF16)],
        compiler_params=_params(("arbitrary", "arbitrary")),
        name="inproj",
    )(x2d, g, w, col_scale)


def _t5_bucket_table(n):
    d = np.arange(n)
    max_exact = N_BUCKETS // 2
    log_ratio = np.log(np.maximum(d, 1).astype(np.float64) / max_exact) / math.log(MAX_DISTANCE / max_exact)
    large = np.minimum(max_exact + (log_ratio * (N_BUCKETS - max_exact)).astype(np.int64), N_BUCKETS - 1)
    return np.where(d < max_exact, d, large).astype(np.int32)


def _bias_body(rb_ref, bkt_ref, o_ref):
    h = pl.program_id(0)
    t = o_ref.shape[2]
    bkt = bkt_ref[...]
    far = rb_ref[N_BUCKETS - 1, h]
    by_dist = jnp.zeros(bkt.shape, F32)
    for b in range(N_BUCKETS - 1):
        by_dist = jnp.where(bkt == b, (rb_ref[b, h] - far) * LOG2E, by_dist)
    rows = (jnp.concatenate([jnp.full((1, t), MASK_VALUE, F32), by_dist[:, :t]], axis=1), by_dist)
    for i, row in enumerate(rows):
        shifted = pltpu.roll(jnp.broadcast_to(row, (t, 2 * t)), 0, 1, stride=1, stride_axis=0)
        o_ref[0, i] = shifted[:, t:]


def _bias_tiles(rel_bias, t):
    bkt = jnp.asarray(_t5_bucket_table(2 * t)).reshape(1, 2 * t)
    return pl.pallas_call(
        _bias_body,
        grid=(N_DIFF_HEADS,),
        in_specs=[
            pl.BlockSpec(memory_space=pltpu.SMEM),
            pl.BlockSpec((1, 2 * t), lambda h: (0, 0)),
        ],
        out_specs=pl.BlockSpec((1, 2, t, t), lambda h: (h, 0, 0, 0)),
        out_shape=jax.ShapeDtypeStruct((N_DIFF_HEADS, 2, t, t), F32),
        compiler_params=_params(("arbitrary",)),
        name="bias_tiles",
    )(rel_bias, bkt)


_NT = (((1,), (1,)), ((), ()))
_TN = (((0,), (0,)), ((), ()))


def _diffattn_body(q_ref, k_ref, v_ref, bias_ref, lq1_ref, lk1_ref, lq2_ref, lk2_ref, g_ref, o_ref, vt_ref, *,
                   t, lam_init, near_band):
    nq = k_ref.shape[1] // t
    half = t // 2
    lane = lax.broadcasted_iota(jnp.int32, (t, LANES), 1)
    lam = (jnp.exp(jnp.sum(lq1_ref[...] * lk1_ref[...], axis=-1, keepdims=True))
           - jnp.exp(jnp.sum(lq2_ref[...] * lk2_ref[...], axis=-1, keepdims=True)) + lam_init)
    vt_ref[:LANES, :] = v_ref[0].T
    vt_ref[LANES:, :] = jnp.ones((vt_ref.shape[0] - LANES, vt_ref.shape[1]), BF16)
    qmask = {}
    state = {}

    def scores(item):
        qi, j, part = item
        if qi not in qmask:
            q = q_ref[0, qi * t:(qi + 1) * t, :]
            zero = jnp.zeros_like(q)
            qmask[qi] = (jnp.where(lane < DIFF_HALF, q, zero), jnp.where(lane >= DIFF_HALF, q, zero))
        k_lo = j * t + (half if part == "diag_lower" else 0)
        k_hi = j * t + (half if part == "diag_upper" else t)
        kb = k_ref[0, k_lo:k_hi, :]
        out = []
        for c in range(2):
            qsel = qmask[qi][c][half:] if part == "diag_lower" else qmask[qi][c]
            s = lax.dot_general(kb, qsel, _NT, preferred_element_type=F32)
            if part == "diag_upper":
                s = s + bias_ref[0, 0, :half, :]
            elif part == "diag_lower":
                s = s + bias_ref[0, 0, half:, half:]
            elif j == qi - 1:
                if near_band >= t:
                    s = s + bias_ref[0, 1]
                else:
                    corner = s[t - near_band:, :near_band] + bias_ref[0, 1, t - near_band:, :near_band]
                    bottom = jnp.concatenate([corner, s[t - near_band:, near_band:]], axis=1)
                    s = jnp.concatenate([s[:t - near_band], bottom], axis=0)
            out.append(s)
        return out

    def softmax_step(item, ss):
        qi, j, part = item
        first = j == 0 and part != "diag_lower"
        ps, alphas = [], []
        for c in range(2):
            s = ss[c]
            smax = jnp.max(s, axis=0, keepdims=True)
            if first:
                m_new, alpha = smax, None
            else:
                m_old = state[qi, c]["m"]
                m_old = m_old[:, half:] if part == "diag_lower" else m_old
                m_new = jnp.maximum(m_old, smax)
                alpha = jnp.exp2(m_old - m_new)
            ps.append(jnp.exp2(s - m_new).astype(BF16))
            alphas.append((m_new, alpha))
        return ps, alphas

    def values_step(item, ps, alphas):
        qi, j, part = item
        k_lo = j * t + (half if part == "diag_lower" else 0)
        k_hi = j * t + (half if part == "diag_upper" else t)
        vtb = vt_ref[:, k_lo:k_hi]
        for c in range(2):
            m_new, alpha = alphas[c]
            r = jnp.dot(vtb, ps[c], preferred_element_type=F32)
            if alpha is None:
                state[qi, c] = dict(m=m_new, acc=r)
            elif part == "diag_lower":
                st = state[qi, c]
                st["m"] = jnp.concatenate([st["m"][:, :half], m_new], axis=1)
                st["acc"] = jnp.concatenate([st["acc"][:, :half], alpha * st["acc"][:, half:] + r], axis=1)
            else:
                st = state[qi, c]
                st["m"] = m_new
                st["acc"] = alpha * st["acc"] + r
        if part == "diag_lower":
            outs = []
            for c in range(2):
                acc = state.pop((qi, c))["acc"]
                outs.append(acc[:LANES] * (1.0 / acc[LANES:LANES + 1]))
            ot = (outs[0] - lam * outs[1]).T
            ms = jnp.mean(ot * ot, axis=-1, keepdims=True)
            o_ref[0, qi * t:(qi + 1) * t, :] = (
                ((ot * lax.rsqrt(ms + EPS)) * g_ref[...]) * (1.0 - lam_init)).astype(o_ref.dtype)
            del qmask[qi]

    items = []
    for qi in range(nq):
        items += [(qi, j, "full") for j in range(qi)] + [(qi, qi, "diag_upper"), (qi, qi, "diag_lower")]
    ss = scores(items[0])
    for n, item in enumerate(items):
        nxt = scores(items[n + 1]) if n + 1 < len(items) else None
        ps, alphas = softmax_step(item, ss)
        values_step(item, ps, alphas)
        ss = nxt


def _diff_attention(proj3, bias, lq1, lk1, lq2, lk2, subln_g, *, t, lam_init):
    b, s, _ = proj3.shape
    vec = pl.BlockSpec((1, DIFF_HALF), lambda bi, h: (0, 0))
    head = lambda col: pl.BlockSpec((1, s, LANES), lambda bi, h: (bi, 0, col + h))
    first_far = int(np.argmax(_t5_bucket_table(2 * t) == N_BUCKETS - 1))
    near_band = -(-first_far // LANES) * LANES
    ones_rows = 16
    return pl.pallas_call(
        functools.partial(_diffattn_body, t=t, lam_init=lam_init, near_band=near_band),
        grid=(b, N_DIFF_HEADS),
        in_specs=[
            head(COL_DQ), head(COL_DK), head(COL_DV),
            pl.BlockSpec((1, 2, t, t), lambda bi, h: (h, 0, 0, 0)),
            vec, vec, vec, vec,
            pl.BlockSpec((1, LANES), lambda bi, h: (0, 0)),
        ],
        out_specs=pl.BlockSpec((1, s, LANES), lambda bi, h: (bi, 0, h)),
        out_shape=jax.ShapeDtypeStruct((b, s, N_DIFF_HEADS * LANES), BF16),
        scratch_shapes=[pltpu.VMEM((LANES + ones_rows, s), BF16)],
        compiler_params=_params(("arbitrary", "arbitrary")),
        name="diff_attention",
    )(proj3, proj3, proj3, bias, lq1, lk1, lq2, lk2, subln_g)


def _retention_tables(s):
    c = RET_CHUNK
    half = RET_QK // 2
    inv_freq = ROPE_BASE ** (-jnp.arange(0, RET_QK, 2, dtype=F32) / RET_QK)
    ang = jnp.arange(s, dtype=F32)[:, None] * inv_freq[None, :]
    cos, sin = jnp.cos(ang), jnp.sin(ang)
    cos_t = jnp.tile(cos, (1, LANES // half))
    sin_t = jnp.tile(jnp.concatenate([-sin, sin], axis=1), (1, LANES // RET_QK))
    log_g = jnp.log(1.0 - jnp.exp2(-5.0 - jnp.arange(N_RET_HEADS, dtype=F32)))
    n = jnp.arange(c, dtype=F32)
    diff = n[:, None] - n[None, :]
    inner = jnp.where(diff[None] >= 0, jnp.exp(jnp.maximum(diff, 0.0)[None] * log_g[:, None, None]), 0.0)
    cross = jnp.exp((n[None] + 1.0) * log_g[:, None])
    key = jnp.exp((c - 1.0 - n[None]) * log_g[:, None])
    chunk = jnp.exp(c * log_g)
    bc = lambda a: jnp.broadcast_to(a[..., None], a.shape + (LANES,)).astype(F32)
    return cos_t, sin_t, inner.astype(F32), bc(cross), bc(key), bc(chunk[:, None])


def _retention_body(rq_ref, rk_ref, rv_ref, rg_ref, cos_ref, sin_ref, inner_ref, cross_ref, key_ref,
                    chunk_ref, gn_ref, o_ref, state_ref):
    c = RET_CHUNK
    nc = rq_ref.shape[1] // c
    nh = N_RET_HEADS
    state_ref[...] = jnp.zeros(state_ref.shape, F32)
    lane = lax.broadcasted_iota(jnp.int32, (c, LANES), 1)
    first_half = (lane % RET_QK) < (RET_QK // 2)
    head_mask = [lane < RET_QK, lane >= RET_QK]
    slab = lambda i: slice(i * LANES, (i + 1) * LANES)

    def rotary(x, cos, sin):
        rot = jnp.where(first_half, pltpu.roll(x, LANES - RET_QK // 2, 1), pltpu.roll(x, RET_QK // 2, 1))
        return x * cos + rot * sin

    def chunk_step(ci, carry):
        r = pl.multiple_of(ci * c, c)
        rows = pl.ds(r, c)
        cos = cos_ref[rows, :]
        sin = sin_ref[rows, :]
        q_pair, k_pair, kb_pair = [], [], []
        for pr in range(nh // 2):
            q_pair.append(rotary(rq_ref[0, rows, slab(pr)].astype(F32), cos, sin))
            k = rotary(rk_ref[0, rows, slab(pr)].astype(F32), cos, sin) * (RET_QK ** -0.5)
            k_pair.append(k)
            kb_pair.append(k.astype(BF16))
        qh = [jnp.where(head_mask[h % 2], q_pair[h // 2], 0.0).astype(BF16) for h in range(nh)]
        kd = [(k_pair[h // 2] * key_ref[h]).astype(BF16) for h in range(nh)]
        vh = [rv_ref[0, rows, slab(h)] for h in range(nh)]
        scores = [lax.dot_general(qh[h], kb_pair[h // 2], _NT, preferred_element_type=F32) for h in range(nh)]
        cross = [jnp.dot(qh[h], state_ref[h].astype(BF16), preferred_element_type=F32) for h in range(nh)]
        kv = [lax.dot_general(kd[h], vh[h], _TN, preferred_element_type=F32) for h in range(nh)]
        decayed = [(scores[h] * inner_ref[h]).astype(BF16) for h in range(nh)]
        for h in range(nh):
            state_ref[h] = state_ref[h] * chunk_ref[h] + kv[h]
        inner = [jnp.dot(decayed[h], vh[h], preferred_element_type=F32) for h in range(nh)]
        for h in range(nh):
            y = inner[h] + cross[h] * cross_ref[h]
            mu = jnp.mean(y, axis=-1, keepdims=True)
            yc = y - mu
            var = jnp.mean(yc * yc, axis=-1, keepdims=True)
            yn = (yc * lax.rsqrt(var + EPS)) * gn_ref[:, slab(h)]
            gate = rg_ref[0, rows, slab(h)].astype(F32)
            silu = gate * (1.0 / (1.0 + jnp.exp(-gate)))
            o_ref[0, rows, slab(h)] = (silu * yn).astype(o_ref.dtype)
        return carry

    lax.fori_loop(0, nc, chunk_step, 0)


def _retention(proj3, tables, gn_g):
    b, s, _ = proj3.shape
    cos_t, sin_t, inner, cross, key, chunk = tables
    c = RET_CHUNK
    nh = N_RET_HEADS
    qk_w = nh * RET_QK
    v_w = nh * LANES
    full = lambda shape: pl.BlockSpec(shape, lambda bi: (0,) * len(shape))
    return pl.pallas_call(
        _retention_body,
        grid=(b,),
        in_specs=[
            pl.BlockSpec((1, s, qk_w), lambda bi: (bi, 0, COL_RQ * LANES // qk_w)),
            pl.BlockSpec((1, s, qk_w), lambda bi: (bi, 0, COL_RK * LANES // qk_w)),
            pl.BlockSpec((1, s, v_w), lambda bi: (bi, 0, COL_RV * LANES // v_w)),
            pl.BlockSpec((1, s, v_w), lambda bi: (bi, 0, COL_RG * LANES // v_w)),
            full((s, LANES)), full((s, LANES)),
            full((nh, c, c)), full((nh, c, LANES)), full((nh, c, LANES)), full((nh, 1, LANES)),
            full((1, v_w)),
        ],
        out_specs=pl.BlockSpec((1, s, v_w), lambda bi: (bi, 0, 0)),
        out_shape=jax.ShapeDtypeStruct((b, s, v_w), BF16),
        scratch_shapes=[pltpu.VMEM((nh, LANES, LANES), F32)],
        compiler_params=_params(("arbitrary",)),
        name="retention",
    )(proj3, proj3, proj3, proj3, cos_t, sin_t, inner, cross, key, chunk, gn_g)


def _route(logits):
    lane = lax.broadcasted_iota(jnp.int32, logits.shape, 1).astype(F32)
    big = float(LANES)
    is_group = lane < N_GROUPS
    gl = jnp.where(is_group, logits, -jnp.inf)
    gmax = jnp.max(gl, axis=-1, keepdims=True)
    gidx = jnp.min(jnp.where(gl == gmax, lane, big), axis=-1, keepdims=True)
    p_group = 1.0 / jnp.sum(jnp.where(is_group, jnp.exp(gl - gmax), 0.0), axis=-1, keepdims=True)
    lo = N_GROUPS + EXPERTS_PER_GROUP * gidx
    il = jnp.where(lane >= lo, jnp.where(lane < lo + EXPERTS_PER_GROUP, logits, -jnp.inf), -jnp.inf)
    v1 = jnp.max(il, axis=-1, keepdims=True)
    i1 = jnp.min(jnp.where(il == v1, lane, big), axis=-1, keepdims=True)
    il2 = jnp.where(lane == i1, -jnp.inf, il)
    v2 = jnp.max(il2, axis=-1, keepdims=True)
    i2 = jnp.min(jnp.where(il2 == v2, lane, big), axis=-1, keepdims=True)
    e2 = jnp.exp(v2 - v1)
    inv = 1.0 / (1.0 + e2)
    g1 = p_group * inv
    g2 = p_group * (e2 * inv)
    out = jnp.where(lane == 0.0, i1 - N_GROUPS,
                    jnp.where(lane == 1.0, i2 - N_GROUPS,
                              jnp.where(lane == 2.0, g1, jnp.where(lane == 3.0, g2, 0.0))))
    return out


def _outproj_body(a_ref, r_ref, x_ref, w_ref, g_ref, wr_ref, br_ref, x1_ref, h2_ref, route_ref, *, sub):
    tm = x_ref.shape[0]
    half = a_ref.shape[1]
    n_sub = tm // sub

    def project(i):
        rows = slice(i * sub, (i + 1) * sub)
        acc = jnp.dot(a_ref[rows, :], w_ref[:half, :], preferred_element_type=F32)
        return acc + jnp.dot(r_ref[rows, :], w_ref[half:, :], preferred_element_type=F32)

    def epilogue(i, acc):
        rows = slice(i * sub, (i + 1) * sub)
        x1 = x_ref[rows, :] + acc
        x1_ref[rows, :] = x1
        ms = jnp.mean(x1 * x1, axis=-1, keepdims=True)
        h2 = (x1 * lax.rsqrt(ms + EPS)) * g_ref[...]
        h2_ref[rows, :] = _pack_halves(h2)
        logits = jnp.dot(h2.astype(BF16), wr_ref[...], preferred_element_type=F32) + br_ref[...]
        route_ref[rows, :] = _route(logits)

    acc = project(0)
    for i in range(n_sub):
        nxt = project(i + 1) if i + 1 < n_sub else None
        epilogue(i, acc)
        acc = nxt


def _outproj_router(a2d, r2d, x2d, w_out_bf16, g, wr_bf16, br, *, tm):
    n, d = x2d.shape
    half = a2d.shape[1]
    row = lambda width: pl.BlockSpec((tm, width), lambda i: (i, 0))
    full = lambda shape: pl.BlockSpec(shape, lambda i: (0, 0))
    return pl.pallas_call(
        functools.partial(_outproj_body, sub=min(tm, 2 * LANES)),
        grid=(n // tm,),
        in_specs=[row(half), row(half), row(d), full((2 * half, d)), full((1, d)), full((d, LANES)),
                  full((1, LANES))],
        out_specs=[row(d), row(d // 2), row(LANES)],
        out_shape=[jax.ShapeDtypeStruct((n, d), F32), jax.ShapeDtypeStruct((n, d // 2), jnp.uint32),
                   jax.ShapeDtypeStruct((n, LANES), F32)],
        compiler_params=_params(("arbitrary",)),
        name="outproj_router",
    )(a2d, r2d, x2d, w_out_bf16, g, wr_bf16, br)


def _moe_plan(expert_id, tm):
    n_slots = expert_id.shape[0]
    n_tiles = n_slots // tm
    order = jnp.argsort(expert_id).astype(jnp.int32)
    sizes = jnp.zeros((N_EXPERTS,), jnp.int32).at[expert_id].add(1)
    ends = jnp.cumsum(sizes)
    starts = ends - sizes
    tile_row0 = jnp.arange(n_tiles, dtype=jnp.int32) * tm
    owner = lambda row: jnp.sum((ends[None, :] <= row[:, None]).astype(jnp.int32), axis=1)
    first_e = owner(tile_row0)
    last_e = owner(tile_row0 + (tm - 1))
    owners_upto = jnp.cumsum((sizes > 0).astype(jnp.int32))
    n_extra = (owners_upto[last_e] - owners_upto[first_e]).astype(jnp.int32)
    prev_last = jnp.concatenate([jnp.full((1,), -1, jnp.int32), last_e[:-1]])
    switch = (prev_last != first_e).astype(jnp.int32)
    eid = jnp.arange(N_EXPERTS, dtype=jnp.int32)
    cand = jnp.where((sizes[None, :] > 0) & (eid[None, :] > eid[:, None]), eid[None, :], N_EXPERTS)
    nxt = jnp.min(cand, axis=1)
    next_expert = jnp.where(nxt < N_EXPERTS, nxt, -1).astype(jnp.int32)
    spare = 2 * (n_slots + jnp.arange(tm, dtype=jnp.int32))
    codes = jnp.concatenate([order.reshape(n_tiles, tm), spare[None, :]], axis=0).reshape(n_tiles + 1, 1, tm)
    return dict(first_e=first_e, n_extra=n_extra, switch=switch, starts=starts.astype(jnp.int32),
                ends=ends.astype(jnp.int32), next_expert=next_expert, codes=codes)


def _moe_body(fe_ref, nextra_ref, sw_ref, st_ref, en_ref, nx_ref,
              codes_hbm, h2_hbm, wg_hbm, wu_hbm, wd_hbm, ys_hbm,
              gidx, sidx, xbuf, obuf, wg_st, wu_st, wd_st, wg_rs, wu_rs, wd_rs,
              gisem, sisem, gsem, ssem, wsem, *, n_tokens, n_tiles, cast_rows, ff_chunks):
    t = pl.program_id(0)
    tm = xbuf.shape[1]
    last = n_tiles - 1
    spare_codes = n_tiles

    def gidx_copy(tile, slot):
        return pltpu.make_async_copy(codes_hbm.at[tile], gidx.at[slot], gisem.at[slot])

    def sidx_copy(tile, slot):
        return pltpu.make_async_copy(codes_hbm.at[tile], sidx.at[slot], sisem.at[slot])

    def weight_copies(ex):
        return (pltpu.make_async_copy(wg_hbm.at[ex], wg_st, wsem.at[0]),
                pltpu.make_async_copy(wu_hbm.at[ex], wu_st, wsem.at[1]),
                pltpu.make_async_copy(wd_hbm.at[ex], wd_st, wsem.at[2]))

    def gather_row(slot, r):
        tok = lax.shift_right_logical(gidx[slot, 0, r], 1)
        return pltpu.make_async_copy(h2_hbm.at[pl.ds(tok, 1)], xbuf.at[slot, pl.ds(r, 1)], gsem.at[slot])

    def scatter_row(slot, r):
        code = sidx[slot, 0, r]
        dst = (code & 1) * n_tokens + lax.shift_right_logical(code, 1)
        return pltpu.make_async_copy(obuf.at[slot, pl.ds(r, 1)], ys_hbm.at[pl.ds(dst, 1)], ssem.at[slot])

    def wait_gather(slot):
        pltpu.make_async_copy(h2_hbm.at[pl.ds(0, tm)], xbuf.at[slot], gsem.at[slot]).wait()

    def wait_scatter(slot):
        pltpu.make_async_copy(obuf.at[slot], ys_hbm.at[pl.ds(0, tm)], ssem.at[slot]).wait()

    def for_rows(fn):
        def body(r, carry):
            fn(r)
            return carry
        lax.fori_loop(0, tm, body, 0, unroll=8)

    def take_over(ex, first):
        @pl.when(first)
        def _():
            for cp in weight_copies(ex):
                cp.start()

        for cp in weight_copies(ex):
            cp.wait()

        def cast_up(i, carry):
            r = pl.multiple_of(i * cast_rows, cast_rows)
            wg_rs[pl.ds(r, cast_rows), :] = wg_st[pl.ds(r, cast_rows), :].astype(BF16)
            wu_rs[pl.ds(r, cast_rows), :] = wu_st[pl.ds(r, cast_rows), :].astype(BF16)
            return carry

        def cast_down(i, carry):
            r = pl.multiple_of(i * cast_rows, cast_rows)
            wd_rs[pl.ds(r, cast_rows), :] = wd_st[pl.ds(r, cast_rows), :].astype(BF16)
            return carry

        lax.fori_loop(0, wg_st.shape[0] // cast_rows, cast_up, 0)
        lax.fori_loop(0, wd_st.shape[0] // cast_rows, cast_down, 0)
        nxt = nx_ref[ex]

        @pl.when(nxt >= 0)
        def _():
            for cp in weight_copies(nxt):
                cp.start()

    def ffn(slot, ex, accumulate, row_dmas=None):
        x_lo, x_hi = (h.astype(BF16) for h in _unpack_halves(xbuf[slot]))
        hd = x_lo.shape[1]
        row = t * tm + lax.broadcasted_iota(jnp.int32, (tm, 1), 0)
        mine = jnp.logical_and(row >= st_ref[ex], row < en_ref[ex])
        d_ff = wg_rs.shape[1]
        n_chunks = ff_chunks if row_dmas is not None else 1
        fc = d_ff // n_chunks
        rows = tm // n_chunks
        contrib = None
        for c in range(n_chunks):
            cols = slice(c * fc, (c + 1) * fc)
            up = lambda w: (jnp.dot(x_lo, w[:hd, cols], preferred_element_type=F32)
                            + jnp.dot(x_hi, w[hd:, cols], preferred_element_type=F32))
            a = up(wg_rs)
            b = up(wu_rs)
            hmid = jnp.where(mine, (a * (1.0 / (1.0 + jnp.exp(-a)))) * b, 0.0).astype(BF16)
            part = jnp.dot(hmid, wd_rs[cols, :], preferred_element_type=F32)
            contrib = part if contrib is None else contrib + part
            if row_dmas is not None:
                row_dmas(c * rows, (c + 1) * rows)
        if accumulate:
            prev_lo, prev_hi = _unpack_halves(obuf[slot])
            contrib = jnp.concatenate([prev_lo, prev_hi], axis=1) + contrib
        obuf[slot] = _pack_halves(contrib)

    e0 = fe_ref[t]

    def step(p):
        q = 1 - p
        if p == 0:
            @pl.when(t == 0)
            def _():
                copies = (gidx_copy(0, 0), gidx_copy(min(1, last), 1), sidx_copy(spare_codes, 1))
                for cp in copies:
                    cp.start()
                for cp in copies:
                    cp.wait()
                obuf[1] = jnp.zeros(obuf.shape[1:], obuf.dtype)
                for_rows(lambda r: gather_row(0, r).start())

        gidx_copy(jnp.minimum(t + 2, last), p).start()
        sidx_copy(t, p).start()

        @pl.when(t >= 1)
        def _():
            gidx_copy(0, q).wait()
            sidx_copy(0, q).wait()
            wait_scatter(p)

        wait_gather(p)

        @pl.when(sw_ref[t] == 1)
        def _():
            take_over(e0, t == 0)

        def row_dmas(lo, hi):
            for r in range(lo, hi):
                scatter_row(q, r).start()
                gather_row(q, r).start()

        ffn(p, e0, accumulate=False, row_dmas=row_dmas)

    for parity in range(2):
        pl.when(lax.rem(t, 2) == parity)(functools.partial(step, parity))

    slot = lax.rem(t, 2)

    def further_owner(_, ex):
        nxt = nx_ref[ex]
        take_over(nxt, False)
        ffn(slot, nxt, accumulate=True)
        return nxt

    lax.fori_loop(0, nextra_ref[t], further_owner, e0)

    @pl.when(t == last)
    def _():
        gidx_copy(0, slot).wait()
        sidx_copy(0, slot).wait()
        for_rows(lambda r: scatter_row(slot, r).start())
        wait_gather(1 - slot)
        wait_scatter(1 - slot)
        wait_scatter(slot)


def _moe_ffn(h2, plan, w_gate, w_up, w_down, *, tm):
    n, hd = h2.shape
    d = 2 * hd
    n_slots = 2 * n
    n_tiles = n_slots // tm
    assert n_tiles >= 2 and n_slots % tm == 0
    d_ff = w_gate.shape[2]
    any_spec = pl.BlockSpec(memory_space=pl.ANY)
    grid_spec = pltpu.PrefetchScalarGridSpec(
        num_scalar_prefetch=6,
        grid=(n_tiles,),
        in_specs=[any_spec] * 5,
        out_specs=any_spec,
        scratch_shapes=[
            pltpu.SMEM((2, 1, tm), jnp.int32),
            pltpu.SMEM((2, 1, tm), jnp.int32),
            pltpu.VMEM((2, tm, hd), jnp.uint32),
            pltpu.VMEM((2, tm, hd), jnp.uint32),
            pltpu.VMEM((d, d_ff), F32),
            pltpu.VMEM((d, d_ff), F32),
            pltpu.VMEM((d_ff, d), F32),
            pltpu.VMEM((d, d_ff), BF16),
            pltpu.VMEM((d, d_ff), BF16),
            pltpu.VMEM((d_ff, d), BF16),
            pltpu.SemaphoreType.DMA((2,)),
            pltpu.SemaphoreType.DMA((2,)),
            pltpu.SemaphoreType.DMA((2,)),
            pltpu.SemaphoreType.DMA((2,)),
            pltpu.SemaphoreType.DMA((3,)),
        ],
    )
    return pl.pallas_call(
        functools.partial(_moe_body, n_tokens=n, n_tiles=n_tiles, cast_rows=128,
                          ff_chunks=d_ff // (2 * LANES)),
        grid_spec=grid_spec,
        out_shape=jax.ShapeDtypeStruct((n_slots + tm, hd), jnp.uint32),
        compiler_params=_params(("arbitrary",), vmem=MOE_VMEM_LIMIT),
        name="moe_ffn",
    )(plan["first_e"], plan["n_extra"], plan["switch"], plan["starts"], plan["ends"], plan["next_expert"],
      plan["codes"], h2, w_gate, w_up, w_down)


def _final_body(x1_ref, ys0_ref, ys1_ref, route_ref, g_ref, o_ref):
    d = x1_ref.shape[1]
    route = route_ref[...]
    g1, g2 = route[:, 2:3], route[:, 3:4]
    ya = _unpack_halves(ys0_ref[...])
    yb = _unpack_halves(ys1_ref[...])
    halves = (slice(0, d // 2), slice(d // 2, d))
    x2 = [x1_ref[:, cols] + (ya[i] * g1 + yb[i] * g2) for i, cols in enumerate(halves)]
    ms = sum(jnp.sum(h * h, axis=-1, keepdims=True) for h in x2) * (1.0 / d)
    inv = lax.rsqrt(ms + EPS)
    for h, cols in zip(x2, halves):
        o_ref[:, cols] = ((h * inv) * g_ref[:, cols]).astype(o_ref.dtype)


def _final(x1, ys, route, g, *, tm):
    n, d = x1.shape
    nb = n // tm
    return pl.pallas_call(
        _final_body,
        grid=(nb,),
        in_specs=[
            pl.BlockSpec((tm, d), lambda i: (i, 0)),
            pl.BlockSpec((tm, d // 2), lambda i: (i, 0)),
            pl.BlockSpec((tm, d // 2), lambda i: (nb + i, 0)),
            pl.BlockSpec((tm, LANES), lambda i: (i, 0)),
            pl.BlockSpec((1, d), lambda i: (0, 0)),
        ],
        out_specs=pl.BlockSpec((tm, d), lambda i: (i, 0)),
        out_shape=jax.ShapeDtypeStruct((n, d), F32),
        compiler_params=_params(("arbitrary",)),
        name="final_norm",
    )(x1, ys, ys, route, g)


def _layer(x, rel_bias, attn_norm_g, w_in, lq1, lk1, lq2, lk2, diff_subln_g, ret_gn_g, w_out, ffn_norm_g,
           w_group_router, b_group, w_inner_router, b_inner, w_gate_exp, w_up_exp, w_down_exp, *, layer,
           attn_tile, row_tile, moe_tile):
    b, s, d = x.shape
    n = b * s
    x2d = x.reshape(n, d)
    lam_init = 0.8 - 0.6 * math.exp(-0.3 * layer)

    dq_cols = (COL_DK - COL_DQ) * LANES
    col_scale = jnp.concatenate([jnp.full((1, dq_cols), DIFF_HALF ** -0.5 * LOG2E, F32),
                                 jnp.ones((1, IN_COLS - dq_cols), F32)], axis=1)
    proj = _inproj(x2d, attn_norm_g.reshape(1, d), w_in.astype(BF16), col_scale, tm=min(2 * row_tile, n),
                   tn=8 * LANES)
    proj3 = proj.reshape(b, s, IN_COLS)

    bias = _bias_tiles(rel_bias, attn_tile)
    vec = lambda a: a.reshape(1, -1)
    a_out = _diff_attention(proj3, bias, vec(lq1), vec(lk1), vec(lq2), vec(lk2), vec(diff_subln_g),
                            t=attn_tile, lam_init=lam_init)
    r_out = _retention(proj3, _retention_tables(s), vec(ret_gn_g))

    pad = LANES - N_GROUPS - N_EXPERTS
    wr = jnp.concatenate([w_group_router, jnp.transpose(w_inner_router, (1, 0, 2)).reshape(d, N_EXPERTS),
                          jnp.zeros((d, pad), F32)], axis=1).astype(BF16)
    br = jnp.concatenate([b_group, b_inner.reshape(-1), jnp.zeros((pad,), F32)]).reshape(1, LANES)
    x1, h2, route = _outproj_router(a_out.reshape(n, -1), r_out.reshape(n, -1), x2d, w_out.astype(BF16),
                                    vec(ffn_norm_g), wr, br, tm=row_tile)

    expert_id = route[:, :2].astype(jnp.int32).reshape(-1)
    plan = _moe_plan(expert_id, moe_tile)
    ys = _moe_ffn(h2, plan, w_gate_exp, w_up_exp, w_down_exp, tm=moe_tile)
    return x1, ys, route


def kernel(x, rel_bias, attn_norm_g, w_in, lambda_q1, lambda_k1, lambda_q2, lambda_k2, diff_subln_g, ret_gn_g,
           w_out, ffn_norm_g, w_group_router, b_group, w_inner_router, b_inner, w_gate_exp, w_up_exp, w_down_exp,
           final_g):
    b, s, d = x.shape
    depth = w_in.shape[0]
    assert depth == 1, "the final norm is fused into the single layer's combine step"
    first = lambda a: a.reshape(a.shape[1:])
    x1, ys, route = _layer(x, rel_bias, attn_norm_g[0], w_in[0], lambda_q1[0], lambda_k1[0], lambda_q2[0],
                           lambda_k2[0], diff_subln_g[0], ret_gn_g[0], w_out[0], ffn_norm_g[0],
                           w_group_router[0], b_group[0], w_inner_router[0], b_inner[0], first(w_gate_exp),
                           first(w_up_exp), first(w_down_exp), layer=0, attn_tile=min(512, s), row_tile=512,
                           moe_tile=256)
    out = _final(x1, ys, route, final_g.reshape(1, d), tm=256)
    return out.reshape(b, s, d)
```

```python
import functools
import math

import numpy as np
import jax
import jax.numpy as jnp
from jax import lax
from jax.experimental import pallas as pl
from jax.experimental.pallas import tpu as pltpu

F32 = jnp.float32
BF16 = jnp.bfloat16

EPS = 1e-6
LANES = 128
N_DIFF_HEADS = 8
DIFF_HALF = 64
N_RET_HEADS = 8
RET_QK = 64
RET_CHUNK = 128
ROPE_BASE = 10000.0
N_BUCKETS = 32
MAX_DISTANCE = 128
N_GROUPS = 4
EXPERTS_PER_GROUP = 8
N_EXPERTS = N_GROUPS * EXPERTS_PER_GROUP
MASK_VALUE = -1e30
LOG2E = math.log2(math.e)
VMEM_LIMIT = 56 * 1024 * 1024
MOE_VMEM_LIMIT = 60 * 1024 * 1024

COL_DQ, COL_DK, COL_DV = 0, 8, 16
COL_RQ, COL_RK, COL_RV, COL_RG = 24, 28, 32, 40
IN_COLS = 48 * LANES


def _params(sem, vmem=VMEM_LIMIT):
    return pltpu.CompilerParams(dimension_semantics=sem, vmem_limit_bytes=vmem)


def _pack_halves(x):
    hd = x.shape[1] // 2
    bits = lambda h: lax.bitcast_convert_type(h.astype(BF16).astype(F32), jnp.uint32)
    return lax.shift_right_logical(bits(x[:, :hd]), jnp.uint32(16)) | bits(x[:, hd:])


def _unpack_halves(w):
    return (lax.bitcast_convert_type(lax.shift_left(w, jnp.uint32(16)), F32),
            lax.bitcast_convert_type(w & jnp.uint32(0xFFFF0000), F32))


def _inproj_body(x_ref, g_ref, w_ref, cs_ref, o_ref):
    x = x_ref[...]
    acc = jnp.dot((x * g_ref[...]).astype(BF16), w_ref[...], preferred_element_type=F32)
    inv = lax.rsqrt(jnp.mean(x * x, axis=-1, keepdims=True) + EPS)
    o_ref[...] = ((acc * inv) * cs_ref[...]).astype(o_ref.dtype)


def _inproj(x2d, g, w, col_scale, *, tm, tn):
    n, d = x2d.shape
    cols = w.shape[1]
    return pl.pallas_call(
        _inproj_body,
        grid=(n // tm, cols // tn),
        in_specs=[
            pl.BlockSpec((tm, d), lambda i, j: (i, 0)),
            pl.BlockSpec((1, d), lambda i, j: (0, 0)),
            pl.BlockSpec((d, tn), lambda i, j: (0, j)),
            pl.BlockSpec((1, tn), lambda i, j: (0, j)),
        ],
        out_specs=pl.BlockSpec((tm, tn), lambda i, j: (i, j)),
        out_shape=jax.ShapeDtypeStruct((n, cols), BF16),
        compiler_params=_params(("arbitrary", "arbitrary")),
        name="inproj",
    )(x2d, g, w, col_scale)


def _t5_bucket_table(n):
    d = np.arange(n)
    max_exact = N_BUCKETS // 2
    log_ratio = np.log(np.maximum(d, 1).astype(np.float64) / max_exact) / math.log(MAX_DISTANCE / max_exact)
    large = np.minimum(max_exact + (log_ratio * (N_BUCKETS - max_exact)).astype(np.int64), N_BUCKETS - 1)
    return np.where(d < max_exact, d, large).astype(np.int32)


def _bias_body(rb_ref, bkt_ref, o_ref):
    h = pl.program_id(0)
    t = o_ref.shape[2]
    bkt = bkt_ref[...]
    far = rb_ref[N_BUCKETS - 1, h]
    by_dist = jnp.zeros(bkt.shape, F32)
    for b in range(N_BUCKETS - 1):
        by_dist = jnp.where(bkt == b, (rb_ref[b, h] - far) * LOG2E, by_dist)
    rows = (jnp.concatenate([jnp.full((1, t), MASK_VALUE, F32), by_dist[:, :t]], axis=1), by_dist)
    for i, row in enumerate(rows):
        shifted = pltpu.roll(jnp.broadcast_to(row, (t, 2 * t)), 0, 1, stride=1, stride_axis=0)
        o_ref[0, i] = shifted[:, t:]


def _bias_tiles(rel_bias, t):
    bkt = jnp.asarray(_t5_bucket_table(2 * t)).reshape(1, 2 * t)
    return pl.pallas_call(
        _bias_body,
        grid=(N_DIFF_HEADS,),
        in_specs=[
            pl.BlockSpec(memory_space=pltpu.SMEM),
            pl.BlockSpec((1, 2 * t), lambda h: (0, 0)),
        ],
        out_specs=pl.BlockSpec((1, 2, t, t), lambda h: (h, 0, 0, 0)),
        out_shape=jax.ShapeDtypeStruct((N_DIFF_HEADS, 2, t, t), F32),
        compiler_params=_params(("arbitrary",)),
        name="bias_tiles",
    )(rel_bias, bkt)


_NT = (((1,), (1,)), ((), ()))
_TN = (((0,), (0,)), ((), ()))


def _diffattn_body(q_ref, k_ref, v_ref, bias_ref, lq1_ref, lk1_ref, lq2_ref, lk2_ref, g_ref, o_ref, vt_ref, *,
                   t, lam_init, near_band):
    nq = k_ref.shape[1] // t
    half = t // 2
    lane = lax.broadcasted_iota(jnp.int32, (t, LANES), 1)
    lam = (jnp.exp(jnp.sum(lq1_ref[...] * lk1_ref[...], axis=-1, keepdims=True))
           - jnp.exp(jnp.sum(lq2_ref[...] * lk2_ref[...], axis=-1, keepdims=True)) + lam_init)
    vt_ref[:LANES, :] = v_ref[0].T
    vt_ref[LANES:, :] = jnp.ones((vt_ref.shape[0] - LANES, vt_ref.shape[1]), BF16)
    qmask = {}
    state = {}

    def scores(item):
        qi, j, part = item
        if qi not in qmask:
            q = q_ref[0, qi * t:(qi + 1) * t, :]
            zero = jnp.zeros_like(q)
            qmask[qi] = (jnp.where(lane < DIFF_HALF, q, zero), jnp.where(lane >= DIFF_HALF, q, zero))
        k_lo = j * t + (half if part == "diag_lower" else 0)
        k_hi = j * t + (half if part == "diag_upper" else t)
        kb = k_ref[0, k_lo:k_hi, :]
        out = []
        for c in range(2):
            qsel = qmask[qi][c][half:] if part == "diag_lower" else qmask[qi][c]
            s = lax.dot_general(kb, qsel, _NT, preferred_element_type=F32)
            if part == "diag_upper":
                s = s + bias_ref[0, 0, :half, :]
            elif part == "diag_lower":
                s = s + bias_ref[0, 0, half:, half:]
            elif j == qi - 1:
                if near_band >= t:
                    s = s + bias_ref[0, 1]
                else:
                    corner = s[t - near_band:, :near_band] + bias_ref[0, 1, t - near_band:, :near_band]
                    bottom = jnp.concatenate([corner, s[t - near_band:, near_band:]], axis=1)
                    s = jnp.concatenate([s[:t - near_band], bottom], axis=0)
            out.append(s)
        return out

    def softmax_step(item, ss):
        qi, j, part = item
        first = j == 0 and part != "diag_lower"
        ps, alphas = [], []
        for c in range(2):
            s = ss[c]
            smax = jnp.max(s, axis=0, keepdims=True)
            if first:
                m_new, alpha = smax, None
            else:
                m_old = state[qi, c]["m"]
                m_old = m_old[:, half:] if part == "diag_lower" else m_old
                m_new = jnp.maximum(m_old, smax)
                alpha = jnp.exp2(m_old - m_new)
            ps.append(jnp.exp2(s - m_new).astype(BF16))
            alphas.append((m_new, alpha))
        return ps, alphas

    def values_step(item, ps, alphas):
        qi, j, part = item
        k_lo = j * t + (half if part == "diag_lower" else 0)
        k_hi = j * t + (half if part == "diag_upper" else t)
        vtb = vt_ref[:, k_lo:k_hi]
        for c in range(2):
            m_new, alpha = alphas[c]
            r = jnp.dot(vtb, ps[c], preferred_element_type=F32)
            if alpha is None:
                state[qi, c] = dict(m=m_new, acc=r)
            elif part == "diag_lower":
                st = state[qi, c]
                st["m"] = jnp.concatenate([st["m"][:, :half], m_new], axis=1)
                st["acc"] = jnp.concatenate([st["acc"][:, :half], alpha * st["acc"][:, half:] + r], axis=1)
            else:
                st = state[qi, c]
                st["m"] = m_new
                st["acc"] = alpha * st["acc"] + r
        if part == "diag_lower":
            outs = []
            for c in range(2):
                acc = state.pop((qi, c))["acc"]
                outs.append(acc[:LANES] * (1.0 / acc[LANES:LANES + 1]))
            ot = (outs[0] - lam * outs[1]).T
            ms = jnp.mean(ot * ot, axis=-1, keepdims=True)
            o_ref[0, qi * t:(qi + 1) * t, :] = (
                ((ot * lax.rsqrt(ms + EPS)) * g_ref[...]) * (1.0 - lam_init)).astype(o_ref.dtype)
            del qmask[qi]

    items = []
    for qi in range(nq):
        items += [(qi, j, "full") for j in range(qi)] + [(qi, qi, "diag_upper"), (qi, qi, "diag_lower")]
    ss = scores(items[0])
    for n, item in enumerate(items):
        nxt = scores(items[n + 1]) if n + 1 < len(items) else None
        ps, alphas = softmax_step(item, ss)
        values_step(item, ps, alphas)
        ss = nxt


def _diff_attention(proj3, bias, lq1, lk1, lq2, lk2, subln_g, *, t, lam_init):
    b, s, _ = proj3.shape
    vec = pl.BlockSpec((1, DIFF_HALF), lambda bi, h: (0, 0))
    head = lambda col: pl.BlockSpec((1, s, LANES), lambda bi, h: (bi, 0, col + h))
    first_far = int(np.argmax(_t5_bucket_table(2 * t) == N_BUCKETS - 1))
    near_band = -(-first_far // LANES) * LANES
    ones_rows = 16
    return pl.pallas_call(
        functools.partial(_diffattn_body, t=t, lam_init=lam_init, near_band=near_band),
        grid=(b, N_DIFF_HEADS),
        in_specs=[
            head(COL_DQ), head(COL_DK), head(COL_DV),
            pl.BlockSpec((1, 2, t, t), lambda bi, h: (h, 0, 0, 0)),
            vec, vec, vec, vec,
            pl.BlockSpec((1, LANES), lambda bi, h: (0, 0)),
        ],
        out_specs=pl.BlockSpec((1, s, LANES), lambda bi, h: (bi, 0, h)),
        out_shape=jax.ShapeDtypeStruct((b, s, N_DIFF_HEADS * LANES), BF16),
        scratch_shapes=[pltpu.VMEM((LANES + ones_rows, s), BF16)],
        compiler_params=_params(("arbitrary", "arbitrary")),
        name="diff_attention",
    )(proj3, proj3, proj3, bias, lq1, lk1, lq2, lk2, subln_g)


def _retention_tables(s):
    c = RET_CHUNK
    half = RET_QK // 2
    inv_freq = ROPE_BASE ** (-jnp.arange(0, RET_QK, 2, dtype=F32) / RET_QK)
    ang = jnp.arange(s, dtype=F32)[:, None] * inv_freq[None, :]
    cos, sin = jnp.cos(ang), jnp.sin(ang)
    cos_t = jnp.tile(cos, (1, LANES // half))
    sin_t = jnp.tile(jnp.concatenate([-sin, sin], axis=1), (1, LANES // RET_QK))
    log_g = jnp.log(1.0 - jnp.exp2(-5.0 - jnp.arange(N_RET_HEADS, dtype=F32)))
    n = jnp.arange(c, dtype=F32)
    diff = n[:, None] - n[None, :]
    inner = jnp.where(diff[None] >= 0, jnp.exp(jnp.maximum(diff, 0.0)[None] * log_g[:, None, None]), 0.0)
    cross = jnp.exp((n[None] + 1.0) * log_g[:, None])
    key = jnp.exp((c - 1.0 - n[None]) * log_g[:, None])
    chunk = jnp.exp(c * log_g)
    bc = lambda a: jnp.broadcast_to(a[..., None], a.shape + (LANES,)).astype(F32)
    return cos_t, sin_t, inner.astype(F32), bc(cross), bc(key), bc(chunk[:, None])


def _retention_body(rq_ref, rk_ref, rv_ref, rg_ref, cos_ref, sin_ref, inner_ref, cross_ref, key_ref,
                    chunk_ref, gn_ref, o_ref, state_ref):
    c = RET_CHUNK
    nc = rq_ref.shape[1] // c
    nh = N_RET_HEADS
    state_ref[...] = jnp.zeros(state_ref.shape, F32)
    lane = lax.broadcasted_iota(jnp.int32, (c, LANES), 1)
    first_half = (lane % RET_QK) < (RET_QK // 2)
    head_mask = [lane < RET_QK, lane >= RET_QK]
    slab = lambda i: slice(i * LANES, (i + 1) * LANES)

    def rotary(x, cos, sin):
        rot = jnp.where(first_half, pltpu.roll(x, LANES - RET_QK // 2, 1), pltpu.roll(x, RET_QK // 2, 1))
        return x * cos + rot * sin

    def chunk_step(ci, carry):
        r = pl.multiple_of(ci * c, c)
        rows = pl.ds(r, c)
        cos = cos_ref[rows, :]
        sin = sin_ref[rows, :]
        q_pair, k_pair, kb_pair = [], [], []
        for pr in range(nh // 2):
            q_pair.append(rotary(rq_ref[0, rows, slab(pr)].astype(F32), cos, sin))
            k = rotary(rk_ref[0, rows, slab(pr)].astype(F32), cos, sin) * (RET_QK ** -0.5)
            k_pair.append(k)
            kb_pair.append(k.astype(BF16))
        qh = [jnp.where(head_mask[h % 2], q_pair[h // 2], 0.0).astype(BF16) for h in range(nh)]
        kd = [(k_pair[h // 2] * key_ref[h]).astype(BF16) for h in range(nh)]
        vh = [rv_ref[0, rows, slab(h)] for h in range(nh)]
        scores = [lax.dot_general(qh[h], kb_pair[h // 2], _NT, preferred_element_type=F32) for h in range(nh)]
        cross = [jnp.dot(qh[h], state_ref[h].astype(BF16), preferred_element_type=F32) for h in range(nh)]
        kv = [lax.dot_general(kd[h], vh[h], _TN, preferred_element_type=F32) for h in range(nh)]
        decayed = [(scores[h] * inner_ref[h]).astype(BF16) for h in range(nh)]
        for h in range(nh):
            state_ref[h] = state_ref[h] * chunk_ref[h] + kv[h]
        inner = [jnp.dot(decayed[h], vh[h], preferred_element_type=F32) for h in range(nh)]
        for h in range(nh):
            y = inner[h] + cross[h] * cross_ref[h]
            mu = jnp.mean(y, axis=-1, keepdims=True)
            yc = y - mu
            var = jnp.mean(yc * yc, axis=-1, keepdims=True)
            yn = (yc * lax.rsqrt(var + EPS)) * gn_ref[:, slab(h)]
            gate = rg_ref[0, rows, slab(h)].astype(F32)
            silu = gate * (1.0 / (1.0 + jnp.exp(-gate)))
            o_ref[0, rows, slab(h)] = (silu * yn).astype(o_ref.dtype)
        return carry

    lax.fori_loop(0, nc, chunk_step, 0)


def _retention(proj3, tables, gn_g):
    b, s, _ = proj3.shape
    cos_t, sin_t, inner, cross, key, chunk = tables
    c = RET_CHUNK
    nh = N_RET_HEADS
    qk_w = nh * RET_QK
    v_w = nh * LANES
    full = lambda shape: pl.BlockSpec(shape, lambda bi: (0,) * len(shape))
    return pl.pallas_call(
        _retention_body,
        grid=(b,),
        in_specs=[
            pl.BlockSpec((1, s, qk_w), lambda bi: (bi, 0, COL_RQ * LANES // qk_w)),
            pl.BlockSpec((1, s, qk_w), lambda bi: (bi, 0, COL_RK * LANES // qk_w)),
            pl.BlockSpec((1, s, v_w), lambda bi: (bi, 0, COL_RV * LANES // v_w)),
            pl.BlockSpec((1, s, v_w), lambda bi: (bi, 0, COL_RG * LANES // v_w)),
            full((s, LANES)), full((s, LANES)),
            full((nh, c, c)), full((nh, c, LANES)), full((nh, c, LANES)), full((nh, 1, LANES)),
            full((1, v_w)),
        ],
        out_specs=pl.BlockSpec((1, s, v_w), lambda bi: (bi, 0, 0)),
        out_shape=jax.ShapeDtypeStruct((b, s, v_w), BF16),
        scratch_shapes=[pltpu.VMEM((nh, LANES, LANES), F32)],
        compiler_params=_params(("arbitrary",)),
        name="retention",
    )(proj3, proj3, proj3, proj3, cos_t, sin_t, inner, cross, key, chunk, gn_g)


def _route(logits):
    lane = lax.broadcasted_iota(jnp.int32, logits.shape, 1).astype(F32)
    big = float(LANES)
    is_group = lane < N_GROUPS
    gl = jnp.where(is_group, logits, -jnp.inf)
    gmax = jnp.max(gl, axis=-1, keepdims=True)
    gidx = jnp.min(jnp.where(gl == gmax, lane, big), axis=-1, keepdims=True)
    p_group = 1.0 / jnp.sum(jnp.where(is_group, jnp.exp(gl - gmax), 0.0), axis=-1, keepdims=True)
    lo = N_GROUPS + EXPERTS_PER_GROUP * gidx
    il = jnp.where(lane >= lo, jnp.where(lane < lo + EXPERTS_PER_GROUP, logits, -jnp.inf), -jnp.inf)
    v1 = jnp.max(il, axis=-1, keepdims=True)
    i1 = jnp.min(jnp.where(il == v1, lane, big), axis=-1, keepdims=True)
    il2 = jnp.where(lane == i1, -jnp.inf, il)
    v2 = jnp.max(il2, axis=-1, keepdims=True)
    i2 = jnp.min(jnp.where(il2 == v2, lane, big), axis=-1, keepdims=True)
    e2 = jnp.exp(v2 - v1)
    inv = 1.0 / (1.0 + e2)
    g1 = p_group * inv
    g2 = p_group * (e2 * inv)
    out = jnp.where(lane == 0.0, i1 - N_GROUPS,
                    jnp.where(lane == 1.0, i2 - N_GROUPS,
                              jnp.where(lane == 2.0, g1, jnp.where(lane == 3.0, g2, 0.0))))
    return out


def _outproj_body(a_ref, r_ref, x_ref, w_ref, g_ref, wr_ref, br_ref, x1_ref, h2_ref, route_ref, *, sub):
    tm = x_ref.shape[0]
    half = a_ref.shape[1]
    n_sub = tm // sub

    def project(i):
        rows = slice(i * sub, (i + 1) * sub)
        acc = jnp.dot(a_ref[rows, :], w_ref[:half, :], preferred_element_type=F32)
        return acc + jnp.dot(r_ref[rows, :], w_ref[half:, :], preferred_element_type=F32)

    def epilogue(i, acc):
        rows = slice(i * sub, (i + 1) * sub)
        x1 = x_ref[rows, :] + acc
        x1_ref[rows, :] = x1
        ms = jnp.mean(x1 * x1, axis=-1, keepdims=True)
        h2 = (x1 * lax.rsqrt(ms + EPS)) * g_ref[...]
        h2_ref[rows, :] = _pack_halves(h2)
        logits = jnp.dot(h2.astype(BF16), wr_ref[...], preferred_element_type=F32) + br_ref[...]
        route_ref[rows, :] = _route(logits)

    acc = project(0)
    for i in range(n_sub):
        nxt = project(i + 1) if i + 1 < n_sub else None
        epilogue(i, acc)
        acc = nxt


def _outproj_router(a2d, r2d, x2d, w_out_bf16, g, wr_bf16, br, *, tm):
    n, d = x2d.shape
    half = a2d.shape[1]
    row = lambda width: pl.BlockSpec((tm, width), lambda i: (i, 0))
    full = lambda shape: pl.BlockSpec(shape, lambda i: (0, 0))
    return pl.pallas_call(
        functools.partial(_outproj_body, sub=min(tm, 2 * LANES)),
        grid=(n // tm,),
        in_specs=[row(half), row(half), row(d), full((2 * half, d)), full((1, d)), full((d, LANES)),
                  full((1, LANES))],
        out_specs=[row(d), row(d // 2), row(LANES)],
        out_shape=[jax.ShapeDtypeStruct((n, d), F32), jax.ShapeDtypeStruct((n, d // 2), jnp.uint32),
                   jax.ShapeDtypeStruct((n, LANES), F32)],
        compiler_params=_params(("arbitrary",)),
        name="outproj_router",
    )(a2d, r2d, x2d, w_out_bf16, g, wr_bf16, br)


def _moe_plan(expert_id, tm):
    n_slots = expert_id.shape[0]
    max_tiles = n_slots // tm + N_EXPERTS - 1
    order = jnp.argsort(expert_id).astype(jnp.int32)
    sizes = jnp.zeros((N_EXPERTS,), jnp.int32).at[expert_id].add(1)
    starts = jnp.cumsum(sizes) - sizes
    tiles_of = (sizes + (tm - 1)) // tm
    tile_end = jnp.cumsum(tiles_of)
    tile_start = tile_end - tiles_of
    n_used = tile_end[-1]
    tile = jnp.minimum(jnp.arange(max_tiles, dtype=jnp.int32), n_used - 1)
    tile_expert = jnp.sum((tile_end[None, :] <= tile[:, None]).astype(jnp.int32), axis=1)
    prev_expert = jnp.concatenate([jnp.full((1,), -1, jnp.int32), tile_expert[:-1]])
    switch = (prev_expert != tile_expert).astype(jnp.int32)
    eid = jnp.arange(N_EXPERTS, dtype=jnp.int32)
    cand = jnp.where((sizes[None, :] > 0) & (eid[None, :] > eid[:, None]), eid[None, :], N_EXPERTS)
    nxt = jnp.min(cand, axis=1)
    next_expert = jnp.where(nxt < N_EXPERTS, nxt, -1).astype(jnp.int32)
    pos = jnp.arange(max_tiles * tm, dtype=jnp.int32)
    e = tile_expert[pos // tm]
    r = pos - tile_start[e] * tm
    real = jnp.logical_and(pos // tm < n_used, r < sizes[e])
    spare = n_slots + jnp.arange(tm, dtype=jnp.int32)
    code = jnp.where(real, order[jnp.clip(starts[e] + r, 0, n_slots - 1)], n_slots + pos % tm)
    codes = jnp.concatenate([code.reshape(max_tiles, tm), spare[None, :]], axis=0).reshape(max_tiles + 1, 1, tm)
    return dict(tile_expert=tile_expert, switch=switch, n_used=n_used.reshape(1).astype(jnp.int32),
                next_expert=next_expert, codes=codes, max_tiles=max_tiles)


def _moe_body(te_ref, sw_ref, nu_ref, nx_ref,
              codes_hbm, h2_hbm, wg_hbm, wu_hbm, wd_hbm, ys_hbm,
              gidx, sidx, xbuf, obuf, wg_st, wu_st, wd_st, wg_rs, wu_rs, wd_rs,
              gisem, sisem, gsem, ssem, wsem, *, n_tokens, max_tiles, cast_rows, ff_chunks):
    t = pl.program_id(0)
    tm = xbuf.shape[1]
    last = nu_ref[0] - 1
    spare_codes = max_tiles

    def gidx_copy(tile, slot):
        return pltpu.make_async_copy(codes_hbm.at[tile], gidx.at[slot], gisem.at[slot])

    def sidx_copy(tile, slot):
        return pltpu.make_async_copy(codes_hbm.at[tile], sidx.at[slot], sisem.at[slot])

    def weight_copies(ex):
        return (pltpu.make_async_copy(wg_hbm.at[ex], wg_st, wsem.at[0]),
                pltpu.make_async_copy(wu_hbm.at[ex], wu_st, wsem.at[1]),
                pltpu.make_async_copy(wd_hbm.at[ex], wd_st, wsem.at[2]))

    def gather_row(slot, r):
        code = gidx[slot, 0, r]
        tok = jnp.where(code >= 2 * n_tokens, 0, jnp.where(code >= n_tokens, code - n_tokens, code))
        return pltpu.make_async_copy(h2_hbm.at[pl.ds(tok, 1)], xbuf.at[slot, pl.ds(r, 1)], gsem.at[slot])

    def scatter_row(slot, r):
        dst = sidx[slot, 0, r]
        return pltpu.make_async_copy(obuf.at[slot, pl.ds(r, 1)], ys_hbm.at[pl.ds(dst, 1)], ssem.at[slot])

    def wait_gather(slot):
        pltpu.make_async_copy(h2_hbm.at[pl.ds(0, tm)], xbuf.at[slot], gsem.at[slot]).wait()

    def wait_scatter(slot):
        pltpu.make_async_copy(obuf.at[slot], ys_hbm.at[pl.ds(0, tm)], ssem.at[slot]).wait()

    def for_rows(fn):
        def body(r, carry):
            fn(r)
            return carry
        lax.fori_loop(0, tm, body, 0, unroll=8)

    def take_over(ex, first):
        @pl.when(first)
        def _():
            for cp in weight_copies(ex):
                cp.start()

        for cp in weight_copies(ex):
            cp.wait()

        def cast_up(i, carry):
            r = pl.multiple_of(i * cast_rows, cast_rows)
            wg_rs[pl.ds(r, cast_rows), :] = wg_st[pl.ds(r, cast_rows), :].astype(BF16)
            wu_rs[pl.ds(r, cast_rows), :] = wu_st[pl.ds(r, cast_rows), :].astype(BF16)
            return carry

        def cast_down(i, carry):
            r = pl.multiple_of(i * cast_rows, cast_rows)
            wd_rs[pl.ds(r, cast_rows), :] = wd_st[pl.ds(r, cast_rows), :].astype(BF16)
            return carry

        lax.fori_loop(0, wg_st.shape[0] // cast_rows, cast_up, 0)
        lax.fori_loop(0, wd_st.shape[0] // cast_rows, cast_down, 0)
        nxt = nx_ref[ex]

        @pl.when(nxt >= 0)
        def _():
            for cp in weight_copies(nxt):
                cp.start()

    def ffn(slot, row_dmas):
        x_lo, x_hi = (h.astype(BF16) for h in _unpack_halves(xbuf[slot]))
        hd = x_lo.shape[1]
        fc = wg_rs.shape[1] // ff_chunks
        rows = tm // ff_chunks
        out = None
        for c in range(ff_chunks):
            cols = slice(c * fc, (c + 1) * fc)
            up = lambda w: (jnp.dot(x_lo, w[:hd, cols], preferred_element_type=F32)
                            + jnp.dot(x_hi, w[hd:, cols], preferred_element_type=F32))
            a = up(wg_rs)
            b = up(wu_rs)
            hmid = ((a * (1.0 / (1.0 + jnp.exp(-a)))) * b).astype(BF16)
            part = jnp.dot(hmid, wd_rs[cols, :], preferred_element_type=F32)
            out = part if out is None else out + part
            row_dmas(c * rows, (c + 1) * rows)
        obuf[slot] = _pack_halves(out)

    def step(p):
        q = 1 - p
        if p == 0:
            @pl.when(t == 0)
            def _():
                copies = (gidx_copy(0, 0), gidx_copy(jnp.minimum(1, last), 1), sidx_copy(spare_codes, 1))
                for cp in copies:
                    cp.start()
                for cp in copies:
                    cp.wait()
                obuf[1] = jnp.zeros(obuf.shape[1:], obuf.dtype)
                for_rows(lambda r: gather_row(0, r).start())

        gidx_copy(jnp.minimum(t + 2, last), p).start()
        sidx_copy(t, p).start()

        @pl.when(t >= 1)
        def _():
            gidx_copy(0, q).wait()
            sidx_copy(0, q).wait()
            wait_scatter(p)

        wait_gather(p)

        @pl.when(sw_ref[t] == 1)
        def _():
            take_over(te_ref[t], t == 0)

        def row_dmas(lo, hi):
            for r in range(lo, hi):
                scatter_row(q, r).start()
                gather_row(q, r).start()

        ffn(p, row_dmas)

    for parity in range(2):
        pl.when(jnp.logical_and(t <= last, lax.rem(t, 2) == parity))(functools.partial(step, parity))

    slot = lax.rem(t, 2)

    @pl.when(t == last)
    def _():
        gidx_copy(0, slot).wait()
        sidx_copy(0, slot).wait()
        for_rows(lambda r: scatter_row(slot, r).start())
        wait_gather(1 - slot)
        wait_scatter(1 - slot)
        wait_scatter(slot)


def _moe_ffn(h2, plan, w_gate, w_up, w_down, *, tm):
    n, hd = h2.shape
    d = 2 * hd
    n_slots = 2 * n
    max_tiles = plan["max_tiles"]
    assert n_slots % tm == 0 and n_slots // tm >= 2
    d_ff = w_gate.shape[2]
    any_spec = pl.BlockSpec(memory_space=pl.ANY)
    grid_spec = pltpu.PrefetchScalarGridSpec(
        num_scalar_prefetch=4,
        grid=(max_tiles,),
        in_specs=[any_spec] * 5,
        out_specs=any_spec,
        scratch_shapes=[
            pltpu.SMEM((2, 1, tm), jnp.int32),
            pltpu.SMEM((2, 1, tm), jnp.int32),
            pltpu.VMEM((2, tm, hd), jnp.uint32),
            pltpu.VMEM((2, tm, hd), jnp.uint32),
            pltpu.VMEM((d, d_ff), F32),
            pltpu.VMEM((d, d_ff), F32),
            pltpu.VMEM((d_ff, d), F32),
            pltpu.VMEM((d, d_ff), BF16),
            pltpu.VMEM((d, d_ff), BF16),
            pltpu.VMEM((d_ff, d), BF16),
            pltpu.SemaphoreType.DMA((2,)),
            pltpu.SemaphoreType.DMA((2,)),
            pltpu.SemaphoreType.DMA((2,)),
            pltpu.SemaphoreType.DMA((2,)),
            pltpu.SemaphoreType.DMA((3,)),
        ],
    )
    return pl.pallas_call(
        functools.partial(_moe_body, n_tokens=n, max_tiles=max_tiles, cast_rows=128,
                          ff_chunks=d_ff // (2 * LANES)),
        grid_spec=grid_spec,
        out_shape=jax.ShapeDtypeStruct((n_slots + tm, hd), jnp.uint32),
        compiler_params=_params(("arbitrary",), vmem=MOE_VMEM_LIMIT),
        name="moe_ffn",
    )(plan["tile_expert"], plan["switch"], plan["n_used"], plan["next_expert"],
      plan["codes"], h2, w_gate, w_up, w_down)


def _final_body(x1_ref, ys0_ref, ys1_ref, route_ref, g_ref, o_ref):
    d = x1_ref.shape[1]
    route = route_ref[...]
    g1, g2 = route[:, 2:3], route[:, 3:4]
    ya = _unpack_halves(ys0_ref[...])
    yb = _unpack_halves(ys1_ref[...])
    halves = (slice(0, d // 2), slice(d // 2, d))
    x2 = [x1_ref[:, cols] + (ya[i] * g1 + yb[i] * g2) for i, cols in enumerate(halves)]
    ms = sum(jnp.sum(h * h, axis=-1, keepdims=True) for h in x2) * (1.0 / d)
    inv = lax.rsqrt(ms + EPS)
    for h, cols in zip(x2, halves):
        o_ref[:, cols] = ((h * inv) * g_ref[:, cols]).astype(o_ref.dtype)


def _final(x1, ys, route, g, *, tm):
    n, d = x1.shape
    nb = n // tm
    return pl.pallas_call(
        _final_body,
        grid=(nb,),
        in_specs=[
            pl.BlockSpec((tm, d), lambda i: (i, 0)),
            pl.BlockSpec((tm, d // 2), lambda i: (i, 0)),
            pl.BlockSpec((tm, d // 2), lambda i: (nb + i, 0)),
            pl.BlockSpec((tm, LANES), lambda i: (i, 0)),
            pl.BlockSpec((1, d), lambda i: (0, 0)),
        ],
        out_specs=pl.BlockSpec((tm, d), lambda i: (i, 0)),
        out_shape=jax.ShapeDtypeStruct((n, d), F32),
        compiler_params=_params(("arbitrary",)),
        name="final_norm",
    )(x1, ys, ys, route, g)


def _layer(x, rel_bias, attn_norm_g, w_in, lq1, lk1, lq2, lk2, diff_subln_g, ret_gn_g, w_out, ffn_norm_g,
           w_group_router, b_group, w_inner_router, b_inner, w_gate_exp, w_up_exp, w_down_exp, *, layer,
           attn_tile, row_tile, moe_tile):
    b, s, d = x.shape
    n = b * s
    x2d = x.reshape(n, d)
    lam_init = 0.8 - 0.6 * math.exp(-0.3 * layer)

    dq_cols = (COL_DK - COL_DQ) * LANES
    col_scale = jnp.concatenate([jnp.full((1, dq_cols), DIFF_HALF ** -0.5 * LOG2E, F32),
                                 jnp.ones((1, IN_COLS - dq_cols), F32)], axis=1)
    proj = _inproj(x2d, attn_norm_g.reshape(1, d), w_in.astype(BF16), col_scale, tm=min(2 * row_tile, n),
                   tn=8 * LANES)
    proj3 = proj.reshape(b, s, IN_COLS)

    bias = _bias_tiles(rel_bias, attn_tile)
    vec = lambda a: a.reshape(1, -1)
    a_out = _diff_attention(proj3, bias, vec(lq1), vec(lk1), vec(lq2), vec(lk2), vec(diff_subln_g),
                            t=attn_tile, lam_init=lam_init)
    r_out = _retention(proj3, _retention_tables(s), vec(ret_gn_g))

    pad = LANES - N_GROUPS - N_EXPERTS
    wr = jnp.concatenate([w_group_router, jnp.transpose(w_inner_router, (1, 0, 2)).reshape(d, N_EXPERTS),
                          jnp.zeros((d, pad), F32)], axis=1).astype(BF16)
    br = jnp.concatenate([b_group, b_inner.reshape(-1), jnp.zeros((pad,), F32)]).reshape(1, LANES)
    x1, h2, route = _outproj_router(a_out.reshape(n, -1), r_out.reshape(n, -1), x2d, w_out.astype(BF16),
                                    vec(ffn_norm_g), wr, br, tm=row_tile)

    expert_id = route[:, :2].astype(jnp.int32).T.reshape(-1)
    plan = _moe_plan(expert_id, moe_tile)
    ys = _moe_ffn(h2, plan, w_gate_exp, w_up_exp, w_down_exp, tm=moe_tile)
    return x1, ys, route


def kernel(x, rel_bias, attn_norm_g, w_in, lambda_q1, lambda_k1, lambda_q2, lambda_k2, diff_subln_g, ret_gn_g,
           w_out, ffn_norm_g, w_group_router, b_group, w_inner_router, b_inner, w_gate_exp, w_up_exp, w_down_exp,
           final_g):
    b, s, d = x.shape
    depth = w_in.shape[0]
    assert depth == 1, "the final norm is fused into the single layer's combine step"
    first = lambda a: a.reshape(a.shape[1:])
    x1, ys, route = _layer(x, rel_bias, attn_norm_g[0], w_in[0], lambda_q1[0], lambda_k1[0], lambda_q2[0],
                           lambda_k2[0], diff_subln_g[0], ret_gn_g[0], w_out[0], ffn_norm_g[0],
                           w_group_router[0], b_group[0], w_inner_router[0], b_inner[0], first(w_gate_exp),
                           first(w_up_exp), first(w_down_exp), layer=0, attn_tile=min(512, s), row_tile=512,
                           moe_tile=256)
    out = _final(x1, ys, route, final_g.reshape(1, d), tm=256)
    return out.reshape(b, s, d)
```

```python
import functools
import math

import numpy as np
import jax
import jax.numpy as jnp
from jax import lax
from jax.experimental import pallas as pl
from jax.experimental.pallas import tpu as pltpu

F32 = jnp.float32
BF16 = jnp.bfloat16

EPS = 1e-6
LANES = 128
N_DIFF_HEADS = 8
DIFF_HALF = 64
N_RET_HEADS = 8
RET_QK = 64
RET_CHUNK = 128
ROPE_BASE = 10000.0
N_BUCKETS = 32
MAX_DISTANCE = 128
N_GROUPS = 4
EXPERTS_PER_GROUP = 8
N_EXPERTS = N_GROUPS * EXPERTS_PER_GROUP
MASK_VALUE = -1e30
LOG2E = math.log2(math.e)
VMEM_LIMIT = 56 * 1024 * 1024
MOE_VMEM_LIMIT = 60 * 1024 * 1024

COL_DQ, COL_DK, COL_DV = 0, 8, 16
COL_RQ, COL_RK, COL_RV, COL_RG = 24, 28, 32, 40
IN_COLS = 48 * LANES


def _params(sem, vmem=VMEM_LIMIT):
    return pltpu.CompilerParams(dimension_semantics=sem, vmem_limit_bytes=vmem)


def _pack_halves(x):
    hd = x.shape[1] // 2
    bits = lambda h: lax.bitcast_convert_type(h.astype(BF16).astype(F32), jnp.uint32)
    return lax.shift_right_logical(bits(x[:, :hd]), jnp.uint32(16)) | bits(x[:, hd:])


def _unpack_halves(w):
    return (lax.bitcast_convert_type(lax.shift_left(w, jnp.uint32(16)), F32),
            lax.bitcast_convert_type(w & jnp.uint32(0xFFFF0000), F32))


def _inproj_body(x_ref, g_ref, w_ref, cs_ref, o_ref):
    x = x_ref[...]
    acc = jnp.dot((x * g_ref[...]).astype(BF16), w_ref[...], preferred_element_type=F32)
    inv = lax.rsqrt(jnp.mean(x * x, axis=-1, keepdims=True) + EPS)
    o_ref[...] = ((acc * inv) * cs_ref[...]).astype(o_ref.dtype)


def _inproj(x2d, g, w, col_scale, *, tm, tn):
    n, d = x2d.shape
    cols = w.shape[1]
    return pl.pallas_call(
        _inproj_body,
        grid=(n // tm, cols // tn),
        in_specs=[
            pl.BlockSpec((tm, d), lambda i, j: (i, 0)),
            pl.BlockSpec((1, d), lambda i, j: (0, 0)),
            pl.BlockSpec((d, tn), lambda i, j: (0, j)),
            pl.BlockSpec((1, tn), lambda i, j: (0, j)),
        ],
        out_specs=pl.BlockSpec((tm, tn), lambda i, j: (i, j)),
        out_shape=jax.ShapeDtypeStruct((n, cols), BF16),
        compiler_params=_params(("arbitrary", "arbitrary")),
        name="inproj",
    )(x2d, g, w, col_scale)


def _t5_bucket_table(n):
    d = np.arange(n)
    max_exact = N_BUCKETS // 2
    log_ratio = np.log(np.maximum(d, 1).astype(np.float64) / max_exact) / math.log(MAX_DISTANCE / max_exact)
    large = np.minimum(max_exact + (log_ratio * (N_BUCKETS - max_exact)).astype(np.int64), N_BUCKETS - 1)
    return np.where(d < max_exact, d, large).astype(np.int32)


def _bias_body(rb_ref, bkt_ref, o_ref):
    h = pl.program_id(0)
    t = o_ref.shape[2]
    bkt = bkt_ref[...]
    far = rb_ref[N_BUCKETS - 1, h]
    by_dist = jnp.zeros(bkt.shape, F32)
    for b in range(N_BUCKETS - 1):
        by_dist = jnp.where(bkt == b, (rb_ref[b, h] - far) * LOG2E, by_dist)
    rows = (jnp.concatenate([jnp.full((1, t), MASK_VALUE, F32), by_dist[:, :t]], axis=1), by_dist)
    for i, row in enumerate(rows):
        shifted = pltpu.roll(jnp.broadcast_to(row, (t, 2 * t)), 0, 1, stride=1, stride_axis=0)
        o_ref[0, i] = shifted[:, t:]


def _bias_tiles(rel_bias, t):
    bkt = jnp.asarray(_t5_bucket_table(2 * t)).reshape(1, 2 * t)
    return pl.pallas_call(
        _bias_body,
        grid=(N_DIFF_HEADS,),
        in_specs=[
            pl.BlockSpec(memory_space=pltpu.SMEM),
            pl.BlockSpec((1, 2 * t), lambda h: (0, 0)),
        ],
        out_specs=pl.BlockSpec((1, 2, t, t), lambda h: (h, 0, 0, 0)),
        out_shape=jax.ShapeDtypeStruct((N_DIFF_HEADS, 2, t, t), F32),
        compiler_params=_params(("arbitrary",)),
        name="bias_tiles",
    )(rel_bias, bkt)


_NT = (((1,), (1,)), ((), ()))
_TN = (((0,), (0,)), ((), ()))


def _diffattn_body(q_ref, k_ref, v_ref, bias_ref, lq1_ref, lk1_ref, lq2_ref, lk2_ref, g_ref, o_ref, vt_ref, *,
                   t, lam_init, near_band):
    nq = k_ref.shape[1] // t
    half = t // 2
    lane = lax.broadcasted_iota(jnp.int32, (t, LANES), 1)
    lam = (jnp.exp(jnp.sum(lq1_ref[...] * lk1_ref[...], axis=-1, keepdims=True))
           - jnp.exp(jnp.sum(lq2_ref[...] * lk2_ref[...], axis=-1, keepdims=True)) + lam_init)
    vt_ref[:LANES, :] = v_ref[0].T
    vt_ref[LANES:, :] = jnp.ones((vt_ref.shape[0] - LANES, vt_ref.shape[1]), BF16)
    qmask = {}
    state = {}

    def scores(item):
        qi, j, part = item
        if qi not in qmask:
            q = q_ref[0, qi * t:(qi + 1) * t, :]
            zero = jnp.zeros_like(q)
            qmask[qi] = (jnp.where(lane < DIFF_HALF, q, zero), jnp.where(lane >= DIFF_HALF, q, zero))
        k_lo = j * t + (half if part == "diag_lower" else 0)
        k_hi = j * t + (half if part == "diag_upper" else t)
        kb = k_ref[0, k_lo:k_hi, :]
        out = []
        for c in range(2):
            qsel = qmask[qi][c][half:] if part == "diag_lower" else qmask[qi][c]
            s = lax.dot_general(kb, qsel, _NT, preferred_element_type=F32)
            if part == "diag_upper":
                s = s + bias_ref[0, 0, :half, :]
            elif part == "diag_lower":
                s = s + bias_ref[0, 0, half:, half:]
            elif j == qi - 1:
                if near_band >= t:
                    s = s + bias_ref[0, 1]
                else:
                    corner = s[t - near_band:, :near_band] + bias_ref[0, 1, t - near_band:, :near_band]
                    bottom = jnp.concatenate([corner, s[t - near_band:, near_band:]], axis=1)
                    s = jnp.concatenate([s[:t - near_band], bottom], axis=0)
            out.append(s)
        return out

    def softmax_step(item, ss):
        qi, j, part = item
        first = j == 0 and part != "diag_lower"
        ps, alphas = [], []
        for c in range(2):
            s = ss[c]
            smax = jnp.max(s, axis=0, keepdims=True)
            if first:
                m_new, alpha = smax, None
            else:
                m_old = state[qi, c]["m"]
                m_old = m_old[:, half:] if part == "diag_lower" else m_old
                m_new = jnp.maximum(m_old, smax)
                alpha = jnp.exp2(m_old - m_new)
            ps.append(jnp.exp2(s - m_new).astype(BF16))
            alphas.append((m_new, alpha))
        return ps, alphas

    def values_step(item, ps, alphas):
        qi, j, part = item
        k_lo = j * t + (half if part == "diag_lower" else 0)
        k_hi = j * t + (half if part == "diag_upper" else t)
        vtb = vt_ref[:, k_lo:k_hi]
        for c in range(2):
            m_new, alpha = alphas[c]
            r = jnp.dot(vtb, ps[c], preferred_element_type=F32)
            if alpha is None:
                state[qi, c] = dict(m=m_new, acc=r)
            elif part == "diag_lower":
                st = state[qi, c]
                st["m"] = jnp.concatenate([st["m"][:, :half], m_new], axis=1)
                st["acc"] = jnp.concatenate([st["acc"][:, :half], alpha * st["acc"][:, half:] + r], axis=1)
            else:
                st = state[qi, c]
                st["m"] = m_new
                st["acc"] = alpha * st["acc"] + r
        if part == "diag_lower":
            outs = []
            for c in range(2):
                acc = state.pop((qi, c))["acc"]
                outs.append(acc[:LANES] * (1.0 / acc[LANES:LANES + 1]))
            ot = (outs[0] - lam * outs[1]).T
            ms = jnp.mean(ot * ot, axis=-1, keepdims=True)
            o_ref[0, qi * t:(qi + 1) * t, :] = (
                ((ot * lax.rsqrt(ms + EPS)) * g_ref[...]) * (1.0 - lam_init)).astype(o_ref.dtype)
            del qmask[qi]

    items = []
    for qi in range(nq):
        items += [(qi, j, "full") for j in range(qi)] + [(qi, qi, "diag_upper"), (qi, qi, "diag_lower")]
    ss = scores(items[0])
    for n, item in enumerate(items):
        nxt = scores(items[n + 1]) if n + 1 < len(items) else None
        ps, alphas = softmax_step(item, ss)
        values_step(item, ps, alphas)
        ss = nxt


def _diff_attention(proj3, bias, lq1, lk1, lq2, lk2, subln_g, *, t, lam_init):
    b, s, _ = proj3.shape
    vec = pl.BlockSpec((1, DIFF_HALF), lambda bi, h: (0, 0))
    head = lambda col: pl.BlockSpec((1, s, LANES), lambda bi, h: (bi, 0, col + h))
    first_far = int(np.argmax(_t5_bucket_table(2 * t) == N_BUCKETS - 1))
    near_band = -(-first_far // LANES) * LANES
    ones_rows = 16
    return pl.pallas_call(
        functools.partial(_diffattn_body, t=t, lam_init=lam_init, near_band=near_band),
        grid=(b, N_DIFF_HEADS),
        in_specs=[
            head(COL_DQ), head(COL_DK), head(COL_DV),
            pl.BlockSpec((1, 2, t, t), lambda bi, h: (h, 0, 0, 0)),
            vec, vec, vec, vec,
            pl.BlockSpec((1, LANES), lambda bi, h: (0, 0)),
        ],
        out_specs=pl.BlockSpec((1, s, LANES), lambda bi, h: (bi, 0, h)),
        out_shape=jax.ShapeDtypeStruct((b, s, N_DIFF_HEADS * LANES), BF16),
        scratch_shapes=[pltpu.VMEM((LANES + ones_rows, s), BF16)],
        compiler_params=_params(("arbitrary", "arbitrary")),
        name="diff_attention",
    )(proj3, proj3, proj3, bias, lq1, lk1, lq2, lk2, subln_g)


def _retention_tables(s):
    c = RET_CHUNK
    half = RET_QK // 2
    inv_freq = ROPE_BASE ** (-jnp.arange(0, RET_QK, 2, dtype=F32) / RET_QK)
    ang = jnp.arange(s, dtype=F32)[:, None] * inv_freq[None, :]
    cos, sin = jnp.cos(ang), jnp.sin(ang)
    cos_t = jnp.tile(cos, (1, LANES // half))
    sin_t = jnp.tile(jnp.concatenate([-sin, sin], axis=1), (1, LANES // RET_QK))
    log_g = jnp.log(1.0 - jnp.exp2(-5.0 - jnp.arange(N_RET_HEADS, dtype=F32)))
    n = jnp.arange(c, dtype=F32)
    diff = n[:, None] - n[None, :]
    inner = jnp.where(diff[None] >= 0, jnp.exp(jnp.maximum(diff, 0.0)[None] * log_g[:, None, None]), 0.0)
    cross = jnp.exp((n[None] + 1.0) * log_g[:, None])
    key = jnp.exp((c - 1.0 - n[None]) * log_g[:, None])
    chunk = jnp.exp(c * log_g)
    bc = lambda a: jnp.broadcast_to(a[..., None], a.shape + (LANES,)).astype(F32)
    return cos_t, sin_t, inner.astype(F32), bc(cross), bc(key), bc(chunk[:, None])


def _retention_body(rq_ref, rk_ref, rv_ref, rg_ref, cos_ref, sin_ref, inner_ref, cross_ref, key_ref,
                    chunk_ref, gn_ref, o_ref, state_ref):
    c = RET_CHUNK
    nc = rq_ref.shape[1] // c
    nh = N_RET_HEADS
    state_ref[...] = jnp.zeros(state_ref.shape, F32)
    lane = lax.broadcasted_iota(jnp.int32, (c, LANES), 1)
    first_half = (lane % RET_QK) < (RET_QK // 2)
    head_mask = [lane < RET_QK, lane >= RET_QK]
    slab = lambda i: slice(i * LANES, (i + 1) * LANES)

    def rotary(x, cos, sin):
        rot = jnp.where(first_half, pltpu.roll(x, LANES - RET_QK // 2, 1), pltpu.roll(x, RET_QK // 2, 1))
        return x * cos + rot * sin

    def chunk_step(ci, carry):
        r = pl.multiple_of(ci * c, c)
        rows = pl.ds(r, c)
        cos = cos_ref[rows, :]
        sin = sin_ref[rows, :]
        q_pair, k_pair, kb_pair = [], [], []
        for pr in range(nh // 2):
            q_pair.append(rotary(rq_ref[0, rows, slab(pr)].astype(F32), cos, sin))
            k = rotary(rk_ref[0, rows, slab(pr)].astype(F32), cos, sin) * (RET_QK ** -0.5)
            k_pair.append(k)
            kb_pair.append(k.astype(BF16))
        qh = [jnp.where(head_mask[h % 2], q_pair[h // 2], 0.0).astype(BF16) for h in range(nh)]
        kd = [(k_pair[h // 2] * key_ref[h]).astype(BF16) for h in range(nh)]
        vh = [rv_ref[0, rows, slab(h)] for h in range(nh)]
        scores = [lax.dot_general(qh[h], kb_pair[h // 2], _NT, preferred_element_type=F32) for h in range(nh)]
        cross = [jnp.dot(qh[h], state_ref[h].astype(BF16), preferred_element_type=F32) for h in range(nh)]
        kv = [lax.dot_general(kd[h], vh[h], _TN, preferred_element_type=F32) for h in range(nh)]
        decayed = [(scores[h] * inner_ref[h]).astype(BF16) for h in range(nh)]
        for h in range(nh):
            state_ref[h] = state_ref[h] * chunk_ref[h] + kv[h]
        inner = [jnp.dot(decayed[h], vh[h], preferred_element_type=F32) for h in range(nh)]
        for h in range(nh):
            y = inner[h] + cross[h] * cross_ref[h]
            mu = jnp.mean(y, axis=-1, keepdims=True)
            yc = y - mu
            var = jnp.mean(yc * yc, axis=-1, keepdims=True)
            yn = (yc * lax.rsqrt(var + EPS)) * gn_ref[:, slab(h)]
            gate = rg_ref[0, rows, slab(h)].astype(F32)
            silu = gate * (1.0 / (1.0 + jnp.exp(-gate)))
            o_ref[0, rows, slab(h)] = (silu * yn).astype(o_ref.dtype)
        return carry

    lax.fori_loop(0, nc, chunk_step, 0)


def _retention(proj3, tables, gn_g):
    b, s, _ = proj3.shape
    cos_t, sin_t, inner, cross, key, chunk = tables
    c = RET_CHUNK
    nh = N_RET_HEADS
    qk_w = nh * RET_QK
    v_w = nh * LANES
    full = lambda shape: pl.BlockSpec(shape, lambda bi: (0,) * len(shape))
    return pl.pallas_call(
        _retention_body,
        grid=(b,),
        in_specs=[
            pl.BlockSpec((1, s, qk_w), lambda bi: (bi, 0, COL_RQ * LANES // qk_w)),
            pl.BlockSpec((1, s, qk_w), lambda bi: (bi, 0, COL_RK * LANES // qk_w)),
            pl.BlockSpec((1, s, v_w), lambda bi: (bi, 0, COL_RV * LANES // v_w)),
            pl.BlockSpec((1, s, v_w), lambda bi: (bi, 0, COL_RG * LANES // v_w)),
            full((s, LANES)), full((s, LANES)),
            full((nh, c, c)), full((nh, c, LANES)), full((nh, c, LANES)), full((nh, 1, LANES)),
            full((1, v_w)),
        ],
        out_specs=pl.BlockSpec((1, s, v_w), lambda bi: (bi, 0, 0)),
        out_shape=jax.ShapeDtypeStruct((b, s, v_w), BF16),
        scratch_shapes=[pltpu.VMEM((nh, LANES, LANES), F32)],
        compiler_params=_params(("arbitrary",)),
        name="retention",
    )(proj3, proj3, proj3, proj3, cos_t, sin_t, inner, cross, key, chunk, gn_g)


def _route(logits):
    lane = lax.broadcasted_iota(jnp.int32, logits.shape, 1).astype(F32)
    big = float(LANES)
    is_group = lane < N_GROUPS
    gl = jnp.where(is_group, logits, -jnp.inf)
    gmax = jnp.max(gl, axis=-1, keepdims=True)
    gidx = jnp.min(jnp.where(gl == gmax, lane, big), axis=-1, keepdims=True)
    p_group = 1.0 / jnp.sum(jnp.where(is_group, jnp.exp(gl - gmax), 0.0), axis=-1, keepdims=True)
    lo = N_GROUPS + EXPERTS_PER_GROUP * gidx
    il = jnp.where(lane >= lo, jnp.where(lane < lo + EXPERTS_PER_GROUP, logits, -jnp.inf), -jnp.inf)
    v1 = jnp.max(il, axis=-1, keepdims=True)
    i1 = jnp.min(jnp.where(il == v1, lane, big), axis=-1, keepdims=True)
    il2 = jnp.where(lane == i1, -jnp.inf, il)
    v2 = jnp.max(il2, axis=-1, keepdims=True)
    i2 = jnp.min(jnp.where(il2 == v2, lane, big), axis=-1, keepdims=True)
    e2 = jnp.exp(v2 - v1)
    inv = 1.0 / (1.0 + e2)
    g1 = p_group * inv
    g2 = p_group * (e2 * inv)
    out = jnp.where(lane == 0.0, i1 - N_GROUPS,
                    jnp.where(lane == 1.0, i2 - N_GROUPS,
                              jnp.where(lane == 2.0, g1, jnp.where(lane == 3.0, g2, 0.0))))
    return out


def _outproj_body(a_ref, r_ref, x_ref, w_ref, g_ref, wr_ref, br_ref, x1_ref, h2_ref, route_ref, *, sub):
    tm = x_ref.shape[0]
    half = a_ref.shape[1]
    n_sub = tm // sub

    def project(i):
        rows = slice(i * sub, (i + 1) * sub)
        acc = jnp.dot(a_ref[rows, :], w_ref[:half, :], preferred_element_type=F32)
        return acc + jnp.dot(r_ref[rows, :], w_ref[half:, :], preferred_element_type=F32)

    def epilogue(i, acc):
        rows = slice(i * sub, (i + 1) * sub)
        x1 = x_ref[rows, :] + acc
        x1_ref[rows, :] = x1
        ms = jnp.mean(x1 * x1, axis=-1, keepdims=True)
        h2 = (x1 * lax.rsqrt(ms + EPS)) * g_ref[...]
        h2_ref[rows, :] = _pack_halves(h2)
        logits = jnp.dot(h2.astype(BF16), wr_ref[...], preferred_element_type=F32) + br_ref[...]
        route_ref[rows, :] = _route(logits)

    acc = project(0)
    for i in range(n_sub):
        nxt = project(i + 1) if i + 1 < n_sub else None
        epilogue(i, acc)
        acc = nxt


def _outproj_router(a2d, r2d, x2d, w_out_bf16, g, wr_bf16, br, *, tm):
    n, d = x2d.shape
    half = a2d.shape[1]
    row = lambda width: pl.BlockSpec((tm, width), lambda i: (i, 0))
    full = lambda shape: pl.BlockSpec(shape, lambda i: (0, 0))
    return pl.pallas_call(
        functools.partial(_outproj_body, sub=min(tm, 2 * LANES)),
        grid=(n // tm,),
        in_specs=[row(half), row(half), row(d), full((2 * half, d)), full((1, d)), full((d, LANES)),
                  full((1, LANES))],
        out_specs=[row(d), row(d // 2), row(LANES)],
        out_shape=[jax.ShapeDtypeStruct((n, d), F32), jax.ShapeDtypeStruct((n, d // 2), jnp.uint32),
                   jax.ShapeDtypeStruct((n, LANES), F32)],
        compiler_params=_params(("arbitrary",)),
        name="outproj_router",
    )(a2d, r2d, x2d, w_out_bf16, g, wr_bf16, br)


def _moe_plan(expert_id, tm):
    n_slots = expert_id.shape[0]
    max_tiles = n_slots // tm + N_EXPERTS - 1
    order = jnp.argsort(expert_id).astype(jnp.int32)
    sizes = jnp.zeros((N_EXPERTS,), jnp.int32).at[expert_id].add(1)
    starts = jnp.cumsum(sizes) - sizes
    tiles_of = (sizes + (tm - 1)) // tm
    tile_end = jnp.cumsum(tiles_of)
    tile_start = tile_end - tiles_of
    n_used = tile_end[-1]
    tile = jnp.minimum(jnp.arange(max_tiles, dtype=jnp.int32), n_used - 1)
    tile_expert = jnp.sum((tile_end[None, :] <= tile[:, None]).astype(jnp.int32), axis=1)
    prev_expert = jnp.concatenate([jnp.full((1,), -1, jnp.int32), tile_expert[:-1]])
    switch = (prev_expert != tile_expert).astype(jnp.int32)
    eid = jnp.arange(N_EXPERTS, dtype=jnp.int32)
    cand = jnp.where((sizes[None, :] > 0) & (eid[None, :] > eid[:, None]), eid[None, :], N_EXPERTS)
    nxt = jnp.min(cand, axis=1)
    next_expert = jnp.where(nxt < N_EXPERTS, nxt, -1).astype(jnp.int32)
    rank0 = (tile - tile_start[tile_expert]) * tm
    order_ext = jnp.concatenate([order, jnp.zeros((tm,), jnp.int32)])
    take = jax.vmap(lambda s0: lax.dynamic_slice(order_ext, (s0,), (tm,)))
    slots = take(starts[tile_expert] + rank0)
    lane = jnp.arange(tm, dtype=jnp.int32)[None, :]
    real = (rank0[:, None] + lane) < sizes[tile_expert][:, None]
    spare = n_slots + lane
    codes = jnp.concatenate([jnp.where(real, slots, spare), spare], axis=0).reshape(max_tiles + 1, 1, tm)
    return dict(tile_expert=tile_expert, switch=switch, n_used=n_used.reshape(1).astype(jnp.int32),
                next_expert=next_expert, codes=codes, max_tiles=max_tiles)


def _moe_body(te_ref, sw_ref, nu_ref, nx_ref,
              codes_hbm, h2_hbm, wg_hbm, wu_hbm, wd_hbm, ys_hbm,
              gidx, sidx, xbuf, obuf, wg_st, wu_st, wd_st, wg_rs, wu_rs, wd_rs,
              gisem, sisem, gsem, ssem, wsem, *, n_tokens, max_tiles, cast_rows, ff_chunks):
    t = pl.program_id(0)
    tm = xbuf.shape[1]
    last = nu_ref[0] - 1
    spare_codes = max_tiles

    def gidx_copy(tile, slot):
        return pltpu.make_async_copy(codes_hbm.at[tile], gidx.at[slot], gisem.at[slot])

    def sidx_copy(tile, slot):
        return pltpu.make_async_copy(codes_hbm.at[tile], sidx.at[slot], sisem.at[slot])

    def weight_copies(ex):
        return (pltpu.make_async_copy(wg_hbm.at[ex], wg_st, wsem.at[0]),
                pltpu.make_async_copy(wu_hbm.at[ex], wu_st, wsem.at[1]),
                pltpu.make_async_copy(wd_hbm.at[ex], wd_st, wsem.at[2]))

    def gather_row(slot, r):
        code = gidx[slot, 0, r]
        tok = code - jnp.where(code >= 2 * n_tokens, 2 * n_tokens, jnp.where(code >= n_tokens, n_tokens, 0))
        return pltpu.make_async_copy(h2_hbm.at[pl.ds(tok, 1)], xbuf.at[slot, pl.ds(r, 1)], gsem.at[slot])

    def scatter_row(slot, r):
        dst = sidx[slot, 0, r]
        return pltpu.make_async_copy(obuf.at[slot, pl.ds(r, 1)], ys_hbm.at[pl.ds(dst, 1)], ssem.at[slot])

    def wait_gather(slot):
        pltpu.make_async_copy(h2_hbm.at[pl.ds(0, tm)], xbuf.at[slot], gsem.at[slot]).wait()

    def wait_scatter(slot):
        pltpu.make_async_copy(obuf.at[slot], ys_hbm.at[pl.ds(0, tm)], ssem.at[slot]).wait()

    def for_rows(fn):
        def body(r, carry):
            fn(r)
            return carry
        lax.fori_loop(0, tm, body, 0, unroll=8)

    def take_over(ex, first):
        @pl.when(first)
        def _():
            for cp in weight_copies(ex):
                cp.start()

        for cp in weight_copies(ex):
            cp.wait()

        def cast_up(i, carry):
            r = pl.multiple_of(i * cast_rows, cast_rows)
            wg_rs[pl.ds(r, cast_rows), :] = wg_st[pl.ds(r, cast_rows), :].astype(BF16)
            wu_rs[pl.ds(r, cast_rows), :] = wu_st[pl.ds(r, cast_rows), :].astype(BF16)
            return carry

        def cast_down(i, carry):
            r = pl.multiple_of(i * cast_rows, cast_rows)
            wd_rs[pl.ds(r, cast_rows), :] = wd_st[pl.ds(r, cast_rows), :].astype(BF16)
            return carry

        lax.fori_loop(0, wg_st.shape[0] // cast_rows, cast_up, 0)
        lax.fori_loop(0, wd_st.shape[0] // cast_rows, cast_down, 0)
        nxt = nx_ref[ex]

        @pl.when(nxt >= 0)
        def _():
            for cp in weight_copies(nxt):
                cp.start()

    def ffn(slot, row_dmas):
        x_lo, x_hi = (h.astype(BF16) for h in _unpack_halves(xbuf[slot]))
        hd = x_lo.shape[1]
        fc = wg_rs.shape[1] // ff_chunks
        rows = tm // ff_chunks
        out = None
        for c in range(ff_chunks):
            cols = slice(c * fc, (c + 1) * fc)
            up = lambda w: (jnp.dot(x_lo, w[:hd, cols], preferred_element_type=F32)
                            + jnp.dot(x_hi, w[hd:, cols], preferred_element_type=F32))
            a = up(wg_rs)
            b = up(wu_rs)
            hmid = ((a * (1.0 / (1.0 + jnp.exp(-a)))) * b).astype(BF16)
            part = jnp.dot(hmid, wd_rs[cols, :], preferred_element_type=F32)
            out = part if out is None else out + part
            row_dmas(c * rows, (c + 1) * rows)
        obuf[slot] = _pack_halves(out)

    def step(p):
        q = 1 - p
        if p == 0:
            @pl.when(t == 0)
            def _():
                copies = (gidx_copy(0, 0), gidx_copy(jnp.minimum(1, last), 1), sidx_copy(spare_codes, 1))
                for cp in copies:
                    cp.start()
                for cp in copies:
                    cp.wait()
                obuf[1] = jnp.zeros(obuf.shape[1:], obuf.dtype)
                for_rows(lambda r: gather_row(0, r).start())

        gidx_copy(jnp.minimum(t + 2, last), p).start()
        sidx_copy(t, p).start()

        @pl.when(t >= 1)
        def _():
            gidx_copy(0, q).wait()
            sidx_copy(0, q).wait()
            wait_scatter(p)

        wait_gather(p)

        @pl.when(sw_ref[t] == 1)
        def _():
            take_over(te_ref[t], t == 0)

        def row_dmas(lo, hi):
            for r in range(lo, hi):
                scatter_row(q, r).start()
                gather_row(q, r).start()

        ffn(p, row_dmas)

    for parity in range(2):
        pl.when(jnp.logical_and(t <= last, lax.rem(t, 2) == parity))(functools.partial(step, parity))

    slot = lax.rem(t, 2)

    @pl.when(t == last)
    def _():
        gidx_copy(0, slot).wait()
        sidx_copy(0, slot).wait()
        for_rows(lambda r: scatter_row(slot, r).start())
        wait_gather(1 - slot)
        wait_scatter(1 - slot)
        wait_scatter(slot)


def _moe_ffn(h2, plan, w_gate, w_up, w_down, *, tm):
    n, hd = h2.shape
    d = 2 * hd
    n_slots = 2 * n
    max_tiles = plan["max_tiles"]
    assert n_slots % tm == 0 and n_slots // tm >= 2
    d_ff = w_gate.shape[2]
    any_spec = pl.BlockSpec(memory_space=pl.ANY)
    grid_spec = pltpu.PrefetchScalarGridSpec(
        num_scalar_prefetch=4,
        grid=(max_tiles,),
        in_specs=[any_spec] * 5,
        out_specs=any_spec,
        scratch_shapes=[
            pltpu.SMEM((2, 1, tm), jnp.int32),
            pltpu.SMEM((2, 1, tm), jnp.int32),
            pltpu.VMEM((2, tm, hd), jnp.uint32),
            pltpu.VMEM((2, tm, hd), jnp.uint32),
            pltpu.VMEM((d, d_ff), F32),
            pltpu.VMEM((d, d_ff), F32),
            pltpu.VMEM((d_ff, d), F32),
            pltpu.VMEM((d, d_ff), BF16),
            pltpu.VMEM((d, d_ff), BF16),
            pltpu.VMEM((d_ff, d), BF16),
            pltpu.SemaphoreType.DMA((2,)),
            pltpu.SemaphoreType.DMA((2,)),
            pltpu.SemaphoreType.DMA((2,)),
            pltpu.SemaphoreType.DMA((2,)),
            pltpu.SemaphoreType.DMA((3,)),
        ],
    )
    return pl.pallas_call(
        functools.partial(_moe_body, n_tokens=n, max_tiles=max_tiles, cast_rows=128,
                          ff_chunks=d_ff // (2 * LANES)),
        grid_spec=grid_spec,
        out_shape=jax.ShapeDtypeStruct((n_slots + tm, hd), jnp.uint32),
        compiler_params=_params(("arbitrary",), vmem=MOE_VMEM_LIMIT),
        name="moe_ffn",
    )(plan["tile_expert"], plan["switch"], plan["n_used"], plan["next_expert"],
      plan["codes"], h2, w_gate, w_up, w_down)


def _final_body(x1_ref, ys0_ref, ys1_ref, route_ref, g_ref, o_ref):
    d = x1_ref.shape[1]
    route = route_ref[...]
    g1, g2 = route[:, 2:3], route[:, 3:4]
    ya = _unpack_halves(ys0_ref[...])
    yb = _unpack_halves(ys1_ref[...])
    halves = (slice(0, d // 2), slice(d // 2, d))
    x2 = [x1_ref[:, cols] + (ya[i] * g1 + yb[i] * g2) for i, cols in enumerate(halves)]
    ms = sum(jnp.sum(h * h, axis=-1, keepdims=True) for h in x2) * (1.0 / d)
    inv = lax.rsqrt(ms + EPS)
    for h, cols in zip(x2, halves):
        o_ref[:, cols] = ((h * inv) * g_ref[:, cols]).astype(o_ref.dtype)


def _final(x1, ys, route, g, *, tm):
    n, d = x1.shape
    nb = n // tm
    return pl.pallas_call(
        _final_body,
        grid=(nb,),
        in_specs=[
            pl.BlockSpec((tm, d), lambda i: (i, 0)),
            pl.BlockSpec((tm, d // 2), lambda i: (i, 0)),
            pl.BlockSpec((tm, d // 2), lambda i: (nb + i, 0)),
            pl.BlockSpec((tm, LANES), lambda i: (i, 0)),
            pl.BlockSpec((1, d), lambda i: (0, 0)),
        ],
        out_specs=pl.BlockSpec((tm, d), lambda i: (i, 0)),
        out_shape=jax.ShapeDtypeStruct((n, d), F32),
        compiler_params=_params(("arbitrary",)),
        name="final_norm",
    )(x1, ys, ys, route, g)


def _layer(x, rel_bias, attn_norm_g, w_in, lq1, lk1, lq2, lk2, diff_subln_g, ret_gn_g, w_out, ffn_norm_g,
           w_group_router, b_group, w_inner_router, b_inner, w_gate_exp, w_up_exp, w_down_exp, *, layer,
           attn_tile, row_tile, moe_tile):
    b, s, d = x.shape
    n = b * s
    x2d = x.reshape(n, d)
    lam_init = 0.8 - 0.6 * math.exp(-0.3 * layer)

    dq_cols = (COL_DK - COL_DQ) * LANES
    col_scale = jnp.concatenate([jnp.full((1, dq_cols), DIFF_HALF ** -0.5 * LOG2E, F32),
                                 jnp.ones((1, IN_COLS - dq_cols), F32)], axis=1)
    proj = _inproj(x2d, attn_norm_g.reshape(1, d), w_in.astype(BF16), col_scale, tm=min(2 * row_tile, n),
                   tn=8 * LANES)
    proj3 = proj.reshape(b, s, IN_COLS)

    bias = _bias_tiles(rel_bias, attn_tile)
    vec = lambda a: a.reshape(1, -1)
    a_out = _diff_attention(proj3, bias, vec(lq1), vec(lk1), vec(lq2), vec(lk2), vec(diff_subln_g),
                            t=attn_tile, lam_init=lam_init)
    r_out = _retention(proj3, _retention_tables(s), vec(ret_gn_g))

    pad = LANES - N_GROUPS - N_EXPERTS
    wr = jnp.concatenate([w_group_router, jnp.transpose(w_inner_router, (1, 0, 2)).reshape(d, N_EXPERTS),
                          jnp.zeros((d, pad), F32)], axis=1).astype(BF16)
    br = jnp.concatenate([b_group, b_inner.reshape(-1), jnp.zeros((pad,), F32)]).reshape(1, LANES)
    x1, h2, route = _outproj_router(a_out.reshape(n, -1), r_out.reshape(n, -1), x2d, w_out.astype(BF16),
                                    vec(ffn_norm_g), wr, br, tm=row_tile)

    expert_id = route[:, :2].astype(jnp.int32).T.reshape(-1)
    plan = _moe_plan(expert_id, moe_tile)
    ys = _moe_ffn(h2, plan, w_gate_exp, w_up_exp, w_down_exp, tm=moe_tile)
    return x1, ys, route


def kernel(x, rel_bias, attn_norm_g, w_in, lambda_q1, lambda_k1, lambda_q2, lambda_k2, diff_subln_g, ret_gn_g,
           w_out, ffn_norm_g, w_group_router, b_group, w_inner_router, b_inner, w_gate_exp, w_up_exp, w_down_exp,
           final_g):
    b, s, d = x.shape
    depth = w_in.shape[0]
    assert depth == 1, "the final norm is fused into the single layer's combine step"
    first = lambda a: a.reshape(a.shape[1:])
    x1, ys, route = _layer(x, rel_bias, attn_norm_g[0], w_in[0], lambda_q1[0], lambda_k1[0], lambda_q2[0],
                           lambda_k2[0], diff_subln_g[0], ret_gn_g[0], w_out[0], ffn_norm_g[0],
                           w_group_router[0], b_group[0], w_inner_router[0], b_inner[0], first(w_gate_exp),
                           first(w_up_exp), first(w_down_exp), layer=0, attn_tile=min(512, s), row_tile=512,
                           moe_tile=256)
    out = _final(x1, ys, route, final_g.reshape(1, d), tm=256)
    return out.reshape(b, s, d)
```

```python
import functools
import math

import numpy as np
import jax
import jax.numpy as jnp
from jax import lax
from jax.experimental import pallas as pl
from jax.experimental.pallas import tpu as pltpu

F32 = jnp.float32
BF16 = jnp.bfloat16

EPS = 1e-6
LANES = 128
N_DIFF_HEADS = 8
DIFF_HALF = 64
N_RET_HEADS = 8
RET_QK = 64
RET_CHUNK = 128
ROPE_BASE = 10000.0
N_BUCKETS = 32
MAX_DISTANCE = 128
N_GROUPS = 4
EXPERTS_PER_GROUP = 8
N_EXPERTS = N_GROUPS * EXPERTS_PER_GROUP
MASK_VALUE = -1e30
LOG2E = math.log2(math.e)
VMEM_LIMIT = 56 * 1024 * 1024
MOE_VMEM_LIMIT = 60 * 1024 * 1024

COL_DQ, COL_DK, COL_DV = 0, 8, 16
COL_RQ, COL_RK, COL_RV, COL_RG = 24, 28, 32, 40
IN_COLS = 48 * LANES


def _params(sem, vmem=VMEM_LIMIT):
    return pltpu.CompilerParams(dimension_semantics=sem, vmem_limit_bytes=vmem)


def _pack_halves(x):
    hd = x.shape[1] // 2
    bits = lambda h: lax.bitcast_convert_type(h.astype(BF16).astype(F32), jnp.uint32)
    return lax.shift_right_logical(bits(x[:, :hd]), jnp.uint32(16)) | bits(x[:, hd:])


def _unpack_halves(w):
    return (lax.bitcast_convert_type(lax.shift_left(w, jnp.uint32(16)), F32),
            lax.bitcast_convert_type(w & jnp.uint32(0xFFFF0000), F32))


def _inproj_body(x_ref, g_ref, w_ref, cs_ref, o_ref):
    x = x_ref[...]
    acc = jnp.dot((x * g_ref[...]).astype(BF16), w_ref[...], preferred_element_type=F32)
    inv = lax.rsqrt(jnp.mean(x * x, axis=-1, keepdims=True) + EPS)
    o_ref[...] = ((acc * inv) * cs_ref[...]).astype(o_ref.dtype)


def _inproj(x2d, g, w, col_scale, *, tm, tn):
    n, d = x2d.shape
    cols = w.shape[1]
    return pl.pallas_call(
        _inproj_body,
        grid=(n // tm, cols // tn),
        in_specs=[
            pl.BlockSpec((tm, d), lambda i, j: (i, 0)),
            pl.BlockSpec((1, d), lambda i, j: (0, 0)),
            pl.BlockSpec((d, tn), lambda i, j: (0, j)),
            pl.BlockSpec((1, tn), lambda i, j: (0, j)),
        ],
        out_specs=pl.BlockSpec((tm, tn), lambda i, j: (i, j)),
        out_shape=jax.ShapeDtypeStruct((n, cols), BF16),
        compiler_params=_params(("arbitrary", "arbitrary")),
        name="inproj",
    )(x2d, g, w, col_scale)


def _t5_bucket_table(n):
    d = np.arange(n)
    max_exact = N_BUCKETS // 2
    log_ratio = np.log(np.maximum(d, 1).astype(np.float64) / max_exact) / math.log(MAX_DISTANCE / max_exact)
    large = np.minimum(max_exact + (log_ratio * (N_BUCKETS - max_exact)).astype(np.int64), N_BUCKETS - 1)
    return np.where(d < max_exact, d, large).astype(np.int32)


def _bias_body(rb_ref, bkt_ref, o_ref):
    h = pl.program_id(0)
    t = o_ref.shape[2]
    bkt = bkt_ref[...]
    far = rb_ref[N_BUCKETS - 1, h]
    by_dist = jnp.zeros(bkt.shape, F32)
    for b in range(N_BUCKETS - 1):
        by_dist = jnp.where(bkt == b, (rb_ref[b, h] - far) * LOG2E, by_dist)
    rows = (jnp.concatenate([jnp.full((1, t), MASK_VALUE, F32), by_dist[:, :t]], axis=1), by_dist)
    for i, row in enumerate(rows):
        shifted = pltpu.roll(jnp.broadcast_to(row, (t, 2 * t)), 0, 1, stride=1, stride_axis=0)
        o_ref[0, i] = shifted[:, t:]


def _bias_tiles(rel_bias, t):
    bkt = jnp.asarray(_t5_bucket_table(2 * t)).reshape(1, 2 * t)
    return pl.pallas_call(
        _bias_body,
        grid=(N_DIFF_HEADS,),
        in_specs=[
            pl.BlockSpec(memory_space=pltpu.SMEM),
            pl.BlockSpec((1, 2 * t), lambda h: (0, 0)),
        ],
        out_specs=pl.BlockSpec((1, 2, t, t), lambda h: (h, 0, 0, 0)),
        out_shape=jax.ShapeDtypeStruct((N_DIFF_HEADS, 2, t, t), F32),
        compiler_params=_params(("arbitrary",)),
        name="bias_tiles",
    )(rel_bias, bkt)


_NT = (((1,), (1,)), ((), ()))
_TN = (((0,), (0,)), ((), ()))


def _diffattn_body(q_ref, k_ref, v_ref, bias_ref, lq1_ref, lk1_ref, lq2_ref, lk2_ref, g_ref, o_ref, vt_ref, *,
                   t, lam_init, near_band):
    nq = k_ref.shape[1] // t
    half = t // 2
    lane = lax.broadcasted_iota(jnp.int32, (t, LANES), 1)
    lam = (jnp.exp(jnp.sum(lq1_ref[...] * lk1_ref[...], axis=-1, keepdims=True))
           - jnp.exp(jnp.sum(lq2_ref[...] * lk2_ref[...], axis=-1, keepdims=True)) + lam_init)
    vt_ref[:LANES, :] = v_ref[0].T
    vt_ref[LANES:, :] = jnp.ones((vt_ref.shape[0] - LANES, vt_ref.shape[1]), BF16)
    qmask = {}
    state = {}

    def scores(item):
        qi, j, part = item
        if qi not in qmask:
            q = q_ref[0, qi * t:(qi + 1) * t, :]
            zero = jnp.zeros_like(q)
            qmask[qi] = (jnp.where(lane < DIFF_HALF, q, zero), jnp.where(lane >= DIFF_HALF, q, zero))
        k_lo = j * t + (half if part == "diag_lower" else 0)
        k_hi = j * t + (half if part == "diag_upper" else t)
        kb = k_ref[0, k_lo:k_hi, :]
        out = []
        for c in range(2):
            qsel = qmask[qi][c][half:] if part == "diag_lower" else qmask[qi][c]
            s = lax.dot_general(kb, qsel, _NT, preferred_element_type=F32)
            if part == "diag_upper":
                s = s + bias_ref[0, 0, :half, :]
            elif part == "diag_lower":
                s = s + bias_ref[0, 0, half:, half:]
            elif j == qi - 1:
                if near_band >= t:
                    s = s + bias_ref[0, 1]
                else:
                    corner = s[t - near_band:, :near_band] + bias_ref[0, 1, t - near_band:, :near_band]
                    bottom = jnp.concatenate([corner, s[t - near_band:, near_band:]], axis=1)
                    s = jnp.concatenate([s[:t - near_band], bottom], axis=0)
            out.append(s)
        return out

    def softmax_step(item, ss):
        qi, j, part = item
        first = j == 0 and part != "diag_lower"
        ps, alphas = [], []
        for c in range(2):
            s = ss[c]
            smax = jnp.max(s, axis=0, keepdims=True)
            if first:
                m_new, alpha = smax, None
            else:
                m_old = state[qi, c]["m"]
                m_old = m_old[:, half:] if part == "diag_lower" else m_old
                m_new = jnp.maximum(m_old, smax)
                alpha = jnp.exp2(m_old - m_new)
            ps.append(jnp.exp2(s - m_new).astype(BF16))
            alphas.append((m_new, alpha))
        return ps, alphas

    def values_step(item, ps, alphas):
        qi, j, part = item
        k_lo = j * t + (half if part == "diag_lower" else 0)
        k_hi = j * t + (half if part == "diag_upper" else t)
        vtb = vt_ref[:, k_lo:k_hi]
        for c in range(2):
            m_new, alpha = alphas[c]
            r = jnp.dot(vtb, ps[c], preferred_element_type=F32)
            if alpha is None:
                state[qi, c] = dict(m=m_new, acc=r)
            elif part == "diag_lower":
                st = state[qi, c]
                st["m"] = jnp.concatenate([st["m"][:, :half], m_new], axis=1)
                st["acc"] = jnp.concatenate([st["acc"][:, :half], alpha * st["acc"][:, half:] + r], axis=1)
            else:
                st = state[qi, c]
                st["m"] = m_new
                st["acc"] = alpha * st["acc"] + r
        if part == "diag_lower":
            outs = []
            for c in range(2):
                acc = state.pop((qi, c))["acc"]
                outs.append(acc[:LANES] * (1.0 / acc[LANES:LANES + 1]))
            ot = (outs[0] - lam * outs[1]).T
            ms = jnp.mean(ot * ot, axis=-1, keepdims=True)
            o_ref[0, qi * t:(qi + 1) * t, :] = (
                ((ot * lax.rsqrt(ms + EPS)) * g_ref[...]) * (1.0 - lam_init)).astype(o_ref.dtype)
            del qmask[qi]

    items = []
    for qi in range(nq):
        items += [(qi, j, "full") for j in range(qi)] + [(qi, qi, "diag_upper"), (qi, qi, "diag_lower")]
    ss = scores(items[0])
    for n, item in enumerate(items):
        nxt = scores(items[n + 1]) if n + 1 < len(items) else None
        ps, alphas = softmax_step(item, ss)
        values_step(item, ps, alphas)
        ss = nxt


def _diff_attention(proj3, bias, lq1, lk1, lq2, lk2, subln_g, *, t, lam_init):
    b, s, _ = proj3.shape
    vec = pl.BlockSpec((1, DIFF_HALF), lambda bi, h: (0, 0))
    head = lambda col: pl.BlockSpec((1, s, LANES), lambda bi, h: (bi, 0, col + h))
    first_far = int(np.argmax(_t5_bucket_table(2 * t) == N_BUCKETS - 1))
    near_band = -(-first_far // LANES) * LANES
    ones_rows = 16
    return pl.pallas_call(
        functools.partial(_diffattn_body, t=t, lam_init=lam_init, near_band=near_band),
        grid=(b, N_DIFF_HEADS),
        in_specs=[
            head(COL_DQ), head(COL_DK), head(COL_DV),
            pl.BlockSpec((1, 2, t, t), lambda bi, h: (h, 0, 0, 0)),
            vec, vec, vec, vec,
            pl.BlockSpec((1, LANES), lambda bi, h: (0, 0)),
        ],
        out_specs=pl.BlockSpec((1, s, LANES), lambda bi, h: (bi, 0, h)),
        out_shape=jax.ShapeDtypeStruct((b, s, N_DIFF_HEADS * LANES), BF16),
        scratch_shapes=[pltpu.VMEM((LANES + ones_rows, s), BF16)],
        compiler_params=_params(("arbitrary", "arbitrary")),
        name="diff_attention",
    )(proj3, proj3, proj3, bias, lq1, lk1, lq2, lk2, subln_g)


def _retention_tables(s):
    c = RET_CHUNK
    half = RET_QK // 2
    inv_freq = ROPE_BASE ** (-jnp.arange(0, RET_QK, 2, dtype=F32) / RET_QK)
    ang = jnp.arange(s, dtype=F32)[:, None] * inv_freq[None, :]
    cos, sin = jnp.cos(ang), jnp.sin(ang)
    cos_t = jnp.tile(cos, (1, LANES // half))
    sin_t = jnp.tile(jnp.concatenate([-sin, sin], axis=1), (1, LANES // RET_QK))
    log_g = jnp.log(1.0 - jnp.exp2(-5.0 - jnp.arange(N_RET_HEADS, dtype=F32)))
    n = jnp.arange(c, dtype=F32)
    diff = n[:, None] - n[None, :]
    inner = jnp.where(diff[None] >= 0, jnp.exp(jnp.maximum(diff, 0.0)[None] * log_g[:, None, None]), 0.0)
    cross = jnp.exp((n[None] + 1.0) * log_g[:, None])
    key = jnp.exp((c - 1.0 - n[None]) * log_g[:, None])
    chunk = jnp.exp(c * log_g)
    bc = lambda a: jnp.broadcast_to(a[..., None], a.shape + (LANES,)).astype(F32)
    return cos_t, sin_t, inner.astype(F32), bc(cross), bc(key), bc(chunk[:, None])


def _retention_body(rq_ref, rk_ref, rv_ref, rg_ref, cos_ref, sin_ref, inner_ref, cross_ref, key_ref,
                    chunk_ref, gn_ref, o_ref, state_ref):
    c = RET_CHUNK
    nc = rq_ref.shape[1] // c
    nh = N_RET_HEADS
    state_ref[...] = jnp.zeros(state_ref.shape, F32)
    lane = lax.broadcasted_iota(jnp.int32, (c, LANES), 1)
    first_half = (lane % RET_QK) < (RET_QK // 2)
    head_mask = [lane < RET_QK, lane >= RET_QK]
    slab = lambda i: slice(i * LANES, (i + 1) * LANES)

    def rotary(x, cos, sin):
        rot = jnp.where(first_half, pltpu.roll(x, LANES - RET_QK // 2, 1), pltpu.roll(x, RET_QK // 2, 1))
        return x * cos + rot * sin

    def chunk_step(ci, carry):
        r = pl.multiple_of(ci * c, c)
        rows = pl.ds(r, c)
        cos = cos_ref[rows, :]
        sin = sin_ref[rows, :]
        q_pair, k_pair, kb_pair = [], [], []
        for pr in range(nh // 2):
            q_pair.append(rotary(rq_ref[0, rows, slab(pr)].astype(F32), cos, sin))
            k = rotary(rk_ref[0, rows, slab(pr)].astype(F32), cos, sin) * (RET_QK ** -0.5)
            k_pair.append(k)
            kb_pair.append(k.astype(BF16))
        qh = [jnp.where(head_mask[h % 2], q_pair[h // 2], 0.0).astype(BF16) for h in range(nh)]
        kd = [(k_pair[h // 2] * key_ref[h]).astype(BF16) for h in range(nh)]
        vh = [rv_ref[0, rows, slab(h)] for h in range(nh)]
        scores = [lax.dot_general(qh[h], kb_pair[h // 2], _NT, preferred_element_type=F32) for h in range(nh)]
        cross = [jnp.dot(qh[h], state_ref[h].astype(BF16), preferred_element_type=F32) for h in range(nh)]
        kv = [lax.dot_general(kd[h], vh[h], _TN, preferred_element_type=F32) for h in range(nh)]
        decayed = [(scores[h] * inner_ref[h]).astype(BF16) for h in range(nh)]
        for h in range(nh):
            state_ref[h] = state_ref[h] * chunk_ref[h] + kv[h]
        inner = [jnp.dot(decayed[h], vh[h], preferred_element_type=F32) for h in range(nh)]
        for h in range(nh):
            y = inner[h] + cross[h] * cross_ref[h]
            mu = jnp.mean(y, axis=-1, keepdims=True)
            yc = y - mu
            var = jnp.mean(yc * yc, axis=-1, keepdims=True)
            yn = (yc * lax.rsqrt(var + EPS)) * gn_ref[:, slab(h)]
            gate = rg_ref[0, rows, slab(h)].astype(F32)
            silu = gate * (1.0 / (1.0 + jnp.exp(-gate)))
            o_ref[0, rows, slab(h)] = (silu * yn).astype(o_ref.dtype)
        return carry

    lax.fori_loop(0, nc, chunk_step, 0)


def _retention(proj3, tables, gn_g):
    b, s, _ = proj3.shape
    cos_t, sin_t, inner, cross, key, chunk = tables
    c = RET_CHUNK
    nh = N_RET_HEADS
    qk_w = nh * RET_QK
    v_w = nh * LANES
    full = lambda shape: pl.BlockSpec(shape, lambda bi: (0,) * len(shape))
    return pl.pallas_call(
        _retention_body,
        grid=(b,),
        in_specs=[
            pl.BlockSpec((1, s, qk_w), lambda bi: (bi, 0, COL_RQ * LANES // qk_w)),
            pl.BlockSpec((1, s, qk_w), lambda bi: (bi, 0, COL_RK * LANES // qk_w)),
            pl.BlockSpec((1, s, v_w), lambda bi: (bi, 0, COL_RV * LANES // v_w)),
            pl.BlockSpec((1, s, v_w), lambda bi: (bi, 0, COL_RG * LANES // v_w)),
            full((s, LANES)), full((s, LANES)),
            full((nh, c, c)), full((nh, c, LANES)), full((nh, c, LANES)), full((nh, 1, LANES)),
            full((1, v_w)),
        ],
        out_specs=pl.BlockSpec((1, s, v_w), lambda bi: (bi, 0, 0)),
        out_shape=jax.ShapeDtypeStruct((b, s, v_w), BF16),
        scratch_shapes=[pltpu.VMEM((nh, LANES, LANES), F32)],
        compiler_params=_params(("arbitrary",)),
        name="retention",
    )(proj3, proj3, proj3, proj3, cos_t, sin_t, inner, cross, key, chunk, gn_g)


def _route(logits):
    lane = lax.broadcasted_iota(jnp.int32, logits.shape, 1).astype(F32)
    big = float(LANES)
    is_group = lane < N_GROUPS
    gl = jnp.where(is_group, logits, -jnp.inf)
    gmax = jnp.max(gl, axis=-1, keepdims=True)
    gidx = jnp.min(jnp.where(gl == gmax, lane, big), axis=-1, keepdims=True)
    p_group = 1.0 / jnp.sum(jnp.where(is_group, jnp.exp(gl - gmax), 0.0), axis=-1, keepdims=True)
    lo = N_GROUPS + EXPERTS_PER_GROUP * gidx
    il = jnp.where(lane >= lo, jnp.where(lane < lo + EXPERTS_PER_GROUP, logits, -jnp.inf), -jnp.inf)
    v1 = jnp.max(il, axis=-1, keepdims=True)
    i1 = jnp.min(jnp.where(il == v1, lane, big), axis=-1, keepdims=True)
    il2 = jnp.where(lane == i1, -jnp.inf, il)
    v2 = jnp.max(il2, axis=-1, keepdims=True)
    i2 = jnp.min(jnp.where(il2 == v2, lane, big), axis=-1, keepdims=True)
    e2 = jnp.exp(v2 - v1)
    inv = 1.0 / (1.0 + e2)
    g1 = p_group * inv
    g2 = p_group * (e2 * inv)
    out = jnp.where(lane == 0.0, i1 - N_GROUPS,
                    jnp.where(lane == 1.0, i2 - N_GROUPS,
                              jnp.where(lane == 2.0, g1, jnp.where(lane == 3.0, g2, 0.0))))
    return out


def _outproj_body(a_ref, r_ref, x_ref, w_ref, g_ref, wr_ref, br_ref, x1_ref, h2_ref, route_ref, *, sub):
    tm = x_ref.shape[0]
    half = a_ref.shape[1]
    n_sub = tm // sub

    def project(i):
        rows = slice(i * sub, (i + 1) * sub)
        acc = jnp.dot(a_ref[rows, :], w_ref[:half, :], preferred_element_type=F32)
        return acc + jnp.dot(r_ref[rows, :], w_ref[half:, :], preferred_element_type=F32)

    def epilogue(i, acc):
        rows = slice(i * sub, (i + 1) * sub)
        x1 = x_ref[rows, :] + acc
        x1_ref[rows, :] = x1
        ms = jnp.mean(x1 * x1, axis=-1, keepdims=True)
        h2 = (x1 * lax.rsqrt(ms + EPS)) * g_ref[...]
        h2_ref[rows, :] = _pack_halves(h2)
        logits = jnp.dot(h2.astype(BF16), wr_ref[...], preferred_element_type=F32) + br_ref[...]
        route_ref[rows, :] = _route(logits)

    acc = project(0)
    for i in range(n_sub):
        nxt = project(i + 1) if i + 1 < n_sub else None
        epilogue(i, acc)
        acc = nxt


def _outproj_router(a2d, r2d, x2d, w_out_bf16, g, wr_bf16, br, *, tm):
    n, d = x2d.shape
    half = a2d.shape[1]
    row = lambda width: pl.BlockSpec((tm, width), lambda i: (i, 0))
    full = lambda shape: pl.BlockSpec(shape, lambda i: (0, 0))
    return pl.pallas_call(
        functools.partial(_outproj_body, sub=min(tm, 2 * LANES)),
        grid=(n // tm,),
        in_specs=[row(half), row(half), row(d), full((2 * half, d)), full((1, d)), full((d, LANES)),
                  full((1, LANES))],
        out_specs=[row(d), row(d // 2), row(LANES)],
        out_shape=[jax.ShapeDtypeStruct((n, d), F32), jax.ShapeDtypeStruct((n, d // 2), jnp.uint32),
                   jax.ShapeDtypeStruct((n, LANES), F32)],
        compiler_params=_params(("arbitrary",)),
        name="outproj_router",
    )(a2d, r2d, x2d, w_out_bf16, g, wr_bf16, br)


def _moe_plan(expert_id, tm):
    n_slots = expert_id.shape[0]
    n_tiles = n_slots // tm
    order = jnp.argsort(expert_id).astype(jnp.int32)
    sizes = jnp.zeros((N_EXPERTS,), jnp.int32).at[expert_id].add(1)
    ends = jnp.cumsum(sizes)
    starts = ends - sizes
    tile_row0 = jnp.arange(n_tiles, dtype=jnp.int32) * tm
    owner = lambda row: jnp.sum((ends[None, :] <= row[:, None]).astype(jnp.int32), axis=1)
    first_e = owner(tile_row0)
    last_e = owner(tile_row0 + (tm - 1))
    owners_upto = jnp.cumsum((sizes > 0).astype(jnp.int32))
    n_extra = (owners_upto[last_e] - owners_upto[first_e]).astype(jnp.int32)
    prev_last = jnp.concatenate([jnp.full((1,), -1, jnp.int32), last_e[:-1]])
    switch = (prev_last != first_e).astype(jnp.int32)
    eid = jnp.arange(N_EXPERTS, dtype=jnp.int32)
    cand = jnp.where((sizes[None, :] > 0) & (eid[None, :] > eid[:, None]), eid[None, :], N_EXPERTS)
    nxt = jnp.min(cand, axis=1)
    next_expert = jnp.where(nxt < N_EXPERTS, nxt, -1).astype(jnp.int32)
    spare = 2 * (n_slots + jnp.arange(tm, dtype=jnp.int32))
    codes = jnp.concatenate([order.reshape(n_tiles, tm), spare[None, :]], axis=0).reshape(n_tiles + 1, 1, tm)
    return dict(first_e=first_e, n_extra=n_extra, switch=switch, starts=starts.astype(jnp.int32),
                ends=ends.astype(jnp.int32), next_expert=next_expert, codes=codes)


def _moe_body(fe_ref, nextra_ref, sw_ref, st_ref, en_ref, nx_ref,
              codes_hbm, h2_hbm, wg_hbm, wu_hbm, wd_hbm, ys_hbm,
              gidx, sidx, xbuf, obuf, wg_st, wu_st, wd_st, wg_rs, wu_rs, wd_rs,
              gisem, sisem, gsem, ssem, wsem, *, n_tokens, n_tiles, cast_rows, ff_chunks):
    t = pl.program_id(0)
    tm = xbuf.shape[1]
    last = n_tiles - 1
    spare_codes = n_tiles

    def gidx_copy(tile, slot):
        return pltpu.make_async_copy(codes_hbm.at[tile], gidx.at[slot], gisem.at[slot])

    def sidx_copy(tile, slot):
        return pltpu.make_async_copy(codes_hbm.at[tile], sidx.at[slot], sisem.at[slot])

    def weight_copies(ex):
        return (pltpu.make_async_copy(wg_hbm.at[ex], wg_st, wsem.at[0]),
                pltpu.make_async_copy(wu_hbm.at[ex], wu_st, wsem.at[1]),
                pltpu.make_async_copy(wd_hbm.at[ex], wd_st, wsem.at[2]))

    def gather_row(slot, r):
        tok = lax.shift_right_logical(gidx[slot, 0, r], 1)
        return pltpu.make_async_copy(h2_hbm.at[pl.ds(tok, 1)], xbuf.at[slot, pl.ds(r, 1)], gsem.at[slot])

    def scatter_row(slot, r):
        code = sidx[slot, 0, r]
        dst = (code & 1) * n_tokens + lax.shift_right_logical(code, 1)
        return pltpu.make_async_copy(obuf.at[slot, pl.ds(r, 1)], ys_hbm.at[pl.ds(dst, 1)], ssem.at[slot])

    def wait_gather(slot):
        pltpu.make_async_copy(h2_hbm.at[pl.ds(0, tm)], xbuf.at[slot], gsem.at[slot]).wait()

    def wait_scatter(slot):
        pltpu.make_async_copy(obuf.at[slot], ys_hbm.at[pl.ds(0, tm)], ssem.at[slot]).wait()

    def for_rows(fn):
        def body(r, carry):
            fn(r)
            return carry
        lax.fori_loop(0, tm, body, 0, unroll=8)

    def take_over(ex, first):
        @pl.when(first)
        def _():
            for cp in weight_copies(ex):
                cp.start()

        for cp in weight_copies(ex):
            cp.wait()

        def cast_up(i, carry):
            r = pl.multiple_of(i * cast_rows, cast_rows)
            wg_rs[pl.ds(r, cast_rows), :] = wg_st[pl.ds(r, cast_rows), :].astype(BF16)
            wu_rs[pl.ds(r, cast_rows), :] = wu_st[pl.ds(r, cast_rows), :].astype(BF16)
            return carry

        def cast_down(i, carry):
            r = pl.multiple_of(i * cast_rows, cast_rows)
            wd_rs[pl.ds(r, cast_rows), :] = wd_st[pl.ds(r, cast_rows), :].astype(BF16)
            return carry

        lax.fori_loop(0, wg_st.shape[0] // cast_rows, cast_up, 0)
        lax.fori_loop(0, wd_st.shape[0] // cast_rows, cast_down, 0)
        nxt = nx_ref[ex]

        @pl.when(nxt >= 0)
        def _():
            for cp in weight_copies(nxt):
                cp.start()

    def ffn(slot, ex, accumulate, row_dmas=None):
        x_lo, x_hi = (h.astype(BF16) for h in _unpack_halves(xbuf[slot]))
        hd = x_lo.shape[1]
        row = t * tm + lax.broadcasted_iota(jnp.int32, (tm, 1), 0)
        mine = jnp.logical_and(row >= st_ref[ex], row < en_ref[ex])
        d_ff = wg_rs.shape[1]
        n_chunks = ff_chunks if row_dmas is not None else 1
        fc = d_ff // n_chunks
        rows = tm // n_chunks
        contrib = None
        for c in range(n_chunks):
            cols = slice(c * fc, (c + 1) * fc)
            up = lambda w: (jnp.dot(x_lo, w[:hd, cols], preferred_element_type=F32)
                            + jnp.dot(x_hi, w[hd:, cols], preferred_element_type=F32))
            a = up(wg_rs)
            b = up(wu_rs)
            hmid = jnp.where(mine, (a * (1.0 / (1.0 + jnp.exp(-a)))) * b, 0.0).astype(BF16)
            part = jnp.dot(hmid, wd_rs[cols, :], preferred_element_type=F32)
            contrib = part if contrib is None else contrib + part
            if row_dmas is not None:
                row_dmas(c * rows, (c + 1) * rows)
        if accumulate:
            prev_lo, prev_hi = _unpack_halves(obuf[slot])
            contrib = jnp.concatenate([prev_lo, prev_hi], axis=1) + contrib
        obuf[slot] = _pack_halves(contrib)

    e0 = fe_ref[t]

    def step(p):
        q = 1 - p
        if p == 0:
            @pl.when(t == 0)
            def _():
                copies = (gidx_copy(0, 0), gidx_copy(min(1, last), 1), sidx_copy(spare_codes, 1))
                for cp in copies:
                    cp.start()
                for cp in copies:
                    cp.wait()
                obuf[1] = jnp.zeros(obuf.shape[1:], obuf.dtype)
                for_rows(lambda r: gather_row(0, r).start())

        gidx_copy(jnp.minimum(t + 2, last), p).start()
        sidx_copy(t, p).start()

        @pl.when(t >= 1)
        def _():
            gidx_copy(0, q).wait()
            sidx_copy(0, q).wait()
            wait_scatter(p)

        wait_gather(p)

        @pl.when(sw_ref[t] == 1)
        def _():
            take_over(e0, t == 0)

        def row_dmas(lo, hi):
            for r in range(lo, hi):
                scatter_row(q, r).start(priority=r % 2)
                gather_row(q, r).start(priority=r % 2)

        ffn(p, e0, accumulate=False, row_dmas=row_dmas)

    for parity in range(2):
        pl.when(lax.rem(t, 2) == parity)(functools.partial(step, parity))

    slot = lax.rem(t, 2)

    def further_owner(_, ex):
        nxt = nx_ref[ex]
        take_over(nxt, False)
        ffn(slot, nxt, accumulate=True)
        return nxt

    lax.fori_loop(0, nextra_ref[t], further_owner, e0)

    @pl.when(t == last)
    def _():
        gidx_copy(0, slot).wait()
        sidx_copy(0, slot).wait()
        for_rows(lambda r: scatter_row(slot, r).start())
        wait_gather(1 - slot)
        wait_scatter(1 - slot)
        wait_scatter(slot)


def _moe_ffn(h2, plan, w_gate, w_up, w_down, *, tm):
    n, hd = h2.shape
    d = 2 * hd
    n_slots = 2 * n
    n_tiles = n_slots // tm
    assert n_tiles >= 2 and n_slots % tm == 0
    d_ff = w_gate.shape[2]
    any_spec = pl.BlockSpec(memory_space=pl.ANY)
    grid_spec = pltpu.PrefetchScalarGridSpec(
        num_scalar_prefetch=6,
        grid=(n_tiles,),
        in_specs=[any_spec] * 5,
        out_specs=any_spec,
        scratch_shapes=[
            pltpu.SMEM((2, 1, tm), jnp.int32),
            pltpu.SMEM((2, 1, tm), jnp.int32),
            pltpu.VMEM((2, tm, hd), jnp.uint32),
            pltpu.VMEM((2, tm, hd), jnp.uint32),
            pltpu.VMEM((d, d_ff), F32),
            pltpu.VMEM((d, d_ff), F32),
            pltpu.VMEM((d_ff, d), F32),
            pltpu.VMEM((d, d_ff), BF16),
            pltpu.VMEM((d, d_ff), BF16),
            pltpu.VMEM((d_ff, d), BF16),
            pltpu.SemaphoreType.DMA((2,)),
            pltpu.SemaphoreType.DMA((2,)),
            pltpu.SemaphoreType.DMA((2,)),
            pltpu.SemaphoreType.DMA((2,)),
            pltpu.SemaphoreType.DMA((3,)),
        ],
    )
    return pl.pallas_call(
        functools.partial(_moe_body, n_tokens=n, n_tiles=n_tiles, cast_rows=128,
                          ff_chunks=d_ff // (2 * LANES)),
        grid_spec=grid_spec,
        out_shape=jax.ShapeDtypeStruct((n_slots + tm, hd), jnp.uint32),
        compiler_params=_params(("arbitrary",), vmem=MOE_VMEM_LIMIT),
        name="moe_ffn",
    )(plan["first_e"], plan["n_extra"], plan["switch"], plan["starts"], plan["ends"], plan["next_expert"],
      plan["codes"], h2, w_gate, w_up, w_down)


def _final_body(x1_ref, ys0_ref, ys1_ref, route_ref, g_ref, o_ref):
    d = x1_ref.shape[1]
    route = route_ref[...]
    g1, g2 = route[:, 2:3], route[:, 3:4]
    ya = _unpack_halves(ys0_ref[...])
    yb = _unpack_halves(ys1_ref[...])
    halves = (slice(0, d // 2), slice(d // 2, d))
    x2 = [x1_ref[:, cols] + (ya[i] * g1 + yb[i] * g2) for i, cols in enumerate(halves)]
    ms = sum(jnp.sum(h * h, axis=-1, keepdims=True) for h in x2) * (1.0 / d)
    inv = lax.rsqrt(ms + EPS)
    for h, cols in zip(x2, halves):
        o_ref[:, cols] = ((h * inv) * g_ref[:, cols]).astype(o_ref.dtype)


def _final(x1, ys, route, g, *, tm):
    n, d = x1.shape
    nb = n // tm
    return pl.pallas_call(
        _final_body,
        grid=(nb,),
        in_specs=[
            pl.BlockSpec((tm, d), lambda i: (i, 0)),
            pl.BlockSpec((tm, d // 2), lambda i: (i, 0)),
            pl.BlockSpec((tm, d // 2), lambda i: (nb + i, 0)),
            pl.BlockSpec((tm, LANES), lambda i: (i, 0)),
            pl.BlockSpec((1, d), lambda i: (0, 0)),
        ],
        out_specs=pl.BlockSpec((tm, d), lambda i: (i, 0)),
        out_shape=jax.ShapeDtypeStruct((n, d), F32),
        compiler_params=_params(("arbitrary",)),
        name="final_norm",
    )(x1, ys, ys, route, g)


def _layer(x, rel_bias, attn_norm_g, w_in, lq1, lk1, lq2, lk2, diff_subln_g, ret_gn_g, w_out, ffn_norm_g,
           w_group_router, b_group, w_inner_router, b_inner, w_gate_exp, w_up_exp, w_down_exp, *, layer,
           attn_tile, row_tile, moe_tile):
    b, s, d = x.shape
    n = b * s
    x2d = x.reshape(n, d)
    lam_init = 0.8 - 0.6 * math.exp(-0.3 * layer)

    dq_cols = (COL_DK - COL_DQ) * LANES
    col_scale = jnp.concatenate([jnp.full((1, dq_cols), DIFF_HALF ** -0.5 * LOG2E, F32),
                                 jnp.ones((1, IN_COLS - dq_cols), F32)], axis=1)
    proj = _inproj(x2d, attn_norm_g.reshape(1, d), w_in.astype(BF16), col_scale, tm=min(2 * row_tile, n),
                   tn=8 * LANES)
    proj3 = proj.reshape(b, s, IN_COLS)

    bias = _bias_tiles(rel_bias, attn_tile)
    vec = lambda a: a.reshape(1, -1)
    a_out = _diff_attention(proj3, bias, vec(lq1), vec(lk1), vec(lq2), vec(lk2), vec(diff_subln_g),
                            t=attn_tile, lam_init=lam_init)
    r_out = _retention(proj3, _retention_tables(s), vec(ret_gn_g))

    pad = LANES - N_GROUPS - N_EXPERTS
    wr = jnp.concatenate([w_group_router, jnp.transpose(w_inner_router, (1, 0, 2)).reshape(d, N_EXPERTS),
                          jnp.zeros((d, pad), F32)], axis=1).astype(BF16)
    br = jnp.concatenate([b_group, b_inner.reshape(-1), jnp.zeros((pad,), F32)]).reshape(1, LANES)
    x1, h2, route = _outproj_router(a_out.reshape(n, -1), r_out.reshape(n, -1), x2d, w_out.astype(BF16),
                                    vec(ffn_norm_g), wr, br, tm=row_tile)

    expert_id = route[:, :2].astype(jnp.int32).reshape(-1)
    plan = _moe_plan(expert_id, moe_tile)
    ys = _moe_ffn(h2, plan, w_gate_exp, w_up_exp, w_down_exp, tm=moe_tile)
    return x1, ys, route


def kernel(x, rel_bias, attn_norm_g, w_in, lambda_q1, lambda_k1, lambda_q2, lambda_k2, diff_subln_g, ret_gn_g,
           w_out, ffn_norm_g, w_group_router, b_group, w_inner_router, b_inner, w_gate_exp, w_up_exp, w_down_exp,
           final_g):
    b, s, d = x.shape
    depth = w_in.shape[0]
    assert depth == 1, "the final norm is fused into the single layer's combine step"
    first = lambda a: a.reshape(a.shape[1:])
    x1, ys, route = _layer(x, rel_bias, attn_norm_g[0], w_in[0], lambda_q1[0], lambda_k1[0], lambda_q2[0],
                           lambda_k2[0], diff_subln_g[0], ret_gn_g[0], w_out[0], ffn_norm_g[0],
                           w_group_router[0], b_group[0], w_inner_router[0], b_inner[0], first(w_gate_exp),
                           first(w_up_exp), first(w_down_exp), layer=0, attn_tile=min(512, s), row_tile=512,
                           moe_tile=256)
    out = _final(x1, ys, route, final_g.reshape(1, d), tm=256)
    return out.reshape(b, s, d)
```

```python
import functools
import math

import numpy as np
import jax
import jax.numpy as jnp
from jax import lax
from jax.experimental import pallas as pl
from jax.experimental.pallas import tpu as pltpu

F32 = jnp.float32
BF16 = jnp.bfloat16

EPS = 1e-6
LANES = 128
N_DIFF_HEADS = 8
DIFF_HALF = 64
N_RET_HEADS = 8
RET_QK = 64
RET_CHUNK = 128
ROPE_BASE = 10000.0
N_BUCKETS = 32
MAX_DISTANCE = 128
N_GROUPS = 4
EXPERTS_PER_GROUP = 8
N_EXPERTS = N_GROUPS * EXPERTS_PER_GROUP
TOP_K = 2
MASK_VALUE = -1e30
LOG2E = math.log2(math.e)
VMEM_LIMIT = 56 * 1024 * 1024
MOE_VMEM_LIMIT = 60 * 1024 * 1024

COL_DQ, COL_DK, COL_DV = 0, 8, 16
COL_RQ, COL_RK, COL_RV, COL_RG = 24, 28, 32, 40
IN_COLS = 48 * LANES


def _params(sem, vmem=VMEM_LIMIT):
    return pltpu.CompilerParams(dimension_semantics=sem, vmem_limit_bytes=vmem)


def _pack_halves(x):
    hd = x.shape[1] // 2
    bits = lambda h: lax.bitcast_convert_type(h.astype(BF16).astype(F32), jnp.uint32)
    return lax.shift_right_logical(bits(x[:, :hd]), jnp.uint32(16)) | bits(x[:, hd:])


def _unpack_halves(w):
    return (lax.bitcast_convert_type(lax.shift_left(w, jnp.uint32(16)), F32),
            lax.bitcast_convert_type(w & jnp.uint32(0xFFFF0000), F32))


def _inproj_body(x_ref, g_ref, w_ref, cs_ref, o_ref):
    x = x_ref[...]
    acc = jnp.dot((x * g_ref[...]).astype(BF16), w_ref[...], preferred_element_type=F32)
    inv = lax.rsqrt(jnp.mean(x * x, axis=-1, keepdims=True) + EPS)
    o_ref[...] = ((acc * inv) * cs_ref[...]).astype(o_ref.dtype)


def _inproj(x2d, g, w, col_scale, *, tm, tn):
    n, d = x2d.shape
    cols = w.shape[1]
    return pl.pallas_call(
        _inproj_body,
        grid=(n // tm, cols // tn),
        in_specs=[
            pl.BlockSpec((tm, d), lambda i, j: (i, 0)),
            pl.BlockSpec((1, d), lambda i, j: (0, 0)),
            pl.BlockSpec((d, tn), lambda i, j: (0, j)),
            pl.BlockSpec((1, tn), lambda i, j: (0, j)),
        ],
        out_specs=pl.BlockSpec((tm, tn), lambda i, j: (i, j)),
        out_shape=jax.ShapeDtypeStruct((n, cols), BF16),
        compiler_params=_params(("arbitrary", "arbitrary")),
        name="inproj",
    )(x2d, g, w, col_scale)


def _t5_bucket_table(n):
    d = np.arange(n)
    max_exact = N_BUCKETS // 2
    log_ratio = np.log(np.maximum(d, 1).astype(np.float64) / max_exact) / math.log(MAX_DISTANCE / max_exact)
    large = np.minimum(max_exact + (log_ratio * (N_BUCKETS - max_exact)).astype(np.int64), N_BUCKETS - 1)
    return np.where(d < max_exact, d, large).astype(np.int32)


def _bias_body(rb_ref, bkt_ref, o_ref):
    h = pl.program_id(0)
    t = o_ref.shape[2]
    bkt = bkt_ref[...]
    far = rb_ref[N_BUCKETS - 1, h]
    by_dist = jnp.zeros(bkt.shape, F32)
    for b in range(N_BUCKETS - 1):
        by_dist = jnp.where(bkt == b, (rb_ref[b, h] - far) * LOG2E, by_dist)
    rows = (jnp.concatenate([jnp.full((1, t), MASK_VALUE, F32), by_dist[:, :t]], axis=1), by_dist)
    for i, row in enumerate(rows):
        shifted = pltpu.roll(jnp.broadcast_to(row, (t, 2 * t)), 0, 1, stride=1, stride_axis=0)
        o_ref[0, i] = shifted[:, t:]


def _bias_tiles(rel_bias, t):
    bkt = jnp.asarray(_t5_bucket_table(2 * t)).reshape(1, 2 * t)
    return pl.pallas_call(
        _bias_body,
        grid=(N_DIFF_HEADS,),
        in_specs=[
            pl.BlockSpec(memory_space=pltpu.SMEM),
            pl.BlockSpec((1, 2 * t), lambda h: (0, 0)),
        ],
        out_specs=pl.BlockSpec((1, 2, t, t), lambda h: (h, 0, 0, 0)),
        out_shape=jax.ShapeDtypeStruct((N_DIFF_HEADS, 2, t, t), F32),
        compiler_params=_params(("arbitrary",)),
        name="bias_tiles",
    )(rel_bias, bkt)


_NT = (((1,), (1,)), ((), ()))
_TN = (((0,), (0,)), ((), ()))


def _diffattn_body(q_ref, k_ref, v_ref, bias_ref, lq1_ref, lk1_ref, lq2_ref, lk2_ref, g_ref, o_ref, vt_ref, *,
                   t, lam_init, near_band):
    nq = k_ref.shape[1] // t
    half = t // 2
    lane = lax.broadcasted_iota(jnp.int32, (t, LANES), 1)
    lam = (jnp.exp(jnp.sum(lq1_ref[...] * lk1_ref[...], axis=-1, keepdims=True))
           - jnp.exp(jnp.sum(lq2_ref[...] * lk2_ref[...], axis=-1, keepdims=True)) + lam_init)
    vt_ref[:LANES, :] = v_ref[0].T
    vt_ref[LANES:, :] = jnp.ones((vt_ref.shape[0] - LANES, vt_ref.shape[1]), BF16)
    qmask = {}
    run_max = {}
    acc = {}

    def scores(item):
        qi, j, part = item
        if qi not in qmask:
            q = q_ref[0, qi * t:(qi + 1) * t, :]
            zero = jnp.zeros_like(q)
            qmask[qi] = (jnp.where(lane < DIFF_HALF, q, zero), jnp.where(lane >= DIFF_HALF, q, zero))
        k_lo = j * t + (half if part == "diag_lower" else 0)
        k_hi = j * t + (half if part == "diag_upper" else t)
        kb = k_ref[0, k_lo:k_hi, :]
        out = []
        for c in range(2):
            qsel = qmask[qi][c][half:] if part == "diag_lower" else qmask[qi][c]
            s = lax.dot_general(kb, qsel, _NT, preferred_element_type=F32)
            if part == "diag_upper":
                s = s + bias_ref[0, 0, :half, :]
            elif part == "diag_lower":
                s = s + bias_ref[0, 0, half:, half:]
            elif j == qi - 1:
                if near_band >= t:
                    s = s + bias_ref[0, 1]
                else:
                    corner = s[t - near_band:, :near_band] + bias_ref[0, 1, t - near_band:, :near_band]
                    bottom = jnp.concatenate([corner, s[t - near_band:, near_band:]], axis=1)
                    s = jnp.concatenate([s[:t - near_band], bottom], axis=0)
            out.append(s)
        return out

    def softmax_step(item, ss):
        qi, j, part = item
        first = j == 0 and part != "diag_lower"
        ps, alphas = [], []
        for c in range(2):
            s = ss[c]
            smax = jnp.max(s, axis=0, keepdims=True)
            if first:
                m_new, alpha = smax, None
                run_max[qi, c] = m_new
            else:
                m_full = run_max[qi, c]
                m_old = m_full[:, half:] if part == "diag_lower" else m_full
                m_new = jnp.maximum(m_old, smax)
                alpha = jnp.exp2(m_old - m_new)
                run_max[qi, c] = (jnp.concatenate([m_full[:, :half], m_new], axis=1)
                                  if part == "diag_lower" else m_new)
            ps.append(jnp.exp2(s - m_new).astype(BF16))
            alphas.append(alpha)
        return ps, alphas

    def values_step(item, ps, alphas):
        qi, j, part = item
        k_lo = j * t + (half if part == "diag_lower" else 0)
        k_hi = j * t + (half if part == "diag_upper" else t)
        vtb = vt_ref[:, k_lo:k_hi]
        for c in range(2):
            alpha = alphas[c]
            r = jnp.dot(vtb, ps[c], preferred_element_type=F32)
            if alpha is None:
                acc[qi, c] = r
            elif part == "diag_lower":
                acc[qi, c] = jnp.concatenate([acc[qi, c][:, :half], alpha * acc[qi, c][:, half:] + r], axis=1)
            else:
                acc[qi, c] = alpha * acc[qi, c] + r
        if part == "diag_lower":
            outs = []
            for c in range(2):
                a = acc.pop((qi, c))
                del run_max[qi, c]
                outs.append(a[:LANES] * (1.0 / a[LANES:LANES + 1]))
            ot = (outs[0] - lam * outs[1]).T
            ms = jnp.mean(ot * ot, axis=-1, keepdims=True)
            o_ref[0, qi * t:(qi + 1) * t, :] = (
                ((ot * lax.rsqrt(ms + EPS)) * g_ref[...]) * (1.0 - lam_init)).astype(o_ref.dtype)
            del qmask[qi]

    items = []
    for qi in range(nq):
        items += [(qi, j, "full") for j in range(qi)] + [(qi, qi, "diag_upper"), (qi, qi, "diag_lower")]
    ss = scores(items[0])
    pending = None
    for n, item in enumerate(items):
        nxt = scores(items[n + 1]) if n + 1 < len(items) else None
        ps, alphas = softmax_step(item, ss)
        if pending is not None:
            values_step(*pending)
        pending = (item, ps, alphas)
        ss = nxt
    values_step(*pending)


def _diff_attention(proj3, bias, lq1, lk1, lq2, lk2, subln_g, *, t, lam_init):
    b, s, _ = proj3.shape
    vec = pl.BlockSpec((1, DIFF_HALF), lambda bi, h: (0, 0))
    head = lambda col: pl.BlockSpec((1, s, LANES), lambda bi, h: (bi, 0, col + h))
    first_far = int(np.argmax(_t5_bucket_table(2 * t) == N_BUCKETS - 1))
    near_band = -(-first_far // LANES) * LANES
    ones_rows = 16
    return pl.pallas_call(
        functools.partial(_diffattn_body, t=t, lam_init=lam_init, near_band=near_band),
        grid=(b, N_DIFF_HEADS),
        in_specs=[
            head(COL_DQ), head(COL_DK), head(COL_DV),
            pl.BlockSpec((1, 2, t, t), lambda bi, h: (h, 0, 0, 0)),
            vec, vec, vec, vec,
            pl.BlockSpec((1, LANES), lambda bi, h: (0, 0)),
        ],
        out_specs=pl.BlockSpec((1, s, LANES), lambda bi, h: (bi, 0, h)),
        out_shape=jax.ShapeDtypeStruct((b, s, N_DIFF_HEADS * LANES), BF16),
        scratch_shapes=[pltpu.VMEM((LANES + ones_rows, s), BF16)],
        compiler_params=_params(("arbitrary", "arbitrary")),
        name="diff_attention",
    )(proj3, proj3, proj3, bias, lq1, lk1, lq2, lk2, subln_g)


def _retention_tables(s):
    c = RET_CHUNK
    half = RET_QK // 2
    inv_freq = ROPE_BASE ** (-jnp.arange(0, RET_QK, 2, dtype=F32) / RET_QK)
    ang = jnp.arange(s, dtype=F32)[:, None] * inv_freq[None, :]
    cos, sin = jnp.cos(ang), jnp.sin(ang)
    cos_t = jnp.tile(cos, (1, LANES // half))
    sin_t = jnp.tile(jnp.concatenate([-sin, sin], axis=1), (1, LANES // RET_QK))
    log_g = jnp.log(1.0 - jnp.exp2(-5.0 - jnp.arange(N_RET_HEADS, dtype=F32)))
    n = jnp.arange(c, dtype=F32)
    diff = n[:, None] - n[None, :]
    inner = jnp.where(diff[None] >= 0, jnp.exp(jnp.maximum(diff, 0.0)[None] * log_g[:, None, None]), 0.0)
    cross = jnp.exp((n[None] + 1.0) * log_g[:, None])
    key = jnp.exp((c - 1.0 - n[None]) * log_g[:, None])
    chunk = jnp.exp(c * log_g)
    bc = lambda a: jnp.broadcast_to(a[..., None], a.shape + (LANES,)).astype(F32)
    return cos_t, sin_t, inner.astype(F32), bc(cross), bc(key), bc(chunk[:, None])


def _retention_body(rq_ref, rk_ref, rv_ref, rg_ref, cos_ref, sin_ref, inner_ref, cross_ref, key_ref,
                    chunk_ref, gn_ref, o_ref, state_ref):
    c = RET_CHUNK
    nc = rq_ref.shape[1] // c
    nh = N_RET_HEADS
    state_ref[...] = jnp.zeros(state_ref.shape, F32)
    lane = lax.broadcasted_iota(jnp.int32, (c, LANES), 1)
    first_half = (lane % RET_QK) < (RET_QK // 2)
    head_mask = [lane < RET_QK, lane >= RET_QK]
    slab = lambda i: slice(i * LANES, (i + 1) * LANES)

    def rotary(x, cos, sin):
        rot = jnp.where(first_half, pltpu.roll(x, LANES - RET_QK // 2, 1), pltpu.roll(x, RET_QK // 2, 1))
        return x * cos + rot * sin

    def chunk_step(ci, carry):
        r = pl.multiple_of(ci * c, c)
        rows = pl.ds(r, c)
        cos = cos_ref[rows, :]
        sin = sin_ref[rows, :]
        q_pair, k_pair, kb_pair = [], [], []
        for pr in range(nh // 2):
            q_pair.append(rotary(rq_ref[0, rows, slab(pr)].astype(F32), cos, sin))
            k = rotary(rk_ref[0, rows, slab(pr)].astype(F32), cos, sin) * (RET_QK ** -0.5)
            k_pair.append(k)
            kb_pair.append(k.astype(BF16))
        qh = [jnp.where(head_mask[h % 2], q_pair[h // 2], 0.0).astype(BF16) for h in range(nh)]
        kd = [(k_pair[h // 2] * key_ref[h]).astype(BF16) for h in range(nh)]
        vh = [rv_ref[0, rows, slab(h)] for h in range(nh)]
        scores = [lax.dot_general(qh[h], kb_pair[h // 2], _NT, preferred_element_type=F32) for h in range(nh)]
        cross = [jnp.dot(qh[h], state_ref[h].astype(BF16), preferred_element_type=F32) for h in range(nh)]
        kv = [lax.dot_general(kd[h], vh[h], _TN, preferred_element_type=F32) for h in range(nh)]
        decayed = [(scores[h] * inner_ref[h]).astype(BF16) for h in range(nh)]
        for h in range(nh):
            state_ref[h] = state_ref[h] * chunk_ref[h] + kv[h]
        inner = [jnp.dot(decayed[h], vh[h], preferred_element_type=F32) for h in range(nh)]
        for h in range(nh):
            y = inner[h] + cross[h] * cross_ref[h]
            mu = jnp.mean(y, axis=-1, keepdims=True)
            yc = y - mu
            var = jnp.mean(yc * yc, axis=-1, keepdims=True)
            yn = (yc * lax.rsqrt(var + EPS)) * gn_ref[:, slab(h)]
            gate = rg_ref[0, rows, slab(h)].astype(F32)
            silu = gate * (1.0 / (1.0 + jnp.exp(-gate)))
            o_ref[0, rows, slab(h)] = (silu * yn).astype(o_ref.dtype)
        return carry

    lax.fori_loop(0, nc, chunk_step, 0)


def _retention(proj3, tables, gn_g):
    b, s, _ = proj3.shape
    cos_t, sin_t, inner, cross, key, chunk = tables
    c = RET_CHUNK
    nh = N_RET_HEADS
    qk_w = nh * RET_QK
    v_w = nh * LANES
    full = lambda shape: pl.BlockSpec(shape, lambda bi: (0,) * len(shape))
    return pl.pallas_call(
        _retention_body,
        grid=(b,),
        in_specs=[
            pl.BlockSpec((1, s, qk_w), lambda bi: (bi, 0, COL_RQ * LANES // qk_w)),
            pl.BlockSpec((1, s, qk_w), lambda bi: (bi, 0, COL_RK * LANES // qk_w)),
            pl.BlockSpec((1, s, v_w), lambda bi: (bi, 0, COL_RV * LANES // v_w)),
            pl.BlockSpec((1, s, v_w), lambda bi: (bi, 0, COL_RG * LANES // v_w)),
            full((s, LANES)), full((s, LANES)),
            full((nh, c, c)), full((nh, c, LANES)), full((nh, c, LANES)), full((nh, 1, LANES)),
            full((1, v_w)),
        ],
        out_specs=pl.BlockSpec((1, s, v_w), lambda bi: (bi, 0, 0)),
        out_shape=jax.ShapeDtypeStruct((b, s, v_w), BF16),
        scratch_shapes=[pltpu.VMEM((nh, LANES, LANES), F32)],
        compiler_params=_params(("arbitrary",)),
        name="retention",
    )(proj3, proj3, proj3, proj3, cos_t, sin_t, inner, cross, key, chunk, gn_g)


def _route(logits):
    lane = lax.broadcasted_iota(jnp.int32, logits.shape, 1).astype(F32)
    big = float(LANES)
    is_group = lane < N_GROUPS
    gl = jnp.where(is_group, logits, -jnp.inf)
    gmax = jnp.max(gl, axis=-1, keepdims=True)
    gidx = jnp.min(jnp.where(gl == gmax, lane, big), axis=-1, keepdims=True)
    p_group = 1.0 / jnp.sum(jnp.where(is_group, jnp.exp(gl - gmax), 0.0), axis=-1, keepdims=True)
    lo = N_GROUPS + EXPERTS_PER_GROUP * gidx
    il = jnp.where(lane >= lo, jnp.where(lane < lo + EXPERTS_PER_GROUP, logits, -jnp.inf), -jnp.inf)
    v1 = jnp.max(il, axis=-1, keepdims=True)
    i1 = jnp.min(jnp.where(il == v1, lane, big), axis=-1, keepdims=True)
    il2 = jnp.where(lane == i1, -jnp.inf, il)
    v2 = jnp.max(il2, axis=-1, keepdims=True)
    i2 = jnp.min(jnp.where(il2 == v2, lane, big), axis=-1, keepdims=True)
    e2 = jnp.exp(v2 - v1)
    inv = 1.0 / (1.0 + e2)
    g1 = p_group * inv
    g2 = p_group * (e2 * inv)
    out = jnp.where(lane == 0.0, i1 - N_GROUPS,
                    jnp.where(lane == 1.0, i2 - N_GROUPS,
                              jnp.where(lane == 2.0, g1, jnp.where(lane == 3.0, g2, 0.0))))
    return out


def _outproj_body(a_ref, r_ref, x_ref, w_ref, g_ref, wr_ref, br_ref, x1_ref, h2_ref, route_ref, eid_ref, *, sub):
    tm = x_ref.shape[0]
    half = a_ref.shape[1]
    n_sub = tm // sub

    def project(i):
        rows = slice(i * sub, (i + 1) * sub)
        acc = jnp.dot(a_ref[rows, :], w_ref[:half, :], preferred_element_type=F32)
        return acc + jnp.dot(r_ref[rows, :], w_ref[half:, :], preferred_element_type=F32)

    def epilogue(i, acc):
        rows = slice(i * sub, (i + 1) * sub)
        x1 = x_ref[rows, :] + acc
        x1_ref[rows, :] = x1
        ms = jnp.mean(x1 * x1, axis=-1, keepdims=True)
        h2 = (x1 * lax.rsqrt(ms + EPS)) * g_ref[...]
        h2_ref[rows, :] = _pack_halves(h2)
        logits = jnp.dot(h2.astype(BF16), wr_ref[...], preferred_element_type=F32) + br_ref[...]
        route = _route(logits)
        route_ref[rows, :] = route
        eid_ref[:, rows] = route.T[:TOP_K, :].astype(jnp.int32)

    acc = project(0)
    for i in range(n_sub):
        nxt = project(i + 1) if i + 1 < n_sub else None
        epilogue(i, acc)
        acc = nxt


def _outproj_router(a2d, r2d, x2d, w_out_bf16, g, wr_bf16, br, *, tm):
    n, d = x2d.shape
    half = a2d.shape[1]
    row = lambda width: pl.BlockSpec((tm, width), lambda i: (i, 0))
    full = lambda shape: pl.BlockSpec(shape, lambda i: (0, 0))
    return pl.pallas_call(
        functools.partial(_outproj_body, sub=min(tm, 2 * LANES)),
        grid=(n // tm,),
        in_specs=[row(half), row(half), row(d), full((2 * half, d)), full((1, d)), full((d, LANES)),
                  full((1, LANES))],
        out_specs=[row(d), row(d // 2), row(LANES), pl.BlockSpec((TOP_K, tm), lambda i: (0, i))],
        out_shape=[jax.ShapeDtypeStruct((n, d), F32), jax.ShapeDtypeStruct((n, d // 2), jnp.uint32),
                   jax.ShapeDtypeStruct((n, LANES), F32), jax.ShapeDtypeStruct((TOP_K, n), jnp.int32)],
        compiler_params=_params(("arbitrary",)),
        name="outproj_router",
    )(a2d, r2d, x2d, w_out_bf16, g, wr_bf16, br)


def _moe_plan(expert_id, tm):
    n_slots = expert_id.shape[0]
    n_tiles = n_slots // tm
    order = jnp.argsort(expert_id).astype(jnp.int32)
    sizes = jnp.zeros((N_EXPERTS,), jnp.int32).at[expert_id].add(1)
    ends = jnp.cumsum(sizes)
    starts = ends - sizes
    tile_row0 = jnp.arange(n_tiles, dtype=jnp.int32) * tm
    owner = lambda row: jnp.sum((ends[None, :] <= row[:, None]).astype(jnp.int32), axis=1)
    first_e = owner(tile_row0)
    last_e = owner(tile_row0 + (tm - 1))
    owners_upto = jnp.cumsum((sizes > 0).astype(jnp.int32))
    n_extra = (owners_upto[last_e] - owners_upto[first_e]).astype(jnp.int32)
    prev_last = jnp.concatenate([jnp.full((1,), -1, jnp.int32), last_e[:-1]])
    switch = (prev_last != first_e).astype(jnp.int32)
    eid = jnp.arange(N_EXPERTS, dtype=jnp.int32)
    cand = jnp.where((sizes[None, :] > 0) & (eid[None, :] > eid[:, None]), eid[None, :], N_EXPERTS)
    nxt = jnp.min(cand, axis=1)
    next_expert = jnp.where(nxt < N_EXPERTS, nxt, -1).astype(jnp.int32)
    spare = n_slots + jnp.arange(tm, dtype=jnp.int32)
    codes = jnp.concatenate([order.reshape(n_tiles, tm), spare[None, :]], axis=0).reshape(n_tiles + 1, 1, tm)
    return dict(first_e=first_e, n_extra=n_extra, switch=switch, starts=starts.astype(jnp.int32),
                ends=ends.astype(jnp.int32), next_expert=next_expert, codes=codes)


def _moe_body(fe_ref, nextra_ref, sw_ref, st_ref, en_ref, nx_ref,
              codes_hbm, h2_hbm, wg_hbm, wu_hbm, wd_hbm, ys_hbm,
              gidx, sidx, xbuf, obuf, wg_st, wu_st, wd_st, wg_rs, wu_rs, wd_rs,
              gisem, sisem, gsem, ssem, wsem, *, n_tokens, n_tiles, cast_rows, ff_chunks):
    t = pl.program_id(0)
    tm = xbuf.shape[1]
    last = n_tiles - 1
    spare_codes = n_tiles

    def gidx_copy(tile, slot):
        return pltpu.make_async_copy(codes_hbm.at[tile], gidx.at[slot], gisem.at[slot])

    def sidx_copy(tile, slot):
        return pltpu.make_async_copy(codes_hbm.at[tile], sidx.at[slot], sisem.at[slot])

    def weight_copies(ex):
        return (pltpu.make_async_copy(wg_hbm.at[ex], wg_st, wsem.at[0]),
                pltpu.make_async_copy(wu_hbm.at[ex], wu_st, wsem.at[1]),
                pltpu.make_async_copy(wd_hbm.at[ex], wd_st, wsem.at[2]))

    def gather_row(slot, r):
        code = gidx[slot, 0, r]
        tok = code - jnp.where(code >= n_tokens, n_tokens, 0)
        return pltpu.make_async_copy(h2_hbm.at[pl.ds(tok, 1)], xbuf.at[slot, pl.ds(r, 1)], gsem.at[slot])

    def scatter_row(slot, r):
        dst = sidx[slot, 0, r]
        return pltpu.make_async_copy(obuf.at[slot, pl.ds(r, 1)], ys_hbm.at[pl.ds(dst, 1)], ssem.at[slot])

    def wait_gather(slot):
        pltpu.make_async_copy(h2_hbm.at[pl.ds(0, tm)], xbuf.at[slot], gsem.at[slot]).wait()

    def wait_scatter(slot):
        pltpu.make_async_copy(obuf.at[slot], ys_hbm.at[pl.ds(0, tm)], ssem.at[slot]).wait()

    def for_rows(fn):
        def body(r, carry):
            fn(r)
            return carry
        lax.fori_loop(0, tm, body, 0, unroll=8)

    def take_over(ex, first):
        @pl.when(first)
        def _():
            for cp in weight_copies(ex):
                cp.start()

        for cp in weight_copies(ex):
            cp.wait()

        def cast_up(i, carry):
            r = pl.multiple_of(i * cast_rows, cast_rows)
            wg_rs[pl.ds(r, cast_rows), :] = wg_st[pl.ds(r, cast_rows), :].astype(BF16)
            wu_rs[pl.ds(r, cast_rows), :] = wu_st[pl.ds(r, cast_rows), :].astype(BF16)
            return carry

        def cast_down(i, carry):
            r = pl.multiple_of(i * cast_rows, cast_rows)
            wd_rs[pl.ds(r, cast_rows), :] = wd_st[pl.ds(r, cast_rows), :].astype(BF16)
            return carry

        lax.fori_loop(0, wg_st.shape[0] // cast_rows, cast_up, 0)
        lax.fori_loop(0, wd_st.shape[0] // cast_rows, cast_down, 0)
        nxt = nx_ref[ex]

        @pl.when(nxt >= 0)
        def _():
            for cp in weight_copies(nxt):
                cp.start()

    def ffn(slot, ex, accumulate, row_dmas=None):
        x_lo, x_hi = (h.astype(BF16) for h in _unpack_halves(xbuf[slot]))
        hd = x_lo.shape[1]
        row = t * tm + lax.broadcasted_iota(jnp.int32, (tm, 1), 0)
        mine = jnp.logical_and(row >= st_ref[ex], row < en_ref[ex])
        d_ff = wg_rs.shape[1]
        n_chunks = ff_chunks if row_dmas is not None else 1
        fc = d_ff // n_chunks
        rows = tm // n_chunks
        contrib = None
        for c in range(n_chunks):
            cols = slice(c * fc, (c + 1) * fc)
            up = lambda w: (jnp.dot(x_lo, w[:hd, cols], preferred_element_type=F32)
                            + jnp.dot(x_hi, w[hd:, cols], preferred_element_type=F32))
            a = up(wg_rs)
            b = up(wu_rs)
            hmid = jnp.where(mine, (a * (1.0 / (1.0 + jnp.exp(-a)))) * b, 0.0).astype(BF16)
            part = jnp.dot(hmid, wd_rs[cols, :], preferred_element_type=F32)
            contrib = part if contrib is None else contrib + part
            if row_dmas is not None:
                row_dmas(c * rows, (c + 1) * rows)
        if accumulate:
            prev_lo, prev_hi = _unpack_halves(obuf[slot])
            contrib = jnp.concatenate([prev_lo, prev_hi], axis=1) + contrib
        obuf[slot] = _pack_halves(contrib)

    e0 = fe_ref[t]

    def step(p):
        q = 1 - p
        if p == 0:
            @pl.when(t == 0)
            def _():
                copies = (gidx_copy(0, 0), gidx_copy(min(1, last), 1), sidx_copy(spare_codes, 1))
                for cp in copies:
                    cp.start()
                for cp in copies:
                    cp.wait()
                obuf[1] = jnp.zeros(obuf.shape[1:], obuf.dtype)
                for_rows(lambda r: gather_row(0, r).start())

        gidx_copy(jnp.minimum(t + 2, last), p).start()
        sidx_copy(t, p).start()

        @pl.when(t >= 1)
        def _():
            gidx_copy(0, q).wait()
            sidx_copy(0, q).wait()
            wait_scatter(p)

        wait_gather(p)

        @pl.when(sw_ref[t] == 1)
        def _():
            take_over(e0, t == 0)

        def row_dmas(lo, hi):
            for r in range(lo, hi):
                scatter_row(q, r).start(priority=r % 2)
                gather_row(q, r).start(priority=r % 2)

        ffn(p, e0, accumulate=False, row_dmas=row_dmas)

    for parity in range(2):
        pl.when(lax.rem(t, 2) == parity)(functools.partial(step, parity))

    slot = lax.rem(t, 2)

    def further_owner(_, ex):
        nxt = nx_ref[ex]
        take_over(nxt, False)
        ffn(slot, nxt, accumulate=True)
        return nxt

    lax.fori_loop(0, nextra_ref[t], further_owner, e0)

    @pl.when(t == last)
    def _():
        gidx_copy(0, slot).wait()
        sidx_copy(0, slot).wait()
        for_rows(lambda r: scatter_row(slot, r).start())
        wait_gather(1 - slot)
        wait_scatter(1 - slot)
        wait_scatter(slot)


def _moe_ffn(h2, plan, w_gate, w_up, w_down, *, tm):
    n, hd = h2.shape
    d = 2 * hd
    n_slots = 2 * n
    n_tiles = n_slots // tm
    assert n_tiles >= 2 and n_slots % tm == 0
    d_ff = w_gate.shape[2]
    any_spec = pl.BlockSpec(memory_space=pl.ANY)
    grid_spec = pltpu.PrefetchScalarGridSpec(
        num_scalar_prefetch=6,
        grid=(n_tiles,),
        in_specs=[any_spec] * 5,
        out_specs=any_spec,
        scratch_shapes=[
            pltpu.SMEM((2, 1, tm), jnp.int32),
            pltpu.SMEM((2, 1, tm), jnp.int32),
            pltpu.VMEM((2, tm, hd), jnp.uint32),
            pltpu.VMEM((2, tm, hd), jnp.uint32),
            pltpu.VMEM((d, d_ff), F32),
            pltpu.VMEM((d, d_ff), F32),
            pltpu.VMEM((d_ff, d), F32),
            pltpu.VMEM((d, d_ff), BF16),
            pltpu.VMEM((d, d_ff), BF16),
            pltpu.VMEM((d_ff, d), BF16),
            pltpu.SemaphoreType.DMA((2,)),
            pltpu.SemaphoreType.DMA((2,)),
            pltpu.SemaphoreType.DMA((2,)),
            pltpu.SemaphoreType.DMA((2,)),
            pltpu.SemaphoreType.DMA((3,)),
        ],
    )
    return pl.pallas_call(
        functools.partial(_moe_body, n_tokens=n, n_tiles=n_tiles, cast_rows=128,
                          ff_chunks=d_ff // (2 * LANES)),
        grid_spec=grid_spec,
        out_shape=jax.ShapeDtypeStruct((n_slots + tm, hd), jnp.uint32),
        compiler_params=_params(("arbitrary",), vmem=MOE_VMEM_LIMIT),
        name="moe_ffn",
    )(plan["first_e"], plan["n_extra"], plan["switch"], plan["starts"], plan["ends"], plan["next_expert"],
      plan["codes"], h2, w_gate, w_up, w_down)


def _final_body(x1_ref, ys0_ref, ys1_ref, route_ref, g_ref, o_ref):
    d = x1_ref.shape[1]
    route = route_ref[...]
    g1, g2 = route[:, 2:3], route[:, 3:4]
    ya = _unpack_halves(ys0_ref[...])
    yb = _unpack_halves(ys1_ref[...])
    halves = (slice(0, d // 2), slice(d // 2, d))
    x2 = [x1_ref[:, cols] + (ya[i] * g1 + yb[i] * g2) for i, cols in enumerate(halves)]
    ms = sum(jnp.sum(h * h, axis=-1, keepdims=True) for h in x2) * (1.0 / d)
    inv = lax.rsqrt(ms + EPS)
    for h, cols in zip(x2, halves):
        o_ref[:, cols] = ((h * inv) * g_ref[:, cols]).astype(o_ref.dtype)


def _final(x1, ys, route, g, *, tm):
    n, d = x1.shape
    nb = n // tm
    return pl.pallas_call(
        _final_body,
        grid=(nb,),
        in_specs=[
            pl.BlockSpec((tm, d), lambda i: (i, 0)),
            pl.BlockSpec((tm, d // 2), lambda i: (i, 0)),
            pl.BlockSpec((tm, d // 2), lambda i: (nb + i, 0)),
            pl.BlockSpec((tm, LANES), lambda i: (i, 0)),
            pl.BlockSpec((1, d), lambda i: (0, 0)),
        ],
        out_specs=pl.BlockSpec((tm, d), lambda i: (i, 0)),
        out_shape=jax.ShapeDtypeStruct((n, d), F32),
        compiler_params=_params(("arbitrary",)),
        name="final_norm",
    )(x1, ys, ys, route, g)


def _layer(x, rel_bias, attn_norm_g, w_in, lq1, lk1, lq2, lk2, diff_subln_g, ret_gn_g, w_out, ffn_norm_g,
           w_group_router, b_group, w_inner_router, b_inner, w_gate_exp, w_up_exp, w_down_exp, *, layer,
           attn_tile, row_tile, moe_tile):
    b, s, d = x.shape
    n = b * s
    x2d = x.reshape(n, d)
    lam_init = 0.8 - 0.6 * math.exp(-0.3 * layer)

    dq_cols = (COL_DK - COL_DQ) * LANES
    col_scale = jnp.concatenate([jnp.full((1, dq_cols), DIFF_HALF ** -0.5 * LOG2E, F32),
                                 jnp.ones((1, IN_COLS - dq_cols), F32)], axis=1)
    proj = _inproj(x2d, attn_norm_g.reshape(1, d), w_in.astype(BF16), col_scale, tm=min(2 * row_tile, n),
                   tn=8 * LANES)
    proj3 = proj.reshape(b, s, IN_COLS)

    bias = _bias_tiles(rel_bias, attn_tile)
    vec = lambda a: a.reshape(1, -1)
    a_out = _diff_attention(proj3, bias, vec(lq1), vec(lk1), vec(lq2), vec(lk2), vec(diff_subln_g),
                            t=attn_tile, lam_init=lam_init)
    r_out = _retention(proj3, _retention_tables(s), vec(ret_gn_g))

    pad = LANES - N_GROUPS - N_EXPERTS
    wr = jnp.concatenate([w_group_router, jnp.transpose(w_inner_router, (1, 0, 2)).reshape(d, N_EXPERTS),
                          jnp.zeros((d, pad), F32)], axis=1).astype(BF16)
    br = jnp.concatenate([b_group, b_inner.reshape(-1), jnp.zeros((pad,), F32)]).reshape(1, LANES)
    x1, h2, route, eid = _outproj_router(a_out.reshape(n, -1), r_out.reshape(n, -1), x2d, w_out.astype(BF16),
                                         vec(ffn_norm_g), wr, br, tm=row_tile)

    plan = _moe_plan(eid.reshape(-1), moe_tile)
    ys = _moe_ffn(h2, plan, w_gate_exp, w_up_exp, w_down_exp, tm=moe_tile)
    return x1, ys, route


def kernel(x, rel_bias, attn_norm_g, w_in, lambda_q1, lambda_k1, lambda_q2, lambda_k2, diff_subln_g, ret_gn_g,
           w_out, ffn_norm_g, w_group_router, b_group, w_inner_router, b_inner, w_gate_exp, w_up_exp, w_down_exp,
           final_g):
    b, s, d = x.shape
    depth = w_in.shape[0]
    assert depth == 1, "the final norm is fused into the single layer's combine step"
    first = lambda a: a.reshape(a.shape[1:])
    x1, ys, route = _layer(x, rel_bias, attn_norm_g[0], w_in[0], lambda_q1[0], lambda_k1[0], lambda_q2[0],
                           lambda_k2[0], diff_subln_g[0], ret_gn_g[0], w_out[0], ffn_norm_g[0],
                           w_group_router[0], b_group[0], w_inner_router[0], b_inner[0], first(w_gate_exp),
                           first(w_up_exp), first(w_down_exp), layer=0, attn_tile=min(256, s), row_tile=512,
                           moe_tile=256)
    out = _final(x1, ys, route, final_g.reshape(1, d), tm=256)
    return out.reshape(b, s, d)
```

```python
import functools
import math

import numpy as np
import jax
import jax.numpy as jnp
from jax import lax
from jax.experimental import pallas as pl
from jax.experimental.pallas import tpu as pltpu

F32 = jnp.float32
BF16 = jnp.bfloat16

EPS = 1e-6
LANES = 128
N_DIFF_HEADS = 8
DIFF_HALF = 64
N_RET_HEADS = 8
RET_QK = 64
RET_CHUNK = 128
ROPE_BASE = 10000.0
N_BUCKETS = 32
MAX_DISTANCE = 128
N_GROUPS = 4
EXPERTS_PER_GROUP = 8
N_EXPERTS = N_GROUPS * EXPERTS_PER_GROUP
TOP_K = 2
MASK_VALUE = -1e30
LOG2E = math.log2(math.e)
VMEM_LIMIT = 56 * 1024 * 1024
MOE_VMEM_LIMIT = 60 * 1024 * 1024

COL_DQ, COL_DK, COL_DV = 0, 8, 16
COL_RQ, COL_RK, COL_RV, COL_RG = 24, 28, 32, 40
IN_COLS = 48 * LANES


def _params(sem, vmem=VMEM_LIMIT):
    return pltpu.CompilerParams(dimension_semantics=sem, vmem_limit_bytes=vmem)


def _pack_halves(x):
    hd = x.shape[1] // 2
    bits = lambda h: lax.bitcast_convert_type(h.astype(BF16).astype(F32), jnp.uint32)
    return lax.shift_right_logical(bits(x[:, :hd]), jnp.uint32(16)) | bits(x[:, hd:])


def _unpack_halves(w):
    return (lax.bitcast_convert_type(lax.shift_left(w, jnp.uint32(16)), F32),
            lax.bitcast_convert_type(w & jnp.uint32(0xFFFF0000), F32))


def _inproj_body(x_ref, g_ref, w_ref, cs_ref, o_ref):
    x = x_ref[...]
    acc = jnp.dot((x * g_ref[...]).astype(BF16), w_ref[...], preferred_element_type=F32)
    inv = lax.rsqrt(jnp.mean(x * x, axis=-1, keepdims=True) + EPS)
    o_ref[...] = ((acc * inv) * cs_ref[...]).astype(o_ref.dtype)


def _inproj(x2d, g, w, col_scale, *, tm):
    n, d = x2d.shape
    cols = w.shape[1]
    once = dict(pipeline_mode=pl.Buffered(1))
    return pl.pallas_call(
        _inproj_body,
        grid=(n // tm,),
        in_specs=[
            pl.BlockSpec((tm, d), lambda i: (i, 0)),
            pl.BlockSpec((1, d), lambda i: (0, 0)),
            pl.BlockSpec((d, cols), lambda i: (0, 0), **once),
            pl.BlockSpec((1, cols), lambda i: (0, 0)),
        ],
        out_specs=pl.BlockSpec((tm, cols), lambda i: (i, 0)),
        out_shape=jax.ShapeDtypeStruct((n, cols), BF16),
        compiler_params=_params(("arbitrary",)),
        name="inproj",
    )(x2d, g, w, col_scale)


def _t5_bucket_table(n):
    d = np.arange(n)
    max_exact = N_BUCKETS // 2
    log_ratio = np.log(np.maximum(d, 1).astype(np.float64) / max_exact) / math.log(MAX_DISTANCE / max_exact)
    large = np.minimum(max_exact + (log_ratio * (N_BUCKETS - max_exact)).astype(np.int64), N_BUCKETS - 1)
    return np.where(d < max_exact, d, large).astype(np.int32)


def _bias_body(rb_ref, bkt_ref, o_ref):
    h = pl.program_id(0)
    t = o_ref.shape[2]
    bkt = bkt_ref[...]
    far = rb_ref[N_BUCKETS - 1, h]
    by_dist = jnp.zeros(bkt.shape, F32)
    for b in range(N_BUCKETS - 1):
        by_dist = jnp.where(bkt == b, (rb_ref[b, h] - far) * LOG2E, by_dist)
    rows = (jnp.concatenate([jnp.full((1, t), MASK_VALUE, F32), by_dist[:, :t]], axis=1), by_dist)
    for i, row in enumerate(rows):
        shifted = pltpu.roll(jnp.broadcast_to(row, (t, 2 * t)), 0, 1, stride=1, stride_axis=0)
        o_ref[0, i] = shifted[:, t:]


def _bias_tiles(rel_bias, t):
    bkt = jnp.asarray(_t5_bucket_table(2 * t)).reshape(1, 2 * t)
    return pl.pallas_call(
        _bias_body,
        grid=(N_DIFF_HEADS,),
        in_specs=[
            pl.BlockSpec(memory_space=pltpu.SMEM),
            pl.BlockSpec((1, 2 * t), lambda h: (0, 0)),
        ],
        out_specs=pl.BlockSpec((1, 2, t, t), lambda h: (h, 0, 0, 0)),
        out_shape=jax.ShapeDtypeStruct((N_DIFF_HEADS, 2, t, t), F32),
        compiler_params=_params(("arbitrary",)),
        name="bias_tiles",
    )(rel_bias, bkt)


_NT = (((1,), (1,)), ((), ()))
_TN = (((0,), (0,)), ((), ()))


def _diffattn_body(q_ref, k_ref, v_ref, bias_ref, lq1_ref, lk1_ref, lq2_ref, lk2_ref, g_ref, o_ref, vt_ref, *,
                   t, lam_init, near_band):
    nq = k_ref.shape[1] // t
    half = t // 2
    lane = lax.broadcasted_iota(jnp.int32, (t, LANES), 1)
    lam = (jnp.exp(jnp.sum(lq1_ref[...] * lk1_ref[...], axis=-1, keepdims=True))
           - jnp.exp(jnp.sum(lq2_ref[...] * lk2_ref[...], axis=-1, keepdims=True)) + lam_init)
    vt_ref[:LANES, :] = v_ref[0].T
    vt_ref[LANES:, :] = jnp.ones((vt_ref.shape[0] - LANES, vt_ref.shape[1]), BF16)
    qmask = {}
    run_max = {}
    acc = {}

    def scores(item):
        qi, j, part = item
        if qi not in qmask:
            q = q_ref[0, qi * t:(qi + 1) * t, :]
            zero = jnp.zeros_like(q)
            qmask[qi] = (jnp.where(lane < DIFF_HALF, q, zero), jnp.where(lane >= DIFF_HALF, q, zero))
        k_lo = j * t + (half if part == "diag_lower" else 0)
        k_hi = j * t + (half if part == "diag_upper" else t)
        kb = k_ref[0, k_lo:k_hi, :]
        out = []
        for c in range(2):
            qsel = qmask[qi][c][half:] if part == "diag_lower" else qmask[qi][c]
            s = lax.dot_general(kb, qsel, _NT, preferred_element_type=F32)
            if part == "diag_upper":
                s = s + bias_ref[0, 0, :half, :]
            elif part == "diag_lower":
                s = s + bias_ref[0, 0, half:, half:]
            elif j == qi - 1:
                if near_band >= t:
                    s = s + bias_ref[0, 1]
                else:
                    corner = s[t - near_band:, :near_band] + bias_ref[0, 1, t - near_band:, :near_band]
                    bottom = jnp.concatenate([corner, s[t - near_band:, near_band:]], axis=1)
                    s = jnp.concatenate([s[:t - near_band], bottom], axis=0)
            out.append(s)
        return out

    def softmax_step(item, ss):
        qi, j, part = item
        first = j == 0 and part != "diag_lower"
        ps, alphas = [], []
        for c in range(2):
            s = ss[c]
            smax = jnp.max(s, axis=0, keepdims=True)
            if first:
                m_new, alpha = smax, None
                run_max[qi, c] = m_new
            else:
                m_full = run_max[qi, c]
                m_old = m_full[:, half:] if part == "diag_lower" else m_full
                m_new = jnp.maximum(m_old, smax)
                alpha = jnp.exp2(m_old - m_new)
                run_max[qi, c] = (jnp.concatenate([m_full[:, :half], m_new], axis=1)
                                  if part == "diag_lower" else m_new)
            ps.append(jnp.exp2(s - m_new).astype(BF16))
            alphas.append(alpha)
        return ps, alphas

    def values_step(item, ps, alphas):
        qi, j, part = item
        k_lo = j * t + (half if part == "diag_lower" else 0)
        k_hi = j * t + (half if part == "diag_upper" else t)
        vtb = vt_ref[:, k_lo:k_hi]
        for c in range(2):
            alpha = alphas[c]
            r = jnp.dot(vtb, ps[c], preferred_element_type=F32)
            if alpha is None:
                acc[qi, c] = r
            elif part == "diag_lower":
                acc[qi, c] = jnp.concatenate([acc[qi, c][:, :half], alpha * acc[qi, c][:, half:] + r], axis=1)
            else:
                acc[qi, c] = alpha * acc[qi, c] + r
        if part == "diag_lower":
            outs = []
            for c in range(2):
                a = acc.pop((qi, c))
                del run_max[qi, c]
                outs.append(a[:LANES] * (1.0 / a[LANES:LANES + 1]))
            ot = (outs[0] - lam * outs[1]).T
            ms = jnp.mean(ot * ot, axis=-1, keepdims=True)
            o_ref[0, qi * t:(qi + 1) * t, :] = (
                ((ot * lax.rsqrt(ms + EPS)) * g_ref[...]) * (1.0 - lam_init)).astype(o_ref.dtype)
            del qmask[qi]

    items = []
    for qi in range(nq):
        items += [(qi, j, "full") for j in range(qi)] + [(qi, qi, "diag_upper"), (qi, qi, "diag_lower")]
    ss = scores(items[0])
    pending = None
    for n, item in enumerate(items):
        nxt = scores(items[n + 1]) if n + 1 < len(items) else None
        ps, alphas = softmax_step(item, ss)
        if pending is not None:
            values_step(*pending)
        pending = (item, ps, alphas)
        ss = nxt
    values_step(*pending)


def _diff_attention(proj3, bias, lq1, lk1, lq2, lk2, subln_g, *, t, lam_init):
    b, s, _ = proj3.shape
    vec = pl.BlockSpec((1, DIFF_HALF), lambda bi, h: (0, 0))
    head = lambda col: pl.BlockSpec((1, s, LANES), lambda bi, h: (bi, 0, col + h))
    first_far = int(np.argmax(_t5_bucket_table(2 * t) == N_BUCKETS - 1))
    near_band = -(-first_far // LANES) * LANES
    ones_rows = 16
    return pl.pallas_call(
        functools.partial(_diffattn_body, t=t, lam_init=lam_init, near_band=near_band),
        grid=(b, N_DIFF_HEADS),
        in_specs=[
            head(COL_DQ), head(COL_DK), head(COL_DV),
            pl.BlockSpec((1, 2, t, t), lambda bi, h: (h, 0, 0, 0)),
            vec, vec, vec, vec,
            pl.BlockSpec((1, LANES), lambda bi, h: (0, 0)),
        ],
        out_specs=pl.BlockSpec((1, s, LANES), lambda bi, h: (bi, 0, h)),
        out_shape=jax.ShapeDtypeStruct((b, s, N_DIFF_HEADS * LANES), BF16),
        scratch_shapes=[pltpu.VMEM((LANES + ones_rows, s), BF16)],
        compiler_params=_params(("arbitrary", "arbitrary")),
        name="diff_attention",
    )(proj3, proj3, proj3, bias, lq1, lk1, lq2, lk2, subln_g)


def _retention_tables(s):
    c = RET_CHUNK
    half = RET_QK // 2
    inv_freq = ROPE_BASE ** (-jnp.arange(0, RET_QK, 2, dtype=F32) / RET_QK)
    ang = jnp.arange(s, dtype=F32)[:, None] * inv_freq[None, :]
    cos, sin = jnp.cos(ang), jnp.sin(ang)
    cos_t = jnp.tile(cos, (1, LANES // half))
    sin_t = jnp.tile(jnp.concatenate([-sin, sin], axis=1), (1, LANES // RET_QK))
    log_g = jnp.log(1.0 - jnp.exp2(-5.0 - jnp.arange(N_RET_HEADS, dtype=F32)))
    n = jnp.arange(c, dtype=F32)
    diff = n[:, None] - n[None, :]
    inner = jnp.where(diff[None] >= 0, jnp.exp(jnp.maximum(diff, 0.0)[None] * log_g[:, None, None]), 0.0)
    cross = jnp.exp((n[None] + 1.0) * log_g[:, None])
    key = jnp.exp((c - 1.0 - n[None]) * log_g[:, None])
    chunk = jnp.exp(c * log_g)
    bc = lambda a: jnp.broadcast_to(a[..., None], a.shape + (LANES,)).astype(F32)
    return cos_t, sin_t, inner.astype(F32), bc(cross), bc(key), bc(chunk[:, None])


def _retention_body(rq_ref, rk_ref, rv_ref, rg_ref, cos_ref, sin_ref, inner_ref, cross_ref, key_ref,
                    chunk_ref, gn_ref, o_ref, state_ref):
    c = RET_CHUNK
    nc = rq_ref.shape[1] // c
    nh = N_RET_HEADS
    state_ref[...] = jnp.zeros(state_ref.shape, F32)
    lane = lax.broadcasted_iota(jnp.int32, (c, LANES), 1)
    first_half = (lane % RET_QK) < (RET_QK // 2)
    head_mask = [lane < RET_QK, lane >= RET_QK]
    slab = lambda i: slice(i * LANES, (i + 1) * LANES)

    def rotary(x, cos, sin):
        rot = jnp.where(first_half, pltpu.roll(x, LANES - RET_QK // 2, 1), pltpu.roll(x, RET_QK // 2, 1))
        return x * cos + rot * sin

    def chunk_step(ci, carry):
        r = pl.multiple_of(ci * c, c)
        rows = pl.ds(r, c)
        cos = cos_ref[rows, :]
        sin = sin_ref[rows, :]
        q_pair, k_pair, kb_pair = [], [], []
        for pr in range(nh // 2):
            q_pair.append(rotary(rq_ref[0, rows, slab(pr)].astype(F32), cos, sin))
            k = rotary(rk_ref[0, rows, slab(pr)].astype(F32), cos, sin) * (RET_QK ** -0.5)
            k_pair.append(k)
            kb_pair.append(k.astype(BF16))
        qh = [jnp.where(head_mask[h % 2], q_pair[h // 2], 0.0).astype(BF16) for h in range(nh)]
        kd = [(k_pair[h // 2] * key_ref[h]).astype(BF16) for h in range(nh)]
        vh = [rv_ref[0, rows, slab(h)] for h in range(nh)]
        scores = [lax.dot_general(qh[h], kb_pair[h // 2], _NT, preferred_element_type=F32) for h in range(nh)]
        cross = [jnp.dot(qh[h], state_ref[h].astype(BF16), preferred_element_type=F32) for h in range(nh)]
        kv = [lax.dot_general(kd[h], vh[h], _TN, preferred_element_type=F32) for h in range(nh)]
        decayed = [(scores[h] * inner_ref[h]).astype(BF16) for h in range(nh)]
        for h in range(nh):
            state_ref[h] = state_ref[h] * chunk_ref[h] + kv[h]
        inner = [jnp.dot(decayed[h], vh[h], preferred_element_type=F32) for h in range(nh)]
        for h in range(nh):
            y = inner[h] + cross[h] * cross_ref[h]
            mu = jnp.mean(y, axis=-1, keepdims=True)
            yc = y - mu
            var = jnp.mean(yc * yc, axis=-1, keepdims=True)
            yn = (yc * lax.rsqrt(var + EPS)) * gn_ref[:, slab(h)]
            gate = rg_ref[0, rows, slab(h)].astype(F32)
            silu = gate * (1.0 / (1.0 + jnp.exp(-gate)))
            o_ref[0, rows, slab(h)] = (silu * yn).astype(o_ref.dtype)
        return carry

    lax.fori_loop(0, nc, chunk_step, 0)


def _retention(proj3, tables, gn_g):
    b, s, _ = proj3.shape
    cos_t, sin_t, inner, cross, key, chunk = tables
    c = RET_CHUNK
    nh = N_RET_HEADS
    qk_w = nh * RET_QK
    v_w = nh * LANES
    full = lambda shape: pl.BlockSpec(shape, lambda bi: (0,) * len(shape))
    return pl.pallas_call(
        _retention_body,
        grid=(b,),
        in_specs=[
            pl.BlockSpec((1, s, qk_w), lambda bi: (bi, 0, COL_RQ * LANES // qk_w)),
            pl.BlockSpec((1, s, qk_w), lambda bi: (bi, 0, COL_RK * LANES // qk_w)),
            pl.BlockSpec((1, s, v_w), lambda bi: (bi, 0, COL_RV * LANES // v_w)),
            pl.BlockSpec((1, s, v_w), lambda bi: (bi, 0, COL_RG * LANES // v_w)),
            full((s, LANES)), full((s, LANES)),
            full((nh, c, c)), full((nh, c, LANES)), full((nh, c, LANES)), full((nh, 1, LANES)),
            full((1, v_w)),
        ],
        out_specs=pl.BlockSpec((1, s, v_w), lambda bi: (bi, 0, 0)),
        out_shape=jax.ShapeDtypeStruct((b, s, v_w), BF16),
        scratch_shapes=[pltpu.VMEM((nh, LANES, LANES), F32)],
        compiler_params=_params(("arbitrary",)),
        name="retention",
    )(proj3, proj3, proj3, proj3, cos_t, sin_t, inner, cross, key, chunk, gn_g)


def _route(logits):
    lane = lax.broadcasted_iota(jnp.int32, logits.shape, 1).astype(F32)
    big = float(LANES)
    is_group = lane < N_GROUPS
    gl = jnp.where(is_group, logits, -jnp.inf)
    gmax = jnp.max(gl, axis=-1, keepdims=True)
    gidx = jnp.min(jnp.where(gl == gmax, lane, big), axis=-1, keepdims=True)
    p_group = 1.0 / jnp.sum(jnp.where(is_group, jnp.exp(gl - gmax), 0.0), axis=-1, keepdims=True)
    lo = N_GROUPS + EXPERTS_PER_GROUP * gidx
    il = jnp.where(lane >= lo, jnp.where(lane < lo + EXPERTS_PER_GROUP, logits, -jnp.inf), -jnp.inf)
    v1 = jnp.max(il, axis=-1, keepdims=True)
    i1 = jnp.min(jnp.where(il == v1, lane, big), axis=-1, keepdims=True)
    il2 = jnp.where(lane == i1, -jnp.inf, il)
    v2 = jnp.max(il2, axis=-1, keepdims=True)
    i2 = jnp.min(jnp.where(il2 == v2, lane, big), axis=-1, keepdims=True)
    e2 = jnp.exp(v2 - v1)
    inv = 1.0 / (1.0 + e2)
    g1 = p_group * inv
    g2 = p_group * (e2 * inv)
    out = jnp.where(lane == 0.0, i1 - N_GROUPS,
                    jnp.where(lane == 1.0, i2 - N_GROUPS,
                              jnp.where(lane == 2.0, g1, jnp.where(lane == 3.0, g2, 0.0))))
    return out


def _outproj_body(a_ref, r_ref, x_ref, w_ref, g_ref, wr_ref, br_ref, x1_ref, h2_ref, route_ref, eid_ref, *, sub):
    tm = x_ref.shape[0]
    half = a_ref.shape[1]
    n_sub = tm // sub

    def project(i):
        rows = slice(i * sub, (i + 1) * sub)
        acc = jnp.dot(a_ref[rows, :], w_ref[:half, :], preferred_element_type=F32)
        return acc + jnp.dot(r_ref[rows, :], w_ref[half:, :], preferred_element_type=F32)

    def epilogue(i, acc):
        rows = slice(i * sub, (i + 1) * sub)
        x1 = x_ref[rows, :] + acc
        x1_ref[rows, :] = x1
        ms = jnp.mean(x1 * x1, axis=-1, keepdims=True)
        h2 = (x1 * lax.rsqrt(ms + EPS)) * g_ref[...]
        h2_ref[rows, :] = _pack_halves(h2)
        logits = jnp.dot(h2.astype(BF16), wr_ref[...], preferred_element_type=F32) + br_ref[...]
        route = _route(logits)
        route_ref[rows, :] = route
        eid_ref[:, rows] = route.T[:TOP_K, :].astype(jnp.int32)

    acc = project(0)
    for i in range(n_sub):
        nxt = project(i + 1) if i + 1 < n_sub else None
        epilogue(i, acc)
        acc = nxt


def _outproj_router(a2d, r2d, x2d, w_out_bf16, g, wr_bf16, br, *, tm):
    n, d = x2d.shape
    half = a2d.shape[1]
    row = lambda width: pl.BlockSpec((tm, width), lambda i: (i, 0))
    full = lambda shape: pl.BlockSpec(shape, lambda i: (0, 0))
    return pl.pallas_call(
        functools.partial(_outproj_body, sub=min(tm, 2 * LANES)),
        grid=(n // tm,),
        in_specs=[row(half), row(half), row(d), full((2 * half, d)), full((1, d)), full((d, LANES)),
                  full((1, LANES))],
        out_specs=[row(d), row(d // 2), row(LANES), pl.BlockSpec((TOP_K, tm), lambda i: (0, i))],
        out_shape=[jax.ShapeDtypeStruct((n, d), F32), jax.ShapeDtypeStruct((n, d // 2), jnp.uint32),
                   jax.ShapeDtypeStruct((n, LANES), F32), jax.ShapeDtypeStruct((TOP_K, n), jnp.int32)],
        compiler_params=_params(("arbitrary",)),
        name="outproj_router",
    )(a2d, r2d, x2d, w_out_bf16, g, wr_bf16, br)


def _moe_plan(expert_id, tm):
    n_slots = expert_id.shape[0]
    n_tiles = n_slots // tm
    order = jnp.argsort(expert_id).astype(jnp.int32)
    sizes = jnp.zeros((N_EXPERTS,), jnp.int32).at[expert_id].add(1)
    ends = jnp.cumsum(sizes)
    starts = ends - sizes
    tile_row0 = jnp.arange(n_tiles, dtype=jnp.int32) * tm
    owner = lambda row: jnp.sum((ends[None, :] <= row[:, None]).astype(jnp.int32), axis=1)
    first_e = owner(tile_row0)
    last_e = owner(tile_row0 + (tm - 1))
    owners_upto = jnp.cumsum((sizes > 0).astype(jnp.int32))
    n_extra = (owners_upto[last_e] - owners_upto[first_e]).astype(jnp.int32)
    prev_last = jnp.concatenate([jnp.full((1,), -1, jnp.int32), last_e[:-1]])
    switch = (prev_last != first_e).astype(jnp.int32)
    eid = jnp.arange(N_EXPERTS, dtype=jnp.int32)
    cand = jnp.where((sizes[None, :] > 0) & (eid[None, :] > eid[:, None]), eid[None, :], N_EXPERTS)
    nxt = jnp.min(cand, axis=1)
    next_expert = jnp.where(nxt < N_EXPERTS, nxt, -1).astype(jnp.int32)
    spare = n_slots + jnp.arange(tm, dtype=jnp.int32)
    codes = jnp.concatenate([order.reshape(n_tiles, tm), spare[None, :]], axis=0).reshape(n_tiles + 1, 1, tm)
    return dict(first_e=first_e, n_extra=n_extra, switch=switch, starts=starts.astype(jnp.int32),
                ends=ends.astype(jnp.int32), next_expert=next_expert, codes=codes)


def _moe_body(fe_ref, nextra_ref, sw_ref, st_ref, en_ref, nx_ref,
              codes_hbm, h2_hbm, wg_hbm, wu_hbm, wd_hbm, ys_hbm,
              gidx, sidx, xbuf, obuf, wg_st, wu_st, wd_st, wg_rs, wu_rs, wd_rs,
              gisem, sisem, gsem, ssem, wsem, *, n_tokens, n_tiles, cast_rows, ff_chunks):
    t = pl.program_id(0)
    tm = xbuf.shape[1]
    last = n_tiles - 1
    spare_codes = n_tiles

    def gidx_copy(tile, slot):
        return pltpu.make_async_copy(codes_hbm.at[tile], gidx.at[slot], gisem.at[slot])

    def sidx_copy(tile, slot):
        return pltpu.make_async_copy(codes_hbm.at[tile], sidx.at[slot], sisem.at[slot])

    def weight_copies(ex):
        return (pltpu.make_async_copy(wg_hbm.at[ex], wg_st, wsem.at[0]),
                pltpu.make_async_copy(wu_hbm.at[ex], wu_st, wsem.at[1]),
                pltpu.make_async_copy(wd_hbm.at[ex], wd_st, wsem.at[2]))

    def gather_row(slot, r):
        code = gidx[slot, 0, r]
        tok = code - jnp.where(code >= n_tokens, n_tokens, 0)
        return pltpu.make_async_copy(h2_hbm.at[pl.ds(tok, 1)], xbuf.at[slot, pl.ds(r, 1)], gsem.at[slot])

    def scatter_row(slot, r):
        dst = sidx[slot, 0, r]
        return pltpu.make_async_copy(obuf.at[slot, pl.ds(r, 1)], ys_hbm.at[pl.ds(dst, 1)], ssem.at[slot])

    def wait_gather(slot):
        pltpu.make_async_copy(h2_hbm.at[pl.ds(0, tm)], xbuf.at[slot], gsem.at[slot]).wait()

    def wait_scatter(slot):
        pltpu.make_async_copy(obuf.at[slot], ys_hbm.at[pl.ds(0, tm)], ssem.at[slot]).wait()

    def for_rows(fn):
        def body(r, carry):
            fn(r)
            return carry
        lax.fori_loop(0, tm, body, 0, unroll=8)

    def take_over(ex, first):
        @pl.when(first)
        def _():
            for cp in weight_copies(ex):
                cp.start()

        for cp in weight_copies(ex):
            cp.wait()

        def cast_up(i, carry):
            r = pl.multiple_of(i * cast_rows, cast_rows)
            wg_rs[pl.ds(r, cast_rows), :] = wg_st[pl.ds(r, cast_rows), :].astype(BF16)
            wu_rs[pl.ds(r, cast_rows), :] = wu_st[pl.ds(r, cast_rows), :].astype(BF16)
            return carry

        def cast_down(i, carry):
            r = pl.multiple_of(i * cast_rows, cast_rows)
            wd_rs[pl.ds(r, cast_rows), :] = wd_st[pl.ds(r, cast_rows), :].astype(BF16)
            return carry

        lax.fori_loop(0, wg_st.shape[0] // cast_rows, cast_up, 0)
        lax.fori_loop(0, wd_st.shape[0] // cast_rows, cast_down, 0)
        nxt = nx_ref[ex]

        @pl.when(nxt >= 0)
        def _():
            for cp in weight_copies(nxt):
                cp.start()

    def ffn(slot, ex, accumulate, row_dmas=None):
        x_lo, x_hi = (h.astype(BF16) for h in _unpack_halves(xbuf[slot]))
        hd = x_lo.shape[1]
        row = t * tm + lax.broadcasted_iota(jnp.int32, (tm, 1), 0)
        mine = jnp.logical_and(row >= st_ref[ex], row < en_ref[ex])
        d_ff = wg_rs.shape[1]
        n_chunks = ff_chunks if row_dmas is not None else 1
        fc = d_ff // n_chunks
        rows = tm // n_chunks
        contrib = None
        for c in range(n_chunks):
            cols = slice(c * fc, (c + 1) * fc)
            up = lambda w: (jnp.dot(x_lo, w[:hd, cols], preferred_element_type=F32)
                            + jnp.dot(x_hi, w[hd:, cols], preferred_element_type=F32))
            a = up(wg_rs)
            b = up(wu_rs)
            hmid = jnp.where(mine, (a * (1.0 / (1.0 + jnp.exp(-a)))) * b, 0.0).astype(BF16)
            part = jnp.dot(hmid, wd_rs[cols, :], preferred_element_type=F32)
            contrib = part if contrib is None else contrib + part
            if row_dmas is not None:
                row_dmas(c * rows, (c + 1) * rows)
        if accumulate:
            prev_lo, prev_hi = _unpack_halves(obuf[slot])
            contrib = jnp.concatenate([prev_lo, prev_hi], axis=1) + contrib
        obuf[slot] = _pack_halves(contrib)

    e0 = fe_ref[t]

    def step(p):
        q = 1 - p
        if p == 0:
            @pl.when(t == 0)
            def _():
                copies = (gidx_copy(0, 0), gidx_copy(min(1, last), 1), sidx_copy(spare_codes, 1))
                for cp in copies:
                    cp.start()
                for cp in copies:
                    cp.wait()
                obuf[1] = jnp.zeros(obuf.shape[1:], obuf.dtype)
                for_rows(lambda r: gather_row(0, r).start())

        gidx_copy(jnp.minimum(t + 2, last), p).start()
        sidx_copy(t, p).start()

        @pl.when(t >= 1)
        def _():
            gidx_copy(0, q).wait()
            sidx_copy(0, q).wait()
            wait_scatter(p)

        wait_gather(p)

        @pl.when(sw_ref[t] == 1)
        def _():
            take_over(e0, t == 0)

        def row_dmas(lo, hi):
            for r in range(lo, hi):
                scatter_row(q, r).start(priority=r % 2)
                gather_row(q, r).start(priority=r % 2)

        ffn(p, e0, accumulate=False, row_dmas=row_dmas)

    for parity in range(2):
        pl.when(lax.rem(t, 2) == parity)(functools.partial(step, parity))

    slot = lax.rem(t, 2)

    def further_owner(_, ex):
        nxt = nx_ref[ex]
        take_over(nxt, False)
        ffn(slot, nxt, accumulate=True)
        return nxt

    lax.fori_loop(0, nextra_ref[t], further_owner, e0)

    @pl.when(t == last)
    def _():
        gidx_copy(0, slot).wait()
        sidx_copy(0, slot).wait()
        for_rows(lambda r: scatter_row(slot, r).start())
        wait_gather(1 - slot)
        wait_scatter(1 - slot)
        wait_scatter(slot)


def _moe_ffn(h2, plan, w_gate, w_up, w_down, *, tm):
    n, hd = h2.shape
    d = 2 * hd
    n_slots = 2 * n
    n_tiles = n_slots // tm
    assert n_tiles >= 2 and n_slots % tm == 0
    d_ff = w_gate.shape[2]
    any_spec = pl.BlockSpec(memory_space=pl.ANY)
    grid_spec = pltpu.PrefetchScalarGridSpec(
        num_scalar_prefetch=6,
        grid=(n_tiles,),
        in_specs=[any_spec] * 5,
        out_specs=any_spec,
        scratch_shapes=[
            pltpu.SMEM((2, 1, tm), jnp.int32),
            pltpu.SMEM((2, 1, tm), jnp.int32),
            pltpu.VMEM((2, tm, hd), jnp.uint32),
            pltpu.VMEM((2, tm, hd), jnp.uint32),
            pltpu.VMEM((d, d_ff), F32),
            pltpu.VMEM((d, d_ff), F32),
            pltpu.VMEM((d_ff, d), F32),
            pltpu.VMEM((d, d_ff), BF16),
            pltpu.VMEM((d, d_ff), BF16),
            pltpu.VMEM((d_ff, d), BF16),
            pltpu.SemaphoreType.DMA((2,)),
            pltpu.SemaphoreType.DMA((2,)),
            pltpu.SemaphoreType.DMA((2,)),
            pltpu.SemaphoreType.DMA((2,)),
            pltpu.SemaphoreType.DMA((3,)),
        ],
    )
    return pl.pallas_call(
        functools.partial(_moe_body, n_tokens=n, n_tiles=n_tiles, cast_rows=128,
                          ff_chunks=d_ff // (2 * LANES)),
        grid_spec=grid_spec,
        out_shape=jax.ShapeDtypeStruct((n_slots + tm, hd), jnp.uint32),
        compiler_params=_params(("arbitrary",), vmem=MOE_VMEM_LIMIT),
        name="moe_ffn",
    )(plan["first_e"], plan["n_extra"], plan["switch"], plan["starts"], plan["ends"], plan["next_expert"],
      plan["codes"], h2, w_gate, w_up, w_down)


def _final_body(x1_ref, ys0_ref, ys1_ref, route_ref, g_ref, o_ref):
    d = x1_ref.shape[1]
    route = route_ref[...]
    g1, g2 = route[:, 2:3], route[:, 3:4]
    ya = _unpack_halves(ys0_ref[...])
    yb = _unpack_halves(ys1_ref[...])
    halves = (slice(0, d // 2), slice(d // 2, d))
    x2 = [x1_ref[:, cols] + (ya[i] * g1 + yb[i] * g2) for i, cols in enumerate(halves)]
    ms = sum(jnp.sum(h * h, axis=-1, keepdims=True) for h in x2) * (1.0 / d)
    inv = lax.rsqrt(ms + EPS)
    for h, cols in zip(x2, halves):
        o_ref[:, cols] = ((h * inv) * g_ref[:, cols]).astype(o_ref.dtype)


def _final(x1, ys, route, g, *, tm):
    n, d = x1.shape
    nb = n // tm
    return pl.pallas_call(
        _final_body,
        grid=(nb,),
        in_specs=[
            pl.BlockSpec((tm, d), lambda i: (i, 0)),
            pl.BlockSpec((tm, d // 2), lambda i: (i, 0)),
            pl.BlockSpec((tm, d // 2), lambda i: (nb + i, 0)),
            pl.BlockSpec((tm, LANES), lambda i: (i, 0)),
            pl.BlockSpec((1, d), lambda i: (0, 0)),
        ],
        out_specs=pl.BlockSpec((tm, d), lambda i: (i, 0)),
        out_shape=jax.ShapeDtypeStruct((n, d), F32),
        compiler_params=_params(("arbitrary",)),
        name="final_norm",
    )(x1, ys, ys, route, g)


def _layer(x, rel_bias, attn_norm_g, w_in, lq1, lk1, lq2, lk2, diff_subln_g, ret_gn_g, w_out, ffn_norm_g,
           w_group_router, b_group, w_inner_router, b_inner, w_gate_exp, w_up_exp, w_down_exp, *, layer,
           attn_tile, row_tile, moe_tile):
    b, s, d = x.shape
    n = b * s
    x2d = x.reshape(n, d)
    lam_init = 0.8 - 0.6 * math.exp(-0.3 * layer)

    dq_cols = (COL_DK - COL_DQ) * LANES
    col_scale = jnp.concatenate([jnp.full((1, dq_cols), DIFF_HALF ** -0.5 * LOG2E, F32),
                                 jnp.ones((1, IN_COLS - dq_cols), F32)], axis=1)
    proj = _inproj(x2d, attn_norm_g.reshape(1, d), w_in.astype(BF16), col_scale, tm=min(row_tile, n))
    proj3 = proj.reshape(b, s, IN_COLS)

    bias = _bias_tiles(rel_bias, attn_tile)
    vec = lambda a: a.reshape(1, -1)
    a_out = _diff_attention(proj3, bias, vec(lq1), vec(lk1), vec(lq2), vec(lk2), vec(diff_subln_g),
                            t=attn_tile, lam_init=lam_init)
    r_out = _retention(proj3, _retention_tables(s), vec(ret_gn_g))

    pad = LANES - N_GROUPS - N_EXPERTS
    wr = jnp.concatenate([w_group_router, jnp.transpose(w_inner_router, (1, 0, 2)).reshape(d, N_EXPERTS),
                          jnp.zeros((d, pad), F32)], axis=1).astype(BF16)
    br = jnp.concatenate([b_group, b_inner.reshape(-1), jnp.zeros((pad,), F32)]).reshape(1, LANES)
    x1, h2, route, eid = _outproj_router(a_out.reshape(n, -1), r_out.reshape(n, -1), x2d, w_out.astype(BF16),
                                         vec(ffn_norm_g), wr, br, tm=row_tile)

    plan = _moe_plan(eid.reshape(-1), moe_tile)
    ys = _moe_ffn(h2, plan, w_gate_exp, w_up_exp, w_down_exp, tm=moe_tile)
    return x1, ys, route


def kernel(x, rel_bias, attn_norm_g, w_in, lambda_q1, lambda_k1, lambda_q2, lambda_k2, diff_subln_g, ret_gn_g,
           w_out, ffn_norm_g, w_group_router, b_group, w_inner_router, b_inner, w_gate_exp, w_up_exp, w_down_exp,
           final_g):
    b, s, d = x.shape
    depth = w_in.shape[0]
    assert depth == 1, "the final norm is fused into the single layer's combine step"
    first = lambda a: a.reshape(a.shape[1:])
    x1, ys, route = _layer(x, rel_bias, attn_norm_g[0], w_in[0], lambda_q1[0], lambda_k1[0], lambda_q2[0],
                           lambda_k2[0], diff_subln_g[0], ret_gn_g[0], w_out[0], ffn_norm_g[0],
                           w_group_router[0], b_group[0], w_inner_router[0], b_inner[0], first(w_gate_exp),
                           first(w_up_exp), first(w_down_exp), layer=0, attn_tile=min(256, s), row_tile=512,
                           moe_tile=256)
    out = _final(x1, ys, route, final_g.reshape(1, d), tm=512)
    return out.reshape(b, s, d)
```

```python
import functools
import math

import numpy as np
import jax
import jax.numpy as jnp
from jax import lax
from jax.experimental import pallas as pl
from jax.experimental.pallas import tpu as pltpu

F32 = jnp.float32
BF16 = jnp.bfloat16

EPS = 1e-6
LANES = 128
N_DIFF_HEADS = 8
DIFF_HALF = 64
N_RET_HEADS = 8
RET_QK = 64
RET_CHUNK = 128
ROPE_BASE = 10000.0
N_BUCKETS = 32
MAX_DISTANCE = 128
N_GROUPS = 4
EXPERTS_PER_GROUP = 8
N_EXPERTS = N_GROUPS * EXPERTS_PER_GROUP
TOP_K = 2
MASK_VALUE = -1e30
LOG2E = math.log2(math.e)
VMEM_LIMIT = 56 * 1024 * 1024
MOE_VMEM_LIMIT = 60 * 1024 * 1024

COL_DQ, COL_DK, COL_DV = 0, 8, 16
COL_RQ, COL_RK, COL_RV, COL_RG = 24, 28, 32, 40
IN_COLS = 48 * LANES


def _params(sem, vmem=VMEM_LIMIT):
    return pltpu.CompilerParams(dimension_semantics=sem, vmem_limit_bytes=vmem)


def _pack_halves(x):
    hd = x.shape[1] // 2
    bits = lambda h: lax.bitcast_convert_type(h.astype(BF16).astype(F32), jnp.uint32)
    return lax.shift_right_logical(bits(x[:, :hd]), jnp.uint32(16)) | bits(x[:, hd:])


def _unpack_halves(w):
    return (lax.bitcast_convert_type(lax.shift_left(w, jnp.uint32(16)), F32),
            lax.bitcast_convert_type(w & jnp.uint32(0xFFFF0000), F32))


def _inproj_body(x_ref, g_ref, w_ref, cs_ref, o_ref):
    x = x_ref[...]
    acc = jnp.dot((x * g_ref[...]).astype(BF16), w_ref[...], preferred_element_type=F32)
    inv = lax.rsqrt(jnp.mean(x * x, axis=-1, keepdims=True) + EPS)
    o_ref[...] = ((acc * inv) * cs_ref[...]).astype(o_ref.dtype)


def _inproj(x2d, g, w, col_scale, *, tm):
    n, d = x2d.shape
    cols = w.shape[1]
    once = dict(pipeline_mode=pl.Buffered(1))
    return pl.pallas_call(
        _inproj_body,
        grid=(n // tm,),
        in_specs=[
            pl.BlockSpec((tm, d), lambda i: (i, 0)),
            pl.BlockSpec((1, d), lambda i: (0, 0)),
            pl.BlockSpec((d, cols), lambda i: (0, 0), **once),
            pl.BlockSpec((1, cols), lambda i: (0, 0)),
        ],
        out_specs=pl.BlockSpec((tm, cols), lambda i: (i, 0)),
        out_shape=jax.ShapeDtypeStruct((n, cols), BF16),
        compiler_params=_params(("arbitrary",)),
        name="inproj",
    )(x2d, g, w, col_scale)


def _t5_bucket_table(n):
    d = np.arange(n)
    max_exact = N_BUCKETS // 2
    log_ratio = np.log(np.maximum(d, 1).astype(np.float64) / max_exact) / math.log(MAX_DISTANCE / max_exact)
    large = np.minimum(max_exact + (log_ratio * (N_BUCKETS - max_exact)).astype(np.int64), N_BUCKETS - 1)
    return np.where(d < max_exact, d, large).astype(np.int32)


def _bias_body(rb_ref, bkt_ref, o_ref):
    h = pl.program_id(0)
    t = o_ref.shape[2]
    bkt = bkt_ref[...]
    far = rb_ref[N_BUCKETS - 1, h]
    by_dist = jnp.zeros(bkt.shape, F32)
    for b in range(N_BUCKETS - 1):
        by_dist = jnp.where(bkt == b, (rb_ref[b, h] - far) * LOG2E, by_dist)
    rows = (jnp.concatenate([jnp.full((1, t), MASK_VALUE, F32), by_dist[:, :t]], axis=1), by_dist)
    for i, row in enumerate(rows):
        shifted = pltpu.roll(jnp.broadcast_to(row, (t, 2 * t)), 0, 1, stride=1, stride_axis=0)
        o_ref[0, i] = shifted[:, t:]


def _bias_tiles(rel_bias, t):
    bkt = jnp.asarray(_t5_bucket_table(2 * t)).reshape(1, 2 * t)
    return pl.pallas_call(
        _bias_body,
        grid=(N_DIFF_HEADS,),
        in_specs=[
            pl.BlockSpec(memory_space=pltpu.SMEM),
            pl.BlockSpec((1, 2 * t), lambda h: (0, 0)),
        ],
        out_specs=pl.BlockSpec((1, 2, t, t), lambda h: (h, 0, 0, 0)),
        out_shape=jax.ShapeDtypeStruct((N_DIFF_HEADS, 2, t, t), F32),
        compiler_params=_params(("arbitrary",)),
        name="bias_tiles",
    )(rel_bias, bkt)


_NT = (((1,), (1,)), ((), ()))
_TN = (((0,), (0,)), ((), ()))


def _diffattn_body(q_ref, k_ref, v_ref, bias_ref, lq1_ref, lk1_ref, lq2_ref, lk2_ref, g_ref, o_ref, vt_ref, *,
                   t, lam_init, near_band):
    nq = k_ref.shape[1] // t
    half = t // 2
    lane = lax.broadcasted_iota(jnp.int32, (t, LANES), 1)
    lam = (jnp.exp(jnp.sum(lq1_ref[...] * lk1_ref[...], axis=-1, keepdims=True))
           - jnp.exp(jnp.sum(lq2_ref[...] * lk2_ref[...], axis=-1, keepdims=True)) + lam_init)
    vt_ref[:LANES, :] = v_ref[0].T
    vt_ref[LANES:, :] = jnp.ones((vt_ref.shape[0] - LANES, vt_ref.shape[1]), BF16)
    qmask = {}
    run_max = {}
    acc = {}

    def scores(item):
        qi, j, part = item
        if qi not in qmask:
            q = q_ref[0, qi * t:(qi + 1) * t, :]
            zero = jnp.zeros_like(q)
            qmask[qi] = (jnp.where(lane < DIFF_HALF, q, zero), jnp.where(lane >= DIFF_HALF, q, zero))
        k_lo = j * t + (half if part == "diag_lower" else 0)
        k_hi = j * t + (half if part == "diag_upper" else t)
        kb = k_ref[0, k_lo:k_hi, :]
        out = []
        for c in range(2):
            qsel = qmask[qi][c][half:] if part == "diag_lower" else qmask[qi][c]
            s = lax.dot_general(kb, qsel, _NT, preferred_element_type=F32)
            if part == "diag_upper":
                s = s + bias_ref[0, 0, :half, :]
            elif part == "diag_lower":
                s = s + bias_ref[0, 0, half:, half:]
            elif j == qi - 1:
                if near_band >= t:
                    s = s + bias_ref[0, 1]
                else:
                    corner = s[t - near_band:, :near_band] + bias_ref[0, 1, t - near_band:, :near_band]
                    bottom = jnp.concatenate([corner, s[t - near_band:, near_band:]], axis=1)
                    s = jnp.concatenate([s[:t - near_band], bottom], axis=0)
            out.append(s)
        return out

    def softmax_step(item, ss):
        qi, j, part = item
        first = j == 0 and part != "diag_lower"
        ps, alphas = [], []
        for c in range(2):
            s = ss[c]
            smax = jnp.max(s, axis=0, keepdims=True)
            if first:
                m_new, alpha = smax, None
                run_max[qi, c] = m_new
            else:
                m_full = run_max[qi, c]
                m_old = m_full[:, half:] if part == "diag_lower" else m_full
                m_new = jnp.maximum(m_old, smax)
                alpha = jnp.exp2(m_old - m_new)
                run_max[qi, c] = (jnp.concatenate([m_full[:, :half], m_new], axis=1)
                                  if part == "diag_lower" else m_new)
            ps.append(jnp.exp2(s - m_new).astype(BF16))
            alphas.append(alpha)
        return ps, alphas

    def values_step(item, ps, alphas):
        qi, j, part = item
        k_lo = j * t + (half if part == "diag_lower" else 0)
        k_hi = j * t + (half if part == "diag_upper" else t)
        vtb = vt_ref[:, k_lo:k_hi]
        for c in range(2):
            alpha = alphas[c]
            r = jnp.dot(vtb, ps[c], preferred_element_type=F32)
            if alpha is None:
                acc[qi, c] = r
            elif part == "diag_lower":
                acc[qi, c] = jnp.concatenate([acc[qi, c][:, :half], alpha * acc[qi, c][:, half:] + r], axis=1)
            else:
                acc[qi, c] = alpha * acc[qi, c] + r
        if part == "diag_lower":
            outs = []
            for c in range(2):
                a = acc.pop((qi, c))
                del run_max[qi, c]
                outs.append(a[:LANES] * (1.0 / a[LANES:LANES + 1]))
            ot = (outs[0] - lam * outs[1]).T
            ms = jnp.mean(ot * ot, axis=-1, keepdims=True)
            o_ref[0, qi * t:(qi + 1) * t, :] = (
                ((ot * lax.rsqrt(ms + EPS)) * g_ref[...]) * (1.0 - lam_init)).astype(o_ref.dtype)
            del qmask[qi]

    items = []
    for qi in range(nq):
        items += [(qi, j, "full") for j in range(qi)] + [(qi, qi, "diag_upper"), (qi, qi, "diag_lower")]
    ss = scores(items[0])
    pending = None
    for n, item in enumerate(items):
        nxt = scores(items[n + 1]) if n + 1 < len(items) else None
        ps, alphas = softmax_step(item, ss)
        if pending is not None:
            values_step(*pending)
        pending = (item, ps, alphas)
        ss = nxt
    values_step(*pending)


def _diff_attention(proj3, bias, lq1, lk1, lq2, lk2, subln_g, *, t, lam_init):
    b, s, _ = proj3.shape
    vec = pl.BlockSpec((1, DIFF_HALF), lambda bi, h: (0, 0))
    head = lambda col: pl.BlockSpec((1, s, LANES), lambda bi, h: (bi, 0, col + h))
    first_far = int(np.argmax(_t5_bucket_table(2 * t) == N_BUCKETS - 1))
    near_band = -(-first_far // LANES) * LANES
    ones_rows = 16
    return pl.pallas_call(
        functools.partial(_diffattn_body, t=t, lam_init=lam_init, near_band=near_band),
        grid=(b, N_DIFF_HEADS),
        in_specs=[
            head(COL_DQ), head(COL_DK), head(COL_DV),
            pl.BlockSpec((1, 2, t, t), lambda bi, h: (h, 0, 0, 0)),
            vec, vec, vec, vec,
            pl.BlockSpec((1, LANES), lambda bi, h: (0, 0)),
        ],
        out_specs=pl.BlockSpec((1, s, LANES), lambda bi, h: (bi, 0, h)),
        out_shape=jax.ShapeDtypeStruct((b, s, N_DIFF_HEADS * LANES), BF16),
        scratch_shapes=[pltpu.VMEM((LANES + ones_rows, s), BF16)],
        compiler_params=_params(("arbitrary", "arbitrary")),
        name="diff_attention",
    )(proj3, proj3, proj3, bias, lq1, lk1, lq2, lk2, subln_g)


def _retention_tables(s):
    c = RET_CHUNK
    half = RET_QK // 2
    inv_freq = ROPE_BASE ** (-jnp.arange(0, RET_QK, 2, dtype=F32) / RET_QK)
    ang = jnp.arange(s, dtype=F32)[:, None] * inv_freq[None, :]
    cos, sin = jnp.cos(ang), jnp.sin(ang)
    cos_t = jnp.tile(cos, (1, LANES // half))
    sin_t = jnp.tile(jnp.concatenate([-sin, sin], axis=1), (1, LANES // RET_QK))
    log_g = jnp.log(1.0 - jnp.exp2(-5.0 - jnp.arange(N_RET_HEADS, dtype=F32)))
    n = jnp.arange(c, dtype=F32)
    diff = n[:, None] - n[None, :]
    inner = jnp.where(diff[None] >= 0, jnp.exp(jnp.maximum(diff, 0.0)[None] * log_g[:, None, None]), 0.0)
    cross = jnp.exp((n[None] + 1.0) * log_g[:, None])
    key = jnp.exp((c - 1.0 - n[None]) * log_g[:, None])
    chunk = jnp.exp(c * log_g)
    bc = lambda a: jnp.broadcast_to(a[..., None], a.shape + (LANES,)).astype(F32)
    return cos_t, sin_t, inner.astype(F32), bc(cross), bc(key), bc(chunk[:, None])


def _retention_body(rq_ref, rk_ref, rv_ref, rg_ref, cos_ref, sin_ref, inner_ref, cross_ref, key_ref,
                    chunk_ref, gn_ref, o_ref, state_ref):
    c = RET_CHUNK
    nc = rq_ref.shape[1] // c
    nh = N_RET_HEADS
    state_ref[...] = jnp.zeros(state_ref.shape, F32)
    lane = lax.broadcasted_iota(jnp.int32, (c, LANES), 1)
    first_half = (lane % RET_QK) < (RET_QK // 2)
    head_mask = [lane < RET_QK, lane >= RET_QK]
    slab = lambda i: slice(i * LANES, (i + 1) * LANES)

    def rotary(x, cos, sin):
        rot = jnp.where(first_half, pltpu.roll(x, LANES - RET_QK // 2, 1), pltpu.roll(x, RET_QK // 2, 1))
        return x * cos + rot * sin

    def chunk_step(ci, carry):
        r = pl.multiple_of(ci * c, c)
        rows = pl.ds(r, c)
        cos = cos_ref[rows, :]
        sin = sin_ref[rows, :]
        q_pair, k_pair, kb_pair = [], [], []
        for pr in range(nh // 2):
            q_pair.append(rotary(rq_ref[0, rows, slab(pr)].astype(F32), cos, sin))
            k = rotary(rk_ref[0, rows, slab(pr)].astype(F32), cos, sin) * (RET_QK ** -0.5)
            k_pair.append(k)
            kb_pair.append(k.astype(BF16))
        qh = [jnp.where(head_mask[h % 2], q_pair[h // 2], 0.0).astype(BF16) for h in range(nh)]
        kd = [(k_pair[h // 2] * key_ref[h]).astype(BF16) for h in range(nh)]
        vh = [rv_ref[0, rows, slab(h)] for h in range(nh)]
        scores = [lax.dot_general(qh[h], kb_pair[h // 2], _NT, preferred_element_type=F32) for h in range(nh)]
        cross = [jnp.dot(qh[h], state_ref[h].astype(BF16), preferred_element_type=F32) for h in range(nh)]
        kv = [lax.dot_general(kd[h], vh[h], _TN, preferred_element_type=F32) for h in range(nh)]
        decayed = [(scores[h] * inner_ref[h]).astype(BF16) for h in range(nh)]
        for h in range(nh):
            state_ref[h] = state_ref[h] * chunk_ref[h] + kv[h]
        inner = [jnp.dot(decayed[h], vh[h], preferred_element_type=F32) for h in range(nh)]
        for h in range(nh):
            y = inner[h] + cross[h] * cross_ref[h]
            mu = jnp.mean(y, axis=-1, keepdims=True)
            yc = y - mu
            var = jnp.mean(yc * yc, axis=-1, keepdims=True)
            yn = (yc * lax.rsqrt(var + EPS)) * gn_ref[:, slab(h)]
            gate = rg_ref[0, rows, slab(h)].astype(F32)
            silu = gate * (1.0 / (1.0 + jnp.exp(-gate)))
            o_ref[0, rows, slab(h)] = (silu * yn).astype(o_ref.dtype)
        return carry

    lax.fori_loop(0, nc, chunk_step, 0)


def _retention(proj3, tables, gn_g):
    b, s, _ = proj3.shape
    cos_t, sin_t, inner, cross, key, chunk = tables
    c = RET_CHUNK
    nh = N_RET_HEADS
    qk_w = nh * RET_QK
    v_w = nh * LANES
    full = lambda shape: pl.BlockSpec(shape, lambda bi: (0,) * len(shape))
    return pl.pallas_call(
        _retention_body,
        grid=(b,),
        in_specs=[
            pl.BlockSpec((1, s, qk_w), lambda bi: (bi, 0, COL_RQ * LANES // qk_w)),
            pl.BlockSpec((1, s, qk_w), lambda bi: (bi, 0, COL_RK * LANES // qk_w)),
            pl.BlockSpec((1, s, v_w), lambda bi: (bi, 0, COL_RV * LANES // v_w)),
            pl.BlockSpec((1, s, v_w), lambda bi: (bi, 0, COL_RG * LANES // v_w)),
            full((s, LANES)), full((s, LANES)),
            full((nh, c, c)), full((nh, c, LANES)), full((nh, c, LANES)), full((nh, 1, LANES)),
            full((1, v_w)),
        ],
        out_specs=pl.BlockSpec((1, s, v_w), lambda bi: (bi, 0, 0)),
        out_shape=jax.ShapeDtypeStruct((b, s, v_w), BF16),
        scratch_shapes=[pltpu.VMEM((nh, LANES, LANES), F32)],
        compiler_params=_params(("arbitrary",)),
        name="retention",
    )(proj3, proj3, proj3, proj3, cos_t, sin_t, inner, cross, key, chunk, gn_g)


def _route(logits):
    lane = lax.broadcasted_iota(jnp.int32, logits.shape, 1).astype(F32)
    big = float(LANES)
    is_group = lane < N_GROUPS
    gl = jnp.where(is_group, logits, -jnp.inf)
    gmax = jnp.max(gl, axis=-1, keepdims=True)
    gidx = jnp.min(jnp.where(gl == gmax, lane, big), axis=-1, keepdims=True)
    p_group = 1.0 / jnp.sum(jnp.where(is_group, jnp.exp(gl - gmax), 0.0), axis=-1, keepdims=True)
    lo = N_GROUPS + EXPERTS_PER_GROUP * gidx
    il = jnp.where(lane >= lo, jnp.where(lane < lo + EXPERTS_PER_GROUP, logits, -jnp.inf), -jnp.inf)
    v1 = jnp.max(il, axis=-1, keepdims=True)
    i1 = jnp.min(jnp.where(il == v1, lane, big), axis=-1, keepdims=True)
    il2 = jnp.where(lane == i1, -jnp.inf, il)
    v2 = jnp.max(il2, axis=-1, keepdims=True)
    i2 = jnp.min(jnp.where(il2 == v2, lane, big), axis=-1, keepdims=True)
    e2 = jnp.exp(v2 - v1)
    inv = 1.0 / (1.0 + e2)
    g1 = p_group * inv
    g2 = p_group * (e2 * inv)
    out = jnp.where(lane == 0.0, i1 - N_GROUPS,
                    jnp.where(lane == 1.0, i2 - N_GROUPS,
                              jnp.where(lane == 2.0, g1, jnp.where(lane == 3.0, g2, 0.0))))
    return out


def _outproj_body(a_ref, r_ref, x_ref, w_ref, g_ref, wr_ref, br_ref, x1_ref, h2_ref, route_ref, eid_ref, *, sub):
    tm = x_ref.shape[0]
    half = a_ref.shape[1]
    n_sub = tm // sub

    def project(i):
        rows = slice(i * sub, (i + 1) * sub)
        acc = jnp.dot(a_ref[rows, :], w_ref[:half, :], preferred_element_type=F32)
        return acc + jnp.dot(r_ref[rows, :], w_ref[half:, :], preferred_element_type=F32)

    def epilogue(i, acc):
        rows = slice(i * sub, (i + 1) * sub)
        x1 = x_ref[rows, :] + acc
        x1_ref[rows, :] = x1
        ms = jnp.mean(x1 * x1, axis=-1, keepdims=True)
        h2 = (x1 * lax.rsqrt(ms + EPS)) * g_ref[...]
        h2_ref[rows, :] = _pack_halves(h2)
        logits = jnp.dot(h2.astype(BF16), wr_ref[...], preferred_element_type=F32) + br_ref[...]
        route = _route(logits)
        route_ref[rows, :] = route
        eid_ref[:, rows] = route.T[:TOP_K, :].astype(jnp.int32)

    acc = project(0)
    for i in range(n_sub):
        nxt = project(i + 1) if i + 1 < n_sub else None
        epilogue(i, acc)
        acc = nxt


def _outproj_router(a2d, r2d, x2d, w_out_bf16, g, wr_bf16, br, *, tm):
    n, d = x2d.shape
    half = a2d.shape[1]
    row = lambda width: pl.BlockSpec((tm, width), lambda i: (i, 0))
    full = lambda shape: pl.BlockSpec(shape, lambda i: (0, 0))
    return pl.pallas_call(
        functools.partial(_outproj_body, sub=min(tm, 2 * LANES)),
        grid=(n // tm,),
        in_specs=[row(half), row(half), row(d), full((2 * half, d)), full((1, d)), full((d, LANES)),
                  full((1, LANES))],
        out_specs=[row(d), row(d // 2), row(LANES), pl.BlockSpec((TOP_K, tm), lambda i: (0, i))],
        out_shape=[jax.ShapeDtypeStruct((n, d), F32), jax.ShapeDtypeStruct((n, d // 2), jnp.uint32),
                   jax.ShapeDtypeStruct((n, LANES), F32), jax.ShapeDtypeStruct((TOP_K, n), jnp.int32)],
        compiler_params=_params(("arbitrary",)),
        name="outproj_router",
    )(a2d, r2d, x2d, w_out_bf16, g, wr_bf16, br)


def _moe_plan(expert_id, tm):
    n_slots = expert_id.shape[0]
    n_tiles = n_slots // tm
    order = jnp.argsort(expert_id).astype(jnp.int32)
    sizes = jnp.zeros((N_EXPERTS,), jnp.int32).at[expert_id].add(1)
    ends = jnp.cumsum(sizes)
    starts = ends - sizes
    tile_row0 = jnp.arange(n_tiles, dtype=jnp.int32) * tm
    owner = lambda row: jnp.sum((ends[None, :] <= row[:, None]).astype(jnp.int32), axis=1)
    first_e = owner(tile_row0)
    last_e = owner(tile_row0 + (tm - 1))
    owners_upto = jnp.cumsum((sizes > 0).astype(jnp.int32))
    n_extra = (owners_upto[last_e] - owners_upto[first_e]).astype(jnp.int32)
    prev_last = jnp.concatenate([jnp.full((1,), -1, jnp.int32), last_e[:-1]])
    switch = (prev_last != first_e).astype(jnp.int32)
    eid = jnp.arange(N_EXPERTS, dtype=jnp.int32)
    cand = jnp.where((sizes[None, :] > 0) & (eid[None, :] > eid[:, None]), eid[None, :], N_EXPERTS)
    nxt = jnp.min(cand, axis=1)
    next_expert = jnp.where(nxt < N_EXPERTS, nxt, -1).astype(jnp.int32)
    spare = n_slots + jnp.arange(tm, dtype=jnp.int32)
    codes = jnp.concatenate([order.reshape(n_tiles, tm), spare[None, :]], axis=0).reshape(n_tiles + 1, 1, tm)
    return dict(first_e=first_e, n_extra=n_extra, switch=switch, starts=starts.astype(jnp.int32),
                ends=ends.astype(jnp.int32), next_expert=next_expert, codes=codes)


def _moe_body(fe_ref, nextra_ref, sw_ref, st_ref, en_ref, nx_ref,
              codes_hbm, h2_hbm, wg_hbm, wu_hbm, wd_hbm, ys_hbm,
              gidx, sidx, xbuf, obuf, wg_st, wu_st, wd_st, wg_rs, wu_rs, wd_rs,
              gisem, sisem, gsem, ssem, wsem, *, n_tokens, n_tiles, cast_rows, ff_chunks):
    t = pl.program_id(0)
    tm = xbuf.shape[1]
    last = n_tiles - 1
    spare_codes = n_tiles

    def gidx_copy(tile, slot):
        return pltpu.make_async_copy(codes_hbm.at[tile], gidx.at[slot], gisem.at[slot])

    def sidx_copy(tile, slot):
        return pltpu.make_async_copy(codes_hbm.at[tile], sidx.at[slot], sisem.at[slot])

    def weight_copies(ex):
        return (pltpu.make_async_copy(wg_hbm.at[ex], wg_st, wsem.at[0]),
                pltpu.make_async_copy(wu_hbm.at[ex], wu_st, wsem.at[1]),
                pltpu.make_async_copy(wd_hbm.at[ex], wd_st, wsem.at[2]))

    def gather_row(slot, r):
        code = gidx[slot, 0, r]
        tok = code - jnp.where(code >= n_tokens, n_tokens, 0)
        return pltpu.make_async_copy(h2_hbm.at[pl.ds(tok, 1)], xbuf.at[slot, pl.ds(r, 1)], gsem.at[slot])

    def scatter_row(slot, r):
        dst = sidx[slot, 0, r]
        return pltpu.make_async_copy(obuf.at[slot, pl.ds(r, 1)], ys_hbm.at[pl.ds(dst, 1)], ssem.at[slot])

    def wait_gather(slot):
        pltpu.make_async_copy(h2_hbm.at[pl.ds(0, tm)], xbuf.at[slot], gsem.at[slot]).wait()

    def wait_scatter(slot):
        pltpu.make_async_copy(obuf.at[slot], ys_hbm.at[pl.ds(0, tm)], ssem.at[slot]).wait()

    def for_rows(fn):
        def body(r, carry):
            fn(r)
            return carry
        lax.fori_loop(0, tm, body, 0, unroll=8)

    def take_over(ex, first):
        @pl.when(first)
        def _():
            for cp in weight_copies(ex):
                cp.start()

        for cp in weight_copies(ex):
            cp.wait()

        def cast_up(i, carry):
            r = pl.multiple_of(i * cast_rows, cast_rows)
            wg_rs[pl.ds(r, cast_rows), :] = wg_st[pl.ds(r, cast_rows), :].astype(BF16)
            wu_rs[pl.ds(r, cast_rows), :] = wu_st[pl.ds(r, cast_rows), :].astype(BF16)
            return carry

        def cast_down(i, carry):
            r = pl.multiple_of(i * cast_rows, cast_rows)
            wd_rs[pl.ds(r, cast_rows), :] = wd_st[pl.ds(r, cast_rows), :].astype(BF16)
            return carry

        lax.fori_loop(0, wg_st.shape[0] // cast_rows, cast_up, 0)
        lax.fori_loop(0, wd_st.shape[0] // cast_rows, cast_down, 0)
        nxt = nx_ref[ex]

        @pl.when(nxt >= 0)
        def _():
            for cp in weight_copies(nxt):
                cp.start()

    def ffn(slot, ex, accumulate, row_dmas=None):
        x_lo, x_hi = (h.astype(BF16) for h in _unpack_halves(xbuf[slot]))
        hd = x_lo.shape[1]
        row = t * tm + lax.broadcasted_iota(jnp.int32, (tm, 1), 0)
        mine = jnp.logical_and(row >= st_ref[ex], row < en_ref[ex])
        d_ff = wg_rs.shape[1]
        n_chunks = ff_chunks if row_dmas is not None else 1
        fc = d_ff // n_chunks
        contrib = None
        for c in range(n_chunks):
            cols = slice(c * fc, (c + 1) * fc)
            up = lambda w: (jnp.dot(x_lo, w[:hd, cols], preferred_element_type=F32)
                            + jnp.dot(x_hi, w[hd:, cols], preferred_element_type=F32))
            a = up(wg_rs)
            b = up(wu_rs)
            hmid = jnp.where(mine, (a * (1.0 / (1.0 + jnp.exp(-a)))) * b, 0.0).astype(BF16)
            part = jnp.dot(hmid, wd_rs[cols, :], preferred_element_type=F32)
            contrib = part if contrib is None else contrib + part
            if row_dmas is not None:
                row_dmas(c, n_chunks)
        if accumulate:
            prev_lo, prev_hi = _unpack_halves(obuf[slot])
            contrib = jnp.concatenate([prev_lo, prev_hi], axis=1) + contrib
        obuf[slot] = _pack_halves(contrib)

    e0 = fe_ref[t]

    def step(p):
        q = 1 - p
        if p == 0:
            @pl.when(t == 0)
            def _():
                copies = (gidx_copy(0, 0), gidx_copy(min(1, last), 1), sidx_copy(spare_codes, 1))
                for cp in copies:
                    cp.start()
                for cp in copies:
                    cp.wait()
                obuf[1] = jnp.zeros(obuf.shape[1:], obuf.dtype)
                for_rows(lambda r: gather_row(0, r).start())

        gidx_copy(jnp.minimum(t + 2, last), p).start()
        sidx_copy(t, p).start()

        @pl.when(t >= 1)
        def _():
            gidx_copy(0, q).wait()
            sidx_copy(0, q).wait()
            wait_scatter(p)

        wait_gather(p)

        @pl.when(sw_ref[t] == 1)
        def _():
            take_over(e0, t == 0)

        def row_dmas(c, n_chunks):
            early = max(n_chunks - 1, 1)
            for r in range(tm):
                if r * early // tm == c:
                    scatter_row(q, r).start(priority=r % 2)
                    gather_row(q, r).start(priority=r % 2)

        ffn(p, e0, accumulate=False, row_dmas=row_dmas)

    for parity in range(2):
        pl.when(lax.rem(t, 2) == parity)(functools.partial(step, parity))

    slot = lax.rem(t, 2)

    def further_owner(_, ex):
        nxt = nx_ref[ex]
        take_over(nxt, False)
        ffn(slot, nxt, accumulate=True)
        return nxt

    lax.fori_loop(0, nextra_ref[t], further_owner, e0)

    @pl.when(t == last)
    def _():
        gidx_copy(0, slot).wait()
        sidx_copy(0, slot).wait()
        for_rows(lambda r: scatter_row(slot, r).start())
        wait_gather(1 - slot)
        wait_scatter(1 - slot)
        wait_scatter(slot)


def _moe_ffn(h2, plan, w_gate, w_up, w_down, *, tm):
    n, hd = h2.shape
    d = 2 * hd
    n_slots = 2 * n
    n_tiles = n_slots // tm
    assert n_tiles >= 2 and n_slots % tm == 0
    d_ff = w_gate.shape[2]
    any_spec = pl.BlockSpec(memory_space=pl.ANY)
    grid_spec = pltpu.PrefetchScalarGridSpec(
        num_scalar_prefetch=6,
        grid=(n_tiles,),
        in_specs=[any_spec] * 5,
        out_specs=any_spec,
        scratch_shapes=[
            pltpu.SMEM((2, 1, tm), jnp.int32),
            pltpu.SMEM((2, 1, tm), jnp.int32),
            pltpu.VMEM((2, tm, hd), jnp.uint32),
            pltpu.VMEM((2, tm, hd), jnp.uint32),
            pltpu.VMEM((d, d_ff), F32),
            pltpu.VMEM((d, d_ff), F32),
            pltpu.VMEM((d_ff, d), F32),
            pltpu.VMEM((d, d_ff), BF16),
            pltpu.VMEM((d, d_ff), BF16),
            pltpu.VMEM((d_ff, d), BF16),
            pltpu.SemaphoreType.DMA((2,)),
            pltpu.SemaphoreType.DMA((2,)),
            pltpu.SemaphoreType.DMA((2,)),
            pltpu.SemaphoreType.DMA((2,)),
            pltpu.SemaphoreType.DMA((3,)),
        ],
    )
    return pl.pallas_call(
        functools.partial(_moe_body, n_tokens=n, n_tiles=n_tiles, cast_rows=128,
                          ff_chunks=d_ff // (2 * LANES)),
        grid_spec=grid_spec,
        out_shape=jax.ShapeDtypeStruct((n_slots + tm, hd), jnp.uint32),
        compiler_params=_params(("arbitrary",), vmem=MOE_VMEM_LIMIT),
        name="moe_ffn",
    )(plan["first_e"], plan["n_extra"], plan["switch"], plan["starts"], plan["ends"], plan["next_expert"],
      plan["codes"], h2, w_gate, w_up, w_down)


def _final_body(x1_ref, ys0_ref, ys1_ref, route_ref, g_ref, o_ref):
    d = x1_ref.shape[1]
    route = route_ref[...]
    g1, g2 = route[:, 2:3], route[:, 3:4]
    ya = _unpack_halves(ys0_ref[...])
    yb = _unpack_halves(ys1_ref[...])
    halves = (slice(0, d // 2), slice(d // 2, d))
    x2 = [x1_ref[:, cols] + (ya[i] * g1 + yb[i] * g2) for i, cols in enumerate(halves)]
    ms = sum(jnp.sum(h * h, axis=-1, keepdims=True) for h in x2) * (1.0 / d)
    inv = lax.rsqrt(ms + EPS)
    for h, cols in zip(x2, halves):
        o_ref[:, cols] = ((h * inv) * g_ref[:, cols]).astype(o_ref.dtype)


def _final(x1, ys, route, g, *, tm):
    n, d = x1.shape
    nb = n // tm
    return pl.pallas_call(
        _final_body,
        grid=(nb,),
        in_specs=[
            pl.BlockSpec((tm, d), lambda i: (i, 0)),
            pl.BlockSpec((tm, d // 2), lambda i: (i, 0)),
            pl.BlockSpec((tm, d // 2), lambda i: (nb + i, 0)),
            pl.BlockSpec((tm, LANES), lambda i: (i, 0)),
            pl.BlockSpec((1, d), lambda i: (0, 0)),
        ],
        out_specs=pl.BlockSpec((tm, d), lambda i: (i, 0)),
        out_shape=jax.ShapeDtypeStruct((n, d), F32),
        compiler_params=_params(("arbitrary",)),
        name="final_norm",
    )(x1, ys, ys, route, g)


def _layer(x, rel_bias, attn_norm_g, w_in, lq1, lk1, lq2, lk2, diff_subln_g, ret_gn_g, w_out, ffn_norm_g,
           w_group_router, b_group, w_inner_router, b_inner, w_gate_exp, w_up_exp, w_down_exp, *, layer,
           attn_tile, row_tile, moe_tile):
    b, s, d = x.shape
    n = b * s
    x2d = x.reshape(n, d)
    lam_init = 0.8 - 0.6 * math.exp(-0.3 * layer)

    dq_cols = (COL_DK - COL_DQ) * LANES
    col_scale = jnp.concatenate([jnp.full((1, dq_cols), DIFF_HALF ** -0.5 * LOG2E, F32),
                                 jnp.ones((1, IN_COLS - dq_cols), F32)], axis=1)
    proj = _inproj(x2d, attn_norm_g.reshape(1, d), w_in.astype(BF16), col_scale, tm=min(row_tile, n))
    proj3 = proj.reshape(b, s, IN_COLS)

    bias = _bias_tiles(rel_bias, attn_tile)
    vec = lambda a: a.reshape(1, -1)
    a_out = _diff_attention(proj3, bias, vec(lq1), vec(lk1), vec(lq2), vec(lk2), vec(diff_subln_g),
                            t=attn_tile, lam_init=lam_init)
    r_out = _retention(proj3, _retention_tables(s), vec(ret_gn_g))

    pad = LANES - N_GROUPS - N_EXPERTS
    wr = jnp.concatenate([w_group_router, jnp.transpose(w_inner_router, (1, 0, 2)).reshape(d, N_EXPERTS),
                          jnp.zeros((d, pad), F32)], axis=1).astype(BF16)
    br = jnp.concatenate([b_group, b_inner.reshape(-1), jnp.zeros((pad,), F32)]).reshape(1, LANES)
    x1, h2, route, eid = _outproj_router(a_out.reshape(n, -1), r_out.reshape(n, -1), x2d, w_out.astype(BF16),
                                         vec(ffn_norm_g), wr, br, tm=row_tile)

    plan = _moe_plan(eid.reshape(-1), moe_tile)
    ys = _moe_ffn(h2, plan, w_gate_exp, w_up_exp, w_down_exp, tm=moe_tile)
    return x1, ys, route


def kernel(x, rel_bias, attn_norm_g, w_in, lambda_q1, lambda_k1, lambda_q2, lambda_k2, diff_subln_g, ret_gn_g,
           w_out, ffn_norm_g, w_group_router, b_group, w_inner_router, b_inner, w_gate_exp, w_up_exp, w_down_exp,
           final_g):
    b, s, d = x.shape
    depth = w_in.shape[0]
    assert depth == 1, "the final norm is fused into the single layer's combine step"
    first = lambda a: a.reshape(a.shape[1:])
    x1, ys, route = _layer(x, rel_bias, attn_norm_g[0], w_in[0], lambda_q1[0], lambda_k1[0], lambda_q2[0],
                           lambda_k2[0], diff_subln_g[0], ret_gn_g[0], w_out[0], ffn_norm_g[0],
                           w_group_router[0], b_group[0], w_inner_router[0], b_inner[0], first(w_gate_exp),
                           first(w_up_exp), first(w_down_exp), layer=0, attn_tile=min(256, s), row_tile=512,
                           moe_tile=256)
    out = _final(x1, ys, route, final_g.reshape(1, d), tm=512)
    return out.reshape(b, s, d)
```

```python
import functools
import math

import numpy as np
import jax
import jax.numpy as jnp
from jax import lax
from jax.experimental import pallas as pl
from jax.experimental.pallas import tpu as pltpu

F32 = jnp.float32
BF16 = jnp.bfloat16

EPS = 1e-6
LANES = 128
N_DIFF_HEADS = 8
DIFF_HALF = 64
N_RET_HEADS = 8
RET_QK = 64
RET_CHUNK = 128
ROPE_BASE = 10000.0
N_BUCKETS = 32
MAX_DISTANCE = 128
N_GROUPS = 4
EXPERTS_PER_GROUP = 8
N_EXPERTS = N_GROUPS * EXPERTS_PER_GROUP
TOP_K = 2
MASK_VALUE = -1e30
LOG2E = math.log2(math.e)
VMEM_LIMIT = 56 * 1024 * 1024
MOE_VMEM_LIMIT = 60 * 1024 * 1024

COL_DQ, COL_DK, COL_DV = 0, 8, 16
COL_RQ, COL_RK, COL_RV, COL_RG = 24, 28, 32, 40
IN_COLS = 48 * LANES


def _params(sem, vmem=VMEM_LIMIT):
    return pltpu.CompilerParams(dimension_semantics=sem, vmem_limit_bytes=vmem)


def _pack_halves(x):
    hd = x.shape[1] // 2
    bits = lambda h: lax.bitcast_convert_type(h.astype(BF16).astype(F32), jnp.uint32)
    return lax.shift_right_logical(bits(x[:, :hd]), jnp.uint32(16)) | bits(x[:, hd:])


def _unpack_halves(w):
    return (lax.bitcast_convert_type(lax.shift_left(w, jnp.uint32(16)), F32),
            lax.bitcast_convert_type(w & jnp.uint32(0xFFFF0000), F32))


def _inproj_body(x_ref, g_ref, w_ref, cs_ref, o_ref):
    x = x_ref[...]
    acc = jnp.dot((x * g_ref[...]).astype(BF16), w_ref[...], preferred_element_type=F32)
    inv = lax.rsqrt(jnp.mean(x * x, axis=-1, keepdims=True) + EPS)
    o_ref[...] = ((acc * inv) * cs_ref[...]).astype(o_ref.dtype)


def _inproj(x2d, g, w, col_scale, *, tm):
    n, d = x2d.shape
    cols = w.shape[1]
    once = dict(pipeline_mode=pl.Buffered(1))
    return pl.pallas_call(
        _inproj_body,
        grid=(n // tm,),
        in_specs=[
            pl.BlockSpec((tm, d), lambda i: (i, 0)),
            pl.BlockSpec((1, d), lambda i: (0, 0)),
            pl.BlockSpec((d, cols), lambda i: (0, 0), **once),
            pl.BlockSpec((1, cols), lambda i: (0, 0)),
        ],
        out_specs=pl.BlockSpec((tm, cols), lambda i: (i, 0)),
        out_shape=jax.ShapeDtypeStruct((n, cols), BF16),
        compiler_params=_params(("arbitrary",)),
        name="inproj",
    )(x2d, g, w, col_scale)


def _t5_bucket_table(n):
    d = np.arange(n)
    max_exact = N_BUCKETS // 2
    log_ratio = np.log(np.maximum(d, 1).astype(np.float64) / max_exact) / math.log(MAX_DISTANCE / max_exact)
    large = np.minimum(max_exact + (log_ratio * (N_BUCKETS - max_exact)).astype(np.int64), N_BUCKETS - 1)
    return np.where(d < max_exact, d, large).astype(np.int32)


def _bias_body(rb_ref, bkt_ref, o_ref):
    h = pl.program_id(0)
    t = o_ref.shape[2]
    bkt = bkt_ref[...]
    far = rb_ref[N_BUCKETS - 1, h]
    by_dist = jnp.zeros(bkt.shape, F32)
    for b in range(N_BUCKETS - 1):
        by_dist = jnp.where(bkt == b, (rb_ref[b, h] - far) * LOG2E, by_dist)
    rows = (jnp.concatenate([jnp.full((1, t), MASK_VALUE, F32), by_dist[:, :t]], axis=1), by_dist)
    for i, row in enumerate(rows):
        shifted = pltpu.roll(jnp.broadcast_to(row, (t, 2 * t)), 0, 1, stride=1, stride_axis=0)
        o_ref[0, i] = shifted[:, t:]


def _bias_tiles(rel_bias, t):
    bkt = jnp.asarray(_t5_bucket_table(2 * t)).reshape(1, 2 * t)
    return pl.pallas_call(
        _bias_body,
        grid=(N_DIFF_HEADS,),
        in_specs=[
            pl.BlockSpec(memory_space=pltpu.SMEM),
            pl.BlockSpec((1, 2 * t), lambda h: (0, 0)),
        ],
        out_specs=pl.BlockSpec((1, 2, t, t), lambda h: (h, 0, 0, 0)),
        out_shape=jax.ShapeDtypeStruct((N_DIFF_HEADS, 2, t, t), F32),
        compiler_params=_params(("arbitrary",)),
        name="bias_tiles",
    )(rel_bias, bkt)


_NT = (((1,), (1,)), ((), ()))
_TN = (((0,), (0,)), ((), ()))


def _diffattn_body(q_ref, k_ref, v_ref, bias_ref, lq1_ref, lk1_ref, lq2_ref, lk2_ref, g_ref, o_ref, vt_ref, *,
                   t, lam_init, near_band):
    nq = k_ref.shape[1] // t
    half = t // 2
    lane = lax.broadcasted_iota(jnp.int32, (t, LANES), 1)
    lam = (jnp.exp(jnp.sum(lq1_ref[...] * lk1_ref[...], axis=-1, keepdims=True))
           - jnp.exp(jnp.sum(lq2_ref[...] * lk2_ref[...], axis=-1, keepdims=True)) + lam_init)
    vt_ref[:LANES, :] = v_ref[0].T
    vt_ref[LANES:, :] = jnp.ones((vt_ref.shape[0] - LANES, vt_ref.shape[1]), BF16)
    qmask = {}
    run_max = {}
    acc = {}

    def scores(item):
        qi, j, part = item
        if qi not in qmask:
            q = q_ref[0, qi * t:(qi + 1) * t, :]
            zero = jnp.zeros_like(q)
            qmask[qi] = (jnp.where(lane < DIFF_HALF, q, zero), jnp.where(lane >= DIFF_HALF, q, zero))
        k_lo = j * t + (half if part == "diag_lower" else 0)
        k_hi = j * t + (half if part == "diag_upper" else t)
        kb = k_ref[0, k_lo:k_hi, :]
        out = []
        for c in range(2):
            qsel = qmask[qi][c][half:] if part == "diag_lower" else qmask[qi][c]
            s = lax.dot_general(kb, qsel, _NT, preferred_element_type=F32)
            if part == "diag_upper":
                s = s + bias_ref[0, 0, :half, :]
            elif part == "diag_lower":
                s = s + bias_ref[0, 0, half:, half:]
            elif j == qi - 1:
                if near_band >= t:
                    s = s + bias_ref[0, 1]
                else:
                    corner = s[t - near_band:, :near_band] + bias_ref[0, 1, t - near_band:, :near_band]
                    bottom = jnp.concatenate([corner, s[t - near_band:, near_band:]], axis=1)
                    s = jnp.concatenate([s[:t - near_band], bottom], axis=0)
            out.append(s)
        return out

    def softmax_step(item, ss):
        qi, j, part = item
        first = j == 0 and part != "diag_lower"
        ps, alphas = [], []
        for c in range(2):
            s = ss[c]
            smax = jnp.max(s, axis=0, keepdims=True)
            if first:
                m_new, alpha = smax, None
                run_max[qi, c] = m_new
            else:
                m_full = run_max[qi, c]
                m_old = m_full[:, half:] if part == "diag_lower" else m_full
                m_new = jnp.maximum(m_old, smax)
                alpha = jnp.exp2(m_old - m_new)
                run_max[qi, c] = (jnp.concatenate([m_full[:, :half], m_new], axis=1)
                                  if part == "diag_lower" else m_new)
            ps.append(jnp.exp2(s - m_new).astype(BF16))
            alphas.append(alpha)
        return ps, alphas

    def values_step(item, ps, alphas):
        qi, j, part = item
        k_lo = j * t + (half if part == "diag_lower" else 0)
        k_hi = j * t + (half if part == "diag_upper" else t)
        vtb = vt_ref[:, k_lo:k_hi]
        for c in range(2):
            alpha = alphas[c]
            r = jnp.dot(vtb, ps[c], preferred_element_type=F32)
            if alpha is None:
                acc[qi, c] = r
            elif part == "diag_lower":
                acc[qi, c] = jnp.concatenate([acc[qi, c][:, :half], alpha * acc[qi, c][:, half:] + r], axis=1)
            else:
                acc[qi, c] = alpha * acc[qi, c] + r
        if part == "diag_lower":
            outs = []
            for c in range(2):
                a = acc.pop((qi, c))
                del run_max[qi, c]
                outs.append(a[:LANES] * (1.0 / a[LANES:LANES + 1]))
            ot = (outs[0] - lam * outs[1]).T
            ms = jnp.mean(ot * ot, axis=-1, keepdims=True)
            o_ref[0, qi * t:(qi + 1) * t, :] = (
                ((ot * lax.rsqrt(ms + EPS)) * g_ref[...]) * (1.0 - lam_init)).astype(o_ref.dtype)
            del qmask[qi]

    items = []
    for qi in range(nq):
        items += [(qi, j, "full") for j in range(qi)] + [(qi, qi, "diag_upper"), (qi, qi, "diag_lower")]
    ss = scores(items[0])
    pending = None
    for n, item in enumerate(items):
        nxt = scores(items[n + 1]) if n + 1 < len(items) else None
        ps, alphas = softmax_step(item, ss)
        if pending is not None:
            values_step(*pending)
        pending = (item, ps, alphas)
        ss = nxt
    values_step(*pending)


def _diff_attention(proj3, bias, lq1, lk1, lq2, lk2, subln_g, *, t, lam_init):
    b, s, _ = proj3.shape
    vec = pl.BlockSpec((1, DIFF_HALF), lambda bi, h: (0, 0))
    head = lambda col: pl.BlockSpec((1, s, LANES), lambda bi, h: (bi, 0, col + h))
    first_far = int(np.argmax(_t5_bucket_table(2 * t) == N_BUCKETS - 1))
    near_band = -(-first_far // LANES) * LANES
    ones_rows = 16
    return pl.pallas_call(
        functools.partial(_diffattn_body, t=t, lam_init=lam_init, near_band=near_band),
        grid=(b, N_DIFF_HEADS),
        in_specs=[
            head(COL_DQ), head(COL_DK), head(COL_DV),
            pl.BlockSpec((1, 2, t, t), lambda bi, h: (h, 0, 0, 0)),
            vec, vec, vec, vec,
            pl.BlockSpec((1, LANES), lambda bi, h: (0, 0)),
        ],
        out_specs=pl.BlockSpec((1, s, LANES), lambda bi, h: (bi, 0, h)),
        out_shape=jax.ShapeDtypeStruct((b, s, N_DIFF_HEADS * LANES), BF16),
        scratch_shapes=[pltpu.VMEM((LANES + ones_rows, s), BF16)],
        compiler_params=_params(("arbitrary", "arbitrary")),
        name="diff_attention",
    )(proj3, proj3, proj3, bias, lq1, lk1, lq2, lk2, subln_g)


def _retention_tables(s):
    c = RET_CHUNK
    half = RET_QK // 2
    inv_freq = ROPE_BASE ** (-jnp.arange(0, RET_QK, 2, dtype=F32) / RET_QK)
    ang = jnp.arange(s, dtype=F32)[:, None] * inv_freq[None, :]
    cos, sin = jnp.cos(ang), jnp.sin(ang)
    cos_t = jnp.tile(cos, (1, LANES // half))
    sin_t = jnp.tile(jnp.concatenate([-sin, sin], axis=1), (1, LANES // RET_QK))
    log_g = jnp.log(1.0 - jnp.exp2(-5.0 - jnp.arange(N_RET_HEADS, dtype=F32)))
    n = jnp.arange(c, dtype=F32)
    diff = n[:, None] - n[None, :]
    inner = jnp.where(diff[None] >= 0, jnp.exp(jnp.maximum(diff, 0.0)[None] * log_g[:, None, None]), 0.0)
    cross = jnp.exp((n[None] + 1.0) * log_g[:, None])
    key = jnp.exp((c - 1.0 - n[None]) * log_g[:, None])
    chunk = jnp.exp(c * log_g)
    bc = lambda a: jnp.broadcast_to(a[..., None], a.shape + (LANES,)).astype(F32)
    return cos_t, sin_t, inner.astype(F32), bc(cross), bc(key), bc(chunk[:, None])


def _retention_body(rq_ref, rk_ref, rv_ref, rg_ref, cos_ref, sin_ref, inner_ref, cross_ref, key_ref,
                    chunk_ref, gn_ref, o_ref, state_ref):
    c = RET_CHUNK
    nc = rq_ref.shape[1] // c
    nh = N_RET_HEADS
    state_ref[...] = jnp.zeros(state_ref.shape, F32)
    lane = lax.broadcasted_iota(jnp.int32, (c, LANES), 1)
    first_half = (lane % RET_QK) < (RET_QK // 2)
    head_mask = [lane < RET_QK, lane >= RET_QK]
    slab = lambda i: slice(i * LANES, (i + 1) * LANES)

    def rotary(x, cos, sin):
        rot = jnp.where(first_half, pltpu.roll(x, LANES - RET_QK // 2, 1), pltpu.roll(x, RET_QK // 2, 1))
        return x * cos + rot * sin

    def chunk_step(ci, carry):
        r = pl.multiple_of(ci * c, c)
        rows = pl.ds(r, c)
        cos = cos_ref[rows, :]
        sin = sin_ref[rows, :]
        q_pair, k_pair, kb_pair = [], [], []
        for pr in range(nh // 2):
            q_pair.append(rotary(rq_ref[0, rows, slab(pr)].astype(F32), cos, sin))
            k = rotary(rk_ref[0, rows, slab(pr)].astype(F32), cos, sin) * (RET_QK ** -0.5)
            k_pair.append(k)
            kb_pair.append(k.astype(BF16))
        qh = [jnp.where(head_mask[h % 2], q_pair[h // 2], 0.0).astype(BF16) for h in range(nh)]
        kd = [(k_pair[h // 2] * key_ref[h]).astype(BF16) for h in range(nh)]
        vh = [rv_ref[0, rows, slab(h)] for h in range(nh)]
        scores = [lax.dot_general(qh[h], kb_pair[h // 2], _NT, preferred_element_type=F32) for h in range(nh)]
        cross = [jnp.dot(qh[h], state_ref[h].astype(BF16), preferred_element_type=F32) for h in range(nh)]
        kv = [lax.dot_general(kd[h], vh[h], _TN, preferred_element_type=F32) for h in range(nh)]
        decayed = [(scores[h] * inner_ref[h]).astype(BF16) for h in range(nh)]
        for h in range(nh):
            state_ref[h] = state_ref[h] * chunk_ref[h] + kv[h]
        inner = [jnp.dot(decayed[h], vh[h], preferred_element_type=F32) for h in range(nh)]
        for h in range(nh):
            y = inner[h] + cross[h] * cross_ref[h]
            mu = jnp.mean(y, axis=-1, keepdims=True)
            yc = y - mu
            var = jnp.mean(yc * yc, axis=-1, keepdims=True)
            yn = (yc * lax.rsqrt(var + EPS)) * gn_ref[:, slab(h)]
            gate = rg_ref[0, rows, slab(h)].astype(F32)
            silu = gate * (1.0 / (1.0 + jnp.exp(-gate)))
            o_ref[0, rows, slab(h)] = (silu * yn).astype(o_ref.dtype)
        return carry

    lax.fori_loop(0, nc, chunk_step, 0)


def _retention(proj3, tables, gn_g):
    b, s, _ = proj3.shape
    cos_t, sin_t, inner, cross, key, chunk = tables
    c = RET_CHUNK
    nh = N_RET_HEADS
    qk_w = nh * RET_QK
    v_w = nh * LANES
    full = lambda shape: pl.BlockSpec(shape, lambda bi: (0,) * len(shape))
    return pl.pallas_call(
        _retention_body,
        grid=(b,),
        in_specs=[
            pl.BlockSpec((1, s, qk_w), lambda bi: (bi, 0, COL_RQ * LANES // qk_w)),
            pl.BlockSpec((1, s, qk_w), lambda bi: (bi, 0, COL_RK * LANES // qk_w)),
            pl.BlockSpec((1, s, v_w), lambda bi: (bi, 0, COL_RV * LANES // v_w)),
            pl.BlockSpec((1, s, v_w), lambda bi: (bi, 0, COL_RG * LANES // v_w)),
            full((s, LANES)), full((s, LANES)),
            full((nh, c, c)), full((nh, c, LANES)), full((nh, c, LANES)), full((nh, 1, LANES)),
            full((1, v_w)),
        ],
        out_specs=pl.BlockSpec((1, s, v_w), lambda bi: (bi, 0, 0)),
        out_shape=jax.ShapeDtypeStruct((b, s, v_w), BF16),
        scratch_shapes=[pltpu.VMEM((nh, LANES, LANES), F32)],
        compiler_params=_params(("arbitrary",)),
        name="retention",
    )(proj3, proj3, proj3, proj3, cos_t, sin_t, inner, cross, key, chunk, gn_g)


def _route(logits):
    lane = lax.broadcasted_iota(jnp.int32, logits.shape, 1).astype(F32)
    big = float(LANES)
    is_group = lane < N_GROUPS
    gl = jnp.where(is_group, logits, -jnp.inf)
    gmax = jnp.max(gl, axis=-1, keepdims=True)
    gidx = jnp.min(jnp.where(gl == gmax, lane, big), axis=-1, keepdims=True)
    p_group = 1.0 / jnp.sum(jnp.where(is_group, jnp.exp(gl - gmax), 0.0), axis=-1, keepdims=True)
    lo = N_GROUPS + EXPERTS_PER_GROUP * gidx
    il = jnp.where(lane >= lo, jnp.where(lane < lo + EXPERTS_PER_GROUP, logits, -jnp.inf), -jnp.inf)
    v1 = jnp.max(il, axis=-1, keepdims=True)
    i1 = jnp.min(jnp.where(il == v1, lane, big), axis=-1, keepdims=True)
    il2 = jnp.where(lane == i1, -jnp.inf, il)
    v2 = jnp.max(il2, axis=-1, keepdims=True)
    i2 = jnp.min(jnp.where(il2 == v2, lane, big), axis=-1, keepdims=True)
    e2 = jnp.exp(v2 - v1)
    inv = 1.0 / (1.0 + e2)
    g1 = p_group * inv
    g2 = p_group * (e2 * inv)
    out = jnp.where(lane == 0.0, i1 - N_GROUPS,
                    jnp.where(lane == 1.0, i2 - N_GROUPS,
                              jnp.where(lane == 2.0, g1, jnp.where(lane == 3.0, g2, 0.0))))
    return out


def _outproj_body(a_ref, r_ref, x_ref, w_ref, g_ref, wr_ref, br_ref, x1_ref, h2_ref, route_ref, eid_ref, *, sub):
    tm = x_ref.shape[0]
    half = a_ref.shape[1]
    n_sub = tm // sub

    def project(i):
        rows = slice(i * sub, (i + 1) * sub)
        acc = jnp.dot(a_ref[rows, :], w_ref[:half, :], preferred_element_type=F32)
        return acc + jnp.dot(r_ref[rows, :], w_ref[half:, :], preferred_element_type=F32)

    def epilogue(i, acc):
        rows = slice(i * sub, (i + 1) * sub)
        x1 = x_ref[rows, :] + acc
        x1_ref[rows, :] = x1
        ms = jnp.mean(x1 * x1, axis=-1, keepdims=True)
        h2 = (x1 * lax.rsqrt(ms + EPS)) * g_ref[...]
        h2_ref[rows, :] = _pack_halves(h2)
        logits = jnp.dot(h2.astype(BF16), wr_ref[...], preferred_element_type=F32) + br_ref[...]
        route = _route(logits)
        route_ref[rows, :] = route
        eid_ref[:, rows] = route.T[:TOP_K, :].astype(jnp.int32)

    acc = project(0)
    for i in range(n_sub):
        nxt = project(i + 1) if i + 1 < n_sub else None
        epilogue(i, acc)
        acc = nxt


def _outproj_router(a2d, r2d, x2d, w_out_bf16, g, wr_bf16, br, *, tm):
    n, d = x2d.shape
    half = a2d.shape[1]
    row = lambda width: pl.BlockSpec((tm, width), lambda i: (i, 0))
    full = lambda shape: pl.BlockSpec(shape, lambda i: (0, 0))
    return pl.pallas_call(
        functools.partial(_outproj_body, sub=min(tm, 2 * LANES)),
        grid=(n // tm,),
        in_specs=[row(half), row(half), row(d), full((2 * half, d)), full((1, d)), full((d, LANES)),
                  full((1, LANES))],
        out_specs=[row(d), row(d // 2), row(LANES), pl.BlockSpec((TOP_K, tm), lambda i: (0, i))],
        out_shape=[jax.ShapeDtypeStruct((n, d), F32), jax.ShapeDtypeStruct((n, d // 2), jnp.uint32),
                   jax.ShapeDtypeStruct((n, LANES), F32), jax.ShapeDtypeStruct((TOP_K, n), jnp.int32)],
        compiler_params=_params(("arbitrary",)),
        name="outproj_router",
    )(a2d, r2d, x2d, w_out_bf16, g, wr_bf16, br)


def _moe_plan(expert_id, tm):
    n_slots = expert_id.shape[0]
    n_tiles = n_slots // tm
    order = jnp.argsort(expert_id).astype(jnp.int32)
    sizes = jnp.zeros((N_EXPERTS,), jnp.int32).at[expert_id].add(1)
    ends = jnp.cumsum(sizes)
    starts = ends - sizes
    tile_row0 = jnp.arange(n_tiles, dtype=jnp.int32) * tm
    owner = lambda row: jnp.sum((ends[None, :] <= row[:, None]).astype(jnp.int32), axis=1)
    first_e = owner(tile_row0)
    last_e = owner(tile_row0 + (tm - 1))
    owners_upto = jnp.cumsum((sizes > 0).astype(jnp.int32))
    n_extra = (owners_upto[last_e] - owners_upto[first_e]).astype(jnp.int32)
    prev_last = jnp.concatenate([jnp.full((1,), -1, jnp.int32), last_e[:-1]])
    switch = (prev_last != first_e).astype(jnp.int32)
    eid = jnp.arange(N_EXPERTS, dtype=jnp.int32)
    cand = jnp.where((sizes[None, :] > 0) & (eid[None, :] > eid[:, None]), eid[None, :], N_EXPERTS)
    nxt = jnp.min(cand, axis=1)
    next_expert = jnp.where(nxt < N_EXPERTS, nxt, -1).astype(jnp.int32)
    spare = n_slots + jnp.arange(tm, dtype=jnp.int32)
    codes = jnp.concatenate([order.reshape(n_tiles, tm), spare[None, :]], axis=0).reshape(n_tiles + 1, 1, tm)
    return dict(first_e=first_e, n_extra=n_extra, switch=switch, starts=starts.astype(jnp.int32),
                ends=ends.astype(jnp.int32), next_expert=next_expert, codes=codes)


def _moe_body(fe_ref, nextra_ref, sw_ref, st_ref, en_ref, nx_ref,
              codes_hbm, h2_hbm, wg_hbm, wu_hbm, wd_hbm, ys_hbm,
              gidx, sidx, xbuf, obuf, wg_st, wu_st, wd_st, wg_rs, wu_rs, wd_rs,
              gisem, sisem, gsem, ssem, wsem, *, n_tokens, n_tiles, cast_rows, ff_chunks):
    t = pl.program_id(0)
    tm = xbuf.shape[1]
    last = n_tiles - 1
    spare_codes = n_tiles

    def gidx_copy(tile, slot):
        return pltpu.make_async_copy(codes_hbm.at[tile], gidx.at[slot], gisem.at[slot])

    def sidx_copy(tile, slot):
        return pltpu.make_async_copy(codes_hbm.at[tile], sidx.at[slot], sisem.at[slot])

    def weight_copies(ex):
        return (pltpu.make_async_copy(wg_hbm.at[ex], wg_st, wsem.at[0]),
                pltpu.make_async_copy(wu_hbm.at[ex], wu_st, wsem.at[1]),
                pltpu.make_async_copy(wd_hbm.at[ex], wd_st, wsem.at[2]))

    def gather_row(slot, r):
        code = gidx[slot, 0, r]
        tok = code - jnp.where(code >= n_tokens, n_tokens, 0)
        return pltpu.make_async_copy(h2_hbm.at[pl.ds(tok, 1)], xbuf.at[slot, pl.ds(r, 1)], gsem.at[slot])

    def scatter_row(slot, r):
        dst = sidx[slot, 0, r]
        return pltpu.make_async_copy(obuf.at[slot, pl.ds(r, 1)], ys_hbm.at[pl.ds(dst, 1)], ssem.at[slot])

    def wait_gather(slot):
        pltpu.make_async_copy(h2_hbm.at[pl.ds(0, tm)], xbuf.at[slot], gsem.at[slot]).wait()

    def wait_scatter(slot):
        pltpu.make_async_copy(obuf.at[slot], ys_hbm.at[pl.ds(0, tm)], ssem.at[slot]).wait()

    def for_rows(fn):
        def body(r, carry):
            fn(r)
            return carry
        lax.fori_loop(0, tm, body, 0, unroll=8)

    def take_over(ex, first):
        @pl.when(first)
        def _():
            for cp in weight_copies(ex):
                cp.start()

        for cp in weight_copies(ex):
            cp.wait()

        def cast_up(i, carry):
            r = pl.multiple_of(i * cast_rows, cast_rows)
            wg_rs[pl.ds(r, cast_rows), :] = wg_st[pl.ds(r, cast_rows), :].astype(BF16)
            wu_rs[pl.ds(r, cast_rows), :] = wu_st[pl.ds(r, cast_rows), :].astype(BF16)
            return carry

        def cast_down(i, carry):
            r = pl.multiple_of(i * cast_rows, cast_rows)
            wd_rs[pl.ds(r, cast_rows), :] = wd_st[pl.ds(r, cast_rows), :].astype(BF16)
            return carry

        lax.fori_loop(0, wg_st.shape[0] // cast_rows, cast_up, 0)
        lax.fori_loop(0, wd_st.shape[0] // cast_rows, cast_down, 0)
        nxt = nx_ref[ex]

        @pl.when(nxt >= 0)
        def _():
            for cp in weight_copies(nxt):
                cp.start()

    def ffn(slot, ex, accumulate, row_dmas=None, cast_weights=False):
        x_lo, x_hi = (h.astype(BF16) for h in _unpack_halves(xbuf[slot]))
        hd = x_lo.shape[1]
        row = t * tm + lax.broadcasted_iota(jnp.int32, (tm, 1), 0)
        mine = jnp.logical_and(row >= st_ref[ex], row < en_ref[ex])
        d_ff = wg_rs.shape[1]
        n_chunks = ff_chunks if (row_dmas is not None or cast_weights) else 1
        fc = d_ff // n_chunks

        def cast_chunk(c):
            cols = slice(c * fc, (c + 1) * fc)
            wg_rs[:, cols] = wg_st[:, cols].astype(BF16)
            wu_rs[:, cols] = wu_st[:, cols].astype(BF16)
            wd_rs[cols, :] = wd_st[cols, :].astype(BF16)

        if cast_weights:
            cast_chunk(0)
        contrib = None
        for c in range(n_chunks):
            cols = slice(c * fc, (c + 1) * fc)
            up = lambda w: (jnp.dot(x_lo, w[:hd, cols], preferred_element_type=F32)
                            + jnp.dot(x_hi, w[hd:, cols], preferred_element_type=F32))
            a = up(wg_rs)
            b = up(wu_rs)
            hmid = jnp.where(mine, (a * (1.0 / (1.0 + jnp.exp(-a)))) * b, 0.0).astype(BF16)
            part = jnp.dot(hmid, wd_rs[cols, :], preferred_element_type=F32)
            contrib = part if contrib is None else contrib + part
            if cast_weights and c + 1 < n_chunks:
                cast_chunk(c + 1)
            if row_dmas is not None:
                row_dmas(c, n_chunks)
        if accumulate:
            prev_lo, prev_hi = _unpack_halves(obuf[slot])
            contrib = jnp.concatenate([prev_lo, prev_hi], axis=1) + contrib
        obuf[slot] = _pack_halves(contrib)

    e0 = fe_ref[t]

    def step(p):
        q = 1 - p
        if p == 0:
            @pl.when(t == 0)
            def _():
                copies = (gidx_copy(0, 0), gidx_copy(min(1, last), 1), sidx_copy(spare_codes, 1))
                for cp in copies:
                    cp.start()
                for cp in copies:
                    cp.wait()
                obuf[1] = jnp.zeros(obuf.shape[1:], obuf.dtype)
                for_rows(lambda r: gather_row(0, r).start())

        gidx_copy(jnp.minimum(t + 2, last), p).start()
        sidx_copy(t, p).start()

        @pl.when(t >= 1)
        def _():
            gidx_copy(0, q).wait()
            sidx_copy(0, q).wait()
            wait_scatter(p)

        wait_gather(p)

        @pl.when(sw_ref[t] == 1)
        def _():
            take_over(e0, t == 0)

        def row_dmas(c, n_chunks):
            early = max(n_chunks - 1, 1)
            for r in range(tm):
                if r * early // tm == c:
                    scatter_row(q, r).start(priority=r % 2)
                    gather_row(q, r).start(priority=r % 2)

        ffn(p, e0, accumulate=False, row_dmas=row_dmas)

    for parity in range(2):
        pl.when(lax.rem(t, 2) == parity)(functools.partial(step, parity))

    slot = lax.rem(t, 2)

    def further_owner(_, ex):
        nxt = nx_ref[ex]
        for cp in weight_copies(nxt):
            cp.wait()
        ffn(slot, nxt, accumulate=True, cast_weights=True)
        after = nx_ref[nxt]

        @pl.when(after >= 0)
        def _():
            for cp in weight_copies(after):
                cp.start()

        return nxt

    lax.fori_loop(0, nextra_ref[t], further_owner, e0)

    @pl.when(t == last)
    def _():
        gidx_copy(0, slot).wait()
        sidx_copy(0, slot).wait()
        for_rows(lambda r: scatter_row(slot, r).start())
        wait_gather(1 - slot)
        wait_scatter(1 - slot)
        wait_scatter(slot)


def _moe_ffn(h2, plan, w_gate, w_up, w_down, *, tm):
    n, hd = h2.shape
    d = 2 * hd
    n_slots = 2 * n
    n_tiles = n_slots // tm
    assert n_tiles >= 2 and n_slots % tm == 0
    d_ff = w_gate.shape[2]
    any_spec = pl.BlockSpec(memory_space=pl.ANY)
    grid_spec = pltpu.PrefetchScalarGridSpec(
        num_scalar_prefetch=6,
        grid=(n_tiles,),
        in_specs=[any_spec] * 5,
        out_specs=any_spec,
        scratch_shapes=[
            pltpu.SMEM((2, 1, tm), jnp.int32),
            pltpu.SMEM((2, 1, tm), jnp.int32),
            pltpu.VMEM((2, tm, hd), jnp.uint32),
            pltpu.VMEM((2, tm, hd), jnp.uint32),
            pltpu.VMEM((d, d_ff), F32),
            pltpu.VMEM((d, d_ff), F32),
            pltpu.VMEM((d_ff, d), F32),
            pltpu.VMEM((d, d_ff), BF16),
            pltpu.VMEM((d, d_ff), BF16),
            pltpu.VMEM((d_ff, d), BF16),
            pltpu.SemaphoreType.DMA((2,)),
            pltpu.SemaphoreType.DMA((2,)),
            pltpu.SemaphoreType.DMA((2,)),
            pltpu.SemaphoreType.DMA((2,)),
            pltpu.SemaphoreType.DMA((3,)),
        ],
    )
    return pl.pallas_call(
        functools.partial(_moe_body, n_tokens=n, n_tiles=n_tiles, cast_rows=128,
                          ff_chunks=d_ff // (2 * LANES)),
        grid_spec=grid_spec,
        out_shape=jax.ShapeDtypeStruct((n_slots + tm, hd), jnp.uint32),
        compiler_params=_params(("arbitrary",), vmem=MOE_VMEM_LIMIT),
        name="moe_ffn",
    )(plan["first_e"], plan["n_extra"], plan["switch"], plan["starts"], plan["ends"], plan["next_expert"],
      plan["codes"], h2, w_gate, w_up, w_down)


def _final_body(x1_ref, ys0_ref, ys1_ref, route_ref, g_ref, o_ref):
    d = x1_ref.shape[1]
    route = route_ref[...]
    g1, g2 = route[:, 2:3], route[:, 3:4]
    ya = _unpack_halves(ys0_ref[...])
    yb = _unpack_halves(ys1_ref[...])
    halves = (slice(0, d // 2), slice(d // 2, d))
    x2 = [x1_ref[:, cols] + (ya[i] * g1 + yb[i] * g2) for i, cols in enumerate(halves)]
    ms = sum(jnp.sum(h * h, axis=-1, keepdims=True) for h in x2) * (1.0 / d)
    inv = lax.rsqrt(ms + EPS)
    for h, cols in zip(x2, halves):
        o_ref[:, cols] = ((h * inv) * g_ref[:, cols]).astype(o_ref.dtype)


def _final(x1, ys, route, g, *, tm):
    n, d = x1.shape
    nb = n // tm
    return pl.pallas_call(
        _final_body,
        grid=(nb,),
        in_specs=[
            pl.BlockSpec((tm, d), lambda i: (i, 0)),
            pl.BlockSpec((tm, d // 2), lambda i: (i, 0)),
            pl.BlockSpec((tm, d // 2), lambda i: (nb + i, 0)),
            pl.BlockSpec((tm, LANES), lambda i: (i, 0)),
            pl.BlockSpec((1, d), lambda i: (0, 0)),
        ],
        out_specs=pl.BlockSpec((tm, d), lambda i: (i, 0)),
        out_shape=jax.ShapeDtypeStruct((n, d), F32),
        compiler_params=_params(("arbitrary",)),
        name="final_norm",
    )(x1, ys, ys, route, g)


def _layer(x, rel_bias, attn_norm_g, w_in, lq1, lk1, lq2, lk2, diff_subln_g, ret_gn_g, w_out, ffn_norm_g,
           w_group_router, b_group, w_inner_router, b_inner, w_gate_exp, w_up_exp, w_down_exp, *, layer,
           attn_tile, row_tile, moe_tile):
    b, s, d = x.shape
    n = b * s
    x2d = x.reshape(n, d)
    lam_init = 0.8 - 0.6 * math.exp(-0.3 * layer)

    dq_cols = (COL_DK - COL_DQ) * LANES
    col_scale = jnp.concatenate([jnp.full((1, dq_cols), DIFF_HALF ** -0.5 * LOG2E, F32),
                                 jnp.ones((1, IN_COLS - dq_cols), F32)], axis=1)
    proj = _inproj(x2d, attn_norm_g.reshape(1, d), w_in.astype(BF16), col_scale, tm=min(row_tile, n))
    proj3 = proj.reshape(b, s, IN_COLS)

    bias = _bias_tiles(rel_bias, attn_tile)
    vec = lambda a: a.reshape(1, -1)
    a_out = _diff_attention(proj3, bias, vec(lq1), vec(lk1), vec(lq2), vec(lk2), vec(diff_subln_g),
                            t=attn_tile, lam_init=lam_init)
    r_out = _retention(proj3, _retention_tables(s), vec(ret_gn_g))

    pad = LANES - N_GROUPS - N_EXPERTS
    wr = jnp.concatenate([w_group_router, jnp.transpose(w_inner_router, (1, 0, 2)).reshape(d, N_EXPERTS),
                          jnp.zeros((d, pad), F32)], axis=1).astype(BF16)
    br = jnp.concatenate([b_group, b_inner.reshape(-1), jnp.zeros((pad,), F32)]).reshape(1, LANES)
    x1, h2, route, eid = _outproj_router(a_out.reshape(n, -1), r_out.reshape(n, -1), x2d, w_out.astype(BF16),
                                         vec(ffn_norm_g), wr, br, tm=row_tile)

    plan = _moe_plan(eid.reshape(-1), moe_tile)
    ys = _moe_ffn(h2, plan, w_gate_exp, w_up_exp, w_down_exp, tm=moe_tile)
    return x1, ys, route


def kernel(x, rel_bias, attn_norm_g, w_in, lambda_q1, lambda_k1, lambda_q2, lambda_k2, diff_subln_g, ret_gn_g,
           w_out, ffn_norm_g, w_group_router, b_group, w_inner_router, b_inner, w_gate_exp, w_up_exp, w_down_exp,
           final_g):
    b, s, d = x.shape
    depth = w_in.shape[0]
    assert depth == 1, "the final norm is fused into the single layer's combine step"
    first = lambda a: a.reshape(a.shape[1:])
    x1, ys, route = _layer(x, rel_bias, attn_norm_g[0], w_in[0], lambda_q1[0], lambda_k1[0], lambda_q2[0],
                           lambda_k2[0], diff_subln_g[0], ret_gn_g[0], w_out[0], ffn_norm_g[0],
                           w_group_router[0], b_group[0], w_inner_router[0], b_inner[0], first(w_gate_exp),
                           first(w_up_exp), first(w_down_exp), layer=0, attn_tile=min(256, s), row_tile=512,
                           moe_tile=256)
    out = _final(x1, ys, route, final_g.reshape(1, d), tm=512)
    return out.reshape(b, s, d)
```

```python
import functools
import math

import numpy as np
import jax
import jax.numpy as jnp
from jax import lax
from jax.experimental import pallas as pl
from jax.experimental.pallas import tpu as pltpu

F32 = jnp.float32
BF16 = jnp.bfloat16

EPS = 1e-6
LANES = 128
N_DIFF_HEADS = 8
DIFF_HALF = 64
N_RET_HEADS = 8
RET_QK = 64
RET_CHUNK = 128
ROPE_BASE = 10000.0
N_BUCKETS = 32
MAX_DISTANCE = 128
N_GROUPS = 4
EXPERTS_PER_GROUP = 8
N_EXPERTS = N_GROUPS * EXPERTS_PER_GROUP
TOP_K = 2
MASK_VALUE = -1e30
LOG2E = math.log2(math.e)
VMEM_LIMIT = 56 * 1024 * 1024
MOE_VMEM_LIMIT = 60 * 1024 * 1024

COL_DQ, COL_DK, COL_DV = 0, 8, 16
COL_RQ, COL_RK, COL_RV, COL_RG = 24, 28, 32, 40
IN_COLS = 48 * LANES


def _params(sem, vmem=VMEM_LIMIT):
    return pltpu.CompilerParams(dimension_semantics=sem, vmem_limit_bytes=vmem)


def _pack_halves(x):
    hd = x.shape[1] // 2
    bits = lambda h: lax.bitcast_convert_type(h.astype(BF16).astype(F32), jnp.uint32)
    return lax.shift_right_logical(bits(x[:, :hd]), jnp.uint32(16)) | bits(x[:, hd:])


def _unpack_halves(w):
    return (lax.bitcast_convert_type(lax.shift_left(w, jnp.uint32(16)), F32),
            lax.bitcast_convert_type(w & jnp.uint32(0xFFFF0000), F32))


def _inproj_body(x_ref, g_ref, w_ref, cs_ref, o_ref):
    x = x_ref[...]
    acc = jnp.dot((x * g_ref[...]).astype(BF16), w_ref[...], preferred_element_type=F32)
    inv = lax.rsqrt(jnp.mean(x * x, axis=-1, keepdims=True) + EPS)
    o_ref[...] = ((acc * inv) * cs_ref[...]).astype(o_ref.dtype)


def _inproj(x2d, g, w, col_scale, *, tm):
    n, d = x2d.shape
    cols = w.shape[1]
    once = dict(pipeline_mode=pl.Buffered(1))
    return pl.pallas_call(
        _inproj_body,
        grid=(n // tm,),
        in_specs=[
            pl.BlockSpec((tm, d), lambda i: (i, 0)),
            pl.BlockSpec((1, d), lambda i: (0, 0)),
            pl.BlockSpec((d, cols), lambda i: (0, 0), **once),
            pl.BlockSpec((1, cols), lambda i: (0, 0)),
        ],
        out_specs=pl.BlockSpec((tm, cols), lambda i: (i, 0)),
        out_shape=jax.ShapeDtypeStruct((n, cols), BF16),
        compiler_params=_params(("arbitrary",)),
        name="inproj",
    )(x2d, g, w, col_scale)


def _t5_bucket_table(n):
    d = np.arange(n)
    max_exact = N_BUCKETS // 2
    log_ratio = np.log(np.maximum(d, 1).astype(np.float64) / max_exact) / math.log(MAX_DISTANCE / max_exact)
    large = np.minimum(max_exact + (log_ratio * (N_BUCKETS - max_exact)).astype(np.int64), N_BUCKETS - 1)
    return np.where(d < max_exact, d, large).astype(np.int32)


def _bias_body(rb_ref, bkt_ref, o_ref):
    h = pl.program_id(0)
    t = o_ref.shape[2]
    bkt = bkt_ref[...]
    far = rb_ref[N_BUCKETS - 1, h]
    by_dist = jnp.zeros(bkt.shape, F32)
    for b in range(N_BUCKETS - 1):
        by_dist = jnp.where(bkt == b, (rb_ref[b, h] - far) * LOG2E, by_dist)
    rows = (jnp.concatenate([jnp.full((1, t), MASK_VALUE, F32), by_dist[:, :t]], axis=1), by_dist)
    for i, row in enumerate(rows):
        shifted = pltpu.roll(jnp.broadcast_to(row, (t, 2 * t)), 0, 1, stride=1, stride_axis=0)
        o_ref[0, i] = shifted[:, t:]


def _bias_tiles(rel_bias, t):
    bkt = jnp.asarray(_t5_bucket_table(2 * t)).reshape(1, 2 * t)
    return pl.pallas_call(
        _bias_body,
        grid=(N_DIFF_HEADS,),
        in_specs=[
            pl.BlockSpec(memory_space=pltpu.SMEM),
            pl.BlockSpec((1, 2 * t), lambda h: (0, 0)),
        ],
        out_specs=pl.BlockSpec((1, 2, t, t), lambda h: (h, 0, 0, 0)),
        out_shape=jax.ShapeDtypeStruct((N_DIFF_HEADS, 2, t, t), F32),
        compiler_params=_params(("arbitrary",)),
        name="bias_tiles",
    )(rel_bias, bkt)


_NT = (((1,), (1,)), ((), ()))
_TN = (((0,), (0,)), ((), ()))


def _diffattn_body(q_ref, k_ref, v_ref, bias_ref, lq1_ref, lk1_ref, lq2_ref, lk2_ref, g_ref, o_ref, vt_ref, *,
                   t, lam_init, near_band):
    nq = k_ref.shape[1] // t
    half = t // 2
    lane = lax.broadcasted_iota(jnp.int32, (t, LANES), 1)
    lam = (jnp.exp(jnp.sum(lq1_ref[...] * lk1_ref[...], axis=-1, keepdims=True))
           - jnp.exp(jnp.sum(lq2_ref[...] * lk2_ref[...], axis=-1, keepdims=True)) + lam_init)
    vt_ref[:LANES, :] = v_ref[0].T
    vt_ref[LANES:, :] = jnp.ones((vt_ref.shape[0] - LANES, vt_ref.shape[1]), BF16)
    qmask = {}
    run_max = {}
    acc = {}

    def scores(item):
        qi, j, part = item
        if qi not in qmask:
            q = q_ref[0, qi * t:(qi + 1) * t, :]
            zero = jnp.zeros_like(q)
            qmask[qi] = (jnp.where(lane < DIFF_HALF, q, zero), jnp.where(lane >= DIFF_HALF, q, zero))
        k_lo = j * t + (half if part == "diag_lower" else 0)
        k_hi = j * t + (half if part == "diag_upper" else t)
        kb = k_ref[0, k_lo:k_hi, :]
        out = []
        for c in range(2):
            qsel = qmask[qi][c][half:] if part == "diag_lower" else qmask[qi][c]
            s = lax.dot_general(kb, qsel, _NT, preferred_element_type=F32)
            if part == "diag_upper":
                s = s + bias_ref[0, 0, :half, :]
            elif part == "diag_lower":
                s = s + bias_ref[0, 0, half:, half:]
            elif j == qi - 1:
                if near_band >= t:
                    s = s + bias_ref[0, 1]
                else:
                    corner = s[t - near_band:, :near_band] + bias_ref[0, 1, t - near_band:, :near_band]
                    bottom = jnp.concatenate([corner, s[t - near_band:, near_band:]], axis=1)
                    s = jnp.concatenate([s[:t - near_band], bottom], axis=0)
            out.append(s)
        return out

    def softmax_step(item, ss):
        qi, j, part = item
        first = j == 0 and part != "diag_lower"
        ps, alphas = [], []
        for c in range(2):
            s = ss[c]
            smax = jnp.max(s, axis=0, keepdims=True)
            if first:
                m_new, alpha = smax, None
                run_max[qi, c] = m_new
            else:
                m_full = run_max[qi, c]
                m_old = m_full[:, half:] if part == "diag_lower" else m_full
                m_new = jnp.maximum(m_old, smax)
                alpha = jnp.exp2(m_old - m_new)
                run_max[qi, c] = (jnp.concatenate([m_full[:, :half], m_new], axis=1)
                                  if part == "diag_lower" else m_new)
            ps.append(jnp.exp2(s - m_new).astype(BF16))
            alphas.append(alpha)
        return ps, alphas

    def values_step(item, ps, alphas):
        qi, j, part = item
        k_lo = j * t + (half if part == "diag_lower" else 0)
        k_hi = j * t + (half if part == "diag_upper" else t)
        vtb = vt_ref[:, k_lo:k_hi]
        for c in range(2):
            alpha = alphas[c]
            r = jnp.dot(vtb, ps[c], preferred_element_type=F32)
            if alpha is None:
                acc[qi, c] = r
            elif part == "diag_lower":
                acc[qi, c] = jnp.concatenate([acc[qi, c][:, :half], alpha * acc[qi, c][:, half:] + r], axis=1)
            else:
                acc[qi, c] = alpha * acc[qi, c] + r
        if part == "diag_lower":
            outs = []
            for c in range(2):
                a = acc.pop((qi, c))
                del run_max[qi, c]
                outs.append(a[:LANES] * (1.0 / a[LANES:LANES + 1]))
            ot = (outs[0] - lam * outs[1]).T
            ms = jnp.mean(ot * ot, axis=-1, keepdims=True)
            o_ref[0, qi * t:(qi + 1) * t, :] = (
                ((ot * lax.rsqrt(ms + EPS)) * g_ref[...]) * (1.0 - lam_init)).astype(o_ref.dtype)
            del qmask[qi]

    items = []
    for qi in range(nq):
        items += [(qi, j, "full") for j in range(qi)] + [(qi, qi, "diag_upper"), (qi, qi, "diag_lower")]
    ss = scores(items[0])
    pending = None
    for n, item in enumerate(items):
        nxt = scores(items[n + 1]) if n + 1 < len(items) else None
        ps, alphas = softmax_step(item, ss)
        if pending is not None:
            values_step(*pending)
        pending = (item, ps, alphas)
        ss = nxt
    values_step(*pending)


def _diff_attention(proj3, bias, lq1, lk1, lq2, lk2, subln_g, *, t, lam_init):
    b, s, _ = proj3.shape
    vec = pl.BlockSpec((1, DIFF_HALF), lambda bi, h: (0, 0))
    head = lambda col: pl.BlockSpec((1, s, LANES), lambda bi, h: (bi, 0, col + h))
    first_far = int(np.argmax(_t5_bucket_table(2 * t) == N_BUCKETS - 1))
    near_band = -(-first_far // LANES) * LANES
    ones_rows = 16
    return pl.pallas_call(
        functools.partial(_diffattn_body, t=t, lam_init=lam_init, near_band=near_band),
        grid=(b, N_DIFF_HEADS),
        in_specs=[
            head(COL_DQ), head(COL_DK), head(COL_DV),
            pl.BlockSpec((1, 2, t, t), lambda bi, h: (h, 0, 0, 0)),
            vec, vec, vec, vec,
            pl.BlockSpec((1, LANES), lambda bi, h: (0, 0)),
        ],
        out_specs=pl.BlockSpec((1, s, LANES), lambda bi, h: (bi, 0, h)),
        out_shape=jax.ShapeDtypeStruct((b, s, N_DIFF_HEADS * LANES), BF16),
        scratch_shapes=[pltpu.VMEM((LANES + ones_rows, s), BF16)],
        compiler_params=_params(("arbitrary", "arbitrary")),
        name="diff_attention",
    )(proj3, proj3, proj3, bias, lq1, lk1, lq2, lk2, subln_g)


def _retention_tables(s):
    c = RET_CHUNK
    half = RET_QK // 2
    inv_freq = ROPE_BASE ** (-jnp.arange(0, RET_QK, 2, dtype=F32) / RET_QK)
    ang = jnp.arange(s, dtype=F32)[:, None] * inv_freq[None, :]
    cos, sin = jnp.cos(ang), jnp.sin(ang)
    cos_t = jnp.tile(cos, (1, LANES // half))
    sin_t = jnp.tile(jnp.concatenate([-sin, sin], axis=1), (1, LANES // RET_QK))
    log_g = jnp.log(1.0 - jnp.exp2(-5.0 - jnp.arange(N_RET_HEADS, dtype=F32)))
    n = jnp.arange(c, dtype=F32)
    diff = n[:, None] - n[None, :]
    inner = jnp.where(diff[None] >= 0, jnp.exp(jnp.maximum(diff, 0.0)[None] * log_g[:, None, None]), 0.0)
    cross = jnp.exp((n[None] + 1.0) * log_g[:, None])
    key = jnp.exp((c - 1.0 - n[None]) * log_g[:, None])
    chunk = jnp.exp(c * log_g)
    bc = lambda a: jnp.broadcast_to(a[..., None], a.shape + (LANES,)).astype(F32)
    return cos_t, sin_t, inner.astype(F32), bc(cross), bc(key), bc(chunk[:, None])


def _retention_body(rq_ref, rk_ref, rv_ref, rg_ref, cos_ref, sin_ref, inner_ref, cross_ref, key_ref,
                    chunk_ref, gn_ref, o_ref, state_ref):
    c = RET_CHUNK
    nc = rq_ref.shape[1] // c
    nh = N_RET_HEADS
    state_ref[...] = jnp.zeros(state_ref.shape, F32)
    lane = lax.broadcasted_iota(jnp.int32, (c, LANES), 1)
    first_half = (lane % RET_QK) < (RET_QK // 2)
    head_mask = [lane < RET_QK, lane >= RET_QK]
    slab = lambda i: slice(i * LANES, (i + 1) * LANES)

    def rotary(x, cos, sin):
        rot = jnp.where(first_half, pltpu.roll(x, LANES - RET_QK // 2, 1), pltpu.roll(x, RET_QK // 2, 1))
        return x * cos + rot * sin

    def chunk_step(ci, carry):
        r = pl.multiple_of(ci * c, c)
        rows = pl.ds(r, c)
        cos = cos_ref[rows, :]
        sin = sin_ref[rows, :]
        q_pair, k_pair, kb_pair = [], [], []
        for pr in range(nh // 2):
            q_pair.append(rotary(rq_ref[0, rows, slab(pr)].astype(F32), cos, sin))
            k = rotary(rk_ref[0, rows, slab(pr)].astype(F32), cos, sin) * (RET_QK ** -0.5)
            k_pair.append(k)
            kb_pair.append(k.astype(BF16))
        qh = [jnp.where(head_mask[h % 2], q_pair[h // 2], 0.0).astype(BF16) for h in range(nh)]
        kd = [(k_pair[h // 2] * key_ref[h]).astype(BF16) for h in range(nh)]
        vh = [rv_ref[0, rows, slab(h)] for h in range(nh)]
        scores = [lax.dot_general(qh[h], kb_pair[h // 2], _NT, preferred_element_type=F32) for h in range(nh)]
        cross = [jnp.dot(qh[h], state_ref[h].astype(BF16), preferred_element_type=F32) for h in range(nh)]
        kv = [lax.dot_general(kd[h], vh[h], _TN, preferred_element_type=F32) for h in range(nh)]
        decayed = [(scores[h] * inner_ref[h]).astype(BF16) for h in range(nh)]
        for h in range(nh):
            state_ref[h] = state_ref[h] * chunk_ref[h] + kv[h]
        inner = [jnp.dot(decayed[h], vh[h], preferred_element_type=F32) for h in range(nh)]
        for h in range(nh):
            y = inner[h] + cross[h] * cross_ref[h]
            mu = jnp.mean(y, axis=-1, keepdims=True)
            yc = y - mu
            var = jnp.mean(yc * yc, axis=-1, keepdims=True)
            yn = (yc * lax.rsqrt(var + EPS)) * gn_ref[:, slab(h)]
            gate = rg_ref[0, rows, slab(h)].astype(F32)
            silu = gate * (1.0 / (1.0 + jnp.exp(-gate)))
            o_ref[0, rows, slab(h)] = (silu * yn).astype(o_ref.dtype)
        return carry

    lax.fori_loop(0, nc, chunk_step, 0)


def _retention(proj3, tables, gn_g):
    b, s, _ = proj3.shape
    cos_t, sin_t, inner, cross, key, chunk = tables
    c = RET_CHUNK
    nh = N_RET_HEADS
    qk_w = nh * RET_QK
    v_w = nh * LANES
    full = lambda shape: pl.BlockSpec(shape, lambda bi: (0,) * len(shape))
    return pl.pallas_call(
        _retention_body,
        grid=(b,),
        in_specs=[
            pl.BlockSpec((1, s, qk_w), lambda bi: (bi, 0, COL_RQ * LANES // qk_w)),
            pl.BlockSpec((1, s, qk_w), lambda bi: (bi, 0, COL_RK * LANES // qk_w)),
            pl.BlockSpec((1, s, v_w), lambda bi: (bi, 0, COL_RV * LANES // v_w)),
            pl.BlockSpec((1, s, v_w), lambda bi: (bi, 0, COL_RG * LANES // v_w)),
            full((s, LANES)), full((s, LANES)),
            full((nh, c, c)), full((nh, c, LANES)), full((nh, c, LANES)), full((nh, 1, LANES)),
            full((1, v_w)),
        ],
        out_specs=pl.BlockSpec((1, s, v_w), lambda bi: (bi, 0, 0)),
        out_shape=jax.ShapeDtypeStruct((b, s, v_w), BF16),
        scratch_shapes=[pltpu.VMEM((nh, LANES, LANES), F32)],
        compiler_params=_params(("arbitrary",)),
        name="retention",
    )(proj3, proj3, proj3, proj3, cos_t, sin_t, inner, cross, key, chunk, gn_g)


def _route(logits):
    lane = lax.broadcasted_iota(jnp.int32, logits.shape, 1).astype(F32)
    big = float(LANES)
    is_group = lane < N_GROUPS
    gl = jnp.where(is_group, logits, -jnp.inf)
    gmax = jnp.max(gl, axis=-1, keepdims=True)
    gidx = jnp.min(jnp.where(gl == gmax, lane, big), axis=-1, keepdims=True)
    p_group = 1.0 / jnp.sum(jnp.where(is_group, jnp.exp(gl - gmax), 0.0), axis=-1, keepdims=True)
    lo = N_GROUPS + EXPERTS_PER_GROUP * gidx
    il = jnp.where(lane >= lo, jnp.where(lane < lo + EXPERTS_PER_GROUP, logits, -jnp.inf), -jnp.inf)
    v1 = jnp.max(il, axis=-1, keepdims=True)
    i1 = jnp.min(jnp.where(il == v1, lane, big), axis=-1, keepdims=True)
    il2 = jnp.where(lane == i1, -jnp.inf, il)
    v2 = jnp.max(il2, axis=-1, keepdims=True)
    i2 = jnp.min(jnp.where(il2 == v2, lane, big), axis=-1, keepdims=True)
    e2 = jnp.exp(v2 - v1)
    inv = 1.0 / (1.0 + e2)
    g1 = p_group * inv
    g2 = p_group * (e2 * inv)
    out = jnp.where(lane == 0.0, i1 - N_GROUPS,
                    jnp.where(lane == 1.0, i2 - N_GROUPS,
                              jnp.where(lane == 2.0, g1, jnp.where(lane == 3.0, g2, 0.0))))
    return out


def _outproj_body(a_ref, r_ref, x_ref, w_ref, g_ref, wr_ref, br_ref, x1_ref, h2_ref, route_ref, eid_ref, *, sub):
    tm = x_ref.shape[0]
    half = a_ref.shape[1]
    n_sub = tm // sub

    def project(i):
        rows = slice(i * sub, (i + 1) * sub)
        acc = jnp.dot(a_ref[rows, :], w_ref[:half, :], preferred_element_type=F32)
        return acc + jnp.dot(r_ref[rows, :], w_ref[half:, :], preferred_element_type=F32)

    def epilogue(i, acc):
        rows = slice(i * sub, (i + 1) * sub)
        x1 = x_ref[rows, :] + acc
        x1_ref[rows, :] = x1
        ms = jnp.mean(x1 * x1, axis=-1, keepdims=True)
        h2 = (x1 * lax.rsqrt(ms + EPS)) * g_ref[...]
        h2_ref[rows, :] = _pack_halves(h2)
        logits = jnp.dot(h2.astype(BF16), wr_ref[...], preferred_element_type=F32) + br_ref[...]
        route = _route(logits)
        route_ref[rows, :] = route
        eid_ref[:, rows] = route.T[:TOP_K, :].astype(jnp.int32)

    acc = project(0)
    for i in range(n_sub):
        nxt = project(i + 1) if i + 1 < n_sub else None
        epilogue(i, acc)
        acc = nxt


def _outproj_router(a2d, r2d, x2d, w_out_bf16, g, wr_bf16, br, *, tm):
    n, d = x2d.shape
    half = a2d.shape[1]
    row = lambda width: pl.BlockSpec((tm, width), lambda i: (i, 0))
    full = lambda shape: pl.BlockSpec(shape, lambda i: (0, 0))
    return pl.pallas_call(
        functools.partial(_outproj_body, sub=min(tm, 2 * LANES)),
        grid=(n // tm,),
        in_specs=[row(half), row(half), row(d), full((2 * half, d)), full((1, d)), full((d, LANES)),
                  full((1, LANES))],
        out_specs=[row(d), row(d // 2), row(LANES), pl.BlockSpec((TOP_K, tm), lambda i: (0, i))],
        out_shape=[jax.ShapeDtypeStruct((n, d), F32), jax.ShapeDtypeStruct((n, d // 2), jnp.uint32),
                   jax.ShapeDtypeStruct((n, LANES), F32), jax.ShapeDtypeStruct((TOP_K, n), jnp.int32)],
        compiler_params=_params(("arbitrary",)),
        name="outproj_router",
    )(a2d, r2d, x2d, w_out_bf16, g, wr_bf16, br)


def _moe_plan(expert_id, tm):
    n_slots = expert_id.shape[0]
    n_tiles = n_slots // tm
    order = jnp.argsort(expert_id).astype(jnp.int32)
    sizes = jnp.zeros((N_EXPERTS,), jnp.int32).at[expert_id].add(1)
    ends = jnp.cumsum(sizes)
    starts = ends - sizes
    tile_row0 = jnp.arange(n_tiles, dtype=jnp.int32) * tm
    owner = lambda row: jnp.sum((ends[None, :] <= row[:, None]).astype(jnp.int32), axis=1)
    first_e = owner(tile_row0)
    last_e = owner(tile_row0 + (tm - 1))
    owners_upto = jnp.cumsum((sizes > 0).astype(jnp.int32))
    n_extra = (owners_upto[last_e] - owners_upto[first_e]).astype(jnp.int32)
    prev_last = jnp.concatenate([jnp.full((1,), -1, jnp.int32), last_e[:-1]])
    switch = (prev_last != first_e).astype(jnp.int32)
    eid = jnp.arange(N_EXPERTS, dtype=jnp.int32)
    cand = jnp.where((sizes[None, :] > 0) & (eid[None, :] > eid[:, None]), eid[None, :], N_EXPERTS)
    nxt = jnp.min(cand, axis=1)
    next_expert = jnp.where(nxt < N_EXPERTS, nxt, -1).astype(jnp.int32)
    spare = n_slots + jnp.arange(tm, dtype=jnp.int32)
    codes = jnp.concatenate([order.reshape(n_tiles, tm), spare[None, :]], axis=0).reshape(n_tiles + 1, 1, tm)
    return dict(first_e=first_e, n_extra=n_extra, switch=switch, starts=starts.astype(jnp.int32),
                ends=ends.astype(jnp.int32), next_expert=next_expert, codes=codes)


def _moe_body(fe_ref, nextra_ref, sw_ref, st_ref, en_ref, nx_ref,
              codes_hbm, h2_hbm, wg_hbm, wu_hbm, wd_hbm, ys_hbm,
              gidx, sidx, xbuf, obuf, wg_st, wu_st, wd_st, wg_rs, wu_rs, wd_rs,
              gisem, sisem, gsem, ssem, wsem, *, n_tokens, n_tiles, cast_rows, ff_chunks):
    t = pl.program_id(0)
    tm = xbuf.shape[1]
    last = n_tiles - 1
    spare_codes = n_tiles

    def gidx_copy(tile, slot):
        return pltpu.make_async_copy(codes_hbm.at[tile], gidx.at[slot], gisem.at[slot])

    def sidx_copy(tile, slot):
        return pltpu.make_async_copy(codes_hbm.at[tile], sidx.at[slot], sisem.at[slot])

    def weight_copies(ex):
        return (pltpu.make_async_copy(wg_hbm.at[ex], wg_st, wsem.at[0]),
                pltpu.make_async_copy(wu_hbm.at[ex], wu_st, wsem.at[1]),
                pltpu.make_async_copy(wd_hbm.at[ex], wd_st, wsem.at[2]))

    def gather_row(slot, r):
        code = gidx[slot, 0, r]
        tok = code - jnp.where(code >= n_tokens, n_tokens, 0)
        return pltpu.make_async_copy(h2_hbm.at[pl.ds(tok, 1)], xbuf.at[slot, pl.ds(r, 1)], gsem.at[slot])

    def scatter_row(slot, r):
        dst = sidx[slot, 0, r]
        return pltpu.make_async_copy(obuf.at[slot, pl.ds(r, 1)], ys_hbm.at[pl.ds(dst, 1)], ssem.at[slot])

    def wait_gather(slot):
        pltpu.make_async_copy(h2_hbm.at[pl.ds(0, tm)], xbuf.at[slot], gsem.at[slot]).wait()

    def wait_scatter(slot):
        pltpu.make_async_copy(obuf.at[slot], ys_hbm.at[pl.ds(0, tm)], ssem.at[slot]).wait()

    def for_rows(fn):
        def body(r, carry):
            fn(r)
            return carry
        lax.fori_loop(0, tm, body, 0, unroll=8)

    def take_over(ex, first):
        @pl.when(first)
        def _():
            for cp in weight_copies(ex):
                cp.start()

        for cp in weight_copies(ex):
            cp.wait()

        def cast_up(i, carry):
            r = pl.multiple_of(i * cast_rows, cast_rows)
            wg_rs[pl.ds(r, cast_rows), :] = wg_st[pl.ds(r, cast_rows), :].astype(BF16)
            wu_rs[pl.ds(r, cast_rows), :] = wu_st[pl.ds(r, cast_rows), :].astype(BF16)
            return carry

        def cast_down(i, carry):
            r = pl.multiple_of(i * cast_rows, cast_rows)
            wd_rs[pl.ds(r, cast_rows), :] = wd_st[pl.ds(r, cast_rows), :].astype(BF16)
            return carry

        lax.fori_loop(0, wg_st.shape[0] // cast_rows, cast_up, 0)
        lax.fori_loop(0, wd_st.shape[0] // cast_rows, cast_down, 0)
        nxt = nx_ref[ex]

        @pl.when(nxt >= 0)
        def _():
            for cp in weight_copies(nxt):
                cp.start(priority=1)

    def ffn(slot, ex, accumulate, row_dmas=None):
        x_lo, x_hi = (h.astype(BF16) for h in _unpack_halves(xbuf[slot]))
        hd = x_lo.shape[1]
        row = t * tm + lax.broadcasted_iota(jnp.int32, (tm, 1), 0)
        mine = jnp.logical_and(row >= st_ref[ex], row < en_ref[ex])
        d_ff = wg_rs.shape[1]
        n_chunks = ff_chunks if row_dmas is not None else 1
        fc = d_ff // n_chunks
        rows = tm // n_chunks
        contrib = None
        for c in range(n_chunks):
            cols = slice(c * fc, (c + 1) * fc)
            up = lambda w: (jnp.dot(x_lo, w[:hd, cols], preferred_element_type=F32)
                            + jnp.dot(x_hi, w[hd:, cols], preferred_element_type=F32))
            a = up(wg_rs)
            b = up(wu_rs)
            hmid = jnp.where(mine, (a * (1.0 / (1.0 + jnp.exp(-a)))) * b, 0.0).astype(BF16)
            part = jnp.dot(hmid, wd_rs[cols, :], preferred_element_type=F32)
            contrib = part if contrib is None else contrib + part
            if row_dmas is not None:
                row_dmas(c * rows, (c + 1) * rows)
        if accumulate:
            prev_lo, prev_hi = _unpack_halves(obuf[slot])
            contrib = jnp.concatenate([prev_lo, prev_hi], axis=1) + contrib
        obuf[slot] = _pack_halves(contrib)

    e0 = fe_ref[t]

    def step(p):
        q = 1 - p
        if p == 0:
            @pl.when(t == 0)
            def _():
                copies = (gidx_copy(0, 0), gidx_copy(min(1, last), 1), sidx_copy(spare_codes, 1))
                for cp in copies:
                    cp.start()
                for cp in copies:
                    cp.wait()
                obuf[1] = jnp.zeros(obuf.shape[1:], obuf.dtype)
                for_rows(lambda r: gather_row(0, r).start())

        gidx_copy(jnp.minimum(t + 2, last), p).start()
        sidx_copy(t, p).start()

        @pl.when(t >= 1)
        def _():
            gidx_copy(0, q).wait()
            sidx_copy(0, q).wait()
            wait_scatter(p)

        wait_gather(p)

        @pl.when(sw_ref[t] == 1)
        def _():
            take_over(e0, t == 0)

        def row_dmas(lo, hi):
            for r in range(lo, hi):
                scatter_row(q, r).start()
                gather_row(q, r).start()

        ffn(p, e0, accumulate=False, row_dmas=row_dmas)

    for parity in range(2):
        pl.when(lax.rem(t, 2) == parity)(functools.partial(step, parity))

    slot = lax.rem(t, 2)

    def further_owner(_, ex):
        nxt = nx_ref[ex]
        take_over(nxt, False)
        ffn(slot, nxt, accumulate=True)
        return nxt

    lax.fori_loop(0, nextra_ref[t], further_owner, e0)

    @pl.when(t == last)
    def _():
        gidx_copy(0, slot).wait()
        sidx_copy(0, slot).wait()
        for_rows(lambda r: scatter_row(slot, r).start())
        wait_gather(1 - slot)
        wait_scatter(1 - slot)
        wait_scatter(slot)


def _moe_ffn(h2, plan, w_gate, w_up, w_down, *, tm):
    n, hd = h2.shape
    d = 2 * hd
    n_slots = 2 * n
    n_tiles = n_slots // tm
    assert n_tiles >= 2 and n_slots % tm == 0
    d_ff = w_gate.shape[2]
    any_spec = pl.BlockSpec(memory_space=pl.ANY)
    grid_spec = pltpu.PrefetchScalarGridSpec(
        num_scalar_prefetch=6,
        grid=(n_tiles,),
        in_specs=[any_spec] * 5,
        out_specs=any_spec,
        scratch_shapes=[
            pltpu.SMEM((2, 1, tm), jnp.int32),
            pltpu.SMEM((2, 1, tm), jnp.int32),
            pltpu.VMEM((2, tm, hd), jnp.uint32),
            pltpu.VMEM((2, tm, hd), jnp.uint32),
            pltpu.VMEM((d, d_ff), F32),
            pltpu.VMEM((d, d_ff), F32),
            pltpu.VMEM((d_ff, d), F32),
            pltpu.VMEM((d, d_ff), BF16),
            pltpu.VMEM((d, d_ff), BF16),
            pltpu.VMEM((d_ff, d), BF16),
            pltpu.SemaphoreType.DMA((2,)),
            pltpu.SemaphoreType.DMA((2,)),
            pltpu.SemaphoreType.DMA((2,)),
            pltpu.SemaphoreType.DMA((2,)),
            pltpu.SemaphoreType.DMA((3,)),
        ],
    )
    return pl.pallas_call(
        functools.partial(_moe_body, n_tokens=n, n_tiles=n_tiles, cast_rows=128,
                          ff_chunks=d_ff // (2 * LANES)),
        grid_spec=grid_spec,
        out_shape=jax.ShapeDtypeStruct((n_slots + tm, hd), jnp.uint32),
        compiler_params=_params(("arbitrary",), vmem=MOE_VMEM_LIMIT),
        name="moe_ffn",
    )(plan["first_e"], plan["n_extra"], plan["switch"], plan["starts"], plan["ends"], plan["next_expert"],
      plan["codes"], h2, w_gate, w_up, w_down)


def _final_body(x1_ref, ys0_ref, ys1_ref, route_ref, g_ref, o_ref):
    d = x1_ref.shape[1]
    route = route_ref[...]
    g1, g2 = route[:, 2:3], route[:, 3:4]
    ya = _unpack_halves(ys0_ref[...])
    yb = _unpack_halves(ys1_ref[...])
    halves = (slice(0, d // 2), slice(d // 2, d))
    x2 = [x1_ref[:, cols] + (ya[i] * g1 + yb[i] * g2) for i, cols in enumerate(halves)]
    ms = sum(jnp.sum(h * h, axis=-1, keepdims=True) for h in x2) * (1.0 / d)
    inv = lax.rsqrt(ms + EPS)
    for h, cols in zip(x2, halves):
        o_ref[:, cols] = ((h * inv) * g_ref[:, cols]).astype(o_ref.dtype)


def _final(x1, ys, route, g, *, tm):
    n, d = x1.shape
    nb = n // tm
    return pl.pallas_call(
        _final_body,
        grid=(nb,),
        in_specs=[
            pl.BlockSpec((tm, d), lambda i: (i, 0)),
            pl.BlockSpec((tm, d // 2), lambda i: (i, 0)),
            pl.BlockSpec((tm, d // 2), lambda i: (nb + i, 0)),
            pl.BlockSpec((tm, LANES), lambda i: (i, 0)),
            pl.BlockSpec((1, d), lambda i: (0, 0)),
        ],
        out_specs=pl.BlockSpec((tm, d), lambda i: (i, 0)),
        out_shape=jax.ShapeDtypeStruct((n, d), F32),
        compiler_params=_params(("arbitrary",)),
        name="final_norm",
    )(x1, ys, ys, route, g)


def _layer(x, rel_bias, attn_norm_g, w_in, lq1, lk1, lq2, lk2, diff_subln_g, ret_gn_g, w_out, ffn_norm_g,
           w_group_router, b_group, w_inner_router, b_inner, w_gate_exp, w_up_exp, w_down_exp, *, layer,
           attn_tile, row_tile, moe_tile):
    b, s, d = x.shape
    n = b * s
    x2d = x.reshape(n, d)
    lam_init = 0.8 - 0.6 * math.exp(-0.3 * layer)

    dq_cols = (COL_DK - COL_DQ) * LANES
    col_scale = jnp.concatenate([jnp.full((1, dq_cols), DIFF_HALF ** -0.5 * LOG2E, F32),
                                 jnp.ones((1, IN_COLS - dq_cols), F32)], axis=1)
    proj = _inproj(x2d, attn_norm_g.reshape(1, d), w_in.astype(BF16), col_scale, tm=min(row_tile, n))
    proj3 = proj.reshape(b, s, IN_COLS)

    bias = _bias_tiles(rel_bias, attn_tile)
    vec = lambda a: a.reshape(1, -1)
    a_out = _diff_attention(proj3, bias, vec(lq1), vec(lk1), vec(lq2), vec(lk2), vec(diff_subln_g),
                            t=attn_tile, lam_init=lam_init)
    r_out = _retention(proj3, _retention_tables(s), vec(ret_gn_g))

    pad = LANES - N_GROUPS - N_EXPERTS
    wr = jnp.concatenate([w_group_router, jnp.transpose(w_inner_router, (1, 0, 2)).reshape(d, N_EXPERTS),
                          jnp.zeros((d, pad), F32)], axis=1).astype(BF16)
    br = jnp.concatenate([b_group, b_inner.reshape(-1), jnp.zeros((pad,), F32)]).reshape(1, LANES)
    x1, h2, route, eid = _outproj_router(a_out.reshape(n, -1), r_out.reshape(n, -1), x2d, w_out.astype(BF16),
                                         vec(ffn_norm_g), wr, br, tm=row_tile)

    plan = _moe_plan(eid.reshape(-1), moe_tile)
    ys = _moe_ffn(h2, plan, w_gate_exp, w_up_exp, w_down_exp, tm=moe_tile)
    return x1, ys, route


def kernel(x, rel_bias, attn_norm_g, w_in, lambda_q1, lambda_k1, lambda_q2, lambda_k2, diff_subln_g, ret_gn_g,
           w_out, ffn_norm_g, w_group_router, b_group, w_inner_router, b_inner, w_gate_exp, w_up_exp, w_down_exp,
           final_g):
    b, s, d = x.shape
    depth = w_in.shape[0]
    assert depth == 1, "the final norm is fused into the single layer's combine step"
    first = lambda a: a.reshape(a.shape[1:])
    x1, ys, route = _layer(x, rel_bias, attn_norm_g[0], w_in[0], lambda_q1[0], lambda_k1[0], lambda_q2[0],
                           lambda_k2[0], diff_subln_g[0], ret_gn_g[0], w_out[0], ffn_norm_g[0],
                           w_group_router[0], b_group[0], w_inner_router[0], b_inner[0], first(w_gate_exp),
                           first(w_up_exp), first(w_down_exp), layer=0, attn_tile=min(256, s), row_tile=512,
                           moe_tile=256)
    out = _final(x1, ys, route, final_g.reshape(1, d), tm=512)
    return out.reshape(b, s, d)
```

```python
import functools
import math

import numpy as np
import jax
import jax.numpy as jnp
from jax import lax
from jax.experimental import pallas as pl
from jax.experimental.pallas import tpu as pltpu

F32 = jnp.float32
BF16 = jnp.bfloat16

EPS = 1e-6
LANES = 128
N_DIFF_HEADS = 8
DIFF_HALF = 64
N_RET_HEADS = 8
RET_QK = 64
RET_CHUNK = 128
ROPE_BASE = 10000.0
N_BUCKETS = 32
MAX_DISTANCE = 128
N_GROUPS = 4
EXPERTS_PER_GROUP = 8
N_EXPERTS = N_GROUPS * EXPERTS_PER_GROUP
TOP_K = 2
MASK_VALUE = -1e30
LOG2E = math.log2(math.e)
VMEM_LIMIT = 56 * 1024 * 1024
MOE_VMEM_LIMIT = 60 * 1024 * 1024

COL_DQ, COL_DK, COL_DV = 0, 8, 16
COL_RQ, COL_RK, COL_RV, COL_RG = 24, 28, 32, 40
IN_COLS = 48 * LANES


def _params(sem, vmem=VMEM_LIMIT):
    return pltpu.CompilerParams(dimension_semantics=sem, vmem_limit_bytes=vmem)


def _pack_halves(x):
    hd = x.shape[1] // 2
    bits = lambda h: lax.bitcast_convert_type(h.astype(BF16).astype(F32), jnp.uint32)
    return lax.shift_right_logical(bits(x[:, :hd]), jnp.uint32(16)) | bits(x[:, hd:])


def _unpack_halves(w):
    return (lax.bitcast_convert_type(lax.shift_left(w, jnp.uint32(16)), F32),
            lax.bitcast_convert_type(w & jnp.uint32(0xFFFF0000), F32))


def _inproj_body(x_ref, g_ref, w_ref, cs_ref, o_ref):
    x = x_ref[...]
    acc = jnp.dot((x * g_ref[...]).astype(BF16), w_ref[...], preferred_element_type=F32)
    inv = lax.rsqrt(jnp.mean(x * x, axis=-1, keepdims=True) + EPS)
    o_ref[...] = ((acc * inv) * cs_ref[...]).astype(o_ref.dtype)


def _inproj(x2d, g, w, col_scale, *, tm):
    n, d = x2d.shape
    cols = w.shape[1]
    once = dict(pipeline_mode=pl.Buffered(1))
    return pl.pallas_call(
        _inproj_body,
        grid=(n // tm,),
        in_specs=[
            pl.BlockSpec((tm, d), lambda i: (i, 0)),
            pl.BlockSpec((1, d), lambda i: (0, 0)),
            pl.BlockSpec((d, cols), lambda i: (0, 0), **once),
            pl.BlockSpec((1, cols), lambda i: (0, 0)),
        ],
        out_specs=pl.BlockSpec((tm, cols), lambda i: (i, 0)),
        out_shape=jax.ShapeDtypeStruct((n, cols), BF16),
        compiler_params=_params(("arbitrary",)),
        name="inproj",
    )(x2d, g, w, col_scale)


def _t5_bucket_table(n):
    d = np.arange(n)
    max_exact = N_BUCKETS // 2
    log_ratio = np.log(np.maximum(d, 1).astype(np.float64) / max_exact) / math.log(MAX_DISTANCE / max_exact)
    large = np.minimum(max_exact + (log_ratio * (N_BUCKETS - max_exact)).astype(np.int64), N_BUCKETS - 1)
    return np.where(d < max_exact, d, large).astype(np.int32)


def _bias_body(rb_ref, bkt_ref, o_ref):
    h = pl.program_id(0)
    t = o_ref.shape[2]
    bkt = bkt_ref[...]
    far = rb_ref[N_BUCKETS - 1, h]
    by_dist = jnp.zeros(bkt.shape, F32)
    for b in range(N_BUCKETS - 1):
        by_dist = jnp.where(bkt == b, (rb_ref[b, h] - far) * LOG2E, by_dist)
    rows = (jnp.concatenate([jnp.full((1, t), MASK_VALUE, F32), by_dist[:, :t]], axis=1), by_dist)
    for i, row in enumerate(rows):
        shifted = pltpu.roll(jnp.broadcast_to(row, (t, 2 * t)), 0, 1, stride=1, stride_axis=0)
        o_ref[0, i] = shifted[:, t:]


def _bias_tiles(rel_bias, t):
    bkt = jnp.asarray(_t5_bucket_table(2 * t)).reshape(1, 2 * t)
    return pl.pallas_call(
        _bias_body,
        grid=(N_DIFF_HEADS,),
        in_specs=[
            pl.BlockSpec(memory_space=pltpu.SMEM),
            pl.BlockSpec((1, 2 * t), lambda h: (0, 0)),
        ],
        out_specs=pl.BlockSpec((1, 2, t, t), lambda h: (h, 0, 0, 0)),
        out_shape=jax.ShapeDtypeStruct((N_DIFF_HEADS, 2, t, t), F32),
        compiler_params=_params(("arbitrary",)),
        name="bias_tiles",
    )(rel_bias, bkt)


_NT = (((1,), (1,)), ((), ()))
_TN = (((0,), (0,)), ((), ()))


def _diffattn_body(q_ref, k_ref, v_ref, bias_ref, lq1_ref, lk1_ref, lq2_ref, lk2_ref, g_ref, o_ref, vt_ref, *,
                   t, lam_init, near_band):
    nq = k_ref.shape[1] // t
    half = t // 2
    lane = lax.broadcasted_iota(jnp.int32, (t, LANES), 1)
    lam = (jnp.exp(jnp.sum(lq1_ref[...] * lk1_ref[...], axis=-1, keepdims=True))
           - jnp.exp(jnp.sum(lq2_ref[...] * lk2_ref[...], axis=-1, keepdims=True)) + lam_init)
    vt_ref[:LANES, :] = v_ref[0].T
    vt_ref[LANES:, :] = jnp.ones((vt_ref.shape[0] - LANES, vt_ref.shape[1]), BF16)
    qmask = {}
    run_max = {}
    acc = {}

    def scores(item):
        qi, j, part = item
        if qi not in qmask:
            q = q_ref[0, qi * t:(qi + 1) * t, :]
            zero = jnp.zeros_like(q)
            qmask[qi] = (jnp.where(lane < DIFF_HALF, q, zero), jnp.where(lane >= DIFF_HALF, q, zero))
        k_lo = j * t + (half if part == "diag_lower" else 0)
        k_hi = j * t + (half if part == "diag_upper" else t)
        kb = k_ref[0, k_lo:k_hi, :]
        out = []
        for c in range(2):
            qsel = qmask[qi][c][half:] if part == "diag_lower" else qmask[qi][c]
            s = lax.dot_general(kb, qsel, _NT, preferred_element_type=F32)
            if part == "diag_upper":
                s = s + bias_ref[0, 0, :half, :]
            elif part == "diag_lower":
                s = s + bias_ref[0, 0, half:, half:]
            elif j == qi - 1:
                if near_band >= t:
                    s = s + bias_ref[0, 1]
                else:
                    corner = s[t - near_band:, :near_band] + bias_ref[0, 1, t - near_band:, :near_band]
                    bottom = jnp.concatenate([corner, s[t - near_band:, near_band:]], axis=1)
                    s = jnp.concatenate([s[:t - near_band], bottom], axis=0)
            out.append(s)
        return out

    def softmax_step(item, ss):
        qi, j, part = item
        first = j == 0 and part != "diag_lower"
        ps, alphas = [], []
        for c in range(2):
            s = ss[c]
            smax = jnp.max(s, axis=0, keepdims=True)
            if first:
                m_new, alpha = smax, None
                run_max[qi, c] = m_new
            else:
                m_full = run_max[qi, c]
                m_old = m_full[:, half:] if part == "diag_lower" else m_full
                m_new = jnp.maximum(m_old, smax)
                alpha = jnp.exp2(m_old - m_new)
                run_max[qi, c] = (jnp.concatenate([m_full[:, :half], m_new], axis=1)
                                  if part == "diag_lower" else m_new)
            ps.append(jnp.exp2(s - m_new).astype(BF16))
            alphas.append(alpha)
        return ps, alphas

    def values_step(item, ps, alphas):
        qi, j, part = item
        k_lo = j * t + (half if part == "diag_lower" else 0)
        k_hi = j * t + (half if part == "diag_upper" else t)
        vtb = vt_ref[:, k_lo:k_hi]
        for c in range(2):
            alpha = alphas[c]
            r = jnp.dot(vtb, ps[c], preferred_element_type=F32)
            if alpha is None:
                acc[qi, c] = r
            elif part == "diag_lower":
                acc[qi, c] = jnp.concatenate([acc[qi, c][:, :half], alpha * acc[qi, c][:, half:] + r], axis=1)
            else:
                acc[qi, c] = alpha * acc[qi, c] + r
        if part == "diag_lower":
            outs = []
            for c in range(2):
                a = acc.pop((qi, c))
                del run_max[qi, c]
                outs.append(a[:LANES] * (1.0 / a[LANES:LANES + 1]))
            ot = (outs[0] - lam * outs[1]).T
            ms = jnp.mean(ot * ot, axis=-1, keepdims=True)
            o_ref[0, qi * t:(qi + 1) * t, :] = (
                ((ot * lax.rsqrt(ms + EPS)) * g_ref[...]) * (1.0 - lam_init)).astype(o_ref.dtype)
            del qmask[qi]

    items = []
    for qi in range(nq):
        items += [(qi, j, "full") for j in range(qi)] + [(qi, qi, "diag_upper"), (qi, qi, "diag_lower")]
    ss = scores(items[0])
    pending = None
    for n, item in enumerate(items):
        nxt = scores(items[n + 1]) if n + 1 < len(items) else None
        ps, alphas = softmax_step(item, ss)
        if pending is not None:
            values_step(*pending)
        pending = (item, ps, alphas)
        ss = nxt
    values_step(*pending)


def _diff_attention(proj3, bias, lq1, lk1, lq2, lk2, subln_g, *, t, lam_init):
    b, s, _ = proj3.shape
    vec = pl.BlockSpec((1, DIFF_HALF), lambda bi, h: (0, 0))
    head = lambda col: pl.BlockSpec((1, s, LANES), lambda bi, h: (bi, 0, col + h))
    first_far = int(np.argmax(_t5_bucket_table(2 * t) == N_BUCKETS - 1))
    near_band = -(-first_far // LANES) * LANES
    ones_rows = 16
    return pl.pallas_call(
        functools.partial(_diffattn_body, t=t, lam_init=lam_init, near_band=near_band),
        grid=(b, N_DIFF_HEADS),
        in_specs=[
            head(COL_DQ), head(COL_DK), head(COL_DV),
            pl.BlockSpec((1, 2, t, t), lambda bi, h: (h, 0, 0, 0)),
            vec, vec, vec, vec,
            pl.BlockSpec((1, LANES), lambda bi, h: (0, 0)),
        ],
        out_specs=pl.BlockSpec((1, s, LANES), lambda bi, h: (bi, 0, h)),
        out_shape=jax.ShapeDtypeStruct((b, s, N_DIFF_HEADS * LANES), BF16),
        scratch_shapes=[pltpu.VMEM((LANES + ones_rows, s), BF16)],
        compiler_params=_params(("arbitrary", "arbitrary")),
        name="diff_attention",
    )(proj3, proj3, proj3, bias, lq1, lk1, lq2, lk2, subln_g)


def _retention_tables(s):
    c = RET_CHUNK
    half = RET_QK // 2
    inv_freq = ROPE_BASE ** (-jnp.arange(0, RET_QK, 2, dtype=F32) / RET_QK)
    ang = jnp.arange(s, dtype=F32)[:, None] * inv_freq[None, :]
    cos, sin = jnp.cos(ang), jnp.sin(ang)
    cos_t = jnp.tile(cos, (1, LANES // half))
    sin_t = jnp.tile(jnp.concatenate([-sin, sin], axis=1), (1, LANES // RET_QK))
    log_g = jnp.log(1.0 - jnp.exp2(-5.0 - jnp.arange(N_RET_HEADS, dtype=F32)))
    n = jnp.arange(c, dtype=F32)
    diff = n[:, None] - n[None, :]
    inner = jnp.where(diff[None] >= 0, jnp.exp(jnp.maximum(diff, 0.0)[None] * log_g[:, None, None]), 0.0)
    cross = jnp.exp((n[None] + 1.0) * log_g[:, None])
    key = jnp.exp((c - 1.0 - n[None]) * log_g[:, None])
    chunk = jnp.exp(c * log_g)
    bc = lambda a: jnp.broadcast_to(a[..., None], a.shape + (LANES,)).astype(F32)
    return cos_t, sin_t, inner.astype(F32), bc(cross), bc(key), bc(chunk[:, None])


def _retention_body(rq_ref, rk_ref, rv_ref, rg_ref, cos_ref, sin_ref, inner_ref, cross_ref, key_ref,
                    chunk_ref, gn_ref, o_ref, state_ref):
    c = RET_CHUNK
    nc = rq_ref.shape[1] // c
    nh = N_RET_HEADS
    state_ref[...] = jnp.zeros(state_ref.shape, F32)
    lane = lax.broadcasted_iota(jnp.int32, (c, LANES), 1)
    first_half = (lane % RET_QK) < (RET_QK // 2)
    head_mask = [lane < RET_QK, lane >= RET_QK]
    slab = lambda i: slice(i * LANES, (i + 1) * LANES)

    def rotary(x, cos, sin):
        rot = jnp.where(first_half, pltpu.roll(x, LANES - RET_QK // 2, 1), pltpu.roll(x, RET_QK // 2, 1))
        return x * cos + rot * sin

    def chunk_step(ci, carry):
        r = pl.multiple_of(ci * c, c)
        rows = pl.ds(r, c)
        cos = cos_ref[rows, :]
        sin = sin_ref[rows, :]
        q_pair, k_pair, kb_pair = [], [], []
        for pr in range(nh // 2):
            q_pair.append(rotary(rq_ref[0, rows, slab(pr)].astype(F32), cos, sin))
            k = rotary(rk_ref[0, rows, slab(pr)].astype(F32), cos, sin) * (RET_QK ** -0.5)
            k_pair.append(k)
            kb_pair.append(k.astype(BF16))
        qh = [jnp.where(head_mask[h % 2], q_pair[h // 2], 0.0).astype(BF16) for h in range(nh)]
        kd = [(k_pair[h // 2] * key_ref[h]).astype(BF16) for h in range(nh)]
        vh = [rv_ref[0, rows, slab(h)] for h in range(nh)]
        scores = [lax.dot_general(qh[h], kb_pair[h // 2], _NT, preferred_element_type=F32) for h in range(nh)]
        cross = [jnp.dot(qh[h], state_ref[h].astype(BF16), preferred_element_type=F32) for h in range(nh)]
        kv = [lax.dot_general(kd[h], vh[h], _TN, preferred_element_type=F32) for h in range(nh)]
        decayed = [(scores[h] * inner_ref[h]).astype(BF16) for h in range(nh)]
        for h in range(nh):
            state_ref[h] = state_ref[h] * chunk_ref[h] + kv[h]
        inner = [jnp.dot(decayed[h], vh[h], preferred_element_type=F32) for h in range(nh)]
        for h in range(nh):
            y = inner[h] + cross[h] * cross_ref[h]
            mu = jnp.mean(y, axis=-1, keepdims=True)
            yc = y - mu
            var = jnp.mean(yc * yc, axis=-1, keepdims=True)
            yn = (yc * lax.rsqrt(var + EPS)) * gn_ref[:, slab(h)]
            gate = rg_ref[0, rows, slab(h)].astype(F32)
            silu = gate * (1.0 / (1.0 + jnp.exp(-gate)))
            o_ref[0, rows, slab(h)] = (silu * yn).astype(o_ref.dtype)
        return carry

    lax.fori_loop(0, nc, chunk_step, 0, unroll=2)


def _retention(proj3, tables, gn_g):
    b, s, _ = proj3.shape
    cos_t, sin_t, inner, cross, key, chunk = tables
    c = RET_CHUNK
    nh = N_RET_HEADS
    qk_w = nh * RET_QK
    v_w = nh * LANES
    full = lambda shape: pl.BlockSpec(shape, lambda bi: (0,) * len(shape))
    return pl.pallas_call(
        _retention_body,
        grid=(b,),
        in_specs=[
            pl.BlockSpec((1, s, qk_w), lambda bi: (bi, 0, COL_RQ * LANES // qk_w)),
            pl.BlockSpec((1, s, qk_w), lambda bi: (bi, 0, COL_RK * LANES // qk_w)),
            pl.BlockSpec((1, s, v_w), lambda bi: (bi, 0, COL_RV * LANES // v_w)),
            pl.BlockSpec((1, s, v_w), lambda bi: (bi, 0, COL_RG * LANES // v_w)),
            full((s, LANES)), full((s, LANES)),
            full((nh, c, c)), full((nh, c, LANES)), full((nh, c, LANES)), full((nh, 1, LANES)),
            full((1, v_w)),
        ],
        out_specs=pl.BlockSpec((1, s, v_w), lambda bi: (bi, 0, 0)),
        out_shape=jax.ShapeDtypeStruct((b, s, v_w), BF16),
        scratch_shapes=[pltpu.VMEM((nh, LANES, LANES), F32)],
        compiler_params=_params(("arbitrary",)),
        name="retention",
    )(proj3, proj3, proj3, proj3, cos_t, sin_t, inner, cross, key, chunk, gn_g)


def _route(logits):
    lane = lax.broadcasted_iota(jnp.int32, logits.shape, 1).astype(F32)
    big = float(LANES)
    is_group = lane < N_GROUPS
    gl = jnp.where(is_group, logits, -jnp.inf)
    gmax = jnp.max(gl, axis=-1, keepdims=True)
    gidx = jnp.min(jnp.where(gl == gmax, lane, big), axis=-1, keepdims=True)
    p_group = 1.0 / jnp.sum(jnp.where(is_group, jnp.exp(gl - gmax), 0.0), axis=-1, keepdims=True)
    lo = N_GROUPS + EXPERTS_PER_GROUP * gidx
    il = jnp.where(lane >= lo, jnp.where(lane < lo + EXPERTS_PER_GROUP, logits, -jnp.inf), -jnp.inf)
    v1 = jnp.max(il, axis=-1, keepdims=True)
    i1 = jnp.min(jnp.where(il == v1, lane, big), axis=-1, keepdims=True)
    il2 = jnp.where(lane == i1, -jnp.inf, il)
    v2 = jnp.max(il2, axis=-1, keepdims=True)
    i2 = jnp.min(jnp.where(il2 == v2, lane, big), axis=-1, keepdims=True)
    e2 = jnp.exp(v2 - v1)
    inv = 1.0 / (1.0 + e2)
    g1 = p_group * inv
    g2 = p_group * (e2 * inv)
    out = jnp.where(lane == 0.0, i1 - N_GROUPS,
                    jnp.where(lane == 1.0, i2 - N_GROUPS,
                              jnp.where(lane == 2.0, g1, jnp.where(lane == 3.0, g2, 0.0))))
    return out


def _outproj_body(a_ref, r_ref, x_ref, w_ref, g_ref, wr_ref, br_ref, x1_ref, h2_ref, route_ref, eid_ref, *, sub):
    tm = x_ref.shape[0]
    half = a_ref.shape[1]
    n_sub = tm // sub

    def project(i):
        rows = slice(i * sub, (i + 1) * sub)
        acc = jnp.dot(a_ref[rows, :], w_ref[:half, :], preferred_element_type=F32)
        return acc + jnp.dot(r_ref[rows, :], w_ref[half:, :], preferred_element_type=F32)

    def epilogue(i, acc):
        rows = slice(i * sub, (i + 1) * sub)
        x1 = x_ref[rows, :] + acc
        x1_ref[rows, :] = x1
        ms = jnp.mean(x1 * x1, axis=-1, keepdims=True)
        h2 = (x1 * lax.rsqrt(ms + EPS)) * g_ref[...]
        h2_ref[rows, :] = _pack_halves(h2)
        logits = jnp.dot(h2.astype(BF16), wr_ref[...], preferred_element_type=F32) + br_ref[...]
        route = _route(logits)
        route_ref[rows, :] = route
        eid_ref[:, rows] = route.T[:TOP_K, :].astype(jnp.int32)

    acc = project(0)
    for i in range(n_sub):
        nxt = project(i + 1) if i + 1 < n_sub else None
        epilogue(i, acc)
        acc = nxt


def _outproj_router(a2d, r2d, x2d, w_out_bf16, g, wr_bf16, br, *, tm):
    n, d = x2d.shape
    half = a2d.shape[1]
    row = lambda width: pl.BlockSpec((tm, width), lambda i: (i, 0))
    full = lambda shape: pl.BlockSpec(shape, lambda i: (0, 0))
    return pl.pallas_call(
        functools.partial(_outproj_body, sub=min(tm, 2 * LANES)),
        grid=(n // tm,),
        in_specs=[row(half), row(half), row(d), full((2 * half, d)), full((1, d)), full((d, LANES)),
                  full((1, LANES))],
        out_specs=[row(d), row(d // 2), row(LANES), pl.BlockSpec((TOP_K, tm), lambda i: (0, i))],
        out_shape=[jax.ShapeDtypeStruct((n, d), F32), jax.ShapeDtypeStruct((n, d // 2), jnp.uint32),
                   jax.ShapeDtypeStruct((n, LANES), F32), jax.ShapeDtypeStruct((TOP_K, n), jnp.int32)],
        compiler_params=_params(("arbitrary",)),
        name="outproj_router",
    )(a2d, r2d, x2d, w_out_bf16, g, wr_bf16, br)


def _moe_plan(expert_id, tm):
    n_slots = expert_id.shape[0]
    n_tiles = n_slots // tm
    order = jnp.argsort(expert_id).astype(jnp.int32)
    sizes = jnp.zeros((N_EXPERTS,), jnp.int32).at[expert_id].add(1)
    ends = jnp.cumsum(sizes)
    starts = ends - sizes
    tile_row0 = jnp.arange(n_tiles, dtype=jnp.int32) * tm
    owner = lambda row: jnp.sum((ends[None, :] <= row[:, None]).astype(jnp.int32), axis=1)
    first_e = owner(tile_row0)
    last_e = owner(tile_row0 + (tm - 1))
    owners_upto = jnp.cumsum((sizes > 0).astype(jnp.int32))
    n_extra = (owners_upto[last_e] - owners_upto[first_e]).astype(jnp.int32)
    prev_last = jnp.concatenate([jnp.full((1,), -1, jnp.int32), last_e[:-1]])
    switch = (prev_last != first_e).astype(jnp.int32)
    eid = jnp.arange(N_EXPERTS, dtype=jnp.int32)
    cand = jnp.where((sizes[None, :] > 0) & (eid[None, :] > eid[:, None]), eid[None, :], N_EXPERTS)
    nxt = jnp.min(cand, axis=1)
    next_expert = jnp.where(nxt < N_EXPERTS, nxt, -1).astype(jnp.int32)
    spare = n_slots + jnp.arange(tm, dtype=jnp.int32)
    codes = jnp.concatenate([order.reshape(n_tiles, tm), spare[None, :]], axis=0).reshape(n_tiles + 1, 1, tm)
    return dict(first_e=first_e, n_extra=n_extra, switch=switch, starts=starts.astype(jnp.int32),
                ends=ends.astype(jnp.int32), next_expert=next_expert, codes=codes)


def _moe_body(fe_ref, nextra_ref, sw_ref, st_ref, en_ref, nx_ref,
              codes_hbm, h2_hbm, wg_hbm, wu_hbm, wd_hbm, ys_hbm,
              gidx, sidx, xbuf, obuf, wg_st, wu_st, wd_st, wg_rs, wu_rs, wd_rs,
              gisem, sisem, gsem, ssem, wsem, *, n_tokens, n_tiles, cast_rows, ff_chunks):
    t = pl.program_id(0)
    tm = xbuf.shape[1]
    last = n_tiles - 1
    spare_codes = n_tiles

    def gidx_copy(tile, slot):
        return pltpu.make_async_copy(codes_hbm.at[tile], gidx.at[slot], gisem.at[slot])

    def sidx_copy(tile, slot):
        return pltpu.make_async_copy(codes_hbm.at[tile], sidx.at[slot], sisem.at[slot])

    def weight_copies(ex):
        return (pltpu.make_async_copy(wg_hbm.at[ex], wg_st, wsem.at[0]),
                pltpu.make_async_copy(wu_hbm.at[ex], wu_st, wsem.at[1]),
                pltpu.make_async_copy(wd_hbm.at[ex], wd_st, wsem.at[2]))

    def gather_row(slot, r):
        code = gidx[slot, 0, r]
        tok = code - jnp.where(code >= n_tokens, n_tokens, 0)
        return pltpu.make_async_copy(h2_hbm.at[pl.ds(tok, 1)], xbuf.at[slot, pl.ds(r, 1)], gsem.at[slot])

    def scatter_row(slot, r):
        dst = sidx[slot, 0, r]
        return pltpu.make_async_copy(obuf.at[slot, pl.ds(r, 1)], ys_hbm.at[pl.ds(dst, 1)], ssem.at[slot])

    def wait_gather(slot):
        pltpu.make_async_copy(h2_hbm.at[pl.ds(0, tm)], xbuf.at[slot], gsem.at[slot]).wait()

    def wait_scatter(slot):
        pltpu.make_async_copy(obuf.at[slot], ys_hbm.at[pl.ds(0, tm)], ssem.at[slot]).wait()

    def for_rows(fn):
        def body(r, carry):
            fn(r)
            return carry
        lax.fori_loop(0, tm, body, 0, unroll=8)

    def take_over(ex, first):
        @pl.when(first)
        def _():
            for cp in weight_copies(ex):
                cp.start()

        for cp in weight_copies(ex):
            cp.wait()

        def cast_up(i, carry):
            r = pl.multiple_of(i * cast_rows, cast_rows)
            wg_rs[pl.ds(r, cast_rows), :] = wg_st[pl.ds(r, cast_rows), :].astype(BF16)
            wu_rs[pl.ds(r, cast_rows), :] = wu_st[pl.ds(r, cast_rows), :].astype(BF16)
            return carry

        def cast_down(i, carry):
            r = pl.multiple_of(i * cast_rows, cast_rows)
            wd_rs[pl.ds(r, cast_rows), :] = wd_st[pl.ds(r, cast_rows), :].astype(BF16)
            return carry

        lax.fori_loop(0, wg_st.shape[0] // cast_rows, cast_up, 0)
        lax.fori_loop(0, wd_st.shape[0] // cast_rows, cast_down, 0)
        nxt = nx_ref[ex]

        @pl.when(nxt >= 0)
        def _():
            for cp in weight_copies(nxt):
                cp.start(priority=1)

    def ffn(slot, ex, accumulate, row_dmas=None):
        x_lo, x_hi = (h.astype(BF16) for h in _unpack_halves(xbuf[slot]))
        hd = x_lo.shape[1]
        row = t * tm + lax.broadcasted_iota(jnp.int32, (tm, 1), 0)
        mine = jnp.logical_and(row >= st_ref[ex], row < en_ref[ex])
        d_ff = wg_rs.shape[1]
        n_chunks = ff_chunks if row_dmas is not None else 1
        fc = d_ff // n_chunks
        rows = tm // n_chunks
        contrib = None
        for c in range(n_chunks):
            cols = slice(c * fc, (c + 1) * fc)
            up = lambda w: (jnp.dot(x_lo, w[:hd, cols], preferred_element_type=F32)
                            + jnp.dot(x_hi, w[hd:, cols], preferred_element_type=F32))
            a = up(wg_rs)
            b = up(wu_rs)
            hmid = jnp.where(mine, (a * (1.0 / (1.0 + jnp.exp(-a)))) * b, 0.0).astype(BF16)
            part = jnp.dot(hmid, wd_rs[cols, :], preferred_element_type=F32)
            contrib = part if contrib is None else contrib + part
            if row_dmas is not None:
                row_dmas(c * rows, (c + 1) * rows)
        if accumulate:
            prev_lo, prev_hi = _unpack_halves(obuf[slot])
            contrib = jnp.concatenate([prev_lo, prev_hi], axis=1) + contrib
        obuf[slot] = _pack_halves(contrib)

    e0 = fe_ref[t]

    def step(p):
        q = 1 - p
        if p == 0:
            @pl.when(t == 0)
            def _():
                copies = (gidx_copy(0, 0), gidx_copy(min(1, last), 1), sidx_copy(spare_codes, 1))
                for cp in copies:
                    cp.start()
                for cp in copies:
                    cp.wait()
                obuf[1] = jnp.zeros(obuf.shape[1:], obuf.dtype)
                for_rows(lambda r: gather_row(0, r).start())

        gidx_copy(jnp.minimum(t + 2, last), p).start()
        sidx_copy(t, p).start()

        @pl.when(t >= 1)
        def _():
            gidx_copy(0, q).wait()
            sidx_copy(0, q).wait()
            wait_scatter(p)

        wait_gather(p)

        @pl.when(sw_ref[t] == 1)
        def _():
            take_over(e0, t == 0)

        def row_dmas(lo, hi):
            for r in range(lo, hi):
                scatter_row(q, r).start()
                gather_row(q, r).start()

        ffn(p, e0, accumulate=False, row_dmas=row_dmas)

    for parity in range(2):
        pl.when(lax.rem(t, 2) == parity)(functools.partial(step, parity))

    slot = lax.rem(t, 2)

    def further_owner(_, ex):
        nxt = nx_ref[ex]
        take_over(nxt, False)
        ffn(slot, nxt, accumulate=True)
        return nxt

    lax.fori_loop(0, nextra_ref[t], further_owner, e0)

    @pl.when(t == last)
    def _():
        gidx_copy(0, slot).wait()
        sidx_copy(0, slot).wait()
        for_rows(lambda r: scatter_row(slot, r).start())
        wait_gather(1 - slot)
        wait_scatter(1 - slot)
        wait_scatter(slot)


def _moe_ffn(h2, plan, w_gate, w_up, w_down, *, tm):
    n, hd = h2.shape
    d = 2 * hd
    n_slots = 2 * n
    n_tiles = n_slots // tm
    assert n_tiles >= 2 and n_slots % tm == 0
    d_ff = w_gate.shape[2]
    any_spec = pl.BlockSpec(memory_space=pl.ANY)
    grid_spec = pltpu.PrefetchScalarGridSpec(
        num_scalar_prefetch=6,
        grid=(n_tiles,),
        in_specs=[any_spec] * 5,
        out_specs=any_spec,
        scratch_shapes=[
            pltpu.SMEM((2, 1, tm), jnp.int32),
            pltpu.SMEM((2, 1, tm), jnp.int32),
            pltpu.VMEM((2, tm, hd), jnp.uint32),
            pltpu.VMEM((2, tm, hd), jnp.uint32),
            pltpu.VMEM((d, d_ff), F32),
            pltpu.VMEM((d, d_ff), F32),
            pltpu.VMEM((d_ff, d), F32),
            pltpu.VMEM((d, d_ff), BF16),
            pltpu.VMEM((d, d_ff), BF16),
            pltpu.VMEM((d_ff, d), BF16),
            pltpu.SemaphoreType.DMA((2,)),
            pltpu.SemaphoreType.DMA((2,)),
            pltpu.SemaphoreType.DMA((2,)),
            pltpu.SemaphoreType.DMA((2,)),
            pltpu.SemaphoreType.DMA((3,)),
        ],
    )
    return pl.pallas_call(
        functools.partial(_moe_body, n_tokens=n, n_tiles=n_tiles, cast_rows=128,
                          ff_chunks=d_ff // (2 * LANES)),
        grid_spec=grid_spec,
        out_shape=jax.ShapeDtypeStruct((n_slots + tm, hd), jnp.uint32),
        compiler_params=_params(("arbitrary",), vmem=MOE_VMEM_LIMIT),
        name="moe_ffn",
    )(plan["first_e"], plan["n_extra"], plan["switch"], plan["starts"], plan["ends"], plan["next_expert"],
      plan["codes"], h2, w_gate, w_up, w_down)


def _final_body(x1_ref, ys0_ref, ys1_ref, route_ref, g_ref, o_ref):
    d = x1_ref.shape[1]
    route = route_ref[...]
    g1, g2 = route[:, 2:3], route[:, 3:4]
    ya = _unpack_halves(ys0_ref[...])
    yb = _unpack_halves(ys1_ref[...])
    halves = (slice(0, d // 2), slice(d // 2, d))
    x2 = [x1_ref[:, cols] + (ya[i] * g1 + yb[i] * g2) for i, cols in enumerate(halves)]
    ms = sum(jnp.sum(h * h, axis=-1, keepdims=True) for h in x2) * (1.0 / d)
    inv = lax.rsqrt(ms + EPS)
    for h, cols in zip(x2, halves):
        o_ref[:, cols] = ((h * inv) * g_ref[:, cols]).astype(o_ref.dtype)


def _final(x1, ys, route, g, *, tm):
    n, d = x1.shape
    nb = n // tm
    return pl.pallas_call(
        _final_body,
        grid=(nb,),
        in_specs=[
            pl.BlockSpec((tm, d), lambda i: (i, 0)),
            pl.BlockSpec((tm, d // 2), lambda i: (i, 0)),
            pl.BlockSpec((tm, d // 2), lambda i: (nb + i, 0)),
            pl.BlockSpec((tm, LANES), lambda i: (i, 0)),
            pl.BlockSpec((1, d), lambda i: (0, 0)),
        ],
        out_specs=pl.BlockSpec((tm, d), lambda i: (i, 0)),
        out_shape=jax.ShapeDtypeStruct((n, d), F32),
        compiler_params=_params(("arbitrary",)),
        name="final_norm",
    )(x1, ys, ys, route, g)


def _layer(x, rel_bias, attn_norm_g, w_in, lq1, lk1, lq2, lk2, diff_subln_g, ret_gn_g, w_out, ffn_norm_g,
           w_group_router, b_group, w_inner_router, b_inner, w_gate_exp, w_up_exp, w_down_exp, *, layer,
           attn_tile, row_tile, moe_tile):
    b, s, d = x.shape
    n = b * s
    x2d = x.reshape(n, d)
    lam_init = 0.8 - 0.6 * math.exp(-0.3 * layer)

    dq_cols = (COL_DK - COL_DQ) * LANES
    col_scale = jnp.concatenate([jnp.full((1, dq_cols), DIFF_HALF ** -0.5 * LOG2E, F32),
                                 jnp.ones((1, IN_COLS - dq_cols), F32)], axis=1)
    proj = _inproj(x2d, attn_norm_g.reshape(1, d), w_in.astype(BF16), col_scale, tm=min(row_tile, n))
    proj3 = proj.reshape(b, s, IN_COLS)

    bias = _bias_tiles(rel_bias, attn_tile)
    vec = lambda a: a.reshape(1, -1)
    a_out = _diff_attention(proj3, bias, vec(lq1), vec(lk1), vec(lq2), vec(lk2), vec(diff_subln_g),
                            t=attn_tile, lam_init=lam_init)
    r_out = _retention(proj3, _retention_tables(s), vec(ret_gn_g))

    pad = LANES - N_GROUPS - N_EXPERTS
    wr = jnp.concatenate([w_group_router, jnp.transpose(w_inner_router, (1, 0, 2)).reshape(d, N_EXPERTS),
                          jnp.zeros((d, pad), F32)], axis=1).astype(BF16)
    br = jnp.concatenate([b_group, b_inner.reshape(-1), jnp.zeros((pad,), F32)]).reshape(1, LANES)
    x1, h2, route, eid = _outproj_router(a_out.reshape(n, -1), r_out.reshape(n, -1), x2d, w_out.astype(BF16),
                                         vec(ffn_norm_g), wr, br, tm=row_tile)

    plan = _moe_plan(eid.reshape(-1), moe_tile)
    ys = _moe_ffn(h2, plan, w_gate_exp, w_up_exp, w_down_exp, tm=moe_tile)
    return x1, ys, route


def kernel(x, rel_bias, attn_norm_g, w_in, lambda_q1, lambda_k1, lambda_q2, lambda_k2, diff_subln_g, ret_gn_g,
           w_out, ffn_norm_g, w_group_router, b_group, w_inner_router, b_inner, w_gate_exp, w_up_exp, w_down_exp,
           final_g):
    b, s, d = x.shape
    depth = w_in.shape[0]
    assert depth == 1, "the final norm is fused into the single layer's combine step"
    first = lambda a: a.reshape(a.shape[1:])
    x1, ys, route = _layer(x, rel_bias, attn_norm_g[0], w_in[0], lambda_q1[0], lambda_k1[0], lambda_q2[0],
                           lambda_k2[0], diff_subln_g[0], ret_gn_g[0], w_out[0], ffn_norm_g[0],
                           w_group_router[0], b_group[0], w_inner_router[0], b_inner[0], first(w_gate_exp),
                           first(w_up_exp), first(w_down_exp), layer=0, attn_tile=min(256, s), row_tile=512,
                           moe_tile=256)
    out = _final(x1, ys, route, final_g.reshape(1, d), tm=512)
    return out.reshape(b, s, d)
```

```python
import functools
import math

import numpy as np
import jax
import jax.numpy as jnp
from jax import lax
from jax.experimental import pallas as pl
from jax.experimental.pallas import tpu as pltpu

F32 = jnp.float32
BF16 = jnp.bfloat16

EPS = 1e-6
LANES = 128
N_DIFF_HEADS = 8
DIFF_HALF = 64
N_RET_HEADS = 8
RET_QK = 64
RET_CHUNK = 128
ROPE_BASE = 10000.0
N_BUCKETS = 32
MAX_DISTANCE = 128
N_GROUPS = 4
EXPERTS_PER_GROUP = 8
N_EXPERTS = N_GROUPS * EXPERTS_PER_GROUP
TOP_K = 2
MASK_VALUE = -1e30
LOG2E = math.log2(math.e)
VMEM_LIMIT = 56 * 1024 * 1024
MOE_VMEM_LIMIT = 60 * 1024 * 1024

COL_DQ, COL_DK, COL_DV = 0, 8, 16
COL_RQ, COL_RK, COL_RV, COL_RG = 24, 28, 32, 40
IN_COLS = 48 * LANES


def _params(sem, vmem=VMEM_LIMIT):
    return pltpu.CompilerParams(dimension_semantics=sem, vmem_limit_bytes=vmem)


def _pack_halves(x):
    hd = x.shape[1] // 2
    bits = lambda h: lax.bitcast_convert_type(h.astype(BF16).astype(F32), jnp.uint32)
    return lax.shift_right_logical(bits(x[:, :hd]), jnp.uint32(16)) | bits(x[:, hd:])


def _unpack_halves(w):
    return (lax.bitcast_convert_type(lax.shift_left(w, jnp.uint32(16)), F32),
            lax.bitcast_convert_type(w & jnp.uint32(0xFFFF0000), F32))


def _inproj_body(x_ref, g_ref, w_ref, cs_ref, o_ref):
    x = x_ref[...]
    acc = jnp.dot((x * g_ref[...]).astype(BF16), w_ref[...], preferred_element_type=F32)
    inv = lax.rsqrt(jnp.mean(x * x, axis=-1, keepdims=True) + EPS)
    o_ref[...] = ((acc * inv) * cs_ref[...]).astype(o_ref.dtype)


def _inproj(x2d, g, w, col_scale, *, tm):
    n, d = x2d.shape
    cols = w.shape[1]
    once = dict(pipeline_mode=pl.Buffered(1))
    return pl.pallas_call(
        _inproj_body,
        grid=(n // tm,),
        in_specs=[
            pl.BlockSpec((tm, d), lambda i: (i, 0)),
            pl.BlockSpec((1, d), lambda i: (0, 0)),
            pl.BlockSpec((d, cols), lambda i: (0, 0), **once),
            pl.BlockSpec((1, cols), lambda i: (0, 0)),
        ],
        out_specs=pl.BlockSpec((tm, cols), lambda i: (i, 0)),
        out_shape=jax.ShapeDtypeStruct((n, cols), BF16),
        compiler_params=_params(("arbitrary",)),
        name="inproj",
    )(x2d, g, w, col_scale)


def _t5_bucket_table(n):
    d = np.arange(n)
    max_exact = N_BUCKETS // 2
    log_ratio = np.log(np.maximum(d, 1).astype(np.float64) / max_exact) / math.log(MAX_DISTANCE / max_exact)
    large = np.minimum(max_exact + (log_ratio * (N_BUCKETS - max_exact)).astype(np.int64), N_BUCKETS - 1)
    return np.where(d < max_exact, d, large).astype(np.int32)


def _bias_body(rb_ref, bkt_ref, o_ref):
    h = pl.program_id(0)
    t = o_ref.shape[2]
    bkt = bkt_ref[...]
    far = rb_ref[N_BUCKETS - 1, h]
    by_dist = jnp.zeros(bkt.shape, F32)
    for b in range(N_BUCKETS - 1):
        by_dist = jnp.where(bkt == b, (rb_ref[b, h] - far) * LOG2E, by_dist)
    rows = (jnp.concatenate([jnp.full((1, t), MASK_VALUE, F32), by_dist[:, :t]], axis=1), by_dist)
    for i, row in enumerate(rows):
        shifted = pltpu.roll(jnp.broadcast_to(row, (t, 2 * t)), 0, 1, stride=1, stride_axis=0)
        o_ref[0, i] = shifted[:, t:]


def _bias_tiles(rel_bias, t):
    bkt = jnp.asarray(_t5_bucket_table(2 * t)).reshape(1, 2 * t)
    return pl.pallas_call(
        _bias_body,
        grid=(N_DIFF_HEADS,),
        in_specs=[
            pl.BlockSpec(memory_space=pltpu.SMEM),
            pl.BlockSpec((1, 2 * t), lambda h: (0, 0)),
        ],
        out_specs=pl.BlockSpec((1, 2, t, t), lambda h: (h, 0, 0, 0)),
        out_shape=jax.ShapeDtypeStruct((N_DIFF_HEADS, 2, t, t), F32),
        compiler_params=_params(("arbitrary",)),
        name="bias_tiles",
    )(rel_bias, bkt)


_NT = (((1,), (1,)), ((), ()))
_TN = (((0,), (0,)), ((), ()))


def _diffattn_body(q_ref, k_ref, v_ref, bias_ref, lq1_ref, lk1_ref, lq2_ref, lk2_ref, g_ref, o_ref, vt_ref, *,
                   t, lam_init, near_band):
    nq = k_ref.shape[1] // t
    half = t // 2
    lane = lax.broadcasted_iota(jnp.int32, (t, LANES), 1)
    lam = (jnp.exp(jnp.sum(lq1_ref[...] * lk1_ref[...], axis=-1, keepdims=True))
           - jnp.exp(jnp.sum(lq2_ref[...] * lk2_ref[...], axis=-1, keepdims=True)) + lam_init)
    vt_ref[:LANES, :] = v_ref[0].T
    vt_ref[LANES:, :] = jnp.ones((vt_ref.shape[0] - LANES, vt_ref.shape[1]), BF16)
    qmask = {}
    run_max = {}
    acc = {}

    def key_range(item):
        _, j, part = item
        return j * t + (half if part == "diag_lower" else 0), j * t + (half if part == "diag_upper" else t)

    def scores(item, c):
        qi, j, part = item
        if qi not in qmask:
            q = q_ref[0, qi * t:(qi + 1) * t, :]
            zero = jnp.zeros_like(q)
            qmask[qi] = (jnp.where(lane < DIFF_HALF, q, zero), jnp.where(lane >= DIFF_HALF, q, zero))
        k_lo, k_hi = key_range(item)
        qsel = qmask[qi][c][half:] if part == "diag_lower" else qmask[qi][c]
        s = lax.dot_general(k_ref[0, k_lo:k_hi, :], qsel, _NT, preferred_element_type=F32)
        if part == "diag_upper":
            s = s + bias_ref[0, 0, :half, :]
        elif part == "diag_lower":
            s = s + bias_ref[0, 0, half:, half:]
        elif j == qi - 1:
            if near_band >= t:
                s = s + bias_ref[0, 1]
            else:
                corner = s[t - near_band:, :near_band] + bias_ref[0, 1, t - near_band:, :near_band]
                bottom = jnp.concatenate([corner, s[t - near_band:, near_band:]], axis=1)
                s = jnp.concatenate([s[:t - near_band], bottom], axis=0)
        return s

    def softmax_step(item, c, s):
        qi, j, part = item
        smax = jnp.max(s, axis=0, keepdims=True)
        if j == 0 and part != "diag_lower":
            m_new, alpha = smax, None
            run_max[qi, c] = m_new
        else:
            m_full = run_max[qi, c]
            m_old = m_full[:, half:] if part == "diag_lower" else m_full
            m_new = jnp.maximum(m_old, smax)
            alpha = jnp.exp2(m_old - m_new)
            run_max[qi, c] = jnp.concatenate([m_full[:, :half], m_new], axis=1) if part == "diag_lower" else m_new
        return jnp.exp2(s - m_new).astype(BF16), alpha

    def values_step(item, c, p, alpha):
        qi, _, part = item
        k_lo, k_hi = key_range(item)
        r = jnp.dot(vt_ref[:, k_lo:k_hi], p, preferred_element_type=F32)
        if alpha is None:
            acc[qi, c] = r
        elif part == "diag_lower":
            acc[qi, c] = jnp.concatenate([acc[qi, c][:, :half], alpha * acc[qi, c][:, half:] + r], axis=1)
        else:
            acc[qi, c] = alpha * acc[qi, c] + r

    def finish(qi):
        outs = []
        for c in range(2):
            a = acc.pop((qi, c))
            del run_max[qi, c]
            outs.append(a[:LANES] * (1.0 / a[LANES:LANES + 1]))
        ot = (outs[0] - lam * outs[1]).T
        ms = jnp.mean(ot * ot, axis=-1, keepdims=True)
        o_ref[0, qi * t:(qi + 1) * t, :] = (
            ((ot * lax.rsqrt(ms + EPS)) * g_ref[...]) * (1.0 - lam_init)).astype(o_ref.dtype)
        del qmask[qi]

    items = []
    for qi in range(nq):
        items += [(qi, j, "full") for j in range(qi)] + [(qi, qi, "diag_upper"), (qi, qi, "diag_lower")]
    ss = [scores(items[0], c) for c in range(2)]
    pending = None
    for n, item in enumerate(items):
        nxt, cur = [], []
        for c in range(2):
            if n + 1 < len(items):
                nxt.append(scores(items[n + 1], c))
            cur.append(softmax_step(item, c, ss[c]))
            if pending is not None:
                values_step(pending[0], c, *pending[1][c])
        if pending is not None and pending[0][2] == "diag_lower":
            finish(pending[0][0])
        pending, ss = (item, cur), nxt
    for c in range(2):
        values_step(pending[0], c, *pending[1][c])
    finish(pending[0][0])


def _diff_attention(proj3, bias, lq1, lk1, lq2, lk2, subln_g, *, t, lam_init):
    b, s, _ = proj3.shape
    vec = pl.BlockSpec((1, DIFF_HALF), lambda bi, h: (0, 0))
    head = lambda col: pl.BlockSpec((1, s, LANES), lambda bi, h: (bi, 0, col + h))
    first_far = int(np.argmax(_t5_bucket_table(2 * t) == N_BUCKETS - 1))
    near_band = -(-first_far // LANES) * LANES
    ones_rows = 16
    return pl.pallas_call(
        functools.partial(_diffattn_body, t=t, lam_init=lam_init, near_band=near_band),
        grid=(b, N_DIFF_HEADS),
        in_specs=[
            head(COL_DQ), head(COL_DK), head(COL_DV),
            pl.BlockSpec((1, 2, t, t), lambda bi, h: (h, 0, 0, 0)),
            vec, vec, vec, vec,
            pl.BlockSpec((1, LANES), lambda bi, h: (0, 0)),
        ],
        out_specs=pl.BlockSpec((1, s, LANES), lambda bi, h: (bi, 0, h)),
        out_shape=jax.ShapeDtypeStruct((b, s, N_DIFF_HEADS * LANES), BF16),
        scratch_shapes=[pltpu.VMEM((LANES + ones_rows, s), BF16)],
        compiler_params=_params(("arbitrary", "arbitrary")),
        name="diff_attention",
    )(proj3, proj3, proj3, bias, lq1, lk1, lq2, lk2, subln_g)


def _retention_tables(s):
    c = RET_CHUNK
    half = RET_QK // 2
    inv_freq = ROPE_BASE ** (-jnp.arange(0, RET_QK, 2, dtype=F32) / RET_QK)
    ang = jnp.arange(s, dtype=F32)[:, None] * inv_freq[None, :]
    cos, sin = jnp.cos(ang), jnp.sin(ang)
    cos_t = jnp.tile(cos, (1, LANES // half))
    sin_t = jnp.tile(jnp.concatenate([-sin, sin], axis=1), (1, LANES // RET_QK))
    log_g = jnp.log(1.0 - jnp.exp2(-5.0 - jnp.arange(N_RET_HEADS, dtype=F32)))
    n = jnp.arange(c, dtype=F32)
    diff = n[:, None] - n[None, :]
    inner = jnp.where(diff[None] >= 0, jnp.exp(jnp.maximum(diff, 0.0)[None] * log_g[:, None, None]), 0.0)
    cross = jnp.exp((n[None] + 1.0) * log_g[:, None])
    key = jnp.exp((c - 1.0 - n[None]) * log_g[:, None])
    chunk = jnp.exp(c * log_g)
    bc = lambda a: jnp.broadcast_to(a[..., None], a.shape + (LANES,)).astype(F32)
    return cos_t, sin_t, inner.astype(F32), bc(cross), bc(key), bc(chunk[:, None])


def _retention_body(rq_ref, rk_ref, rv_ref, rg_ref, cos_ref, sin_ref, inner_ref, cross_ref, key_ref,
                    chunk_ref, gn_ref, o_ref, state_ref):
    c = RET_CHUNK
    nc = rq_ref.shape[1] // c
    nh = N_RET_HEADS
    state_ref[...] = jnp.zeros(state_ref.shape, F32)
    lane = lax.broadcasted_iota(jnp.int32, (c, LANES), 1)
    first_half = (lane % RET_QK) < (RET_QK // 2)
    head_mask = [lane < RET_QK, lane >= RET_QK]
    slab = lambda i: slice(i * LANES, (i + 1) * LANES)

    def rotary(x, cos, sin):
        rot = jnp.where(first_half, pltpu.roll(x, LANES - RET_QK // 2, 1), pltpu.roll(x, RET_QK // 2, 1))
        return x * cos + rot * sin

    def chunk_step(ci, carry):
        r = pl.multiple_of(ci * c, c)
        rows = pl.ds(r, c)
        cos = cos_ref[rows, :]
        sin = sin_ref[rows, :]
        q_pair, k_pair, kb_pair = [], [], []
        for pr in range(nh // 2):
            q_pair.append(rotary(rq_ref[0, rows, slab(pr)].astype(F32), cos, sin))
            k = rotary(rk_ref[0, rows, slab(pr)].astype(F32), cos, sin) * (RET_QK ** -0.5)
            k_pair.append(k)
            kb_pair.append(k.astype(BF16))
        qh = [jnp.where(head_mask[h % 2], q_pair[h // 2], 0.0).astype(BF16) for h in range(nh)]
        kd = [(k_pair[h // 2] * key_ref[h]).astype(BF16) for h in range(nh)]
        vh = [rv_ref[0, rows, slab(h)] for h in range(nh)]
        scores = [lax.dot_general(qh[h], kb_pair[h // 2], _NT, preferred_element_type=F32) for h in range(nh)]
        cross = [jnp.dot(qh[h], state_ref[h].astype(BF16), preferred_element_type=F32) for h in range(nh)]
        kv = [lax.dot_general(kd[h], vh[h], _TN, preferred_element_type=F32) for h in range(nh)]
        decayed = [(scores[h] * inner_ref[h]).astype(BF16) for h in range(nh)]
        for h in range(nh):
            state_ref[h] = state_ref[h] * chunk_ref[h] + kv[h]
        inner = [jnp.dot(decayed[h], vh[h], preferred_element_type=F32) for h in range(nh)]
        for h in range(nh):
            y = inner[h] + cross[h] * cross_ref[h]
            mu = jnp.mean(y, axis=-1, keepdims=True)
            yc = y - mu
            var = jnp.mean(yc * yc, axis=-1, keepdims=True)
            yn = (yc * lax.rsqrt(var + EPS)) * gn_ref[:, slab(h)]
            gate = rg_ref[0, rows, slab(h)].astype(F32)
            silu = gate * (1.0 / (1.0 + jnp.exp(-gate)))
            o_ref[0, rows, slab(h)] = (silu * yn).astype(o_ref.dtype)
        return carry

    lax.fori_loop(0, nc, chunk_step, 0, unroll=2)


def _retention(proj3, tables, gn_g):
    b, s, _ = proj3.shape
    cos_t, sin_t, inner, cross, key, chunk = tables
    c = RET_CHUNK
    nh = N_RET_HEADS
    qk_w = nh * RET_QK
    v_w = nh * LANES
    full = lambda shape: pl.BlockSpec(shape, lambda bi: (0,) * len(shape))
    return pl.pallas_call(
        _retention_body,
        grid=(b,),
        in_specs=[
            pl.BlockSpec((1, s, qk_w), lambda bi: (bi, 0, COL_RQ * LANES // qk_w)),
            pl.BlockSpec((1, s, qk_w), lambda bi: (bi, 0, COL_RK * LANES // qk_w)),
            pl.BlockSpec((1, s, v_w), lambda bi: (bi, 0, COL_RV * LANES // v_w)),
            pl.BlockSpec((1, s, v_w), lambda bi: (bi, 0, COL_RG * LANES // v_w)),
            full((s, LANES)), full((s, LANES)),
            full((nh, c, c)), full((nh, c, LANES)), full((nh, c, LANES)), full((nh, 1, LANES)),
            full((1, v_w)),
        ],
        out_specs=pl.BlockSpec((1, s, v_w), lambda bi: (bi, 0, 0)),
        out_shape=jax.ShapeDtypeStruct((b, s, v_w), BF16),
        scratch_shapes=[pltpu.VMEM((nh, LANES, LANES), F32)],
        compiler_params=_params(("arbitrary",)),
        name="retention",
    )(proj3, proj3, proj3, proj3, cos_t, sin_t, inner, cross, key, chunk, gn_g)


def _route(logits):
    lane = lax.broadcasted_iota(jnp.int32, logits.shape, 1).astype(F32)
    big = float(LANES)
    is_group = lane < N_GROUPS
    gl = jnp.where(is_group, logits, -jnp.inf)
    gmax = jnp.max(gl, axis=-1, keepdims=True)
    gidx = jnp.min(jnp.where(gl == gmax, lane, big), axis=-1, keepdims=True)
    p_group = 1.0 / jnp.sum(jnp.where(is_group, jnp.exp(gl - gmax), 0.0), axis=-1, keepdims=True)
    lo = N_GROUPS + EXPERTS_PER_GROUP * gidx
    il = jnp.where(lane >= lo, jnp.where(lane < lo + EXPERTS_PER_GROUP, logits, -jnp.inf), -jnp.inf)
    v1 = jnp.max(il, axis=-1, keepdims=True)
    i1 = jnp.min(jnp.where(il == v1, lane, big), axis=-1, keepdims=True)
    il2 = jnp.where(lane == i1, -jnp.inf, il)
    v2 = jnp.max(il2, axis=-1, keepdims=True)
    i2 = jnp.min(jnp.where(il2 == v2, lane, big), axis=-1, keepdims=True)
    e2 = jnp.exp(v2 - v1)
    inv = 1.0 / (1.0 + e2)
    g1 = p_group * inv
    g2 = p_group * (e2 * inv)
    out = jnp.where(lane == 0.0, i1 - N_GROUPS,
                    jnp.where(lane == 1.0, i2 - N_GROUPS,
                              jnp.where(lane == 2.0, g1, jnp.where(lane == 3.0, g2, 0.0))))
    return out


def _outproj_body(a_ref, r_ref, x_ref, w_ref, g_ref, wr_ref, br_ref, x1_ref, h2_ref, route_ref, eid_ref, *, sub):
    tm = x_ref.shape[0]
    half = a_ref.shape[1]
    n_sub = tm // sub

    def project(i):
        rows = slice(i * sub, (i + 1) * sub)
        acc = jnp.dot(a_ref[rows, :], w_ref[:half, :], preferred_element_type=F32)
        return acc + jnp.dot(r_ref[rows, :], w_ref[half:, :], preferred_element_type=F32)

    def epilogue(i, acc):
        rows = slice(i * sub, (i + 1) * sub)
        x1 = x_ref[rows, :] + acc
        x1_ref[rows, :] = x1
        ms = jnp.mean(x1 * x1, axis=-1, keepdims=True)
        h2 = (x1 * lax.rsqrt(ms + EPS)) * g_ref[...]
        h2_ref[rows, :] = _pack_halves(h2)
        logits = jnp.dot(h2.astype(BF16), wr_ref[...], preferred_element_type=F32) + br_ref[...]
        route = _route(logits)
        route_ref[rows, :] = route
        eid_ref[:, rows] = route.T[:TOP_K, :].astype(jnp.int32)

    acc = project(0)
    for i in range(n_sub):
        nxt = project(i + 1) if i + 1 < n_sub else None
        epilogue(i, acc)
        acc = nxt


def _outproj_router(a2d, r2d, x2d, w_out_bf16, g, wr_bf16, br, *, tm):
    n, d = x2d.shape
    half = a2d.shape[1]
    row = lambda width: pl.BlockSpec((tm, width), lambda i: (i, 0))
    full = lambda shape: pl.BlockSpec(shape, lambda i: (0, 0))
    return pl.pallas_call(
        functools.partial(_outproj_body, sub=min(tm, 2 * LANES)),
        grid=(n // tm,),
        in_specs=[row(half), row(half), row(d), full((2 * half, d)), full((1, d)), full((d, LANES)),
                  full((1, LANES))],
        out_specs=[row(d), row(d // 2), row(LANES), pl.BlockSpec((TOP_K, tm), lambda i: (0, i))],
        out_shape=[jax.ShapeDtypeStruct((n, d), F32), jax.ShapeDtypeStruct((n, d // 2), jnp.uint32),
                   jax.ShapeDtypeStruct((n, LANES), F32), jax.ShapeDtypeStruct((TOP_K, n), jnp.int32)],
        compiler_params=_params(("arbitrary",)),
        name="outproj_router",
    )(a2d, r2d, x2d, w_out_bf16, g, wr_bf16, br)


def _moe_plan(expert_id, tm):
    n_slots = expert_id.shape[0]
    n_tiles = n_slots // tm
    order = jnp.argsort(expert_id).astype(jnp.int32)
    sizes = jnp.zeros((N_EXPERTS,), jnp.int32).at[expert_id].add(1)
    ends = jnp.cumsum(sizes)
    starts = ends - sizes
    tile_row0 = jnp.arange(n_tiles, dtype=jnp.int32) * tm
    owner = lambda row: jnp.sum((ends[None, :] <= row[:, None]).astype(jnp.int32), axis=1)
    first_e = owner(tile_row0)
    last_e = owner(tile_row0 + (tm - 1))
    owners_upto = jnp.cumsum((sizes > 0).astype(jnp.int32))
    n_extra = (owners_upto[last_e] - owners_upto[first_e]).astype(jnp.int32)
    prev_last = jnp.concatenate([jnp.full((1,), -1, jnp.int32), last_e[:-1]])
    switch = (prev_last != first_e).astype(jnp.int32)
    eid = jnp.arange(N_EXPERTS, dtype=jnp.int32)
    cand = jnp.where((sizes[None, :] > 0) & (eid[None, :] > eid[:, None]), eid[None, :], N_EXPERTS)
    nxt = jnp.min(cand, axis=1)
    next_expert = jnp.where(nxt < N_EXPERTS, nxt, -1).astype(jnp.int32)
    spare = n_slots + jnp.arange(tm, dtype=jnp.int32)
    codes = jnp.concatenate([order.reshape(n_tiles, tm), spare[None, :]], axis=0).reshape(n_tiles + 1, 1, tm)
    return dict(first_e=first_e, n_extra=n_extra, switch=switch, starts=starts.astype(jnp.int32),
                ends=ends.astype(jnp.int32), next_expert=next_expert, codes=codes)


def _moe_body(fe_ref, nextra_ref, sw_ref, st_ref, en_ref, nx_ref,
              codes_hbm, h2_hbm, wg_hbm, wu_hbm, wd_hbm, ys_hbm,
              gidx, sidx, xbuf, obuf, wg_st, wu_st, wd_st, wg_rs, wu_rs, wd_rs,
              gisem, sisem, gsem, ssem, wsem, *, n_tokens, n_tiles, cast_rows, ff_chunks):
    t = pl.program_id(0)
    tm = xbuf.shape[1]
    last = n_tiles - 1
    spare_codes = n_tiles

    def gidx_copy(tile, slot):
        return pltpu.make_async_copy(codes_hbm.at[tile], gidx.at[slot], gisem.at[slot])

    def sidx_copy(tile, slot):
        return pltpu.make_async_copy(codes_hbm.at[tile], sidx.at[slot], sisem.at[slot])

    def weight_copies(ex):
        return (pltpu.make_async_copy(wg_hbm.at[ex], wg_st, wsem.at[0]),
                pltpu.make_async_copy(wu_hbm.at[ex], wu_st, wsem.at[1]),
                pltpu.make_async_copy(wd_hbm.at[ex], wd_st, wsem.at[2]))

    def gather_row(slot, r):
        code = gidx[slot, 0, r]
        tok = code - jnp.where(code >= n_tokens, n_tokens, 0)
        return pltpu.make_async_copy(h2_hbm.at[pl.ds(tok, 1)], xbuf.at[slot, pl.ds(r, 1)], gsem.at[slot])

    def scatter_row(slot, r):
        dst = sidx[slot, 0, r]
        return pltpu.make_async_copy(obuf.at[slot, pl.ds(r, 1)], ys_hbm.at[pl.ds(dst, 1)], ssem.at[slot])

    def wait_gather(slot):
        pltpu.make_async_copy(h2_hbm.at[pl.ds(0, tm)], xbuf.at[slot], gsem.at[slot]).wait()

    def wait_scatter(slot):
        pltpu.make_async_copy(obuf.at[slot], ys_hbm.at[pl.ds(0, tm)], ssem.at[slot]).wait()

    def for_rows(fn):
        def body(r, carry):
            fn(r)
            return carry
        lax.fori_loop(0, tm, body, 0, unroll=8)

    def take_over(ex, first):
        @pl.when(first)
        def _():
            for cp in weight_copies(ex):
                cp.start()

        for cp in weight_copies(ex):
            cp.wait()

        def cast_up(i, carry):
            r = pl.multiple_of(i * cast_rows, cast_rows)
            wg_rs[pl.ds(r, cast_rows), :] = wg_st[pl.ds(r, cast_rows), :].astype(BF16)
            wu_rs[pl.ds(r, cast_rows), :] = wu_st[pl.ds(r, cast_rows), :].astype(BF16)
            return carry

        def cast_down(i, carry):
            r = pl.multiple_of(i * cast_rows, cast_rows)
            wd_rs[pl.ds(r, cast_rows), :] = wd_st[pl.ds(r, cast_rows), :].astype(BF16)
            return carry

        lax.fori_loop(0, wg_st.shape[0] // cast_rows, cast_up, 0)
        lax.fori_loop(0, wd_st.shape[0] // cast_rows, cast_down, 0)
        nxt = nx_ref[ex]

        @pl.when(nxt >= 0)
        def _():
            for cp in weight_copies(nxt):
                cp.start(priority=1)

    def ffn(slot, ex, accumulate, row_dmas=None):
        x_lo, x_hi = (h.astype(BF16) for h in _unpack_halves(xbuf[slot]))
        hd = x_lo.shape[1]
        row = t * tm + lax.broadcasted_iota(jnp.int32, (tm, 1), 0)
        mine = jnp.logical_and(row >= st_ref[ex], row < en_ref[ex])
        d_ff = wg_rs.shape[1]
        n_chunks = ff_chunks if row_dmas is not None else 1
        fc = d_ff // n_chunks
        rows = tm // n_chunks
        contrib = None
        for c in range(n_chunks):
            cols = slice(c * fc, (c + 1) * fc)
            up = lambda w: (jnp.dot(x_lo, w[:hd, cols], preferred_element_type=F32)
                            + jnp.dot(x_hi, w[hd:, cols], preferred_element_type=F32))
            a = up(wg_rs)
            b = up(wu_rs)
            hmid = jnp.where(mine, (a * (1.0 / (1.0 + jnp.exp(-a)))) * b, 0.0).astype(BF16)
            part = jnp.dot(hmid, wd_rs[cols, :], preferred_element_type=F32)
            contrib = part if contrib is None else contrib + part
            if row_dmas is not None:
                row_dmas(c * rows, (c + 1) * rows)
        if accumulate:
            prev_lo, prev_hi = _unpack_halves(obuf[slot])
            contrib = jnp.concatenate([prev_lo, prev_hi], axis=1) + contrib
        obuf[slot] = _pack_halves(contrib)

    e0 = fe_ref[t]

    def step(p):
        q = 1 - p
        if p == 0:
            @pl.when(t == 0)
            def _():
                copies = (gidx_copy(0, 0), gidx_copy(min(1, last), 1), sidx_copy(spare_codes, 1))
                for cp in copies:
                    cp.start()
                for cp in copies:
                    cp.wait()
                obuf[1] = jnp.zeros(obuf.shape[1:], obuf.dtype)
                for_rows(lambda r: gather_row(0, r).start())

        gidx_copy(jnp.minimum(t + 2, last), p).start()
        sidx_copy(t, p).start()

        @pl.when(t >= 1)
        def _():
            gidx_copy(0, q).wait()
            sidx_copy(0, q).wait()
            wait_scatter(p)

        wait_gather(p)

        @pl.when(sw_ref[t] == 1)
        def _():
            take_over(e0, t == 0)

        def row_dmas(lo, hi):
            for r in range(lo, hi):
                scatter_row(q, r).start()
                gather_row(q, r).start()

        ffn(p, e0, accumulate=False, row_dmas=row_dmas)

    for parity in range(2):
        pl.when(lax.rem(t, 2) == parity)(functools.partial(step, parity))

    slot = lax.rem(t, 2)

    def further_owner(_, ex):
        nxt = nx_ref[ex]
        take_over(nxt, False)
        ffn(slot, nxt, accumulate=True)
        return nxt

    lax.fori_loop(0, nextra_ref[t], further_owner, e0)

    @pl.when(t == last)
    def _():
        gidx_copy(0, slot).wait()
        sidx_copy(0, slot).wait()
        for_rows(lambda r: scatter_row(slot, r).start())
        wait_gather(1 - slot)
        wait_scatter(1 - slot)
        wait_scatter(slot)


def _moe_ffn(h2, plan, w_gate, w_up, w_down, *, tm):
    n, hd = h2.shape
    d = 2 * hd
    n_slots = 2 * n
    n_tiles = n_slots // tm
    assert n_tiles >= 2 and n_slots % tm == 0
    d_ff = w_gate.shape[2]
    any_spec = pl.BlockSpec(memory_space=pl.ANY)
    grid_spec = pltpu.PrefetchScalarGridSpec(
        num_scalar_prefetch=6,
        grid=(n_tiles,),
        in_specs=[any_spec] * 5,
        out_specs=any_spec,
        scratch_shapes=[
            pltpu.SMEM((2, 1, tm), jnp.int32),
            pltpu.SMEM((2, 1, tm), jnp.int32),
            pltpu.VMEM((2, tm, hd), jnp.uint32),
            pltpu.VMEM((2, tm, hd), jnp.uint32),
            pltpu.VMEM((d, d_ff), F32),
            pltpu.VMEM((d, d_ff), F32),
            pltpu.VMEM((d_ff, d), F32),
            pltpu.VMEM((d, d_ff), BF16),
            pltpu.VMEM((d, d_ff), BF16),
            pltpu.VMEM((d_ff, d), BF16),
            pltpu.SemaphoreType.DMA((2,)),
            pltpu.SemaphoreType.DMA((2,)),
            pltpu.SemaphoreType.DMA((2,)),
            pltpu.SemaphoreType.DMA((2,)),
            pltpu.SemaphoreType.DMA((3,)),
        ],
    )
    return pl.pallas_call(
        functools.partial(_moe_body, n_tokens=n, n_tiles=n_tiles, cast_rows=128,
                          ff_chunks=d_ff // (2 * LANES)),
        grid_spec=grid_spec,
        out_shape=jax.ShapeDtypeStruct((n_slots + tm, hd), jnp.uint32),
        compiler_params=_params(("arbitrary",), vmem=MOE_VMEM_LIMIT),
        name="moe_ffn",
    )(plan["first_e"], plan["n_extra"], plan["switch"], plan["starts"], plan["ends"], plan["next_expert"],
      plan["codes"], h2, w_gate, w_up, w_down)


def _final_body(x1_ref, ys0_ref, ys1_ref, route_ref, g_ref, o_ref):
    d = x1_ref.shape[1]
    route = route_ref[...]
    g1, g2 = route[:, 2:3], route[:, 3:4]
    ya = _unpack_halves(ys0_ref[...])
    yb = _unpack_halves(ys1_ref[...])
    halves = (slice(0, d // 2), slice(d // 2, d))
    x2 = [x1_ref[:, cols] + (ya[i] * g1 + yb[i] * g2) for i, cols in enumerate(halves)]
    ms = sum(jnp.sum(h * h, axis=-1, keepdims=True) for h in x2) * (1.0 / d)
    inv = lax.rsqrt(ms + EPS)
    for h, cols in zip(x2, halves):
        o_ref[:, cols] = ((h * inv) * g_ref[:, cols]).astype(o_ref.dtype)


def _final(x1, ys, route, g, *, tm):
    n, d = x1.shape
    nb = n // tm
    return pl.pallas_call(
        _final_body,
        grid=(nb,),
        in_specs=[
            pl.BlockSpec((tm, d), lambda i: (i, 0)),
            pl.BlockSpec((tm, d // 2), lambda i: (i, 0)),
            pl.BlockSpec((tm, d // 2), lambda i: (nb + i, 0)),
            pl.BlockSpec((tm, LANES), lambda i: (i, 0)),
            pl.BlockSpec((1, d), lambda i: (0, 0)),
        ],
        out_specs=pl.BlockSpec((tm, d), lambda i: (i, 0)),
        out_shape=jax.ShapeDtypeStruct((n, d), F32),
        compiler_params=_params(("arbitrary",)),
        name="final_norm",
    )(x1, ys, ys, route, g)


def _layer(x, rel_bias, attn_norm_g, w_in, lq1, lk1, lq2, lk2, diff_subln_g, ret_gn_g, w_out, ffn_norm_g,
           w_group_router, b_group, w_inner_router, b_inner, w_gate_exp, w_up_exp, w_down_exp, *, layer,
           attn_tile, row_tile, moe_tile):
    b, s, d = x.shape
    n = b * s
    x2d = x.reshape(n, d)
    lam_init = 0.8 - 0.6 * math.exp(-0.3 * layer)

    dq_cols = (COL_DK - COL_DQ) * LANES
    col_scale = jnp.concatenate([jnp.full((1, dq_cols), DIFF_HALF ** -0.5 * LOG2E, F32),
                                 jnp.ones((1, IN_COLS - dq_cols), F32)], axis=1)
    proj = _inproj(x2d, attn_norm_g.reshape(1, d), w_in.astype(BF16), col_scale, tm=min(row_tile, n))
    proj3 = proj.reshape(b, s, IN_COLS)

    bias = _bias_tiles(rel_bias, attn_tile)
    vec = lambda a: a.reshape(1, -1)
    a_out = _diff_attention(proj3, bias, vec(lq1), vec(lk1), vec(lq2), vec(lk2), vec(diff_subln_g),
                            t=attn_tile, lam_init=lam_init)
    r_out = _retention(proj3, _retention_tables(s), vec(ret_gn_g))

    pad = LANES - N_GROUPS - N_EXPERTS
    wr = jnp.concatenate([w_group_router, jnp.transpose(w_inner_router, (1, 0, 2)).reshape(d, N_EXPERTS),
                          jnp.zeros((d, pad), F32)], axis=1).astype(BF16)
    br = jnp.concatenate([b_group, b_inner.reshape(-1), jnp.zeros((pad,), F32)]).reshape(1, LANES)
    x1, h2, route, eid = _outproj_router(a_out.reshape(n, -1), r_out.reshape(n, -1), x2d, w_out.astype(BF16),
                                         vec(ffn_norm_g), wr, br, tm=row_tile)

    plan = _moe_plan(eid.reshape(-1), moe_tile)
    ys = _moe_ffn(h2, plan, w_gate_exp, w_up_exp, w_down_exp, tm=moe_tile)
    return x1, ys, route


def kernel(x, rel_bias, attn_norm_g, w_in, lambda_q1, lambda_k1, lambda_q2, lambda_k2, diff_subln_g, ret_gn_g,
           w_out, ffn_norm_g, w_group_router, b_group, w_inner_router, b_inner, w_gate_exp, w_up_exp, w_down_exp,
           final_g):
    b, s, d = x.shape
    depth = w_in.shape[0]
    assert depth == 1, "the final norm is fused into the single layer's combine step"
    first = lambda a: a.reshape(a.shape[1:])
    x1, ys, route = _layer(x, rel_bias, attn_norm_g[0], w_in[0], lambda_q1[0], lambda_k1[0], lambda_q2[0],
                           lambda_k2[0], diff_subln_g[0], ret_gn_g[0], w_out[0], ffn_norm_g[0],
                           w_group_router[0], b_group[0], w_inner_router[0], b_inner[0], first(w_gate_exp),
                           first(w_up_exp), first(w_down_exp), layer=0, attn_tile=min(256, s), row_tile=512,
                           moe_tile=256)
    out = _final(x1, ys, route, final_g.reshape(1, d), tm=512)
    return out.reshape(b, s, d)
```

```python
import functools
import math

import numpy as np
import jax
import jax.numpy as jnp
from jax import lax
from jax.experimental import pallas as pl
from jax.experimental.pallas import tpu as pltpu

F32 = jnp.float32
BF16 = jnp.bfloat16

EPS = 1e-6
LANES = 128
N_DIFF_HEADS = 8
DIFF_HALF = 64
N_RET_HEADS = 8
RET_QK = 64
RET_CHUNK = 128
ROPE_BASE = 10000.0
N_BUCKETS = 32
MAX_DISTANCE = 128
N_GROUPS = 4
EXPERTS_PER_GROUP = 8
N_EXPERTS = N_GROUPS * EXPERTS_PER_GROUP
TOP_K = 2
MASK_VALUE = -1e30
LOG2E = math.log2(math.e)
VMEM_LIMIT = 56 * 1024 * 1024
MOE_VMEM_LIMIT = 60 * 1024 * 1024

COL_DQ, COL_DK, COL_DV = 0, 8, 16
COL_RQ, COL_RK, COL_RV, COL_RG = 24, 28, 32, 40
IN_COLS = 48 * LANES


def _params(sem, vmem=VMEM_LIMIT):
    return pltpu.CompilerParams(dimension_semantics=sem, vmem_limit_bytes=vmem)


def _pack_halves(x):
    hd = x.shape[1] // 2
    bits = lambda h: lax.bitcast_convert_type(h.astype(BF16).astype(F32), jnp.uint32)
    return lax.shift_right_logical(bits(x[:, :hd]), jnp.uint32(16)) | bits(x[:, hd:])


def _unpack_halves(w):
    return (lax.bitcast_convert_type(lax.shift_left(w, jnp.uint32(16)), F32),
            lax.bitcast_convert_type(w & jnp.uint32(0xFFFF0000), F32))


def _inproj_body(x_ref, g_ref, w_ref, cs_ref, o_ref):
    x = x_ref[...]
    acc = jnp.dot((x * g_ref[...]).astype(BF16), w_ref[...], preferred_element_type=F32)
    inv = lax.rsqrt(jnp.mean(x * x, axis=-1, keepdims=True) + EPS)
    o_ref[...] = ((acc * inv) * cs_ref[...]).astype(o_ref.dtype)


def _inproj(x2d, g, w, col_scale, *, tm):
    n, d = x2d.shape
    cols = w.shape[1]
    once = dict(pipeline_mode=pl.Buffered(1))
    return pl.pallas_call(
        _inproj_body,
        grid=(n // tm,),
        in_specs=[
            pl.BlockSpec((tm, d), lambda i: (i, 0)),
            pl.BlockSpec((1, d), lambda i: (0, 0)),
            pl.BlockSpec((d, cols), lambda i: (0, 0), **once),
            pl.BlockSpec((1, cols), lambda i: (0, 0)),
        ],
        out_specs=pl.BlockSpec((tm, cols), lambda i: (i, 0)),
        out_shape=jax.ShapeDtypeStruct((n, cols), BF16),
        compiler_params=_params(("arbitrary",)),
        name="inproj",
    )(x2d, g, w, col_scale)


def _t5_bucket_table(n):
    d = np.arange(n)
    max_exact = N_BUCKETS // 2
    log_ratio = np.log(np.maximum(d, 1).astype(np.float64) / max_exact) / math.log(MAX_DISTANCE / max_exact)
    large = np.minimum(max_exact + (log_ratio * (N_BUCKETS - max_exact)).astype(np.int64), N_BUCKETS - 1)
    return np.where(d < max_exact, d, large).astype(np.int32)


def _bias_body(rb_ref, bkt_ref, o_ref):
    h = pl.program_id(0)
    t = o_ref.shape[2]
    bkt = bkt_ref[...]
    far = rb_ref[N_BUCKETS - 1, h]
    by_dist = jnp.zeros(bkt.shape, F32)
    for b in range(N_BUCKETS - 1):
        by_dist = jnp.where(bkt == b, (rb_ref[b, h] - far) * LOG2E, by_dist)
    rows = (jnp.concatenate([jnp.full((1, t), MASK_VALUE, F32), by_dist[:, :t]], axis=1), by_dist)
    for i, row in enumerate(rows):
        shifted = pltpu.roll(jnp.broadcast_to(row, (t, 2 * t)), 0, 1, stride=1, stride_axis=0)
        o_ref[0, i] = shifted[:, t:]


def _bias_tiles(rel_bias, t):
    bkt = jnp.asarray(_t5_bucket_table(2 * t)).reshape(1, 2 * t)
    return pl.pallas_call(
        _bias_body,
        grid=(N_DIFF_HEADS,),
        in_specs=[
            pl.BlockSpec(memory_space=pltpu.SMEM),
            pl.BlockSpec((1, 2 * t), lambda h: (0, 0)),
        ],
        out_specs=pl.BlockSpec((1, 2, t, t), lambda h: (h, 0, 0, 0)),
        out_shape=jax.ShapeDtypeStruct((N_DIFF_HEADS, 2, t, t), F32),
        compiler_params=_params(("arbitrary",)),
        name="bias_tiles",
    )(rel_bias, bkt)


_NT = (((1,), (1,)), ((), ()))
_TN = (((0,), (0,)), ((), ()))


def _diffattn_body(q_ref, k_ref, v_ref, bias_ref, lq1_ref, lk1_ref, lq2_ref, lk2_ref, g_ref, o_ref, vt_ref, *,
                   t, lam_init, near_band):
    nq = k_ref.shape[1] // t
    half = t // 2
    lane = lax.broadcasted_iota(jnp.int32, (t, LANES), 1)
    lam = (jnp.exp(jnp.sum(lq1_ref[...] * lk1_ref[...], axis=-1, keepdims=True))
           - jnp.exp(jnp.sum(lq2_ref[...] * lk2_ref[...], axis=-1, keepdims=True)) + lam_init)
    vt_ref[:LANES, :] = v_ref[0].T
    vt_ref[LANES:, :] = jnp.ones((vt_ref.shape[0] - LANES, vt_ref.shape[1]), BF16)
    qmask = {}
    run_max = {}
    acc = {}

    def key_range(item):
        _, j, part = item
        return j * t + (half if part == "diag_lower" else 0), j * t + (half if part == "diag_upper" else t)

    def scores(item, c):
        qi, j, part = item
        if qi not in qmask:
            q = q_ref[0, qi * t:(qi + 1) * t, :]
            zero = jnp.zeros_like(q)
            qmask[qi] = (jnp.where(lane < DIFF_HALF, q, zero), jnp.where(lane >= DIFF_HALF, q, zero))
        k_lo, k_hi = key_range(item)
        qsel = qmask[qi][c][half:] if part == "diag_lower" else qmask[qi][c]
        s = lax.dot_general(k_ref[0, k_lo:k_hi, :], qsel, _NT, preferred_element_type=F32)
        if part == "diag_upper":
            s = s + bias_ref[0, 0, :half, :]
        elif part == "diag_lower":
            s = s + bias_ref[0, 0, half:, half:]
        elif j == qi - 1:
            if near_band >= t:
                s = s + bias_ref[0, 1]
            else:
                corner = s[t - near_band:, :near_band] + bias_ref[0, 1, t - near_band:, :near_band]
                bottom = jnp.concatenate([corner, s[t - near_band:, near_band:]], axis=1)
                s = jnp.concatenate([s[:t - near_band], bottom], axis=0)
        return s

    def softmax_step(item, c, s):
        qi, j, part = item
        smax = jnp.max(s, axis=0, keepdims=True)
        if j == 0 and part != "diag_lower":
            m_new, alpha = smax, None
            run_max[qi, c] = m_new
        else:
            m_full = run_max[qi, c]
            m_old = m_full[:, half:] if part == "diag_lower" else m_full
            m_new = jnp.maximum(m_old, smax)
            alpha = jnp.exp2(m_old - m_new)
            run_max[qi, c] = jnp.concatenate([m_full[:, :half], m_new], axis=1) if part == "diag_lower" else m_new
        return jnp.exp2(s - m_new).astype(BF16), alpha

    def values_step(item, c, p, alpha):
        qi, _, part = item
        k_lo, k_hi = key_range(item)
        r = jnp.dot(vt_ref[:, k_lo:k_hi], p, preferred_element_type=F32)
        if alpha is None:
            acc[qi, c] = r
        elif part == "diag_lower":
            acc[qi, c] = jnp.concatenate([acc[qi, c][:, :half], alpha * acc[qi, c][:, half:] + r], axis=1)
        else:
            acc[qi, c] = alpha * acc[qi, c] + r

    def finish(qi):
        outs = []
        for c in range(2):
            a = acc.pop((qi, c))
            del run_max[qi, c]
            outs.append(a[:LANES] * (1.0 / a[LANES:LANES + 1]))
        ot = (outs[0] - lam * outs[1]).T
        ms = jnp.mean(ot * ot, axis=-1, keepdims=True)
        o_ref[0, qi * t:(qi + 1) * t, :] = (
            ((ot * lax.rsqrt(ms + EPS)) * g_ref[...]) * (1.0 - lam_init)).astype(o_ref.dtype)
        del qmask[qi]

    items = []
    for qi in range(nq):
        items += [(qi, j, "full") for j in range(qi)] + [(qi, qi, "diag_upper"), (qi, qi, "diag_lower")]
    ss = [scores(items[0], c) for c in range(2)]
    pending = None
    for n, item in enumerate(items):
        nxt, cur = [], []
        for c in range(2):
            if n + 1 < len(items):
                nxt.append(scores(items[n + 1], c))
            cur.append(softmax_step(item, c, ss[c]))
            if pending is not None:
                values_step(pending[0], c, *pending[1][c])
        if pending is not None and pending[0][2] == "diag_lower":
            finish(pending[0][0])
        pending, ss = (item, cur), nxt
    for c in range(2):
        values_step(pending[0], c, *pending[1][c])
    finish(pending[0][0])


def _diff_attention(proj3, bias, lq1, lk1, lq2, lk2, subln_g, *, t, lam_init):
    b, s, _ = proj3.shape
    vec = pl.BlockSpec((1, DIFF_HALF), lambda bi, h: (0, 0))
    head = lambda col: pl.BlockSpec((1, s, LANES), lambda bi, h: (bi, 0, col + h))
    first_far = int(np.argmax(_t5_bucket_table(2 * t) == N_BUCKETS - 1))
    near_band = -(-first_far // LANES) * LANES
    ones_rows = 16
    return pl.pallas_call(
        functools.partial(_diffattn_body, t=t, lam_init=lam_init, near_band=near_band),
        grid=(b, N_DIFF_HEADS),
        in_specs=[
            head(COL_DQ), head(COL_DK), head(COL_DV),
            pl.BlockSpec((1, 2, t, t), lambda bi, h: (h, 0, 0, 0)),
            vec, vec, vec, vec,
            pl.BlockSpec((1, LANES), lambda bi, h: (0, 0)),
        ],
        out_specs=pl.BlockSpec((1, s, LANES), lambda bi, h: (bi, 0, h)),
        out_shape=jax.ShapeDtypeStruct((b, s, N_DIFF_HEADS * LANES), BF16),
        scratch_shapes=[pltpu.VMEM((LANES + ones_rows, s), BF16)],
        compiler_params=_params(("arbitrary", "arbitrary")),
        name="diff_attention",
    )(proj3, proj3, proj3, bias, lq1, lk1, lq2, lk2, subln_g)


def _retention_tables(s):
    c = RET_CHUNK
    half = RET_QK // 2
    inv_freq = ROPE_BASE ** (-jnp.arange(0, RET_QK, 2, dtype=F32) / RET_QK)
    ang = jnp.arange(s, dtype=F32)[:, None] * inv_freq[None, :]
    cos, sin = jnp.cos(ang), jnp.sin(ang)
    cos_t = jnp.tile(cos, (1, LANES // half))
    sin_t = jnp.tile(jnp.concatenate([-sin, sin], axis=1), (1, LANES // RET_QK))
    log_g = jnp.log(1.0 - jnp.exp2(-5.0 - jnp.arange(N_RET_HEADS, dtype=F32)))
    n = jnp.arange(c, dtype=F32)
    diff = n[:, None] - n[None, :]
    inner = jnp.where(diff[None] >= 0, jnp.exp(jnp.maximum(diff, 0.0)[None] * log_g[:, None, None]), 0.0)
    cross = jnp.exp((n[None] + 1.0) * log_g[:, None])
    key = jnp.exp((c - 1.0 - n[None]) * log_g[:, None])
    chunk = jnp.exp(c * log_g)
    bc = lambda a: jnp.broadcast_to(a[..., None], a.shape + (LANES,)).astype(F32)
    return cos_t, sin_t, inner.astype(F32), bc(cross), bc(key), bc(chunk[:, None])


def _retention_body(rq_ref, rk_ref, rv_ref, rg_ref, cos_ref, sin_ref, inner_ref, cross_ref, key_ref,
                    chunk_ref, gn_ref, o_ref, state_ref):
    c = RET_CHUNK
    nc = rq_ref.shape[1] // c
    nh = N_RET_HEADS
    state_ref[...] = jnp.zeros(state_ref.shape, F32)
    lane = lax.broadcasted_iota(jnp.int32, (c, LANES), 1)
    first_half = (lane % RET_QK) < (RET_QK // 2)
    head_mask = [lane < RET_QK, lane >= RET_QK]
    slab = lambda i: slice(i * LANES, (i + 1) * LANES)

    def rotary(x, cos, sin):
        rot = jnp.where(first_half, pltpu.roll(x, LANES - RET_QK // 2, 1), pltpu.roll(x, RET_QK // 2, 1))
        return x * cos + rot * sin

    def chunk_step(ci, carry):
        r = pl.multiple_of(ci * c, c)
        rows = pl.ds(r, c)
        cos = cos_ref[rows, :]
        sin = sin_ref[rows, :]
        q_pair, k_pair, kb_pair = [], [], []
        for pr in range(nh // 2):
            q_pair.append(rotary(rq_ref[0, rows, slab(pr)].astype(F32), cos, sin))
            k = rotary(rk_ref[0, rows, slab(pr)].astype(F32), cos, sin) * (RET_QK ** -0.5)
            k_pair.append(k)
            kb_pair.append(k.astype(BF16))
        qh = [jnp.where(head_mask[h % 2], q_pair[h // 2], 0.0).astype(BF16) for h in range(nh)]
        kd = [(k_pair[h // 2] * key_ref[h]).astype(BF16) for h in range(nh)]
        vh = [rv_ref[0, rows, slab(h)] for h in range(nh)]
        scores = [lax.dot_general(qh[h], kb_pair[h // 2], _NT, preferred_element_type=F32) for h in range(nh)]
        cross = [jnp.dot(qh[h], state_ref[h].astype(BF16), preferred_element_type=F32) for h in range(nh)]
        kv = [lax.dot_general(kd[h], vh[h], _TN, preferred_element_type=F32) for h in range(nh)]
        decayed = [(scores[h] * inner_ref[h]).astype(BF16) for h in range(nh)]
        for h in range(nh):
            state_ref[h] = state_ref[h] * chunk_ref[h] + kv[h]
        inner = [jnp.dot(decayed[h], vh[h], preferred_element_type=F32) for h in range(nh)]
        for h in range(nh):
            y = inner[h] + cross[h] * cross_ref[h]
            mu = jnp.mean(y, axis=-1, keepdims=True)
            yc = y - mu
            var = jnp.mean(yc * yc, axis=-1, keepdims=True)
            yn = (yc * lax.rsqrt(var + EPS)) * gn_ref[:, slab(h)]
            gate = rg_ref[0, rows, slab(h)].astype(F32)
            silu = gate * (1.0 / (1.0 + jnp.exp(-gate)))
            o_ref[0, rows, slab(h)] = (silu * yn).astype(o_ref.dtype)
        return carry

    lax.fori_loop(0, nc, chunk_step, 0, unroll=2)


def _retention(proj3, tables, gn_g):
    b, s, _ = proj3.shape
    cos_t, sin_t, inner, cross, key, chunk = tables
    c = RET_CHUNK
    nh = N_RET_HEADS
    qk_w = nh * RET_QK
    v_w = nh * LANES
    full = lambda shape: pl.BlockSpec(shape, lambda bi: (0,) * len(shape))
    return pl.pallas_call(
        _retention_body,
        grid=(b,),
        in_specs=[
            pl.BlockSpec((1, s, qk_w), lambda bi: (bi, 0, COL_RQ * LANES // qk_w)),
            pl.BlockSpec((1, s, qk_w), lambda bi: (bi, 0, COL_RK * LANES // qk_w)),
            pl.BlockSpec((1, s, v_w), lambda bi: (bi, 0, COL_RV * LANES // v_w)),
            pl.BlockSpec((1, s, v_w), lambda bi: (bi, 0, COL_RG * LANES // v_w)),
            full((s, LANES)), full((s, LANES)),
            full((nh, c, c)), full((nh, c, LANES)), full((nh, c, LANES)), full((nh, 1, LANES)),
            full((1, v_w)),
        ],
        out_specs=pl.BlockSpec((1, s, v_w), lambda bi: (bi, 0, 0)),
        out_shape=jax.ShapeDtypeStruct((b, s, v_w), BF16),
        scratch_shapes=[pltpu.VMEM((nh, LANES, LANES), F32)],
        compiler_params=_params(("arbitrary",)),
        name="retention",
    )(proj3, proj3, proj3, proj3, cos_t, sin_t, inner, cross, key, chunk, gn_g)


def _route(logits):
    lane = lax.broadcasted_iota(jnp.int32, logits.shape, 1).astype(F32)
    big = float(LANES)
    is_group = lane < N_GROUPS
    gl = jnp.where(is_group, logits, -jnp.inf)
    gmax = jnp.max(gl, axis=-1, keepdims=True)
    gidx = jnp.min(jnp.where(gl == gmax, lane, big), axis=-1, keepdims=True)
    p_group = 1.0 / jnp.sum(jnp.where(is_group, jnp.exp(gl - gmax), 0.0), axis=-1, keepdims=True)
    lo = N_GROUPS + EXPERTS_PER_GROUP * gidx
    il = jnp.where(lane >= lo, jnp.where(lane < lo + EXPERTS_PER_GROUP, logits, -jnp.inf), -jnp.inf)
    v1 = jnp.max(il, axis=-1, keepdims=True)
    i1 = jnp.min(jnp.where(il == v1, lane, big), axis=-1, keepdims=True)
    il2 = jnp.where(lane == i1, -jnp.inf, il)
    v2 = jnp.max(il2, axis=-1, keepdims=True)
    i2 = jnp.min(jnp.where(il2 == v2, lane, big), axis=-1, keepdims=True)
    e2 = jnp.exp(v2 - v1)
    inv = 1.0 / (1.0 + e2)
    g1 = p_group * inv
    g2 = p_group * (e2 * inv)
    out = jnp.where(lane == 0.0, i1 - N_GROUPS,
                    jnp.where(lane == 1.0, i2 - N_GROUPS,
                              jnp.where(lane == 2.0, g1, jnp.where(lane == 3.0, g2, 0.0))))
    return out


def _outproj_body(a_ref, r_ref, x_ref, w_ref, g_ref, wr_ref, br_ref, x1_ref, h2_ref, route_ref, eid_ref, *, sub):
    tm = x_ref.shape[0]
    half = a_ref.shape[1]
    n_sub = tm // sub

    def project(i):
        rows = slice(i * sub, (i + 1) * sub)
        acc = jnp.dot(a_ref[rows, :], w_ref[:half, :], preferred_element_type=F32)
        acc = acc + jnp.dot(r_ref[rows, :], w_ref[half:, :], preferred_element_type=F32)
        x1_ref[rows, :] = x_ref[rows, :] + acc

    def epilogue(i):
        rows = slice(i * sub, (i + 1) * sub)
        x1 = x1_ref[rows, :]
        ms = jnp.mean(x1 * x1, axis=-1, keepdims=True)
        h2 = (x1 * lax.rsqrt(ms + EPS)) * g_ref[...]
        h2_ref[rows, :] = _pack_halves(h2)
        logits = jnp.dot(h2.astype(BF16), wr_ref[...], preferred_element_type=F32) + br_ref[...]
        route = _route(logits)
        route_ref[rows, :] = route
        eid_ref[:, rows] = route.T[:TOP_K, :].astype(jnp.int32)

    project(0)
    for i in range(n_sub):
        if i + 1 < n_sub:
            project(i + 1)
        epilogue(i)


def _outproj_router(a2d, r2d, x2d, w_out_bf16, g, wr_bf16, br, *, tm):
    n, d = x2d.shape
    half = a2d.shape[1]
    row = lambda width: pl.BlockSpec((tm, width), lambda i: (i, 0))
    full = lambda shape: pl.BlockSpec(shape, lambda i: (0, 0))
    return pl.pallas_call(
        functools.partial(_outproj_body, sub=min(tm, 2 * LANES)),
        grid=(n // tm,),
        in_specs=[row(half), row(half), row(d), full((2 * half, d)), full((1, d)), full((d, LANES)),
                  full((1, LANES))],
        out_specs=[row(d), row(d // 2), row(LANES), pl.BlockSpec((TOP_K, tm), lambda i: (0, i))],
        out_shape=[jax.ShapeDtypeStruct((n, d), F32), jax.ShapeDtypeStruct((n, d // 2), jnp.uint32),
                   jax.ShapeDtypeStruct((n, LANES), F32), jax.ShapeDtypeStruct((TOP_K, n), jnp.int32)],
        compiler_params=_params(("arbitrary",)),
        name="outproj_router",
    )(a2d, r2d, x2d, w_out_bf16, g, wr_bf16, br)


def _moe_plan(expert_id, tm):
    n_slots = expert_id.shape[0]
    n_tiles = n_slots // tm
    order = jnp.argsort(expert_id).astype(jnp.int32)
    sizes = jnp.zeros((N_EXPERTS,), jnp.int32).at[expert_id].add(1)
    ends = jnp.cumsum(sizes)
    starts = ends - sizes
    tile_row0 = jnp.arange(n_tiles, dtype=jnp.int32) * tm
    owner = lambda row: jnp.sum((ends[None, :] <= row[:, None]).astype(jnp.int32), axis=1)
    first_e = owner(tile_row0)
    last_e = owner(tile_row0 + (tm - 1))
    owners_upto = jnp.cumsum((sizes > 0).astype(jnp.int32))
    n_extra = (owners_upto[last_e] - owners_upto[first_e]).astype(jnp.int32)
    prev_last = jnp.concatenate([jnp.full((1,), -1, jnp.int32), last_e[:-1]])
    switch = (prev_last != first_e).astype(jnp.int32)
    eid = jnp.arange(N_EXPERTS, dtype=jnp.int32)
    cand = jnp.where((sizes[None, :] > 0) & (eid[None, :] > eid[:, None]), eid[None, :], N_EXPERTS)
    nxt = jnp.min(cand, axis=1)
    next_expert = jnp.where(nxt < N_EXPERTS, nxt, -1).astype(jnp.int32)
    spare = n_slots + jnp.arange(tm, dtype=jnp.int32)
    codes = jnp.concatenate([order.reshape(n_tiles, tm), spare[None, :]], axis=0).reshape(n_tiles + 1, 1, tm)
    return dict(first_e=first_e, n_extra=n_extra, switch=switch, starts=starts.astype(jnp.int32),
                ends=ends.astype(jnp.int32), next_expert=next_expert, codes=codes)


def _moe_body(fe_ref, nextra_ref, sw_ref, st_ref, en_ref, nx_ref,
              codes_hbm, h2_hbm, wg_hbm, wu_hbm, wd_hbm, ys_hbm,
              gidx, sidx, xbuf, obuf, wg_st, wu_st, wd_st, wg_rs, wu_rs, wd_rs,
              gisem, sisem, gsem, ssem, wsem, *, n_tokens, n_tiles, cast_rows, ff_chunks):
    t = pl.program_id(0)
    tm = xbuf.shape[1]
    last = n_tiles - 1
    spare_codes = n_tiles

    def gidx_copy(tile, slot):
        return pltpu.make_async_copy(codes_hbm.at[tile], gidx.at[slot], gisem.at[slot])

    def sidx_copy(tile, slot):
        return pltpu.make_async_copy(codes_hbm.at[tile], sidx.at[slot], sisem.at[slot])

    def weight_copies(ex):
        return (pltpu.make_async_copy(wg_hbm.at[ex], wg_st, wsem.at[0]),
                pltpu.make_async_copy(wu_hbm.at[ex], wu_st, wsem.at[1]),
                pltpu.make_async_copy(wd_hbm.at[ex], wd_st, wsem.at[2]))

    def gather_row(slot, r):
        code = gidx[slot, 0, r]
        tok = code - jnp.where(code >= n_tokens, n_tokens, 0)
        return pltpu.make_async_copy(h2_hbm.at[pl.ds(tok, 1)], xbuf.at[slot, pl.ds(r, 1)], gsem.at[slot])

    def scatter_row(slot, r):
        dst = sidx[slot, 0, r]
        return pltpu.make_async_copy(obuf.at[slot, pl.ds(r, 1)], ys_hbm.at[pl.ds(dst, 1)], ssem.at[slot])

    def wait_gather(slot):
        pltpu.make_async_copy(h2_hbm.at[pl.ds(0, tm)], xbuf.at[slot], gsem.at[slot]).wait()

    def wait_scatter(slot):
        pltpu.make_async_copy(obuf.at[slot], ys_hbm.at[pl.ds(0, tm)], ssem.at[slot]).wait()

    def for_rows(fn):
        def body(r, carry):
            fn(r)
            return carry
        lax.fori_loop(0, tm, body, 0, unroll=8)

    def take_over(ex, first):
        @pl.when(first)
        def _():
            for cp in weight_copies(ex):
                cp.start()

        for cp in weight_copies(ex):
            cp.wait()

        def cast_up(i, carry):
            r = pl.multiple_of(i * cast_rows, cast_rows)
            wg_rs[pl.ds(r, cast_rows), :] = wg_st[pl.ds(r, cast_rows), :].astype(BF16)
            wu_rs[pl.ds(r, cast_rows), :] = wu_st[pl.ds(r, cast_rows), :].astype(BF16)
            return carry

        def cast_down(i, carry):
            r = pl.multiple_of(i * cast_rows, cast_rows)
            wd_rs[pl.ds(r, cast_rows), :] = wd_st[pl.ds(r, cast_rows), :].astype(BF16)
            return carry

        lax.fori_loop(0, wg_st.shape[0] // cast_rows, cast_up, 0)
        lax.fori_loop(0, wd_st.shape[0] // cast_rows, cast_down, 0)
        nxt = nx_ref[ex]

        @pl.when(nxt >= 0)
        def _():
            for cp in weight_copies(nxt):
                cp.start(priority=1)

    def ffn(slot, ex, accumulate, row_dmas=None):
        x_lo, x_hi = (h.astype(BF16) for h in _unpack_halves(xbuf[slot]))
        hd = x_lo.shape[1]
        row = t * tm + lax.broadcasted_iota(jnp.int32, (tm, 1), 0)
        mine = jnp.logical_and(row >= st_ref[ex], row < en_ref[ex])
        d_ff = wg_rs.shape[1]
        n_chunks = ff_chunks if row_dmas is not None else 1
        fc = d_ff // n_chunks
        rows = tm // n_chunks
        contrib = None
        for c in range(n_chunks):
            cols = slice(c * fc, (c + 1) * fc)
            up = lambda w: (jnp.dot(x_lo, w[:hd, cols], preferred_element_type=F32)
                            + jnp.dot(x_hi, w[hd:, cols], preferred_element_type=F32))
            a = up(wg_rs)
            b = up(wu_rs)
            hmid = jnp.where(mine, (a * (1.0 / (1.0 + jnp.exp(-a)))) * b, 0.0).astype(BF16)
            part = jnp.dot(hmid, wd_rs[cols, :], preferred_element_type=F32)
            contrib = part if contrib is None else contrib + part
            if row_dmas is not None:
                row_dmas(c * rows, (c + 1) * rows)
        if accumulate:
            prev_lo, prev_hi = _unpack_halves(obuf[slot])
            contrib = jnp.concatenate([prev_lo, prev_hi], axis=1) + contrib
        obuf[slot] = _pack_halves(contrib)

    e0 = fe_ref[t]

    def step(p):
        q = 1 - p
        if p == 0:
            @pl.when(t == 0)
            def _():
                copies = (gidx_copy(0, 0), gidx_copy(min(1, last), 1), sidx_copy(spare_codes, 1))
                for cp in copies:
                    cp.start()
                for cp in copies:
                    cp.wait()
                obuf[1] = jnp.zeros(obuf.shape[1:], obuf.dtype)
                for_rows(lambda r: gather_row(0, r).start())

        gidx_copy(jnp.minimum(t + 2, last), p).start()
        sidx_copy(t, p).start()

        @pl.when(t >= 1)
        def _():
            gidx_copy(0, q).wait()
            sidx_copy(0, q).wait()
            wait_scatter(p)

        wait_gather(p)

        @pl.when(sw_ref[t] == 1)
        def _():
            take_over(e0, t == 0)

        def row_dmas(lo, hi):
            for r in range(lo, hi):
                scatter_row(q, r).start()
                gather_row(q, r).start()

        ffn(p, e0, accumulate=False, row_dmas=row_dmas)

    for parity in range(2):
        pl.when(lax.rem(t, 2) == parity)(functools.partial(step, parity))

    slot = lax.rem(t, 2)

    def further_owner(_, ex):
        nxt = nx_ref[ex]
        take_over(nxt, False)
        ffn(slot, nxt, accumulate=True)
        return nxt

    lax.fori_loop(0, nextra_ref[t], further_owner, e0)

    @pl.when(t == last)
    def _():
        gidx_copy(0, slot).wait()
        sidx_copy(0, slot).wait()
        for_rows(lambda r: scatter_row(slot, r).start())
        wait_gather(1 - slot)
        wait_scatter(1 - slot)
        wait_scatter(slot)


def _moe_ffn(h2, plan, w_gate, w_up, w_down, *, tm):
    n, hd = h2.shape
    d = 2 * hd
    n_slots = 2 * n
    n_tiles = n_slots // tm
    assert n_tiles >= 2 and n_slots % tm == 0
    d_ff = w_gate.shape[2]
    any_spec = pl.BlockSpec(memory_space=pl.ANY)
    grid_spec = pltpu.PrefetchScalarGridSpec(
        num_scalar_prefetch=6,
        grid=(n_tiles,),
        in_specs=[any_spec] * 5,
        out_specs=any_spec,
        scratch_shapes=[
            pltpu.SMEM((2, 1, tm), jnp.int32),
            pltpu.SMEM((2, 1, tm), jnp.int32),
            pltpu.VMEM((2, tm, hd), jnp.uint32),
            pltpu.VMEM((2, tm, hd), jnp.uint32),
            pltpu.VMEM((d, d_ff), F32),
            pltpu.VMEM((d, d_ff), F32),
            pltpu.VMEM((d_ff, d), F32),
            pltpu.VMEM((d, d_ff), BF16),
            pltpu.VMEM((d, d_ff), BF16),
            pltpu.VMEM((d_ff, d), BF16),
            pltpu.SemaphoreType.DMA((2,)),
            pltpu.SemaphoreType.DMA((2,)),
            pltpu.SemaphoreType.DMA((2,)),
            pltpu.SemaphoreType.DMA((2,)),
            pltpu.SemaphoreType.DMA((3,)),
        ],
    )
    return pl.pallas_call(
        functools.partial(_moe_body, n_tokens=n, n_tiles=n_tiles, cast_rows=128,
                          ff_chunks=d_ff // (2 * LANES)),
        grid_spec=grid_spec,
        out_shape=jax.ShapeDtypeStruct((n_slots + tm, hd), jnp.uint32),
        compiler_params=_params(("arbitrary",), vmem=MOE_VMEM_LIMIT),
        name="moe_ffn",
    )(plan["first_e"], plan["n_extra"], plan["switch"], plan["starts"], plan["ends"], plan["next_expert"],
      plan["codes"], h2, w_gate, w_up, w_down)


def _final_body(x1_ref, ys0_ref, ys1_ref, route_ref, g_ref, o_ref):
    d = x1_ref.shape[1]
    route = route_ref[...]
    g1, g2 = route[:, 2:3], route[:, 3:4]
    ya = _unpack_halves(ys0_ref[...])
    yb = _unpack_halves(ys1_ref[...])
    halves = (slice(0, d // 2), slice(d // 2, d))
    x2 = [x1_ref[:, cols] + (ya[i] * g1 + yb[i] * g2) for i, cols in enumerate(halves)]
    ms = sum(jnp.sum(h * h, axis=-1, keepdims=True) for h in x2) * (1.0 / d)
    inv = lax.rsqrt(ms + EPS)
    for h, cols in zip(x2, halves):
        o_ref[:, cols] = ((h * inv) * g_ref[:, cols]).astype(o_ref.dtype)


def _final(x1, ys, route, g, *, tm):
    n, d = x1.shape
    nb = n // tm
    return pl.pallas_call(
        _final_body,
        grid=(nb,),
        in_specs=[
            pl.BlockSpec((tm, d), lambda i: (i, 0)),
            pl.BlockSpec((tm, d // 2), lambda i: (i, 0)),
            pl.BlockSpec((tm, d // 2), lambda i: (nb + i, 0)),
            pl.BlockSpec((tm, LANES), lambda i: (i, 0)),
            pl.BlockSpec((1, d), lambda i: (0, 0)),
        ],
        out_specs=pl.BlockSpec((tm, d), lambda i: (i, 0)),
        out_shape=jax.ShapeDtypeStruct((n, d), F32),
        compiler_params=_params(("arbitrary",)),
        name="final_norm",
    )(x1, ys, ys, route, g)


def _layer(x, rel_bias, attn_norm_g, w_in, lq1, lk1, lq2, lk2, diff_subln_g, ret_gn_g, w_out, ffn_norm_g,
           w_group_router, b_group, w_inner_router, b_inner, w_gate_exp, w_up_exp, w_down_exp, *, layer,
           attn_tile, row_tile, moe_tile):
    b, s, d = x.shape
    n = b * s
    x2d = x.reshape(n, d)
    lam_init = 0.8 - 0.6 * math.exp(-0.3 * layer)

    dq_cols = (COL_DK - COL_DQ) * LANES
    col_scale = jnp.concatenate([jnp.full((1, dq_cols), DIFF_HALF ** -0.5 * LOG2E, F32),
                                 jnp.ones((1, IN_COLS - dq_cols), F32)], axis=1)
    proj = _inproj(x2d, attn_norm_g.reshape(1, d), w_in.astype(BF16), col_scale, tm=min(row_tile, n))
    proj3 = proj.reshape(b, s, IN_COLS)

    bias = _bias_tiles(rel_bias, attn_tile)
    vec = lambda a: a.reshape(1, -1)
    a_out = _diff_attention(proj3, bias, vec(lq1), vec(lk1), vec(lq2), vec(lk2), vec(diff_subln_g),
                            t=attn_tile, lam_init=lam_init)
    r_out = _retention(proj3, _retention_tables(s), vec(ret_gn_g))

    pad = LANES - N_GROUPS - N_EXPERTS
    wr = jnp.concatenate([w_group_router, jnp.transpose(w_inner_router, (1, 0, 2)).reshape(d, N_EXPERTS),
                          jnp.zeros((d, pad), F32)], axis=1).astype(BF16)
    br = jnp.concatenate([b_group, b_inner.reshape(-1), jnp.zeros((pad,), F32)]).reshape(1, LANES)
    x1, h2, route, eid = _outproj_router(a_out.reshape(n, -1), r_out.reshape(n, -1), x2d, w_out.astype(BF16),
                                         vec(ffn_norm_g), wr, br, tm=row_tile)

    plan = _moe_plan(eid.reshape(-1), moe_tile)
    ys = _moe_ffn(h2, plan, w_gate_exp, w_up_exp, w_down_exp, tm=moe_tile)
    return x1, ys, route


def kernel(x, rel_bias, attn_norm_g, w_in, lambda_q1, lambda_k1, lambda_q2, lambda_k2, diff_subln_g, ret_gn_g,
           w_out, ffn_norm_g, w_group_router, b_group, w_inner_router, b_inner, w_gate_exp, w_up_exp, w_down_exp,
           final_g):
    b, s, d = x.shape
    depth = w_in.shape[0]
    assert depth == 1, "the final norm is fused into the single layer's combine step"
    first = lambda a: a.reshape(a.shape[1:])
    x1, ys, route = _layer(x, rel_bias, attn_norm_g[0], w_in[0], lambda_q1[0], lambda_k1[0], lambda_q2[0],
                           lambda_k2[0], diff_subln_g[0], ret_gn_g[0], w_out[0], ffn_norm_g[0],
                           w_group_router[0], b_group[0], w_inner_router[0], b_inner[0], first(w_gate_exp),
                           first(w_up_exp), first(w_down_exp), layer=0, attn_tile=min(256, s), row_tile=512,
                           moe_tile=256)
    out = _final(x1, ys, route, final_g.reshape(1, d), tm=512)
    return out.reshape(b, s, d)
```

```python
import functools
import math

import numpy as np
import jax
import jax.numpy as jnp
from jax import lax
from jax.experimental import pallas as pl
from jax.experimental.pallas import tpu as pltpu

F32 = jnp.float32
BF16 = jnp.bfloat16

EPS = 1e-6
LANES = 128
N_DIFF_HEADS = 8
DIFF_HALF = 64
N_RET_HEADS = 8
RET_QK = 64
RET_CHUNK = 128
ROPE_BASE = 10000.0
N_BUCKETS = 32
MAX_DISTANCE = 128
N_GROUPS = 4
EXPERTS_PER_GROUP = 8
N_EXPERTS = N_GROUPS * EXPERTS_PER_GROUP
TOP_K = 2
MASK_VALUE = -1e30
LOG2E = math.log2(math.e)
VMEM_LIMIT = 56 * 1024 * 1024
MOE_VMEM_LIMIT = 60 * 1024 * 1024

COL_DQ, COL_DK, COL_DV = 0, 8, 16
COL_RQ, COL_RK, COL_RV, COL_RG = 24, 28, 32, 40
IN_COLS = 48 * LANES


def _params(sem, vmem=VMEM_LIMIT):
    return pltpu.CompilerParams(dimension_semantics=sem, vmem_limit_bytes=vmem)


def _pack_halves(x):
    hd = x.shape[1] // 2
    bits = lambda h: lax.bitcast_convert_type(h.astype(BF16).astype(F32), jnp.uint32)
    return lax.shift_right_logical(bits(x[:, :hd]), jnp.uint32(16)) | bits(x[:, hd:])


def _unpack_halves(w):
    return (lax.bitcast_convert_type(lax.shift_left(w, jnp.uint32(16)), F32),
            lax.bitcast_convert_type(w & jnp.uint32(0xFFFF0000), F32))


def _inproj_body(x_ref, g_ref, w_ref, cs_ref, o_ref):
    x = x_ref[...]
    acc = jnp.dot((x * g_ref[...]).astype(BF16), w_ref[...], preferred_element_type=F32)
    inv = lax.rsqrt(jnp.mean(x * x, axis=-1, keepdims=True) + EPS)
    o_ref[...] = ((acc * inv) * cs_ref[...]).astype(o_ref.dtype)


def _inproj(x2d, g, w, col_scale, *, tm):
    n, d = x2d.shape
    cols = w.shape[1]
    once = dict(pipeline_mode=pl.Buffered(1))
    return pl.pallas_call(
        _inproj_body,
        grid=(n // tm,),
        in_specs=[
            pl.BlockSpec((tm, d), lambda i: (i, 0)),
            pl.BlockSpec((1, d), lambda i: (0, 0)),
            pl.BlockSpec((d, cols), lambda i: (0, 0), **once),
            pl.BlockSpec((1, cols), lambda i: (0, 0)),
        ],
        out_specs=pl.BlockSpec((tm, cols), lambda i: (i, 0)),
        out_shape=jax.ShapeDtypeStruct((n, cols), BF16),
        compiler_params=_params(("arbitrary",)),
        name="inproj",
    )(x2d, g, w, col_scale)


def _t5_bucket_table(n):
    d = np.arange(n)
    max_exact = N_BUCKETS // 2
    log_ratio = np.log(np.maximum(d, 1).astype(np.float64) / max_exact) / math.log(MAX_DISTANCE / max_exact)
    large = np.minimum(max_exact + (log_ratio * (N_BUCKETS - max_exact)).astype(np.int64), N_BUCKETS - 1)
    return np.where(d < max_exact, d, large).astype(np.int32)


def _bias_body(rb_ref, bkt_ref, o_ref):
    h = pl.program_id(0)
    t = o_ref.shape[2]
    bkt = bkt_ref[...]
    far = rb_ref[N_BUCKETS - 1, h]
    by_dist = jnp.zeros(bkt.shape, F32)
    for b in range(N_BUCKETS - 1):
        by_dist = jnp.where(bkt == b, (rb_ref[b, h] - far) * LOG2E, by_dist)
    rows = (jnp.concatenate([jnp.full((1, t), MASK_VALUE, F32), by_dist[:, :t]], axis=1), by_dist)
    for i, row in enumerate(rows):
        shifted = pltpu.roll(jnp.broadcast_to(row, (t, 2 * t)), 0, 1, stride=1, stride_axis=0)
        o_ref[0, i] = shifted[:, t:]


def _bias_tiles(rel_bias, t):
    bkt = jnp.asarray(_t5_bucket_table(2 * t)).reshape(1, 2 * t)
    return pl.pallas_call(
        _bias_body,
        grid=(N_DIFF_HEADS,),
        in_specs=[
            pl.BlockSpec(memory_space=pltpu.SMEM),
            pl.BlockSpec((1, 2 * t), lambda h: (0, 0)),
        ],
        out_specs=pl.BlockSpec((1, 2, t, t), lambda h: (h, 0, 0, 0)),
        out_shape=jax.ShapeDtypeStruct((N_DIFF_HEADS, 2, t, t), F32),
        compiler_params=_params(("arbitrary",)),
        name="bias_tiles",
    )(rel_bias, bkt)


_NT = (((1,), (1,)), ((), ()))
_TN = (((0,), (0,)), ((), ()))


def _diffattn_body(q_ref, k_ref, v_ref, bias_ref, lq1_ref, lk1_ref, lq2_ref, lk2_ref, g_ref, o_ref, vt_ref, *,
                   t, lam_init, near_band):
    nq = k_ref.shape[1] // t
    half = t // 2
    lane = lax.broadcasted_iota(jnp.int32, (t, LANES), 1)
    lam = (jnp.exp(jnp.sum(lq1_ref[...] * lk1_ref[...], axis=-1, keepdims=True))
           - jnp.exp(jnp.sum(lq2_ref[...] * lk2_ref[...], axis=-1, keepdims=True)) + lam_init)
    vt_ref[:LANES, :] = v_ref[0].T
    vt_ref[LANES:, :] = jnp.ones((vt_ref.shape[0] - LANES, vt_ref.shape[1]), BF16)
    qmask = {}
    run_max = {}
    acc = {}

    def key_range(item):
        _, j, part = item
        return j * t + (half if part == "diag_lower" else 0), j * t + (half if part == "diag_upper" else t)

    def scores(item, c):
        qi, j, part = item
        if qi not in qmask:
            q = q_ref[0, qi * t:(qi + 1) * t, :]
            zero = jnp.zeros_like(q)
            qmask[qi] = (jnp.where(lane < DIFF_HALF, q, zero), jnp.where(lane >= DIFF_HALF, q, zero))
        k_lo, k_hi = key_range(item)
        qsel = qmask[qi][c][half:] if part == "diag_lower" else qmask[qi][c]
        s = lax.dot_general(k_ref[0, k_lo:k_hi, :], qsel, _NT, preferred_element_type=F32)
        if part == "diag_upper":
            s = s + bias_ref[0, 0, :half, :]
        elif part == "diag_lower":
            s = s + bias_ref[0, 0, half:, half:]
        elif j == qi - 1:
            if near_band >= t:
                s = s + bias_ref[0, 1]
            else:
                corner = s[t - near_band:, :near_band] + bias_ref[0, 1, t - near_band:, :near_band]
                bottom = jnp.concatenate([corner, s[t - near_band:, near_band:]], axis=1)
                s = jnp.concatenate([s[:t - near_band], bottom], axis=0)
        return s

    def softmax_step(item, c, s):
        qi, j, part = item
        smax = jnp.max(s, axis=0, keepdims=True)
        if j == 0 and part != "diag_lower":
            m_new, alpha = smax, None
            run_max[qi, c] = m_new
        else:
            m_full = run_max[qi, c]
            m_old = m_full[:, half:] if part == "diag_lower" else m_full
            m_new = jnp.maximum(m_old, smax)
            alpha = jnp.exp2(m_old - m_new)
            run_max[qi, c] = jnp.concatenate([m_full[:, :half], m_new], axis=1) if part == "diag_lower" else m_new
        return jnp.exp2(s - m_new).astype(BF16), alpha

    def values_step(item, c, p, alpha):
        qi, _, part = item
        k_lo, k_hi = key_range(item)
        r = jnp.dot(vt_ref[:, k_lo:k_hi], p, preferred_element_type=F32)
        if alpha is None:
            acc[qi, c] = r
        elif part == "diag_lower":
            acc[qi, c] = jnp.concatenate([acc[qi, c][:, :half], alpha * acc[qi, c][:, half:] + r], axis=1)
        else:
            acc[qi, c] = alpha * acc[qi, c] + r

    def finish(qi):
        outs = []
        for c in range(2):
            a = acc.pop((qi, c))
            del run_max[qi, c]
            outs.append(a[:LANES] * (1.0 / a[LANES:LANES + 1]))
        ot = (outs[0] - lam * outs[1]).T
        ms = jnp.mean(ot * ot, axis=-1, keepdims=True)
        o_ref[0, qi * t:(qi + 1) * t, :] = (
            ((ot * lax.rsqrt(ms + EPS)) * g_ref[...]) * (1.0 - lam_init)).astype(o_ref.dtype)
        del qmask[qi]

    items = []
    for qi in range(nq):
        items += [(qi, j, "full") for j in range(qi)] + [(qi, qi, "diag_upper"), (qi, qi, "diag_lower")]
    ss = [scores(items[0], c) for c in range(2)]
    pending = None
    for n, item in enumerate(items):
        nxt, cur = [], []
        for c in range(2):
            if n + 1 < len(items):
                nxt.append(scores(items[n + 1], c))
            cur.append(softmax_step(item, c, ss[c]))
            if pending is not None:
                values_step(pending[0], c, *pending[1][c])
        if pending is not None and pending[0][2] == "diag_lower":
            finish(pending[0][0])
        pending, ss = (item, cur), nxt
    for c in range(2):
        values_step(pending[0], c, *pending[1][c])
    finish(pending[0][0])


def _diff_attention(proj3, bias, lq1, lk1, lq2, lk2, subln_g, *, t, lam_init):
    b, s, _ = proj3.shape
    vec = pl.BlockSpec((1, DIFF_HALF), lambda bi, h: (0, 0))
    head = lambda col: pl.BlockSpec((1, s, LANES), lambda bi, h: (bi, 0, col + h))
    first_far = int(np.argmax(_t5_bucket_table(2 * t) == N_BUCKETS - 1))
    near_band = -(-first_far // LANES) * LANES
    ones_rows = 16
    return pl.pallas_call(
        functools.partial(_diffattn_body, t=t, lam_init=lam_init, near_band=near_band),
        grid=(b, N_DIFF_HEADS),
        in_specs=[
            head(COL_DQ), head(COL_DK), head(COL_DV),
            pl.BlockSpec((1, 2, t, t), lambda bi, h: (h, 0, 0, 0)),
            vec, vec, vec, vec,
            pl.BlockSpec((1, LANES), lambda bi, h: (0, 0)),
        ],
        out_specs=pl.BlockSpec((1, s, LANES), lambda bi, h: (bi, 0, h)),
        out_shape=jax.ShapeDtypeStruct((b, s, N_DIFF_HEADS * LANES), BF16),
        scratch_shapes=[pltpu.VMEM((LANES + ones_rows, s), BF16)],
        compiler_params=_params(("arbitrary", "arbitrary")),
        name="diff_attention",
    )(proj3, proj3, proj3, bias, lq1, lk1, lq2, lk2, subln_g)


def _retention_tables(s):
    c = RET_CHUNK
    half = RET_QK // 2
    inv_freq = ROPE_BASE ** (-jnp.arange(0, RET_QK, 2, dtype=F32) / RET_QK)
    ang = jnp.arange(s, dtype=F32)[:, None] * inv_freq[None, :]
    cos, sin = jnp.cos(ang), jnp.sin(ang)
    cos_t = jnp.tile(cos, (1, LANES // half))
    sin_t = jnp.tile(jnp.concatenate([-sin, sin], axis=1), (1, LANES // RET_QK))
    log_g = jnp.log(1.0 - jnp.exp2(-5.0 - jnp.arange(N_RET_HEADS, dtype=F32)))
    n = jnp.arange(c, dtype=F32)
    diff = n[:, None] - n[None, :]
    inner = jnp.where(diff[None] >= 0, jnp.exp(jnp.maximum(diff, 0.0)[None] * log_g[:, None, None]), 0.0)
    cross = jnp.exp((n[None] + 1.0) * log_g[:, None])
    key = jnp.exp((c - 1.0 - n[None]) * log_g[:, None])
    chunk = jnp.exp(c * log_g)
    bc = lambda a: jnp.broadcast_to(a[..., None], a.shape + (LANES,)).astype(F32)
    return cos_t, sin_t, inner.astype(F32), bc(cross), bc(key), bc(chunk[:, None])


def _retention_body(rq_ref, rk_ref, rv_ref, rg_ref, cos_ref, sin_ref, inner_ref, cross_ref, key_ref,
                    chunk_ref, gn_ref, o_ref, state_ref):
    c = RET_CHUNK
    nc = rq_ref.shape[1] // c
    nh = N_RET_HEADS
    state_ref[...] = jnp.zeros(state_ref.shape, F32)
    lane = lax.broadcasted_iota(jnp.int32, (c, LANES), 1)
    first_half = (lane % RET_QK) < (RET_QK // 2)
    head_mask = [lane < RET_QK, lane >= RET_QK]
    slab = lambda i: slice(i * LANES, (i + 1) * LANES)

    def rotary(x, cos, sin):
        rot = jnp.where(first_half, pltpu.roll(x, LANES - RET_QK // 2, 1), pltpu.roll(x, RET_QK // 2, 1))
        return x * cos + rot * sin

    def chunk_step(ci, carry):
        r = pl.multiple_of(ci * c, c)
        rows = pl.ds(r, c)
        cos = cos_ref[rows, :]
        sin = sin_ref[rows, :]
        q_pair, k_pair, kb_pair = [], [], []
        for pr in range(nh // 2):
            q_pair.append(rotary(rq_ref[0, rows, slab(pr)].astype(F32), cos, sin))
            k = rotary(rk_ref[0, rows, slab(pr)].astype(F32), cos, sin) * (RET_QK ** -0.5)
            k_pair.append(k)
            kb_pair.append(k.astype(BF16))
        qh = [jnp.where(head_mask[h % 2], q_pair[h // 2], 0.0).astype(BF16) for h in range(nh)]
        kd = [(k_pair[h // 2] * key_ref[h]).astype(BF16) for h in range(nh)]
        vh = [rv_ref[0, rows, slab(h)] for h in range(nh)]
        scores = [lax.dot_general(qh[h], kb_pair[h // 2], _NT, preferred_element_type=F32) for h in range(nh)]
        cross = [jnp.dot(qh[h], state_ref[h].astype(BF16), preferred_element_type=F32) for h in range(nh)]
        kv = [lax.dot_general(kd[h], vh[h], _TN, preferred_element_type=F32) for h in range(nh)]
        decayed = [(scores[h] * inner_ref[h]).astype(BF16) for h in range(nh)]
        for h in range(nh):
            state_ref[h] = state_ref[h] * chunk_ref[h] + kv[h]
        inner = [jnp.dot(decayed[h], vh[h], preferred_element_type=F32) for h in range(nh)]
        for h in range(nh):
            y = inner[h] + cross[h] * cross_ref[h]
            mu = jnp.mean(y, axis=-1, keepdims=True)
            yc = y - mu
            var = jnp.mean(yc * yc, axis=-1, keepdims=True)
            yn = (yc * lax.rsqrt(var + EPS)) * gn_ref[:, slab(h)]
            gate = rg_ref[0, rows, slab(h)].astype(F32)
            silu = gate * (1.0 / (1.0 + jnp.exp(-gate)))
            o_ref[0, rows, slab(h)] = (silu * yn).astype(o_ref.dtype)
        return carry

    lax.fori_loop(0, nc, chunk_step, 0, unroll=2)


def _retention(proj3, tables, gn_g):
    b, s, _ = proj3.shape
    cos_t, sin_t, inner, cross, key, chunk = tables
    c = RET_CHUNK
    nh = N_RET_HEADS
    qk_w = nh * RET_QK
    v_w = nh * LANES
    full = lambda shape: pl.BlockSpec(shape, lambda bi: (0,) * len(shape))
    return pl.pallas_call(
        _retention_body,
        grid=(b,),
        in_specs=[
            pl.BlockSpec((1, s, qk_w), lambda bi: (bi, 0, COL_RQ * LANES // qk_w)),
            pl.BlockSpec((1, s, qk_w), lambda bi: (bi, 0, COL_RK * LANES // qk_w)),
            pl.BlockSpec((1, s, v_w), lambda bi: (bi, 0, COL_RV * LANES // v_w)),
            pl.BlockSpec((1, s, v_w), lambda bi: (bi, 0, COL_RG * LANES // v_w)),
            full((s, LANES)), full((s, LANES)),
            full((nh, c, c)), full((nh, c, LANES)), full((nh, c, LANES)), full((nh, 1, LANES)),
            full((1, v_w)),
        ],
        out_specs=pl.BlockSpec((1, s, v_w), lambda bi: (bi, 0, 0)),
        out_shape=jax.ShapeDtypeStruct((b, s, v_w), BF16),
        scratch_shapes=[pltpu.VMEM((nh, LANES, LANES), F32)],
        compiler_params=_params(("arbitrary",)),
        name="retention",
    )(proj3, proj3, proj3, proj3, cos_t, sin_t, inner, cross, key, chunk, gn_g)


def _route(logits):
    lane = lax.broadcasted_iota(jnp.int32, logits.shape, 1).astype(F32)
    big = float(LANES)
    is_group = lane < N_GROUPS
    gl = jnp.where(is_group, logits, -jnp.inf)
    gmax = jnp.max(gl, axis=-1, keepdims=True)
    gidx = jnp.min(jnp.where(gl == gmax, lane, big), axis=-1, keepdims=True)
    p_group = 1.0 / jnp.sum(jnp.where(is_group, jnp.exp(gl - gmax), 0.0), axis=-1, keepdims=True)
    lo = N_GROUPS + EXPERTS_PER_GROUP * gidx
    il = jnp.where(lane >= lo, jnp.where(lane < lo + EXPERTS_PER_GROUP, logits, -jnp.inf), -jnp.inf)
    v1 = jnp.max(il, axis=-1, keepdims=True)
    i1 = jnp.min(jnp.where(il == v1, lane, big), axis=-1, keepdims=True)
    il2 = jnp.where(lane == i1, -jnp.inf, il)
    v2 = jnp.max(il2, axis=-1, keepdims=True)
    i2 = jnp.min(jnp.where(il2 == v2, lane, big), axis=-1, keepdims=True)
    e2 = jnp.exp(v2 - v1)
    inv = 1.0 / (1.0 + e2)
    g1 = p_group * inv
    g2 = p_group * (e2 * inv)
    out = jnp.where(lane == 0.0, i1 - N_GROUPS,
                    jnp.where(lane == 1.0, i2 - N_GROUPS,
                              jnp.where(lane == 2.0, g1, jnp.where(lane == 3.0, g2, 0.0))))
    return out


def _outproj_body(a_ref, r_ref, x_ref, w_ref, g_ref, wr_ref, br_ref, x1_ref, h2_ref, route_ref, eid_ref, *, sub):
    tm = x_ref.shape[0]
    half = a_ref.shape[1]
    n_sub = tm // sub

    def project(i):
        rows = slice(i * sub, (i + 1) * sub)
        acc = jnp.dot(a_ref[rows, :], w_ref[:half, :], preferred_element_type=F32)
        return acc + jnp.dot(r_ref[rows, :], w_ref[half:, :], preferred_element_type=F32)

    def epilogue(i, acc):
        rows = slice(i * sub, (i + 1) * sub)
        x1 = x_ref[rows, :] + acc
        x1_ref[rows, :] = x1
        ms = jnp.mean(x1 * x1, axis=-1, keepdims=True)
        h2 = (x1 * lax.rsqrt(ms + EPS)) * g_ref[...]
        h2_ref[rows, :] = _pack_halves(h2)
        logits = jnp.dot(h2.astype(BF16), wr_ref[...], preferred_element_type=F32) + br_ref[...]
        route = _route(logits)
        route_ref[rows, :] = route
        eid_ref[:, rows] = route.T[:TOP_K, :].astype(jnp.int32)

    acc = project(0)
    for i in range(n_sub):
        nxt = project(i + 1) if i + 1 < n_sub else None
        epilogue(i, acc)
        acc = nxt


def _outproj_router(a2d, r2d, x2d, w_out_bf16, g, wr_bf16, br, *, tm):
    n, d = x2d.shape
    half = a2d.shape[1]
    row = lambda width: pl.BlockSpec((tm, width), lambda i: (i, 0))
    full = lambda shape: pl.BlockSpec(shape, lambda i: (0, 0))
    return pl.pallas_call(
        functools.partial(_outproj_body, sub=min(tm, 2 * LANES)),
        grid=(n // tm,),
        in_specs=[row(half), row(half), row(d), full((2 * half, d)), full((1, d)), full((d, LANES)),
                  full((1, LANES))],
        out_specs=[row(d), row(d // 2), row(LANES), pl.BlockSpec((TOP_K, tm), lambda i: (0, i))],
        out_shape=[jax.ShapeDtypeStruct((n, d), F32), jax.ShapeDtypeStruct((n, d // 2), jnp.uint32),
                   jax.ShapeDtypeStruct((n, LANES), F32), jax.ShapeDtypeStruct((TOP_K, n), jnp.int32)],
        compiler_params=_params(("arbitrary",)),
        name="outproj_router",
    )(a2d, r2d, x2d, w_out_bf16, g, wr_bf16, br)


def _moe_plan(expert_id, tm):
    n_slots = expert_id.shape[0]
    n_tiles = n_slots // tm
    order = jnp.argsort(expert_id).astype(jnp.int32)
    sizes = jnp.zeros((N_EXPERTS,), jnp.int32).at[expert_id].add(1)
    ends = jnp.cumsum(sizes)
    starts = ends - sizes
    tile_row0 = jnp.arange(n_tiles, dtype=jnp.int32) * tm
    owner = lambda row: jnp.sum((ends[None, :] <= row[:, None]).astype(jnp.int32), axis=1)
    first_e = owner(tile_row0)
    last_e = owner(tile_row0 + (tm - 1))
    owners_upto = jnp.cumsum((sizes > 0).astype(jnp.int32))
    n_extra = (owners_upto[last_e] - owners_upto[first_e]).astype(jnp.int32)
    prev_last = jnp.concatenate([jnp.full((1,), -1, jnp.int32), last_e[:-1]])
    switch = (prev_last != first_e).astype(jnp.int32)
    eid = jnp.arange(N_EXPERTS, dtype=jnp.int32)
    cand = jnp.where((sizes[None, :] > 0) & (eid[None, :] > eid[:, None]), eid[None, :], N_EXPERTS)
    nxt = jnp.min(cand, axis=1)
    next_expert = jnp.where(nxt < N_EXPERTS, nxt, -1).astype(jnp.int32)
    spare = n_slots + jnp.arange(tm, dtype=jnp.int32)
    codes = jnp.concatenate([order.reshape(n_tiles, tm), spare[None, :]], axis=0).reshape(n_tiles + 1, 1, tm)
    return dict(first_e=first_e, n_extra=n_extra, switch=switch, starts=starts.astype(jnp.int32),
                ends=ends.astype(jnp.int32), next_expert=next_expert, codes=codes)


def _moe_body(fe_ref, nextra_ref, sw_ref, st_ref, en_ref, nx_ref,
              codes_hbm, h2_hbm, wg_hbm, wu_hbm, wd_hbm, ys_hbm,
              gidx, sidx, xbuf, obuf, wg_st, wu_st, wd_st, wg_rs, wu_rs, wd_rs,
              gisem, sisem, gsem, ssem, wsem, *, n_tokens, n_tiles, cast_rows, ff_chunks):
    t = pl.program_id(0)
    tm = xbuf.shape[1]
    last = n_tiles - 1
    spare_codes = n_tiles

    def gidx_copy(tile, slot):
        return pltpu.make_async_copy(codes_hbm.at[tile], gidx.at[slot], gisem.at[slot])

    def sidx_copy(tile, slot):
        return pltpu.make_async_copy(codes_hbm.at[tile], sidx.at[slot], sisem.at[slot])

    def weight_copies(ex):
        return (pltpu.make_async_copy(wg_hbm.at[ex], wg_st, wsem.at[0]),
                pltpu.make_async_copy(wu_hbm.at[ex], wu_st, wsem.at[1]),
                pltpu.make_async_copy(wd_hbm.at[ex], wd_st, wsem.at[2]))

    def gather_row(slot, r):
        code = gidx[slot, 0, r]
        tok = code - jnp.where(code >= n_tokens, n_tokens, 0)
        return pltpu.make_async_copy(h2_hbm.at[pl.ds(tok, 1)], xbuf.at[slot, pl.ds(r, 1)], gsem.at[slot])

    def scatter_row(slot, r):
        dst = sidx[slot, 0, r]
        return pltpu.make_async_copy(obuf.at[slot, pl.ds(r, 1)], ys_hbm.at[pl.ds(dst, 1)], ssem.at[slot])

    def wait_gather(slot):
        pltpu.make_async_copy(h2_hbm.at[pl.ds(0, tm)], xbuf.at[slot], gsem.at[slot]).wait()

    def wait_scatter(slot):
        pltpu.make_async_copy(obuf.at[slot], ys_hbm.at[pl.ds(0, tm)], ssem.at[slot]).wait()

    def for_rows(fn):
        def body(r, carry):
            fn(r)
            return carry
        lax.fori_loop(0, tm, body, 0, unroll=8)

    def take_over(ex, first):
        @pl.when(first)
        def _():
            for cp in weight_copies(ex):
                cp.start()

        for cp in weight_copies(ex):
            cp.wait()

        def cast_up(i, carry):
            r = pl.multiple_of(i * cast_rows, cast_rows)
            wg_rs[pl.ds(r, cast_rows), :] = wg_st[pl.ds(r, cast_rows), :].astype(BF16)
            wu_rs[pl.ds(r, cast_rows), :] = wu_st[pl.ds(r, cast_rows), :].astype(BF16)
            return carry

        def cast_down(i, carry):
            r = pl.multiple_of(i * cast_rows, cast_rows)
            wd_rs[pl.ds(r, cast_rows), :] = wd_st[pl.ds(r, cast_rows), :].astype(BF16)
            return carry

        lax.fori_loop(0, wg_st.shape[0] // cast_rows, cast_up, 0)
        lax.fori_loop(0, wd_st.shape[0] // cast_rows, cast_down, 0)
        nxt = nx_ref[ex]

        @pl.when(nxt >= 0)
        def _():
            for cp in weight_copies(nxt):
                cp.start(priority=1)

    def ffn(slot, ex, accumulate, row_dmas=None):
        x_lo, x_hi = (h.astype(BF16) for h in _unpack_halves(xbuf[slot]))
        hd = x_lo.shape[1]
        row = t * tm + lax.broadcasted_iota(jnp.int32, (tm, 1), 0)
        mine = jnp.logical_and(row >= st_ref[ex], row < en_ref[ex])
        d_ff = wg_rs.shape[1]
        n_chunks = ff_chunks if row_dmas is not None else 1
        fc = d_ff // n_chunks
        rows = tm // n_chunks

        def gate_up(c):
            cols = slice(c * fc, (c + 1) * fc)
            up = lambda w: (jnp.dot(x_lo, w[:hd, cols], preferred_element_type=F32)
                            + jnp.dot(x_hi, w[hd:, cols], preferred_element_type=F32))
            return up(wg_rs), up(wu_rs)

        contrib = None
        ab = gate_up(0)
        for c in range(n_chunks):
            nxt = gate_up(c + 1) if c + 1 < n_chunks else None
            a, b = ab
            hmid = jnp.where(mine, (a * (1.0 / (1.0 + jnp.exp(-a)))) * b, 0.0).astype(BF16)
            part = jnp.dot(hmid, wd_rs[c * fc:(c + 1) * fc, :], preferred_element_type=F32)
            contrib = part if contrib is None else contrib + part
            if row_dmas is not None:
                row_dmas(c * rows, (c + 1) * rows)
            ab = nxt
        if accumulate:
            prev_lo, prev_hi = _unpack_halves(obuf[slot])
            contrib = jnp.concatenate([prev_lo, prev_hi], axis=1) + contrib
        obuf[slot] = _pack_halves(contrib)

    e0 = fe_ref[t]

    def step(p):
        q = 1 - p
        if p == 0:
            @pl.when(t == 0)
            def _():
                copies = (gidx_copy(0, 0), gidx_copy(min(1, last), 1), sidx_copy(spare_codes, 1))
                for cp in copies:
                    cp.start()
                for cp in copies:
                    cp.wait()
                obuf[1] = jnp.zeros(obuf.shape[1:], obuf.dtype)
                for_rows(lambda r: gather_row(0, r).start())

        gidx_copy(jnp.minimum(t + 2, last), p).start()
        sidx_copy(t, p).start()

        @pl.when(t >= 1)
        def _():
            gidx_copy(0, q).wait()
            sidx_copy(0, q).wait()
            wait_scatter(p)

        wait_gather(p)

        @pl.when(sw_ref[t] == 1)
        def _():
            take_over(e0, t == 0)

        def row_dmas(lo, hi):
            for r in range(lo, hi):
                scatter_row(q, r).start()
                gather_row(q, r).start()

        ffn(p, e0, accumulate=False, row_dmas=row_dmas)

    for parity in range(2):
        pl.when(lax.rem(t, 2) == parity)(functools.partial(step, parity))

    slot = lax.rem(t, 2)

    def further_owner(_, ex):
        nxt = nx_ref[ex]
        take_over(nxt, False)
        ffn(slot, nxt, accumulate=True)
        return nxt

    lax.fori_loop(0, nextra_ref[t], further_owner, e0)

    @pl.when(t == last)
    def _():
        gidx_copy(0, slot).wait()
        sidx_copy(0, slot).wait()
        for_rows(lambda r: scatter_row(slot, r).start())
        wait_gather(1 - slot)
        wait_scatter(1 - slot)
        wait_scatter(slot)


def _moe_ffn(h2, plan, w_gate, w_up, w_down, *, tm):
    n, hd = h2.shape
    d = 2 * hd
    n_slots = 2 * n
    n_tiles = n_slots // tm
    assert n_tiles >= 2 and n_slots % tm == 0
    d_ff = w_gate.shape[2]
    any_spec = pl.BlockSpec(memory_space=pl.ANY)
    grid_spec = pltpu.PrefetchScalarGridSpec(
        num_scalar_prefetch=6,
        grid=(n_tiles,),
        in_specs=[any_spec] * 5,
        out_specs=any_spec,
        scratch_shapes=[
            pltpu.SMEM((2, 1, tm), jnp.int32),
            pltpu.SMEM((2, 1, tm), jnp.int32),
            pltpu.VMEM((2, tm, hd), jnp.uint32),
            pltpu.VMEM((2, tm, hd), jnp.uint32),
            pltpu.VMEM((d, d_ff), F32),
            pltpu.VMEM((d, d_ff), F32),
            pltpu.VMEM((d_ff, d), F32),
            pltpu.VMEM((d, d_ff), BF16),
            pltpu.VMEM((d, d_ff), BF16),
            pltpu.VMEM((d_ff, d), BF16),
            pltpu.SemaphoreType.DMA((2,)),
            pltpu.SemaphoreType.DMA((2,)),
            pltpu.SemaphoreType.DMA((2,)),
            pltpu.SemaphoreType.DMA((2,)),
            pltpu.SemaphoreType.DMA((3,)),
        ],
    )
    return pl.pallas_call(
        functools.partial(_moe_body, n_tokens=n, n_tiles=n_tiles, cast_rows=128,
                          ff_chunks=d_ff // (2 * LANES)),
        grid_spec=grid_spec,
        out_shape=jax.ShapeDtypeStruct((n_slots + tm, hd), jnp.uint32),
        compiler_params=_params(("arbitrary",), vmem=MOE_VMEM_LIMIT),
        name="moe_ffn",
    )(plan["first_e"], plan["n_extra"], plan["switch"], plan["starts"], plan["ends"], plan["next_expert"],
      plan["codes"], h2, w_gate, w_up, w_down)


def _final_body(x1_ref, ys0_ref, ys1_ref, route_ref, g_ref, o_ref):
    d = x1_ref.shape[1]
    route = route_ref[...]
    g1, g2 = route[:, 2:3], route[:, 3:4]
    ya = _unpack_halves(ys0_ref[...])
    yb = _unpack_halves(ys1_ref[...])
    halves = (slice(0, d // 2), slice(d // 2, d))
    x2 = [x1_ref[:, cols] + (ya[i] * g1 + yb[i] * g2) for i, cols in enumerate(halves)]
    ms = sum(jnp.sum(h * h, axis=-1, keepdims=True) for h in x2) * (1.0 / d)
    inv = lax.rsqrt(ms + EPS)
    for h, cols in zip(x2, halves):
        o_ref[:, cols] = ((h * inv) * g_ref[:, cols]).astype(o_ref.dtype)


def _final(x1, ys, route, g, *, tm):
    n, d = x1.shape
    nb = n // tm
    return pl.pallas_call(
        _final_body,
        grid=(nb,),
        in_specs=[
            pl.BlockSpec((tm, d), lambda i: (i, 0)),
            pl.BlockSpec((tm, d // 2), lambda i: (i, 0)),
            pl.BlockSpec((tm, d // 2), lambda i: (nb + i, 0)),
            pl.BlockSpec((tm, LANES), lambda i: (i, 0)),
            pl.BlockSpec((1, d), lambda i: (0, 0)),
        ],
        out_specs=pl.BlockSpec((tm, d), lambda i: (i, 0)),
        out_shape=jax.ShapeDtypeStruct((n, d), F32),
        compiler_params=_params(("arbitrary",)),
        name="final_norm",
    )(x1, ys, ys, route, g)


def _layer(x, rel_bias, attn_norm_g, w_in, lq1, lk1, lq2, lk2, diff_subln_g, ret_gn_g, w_out, ffn_norm_g,
           w_group_router, b_group, w_inner_router, b_inner, w_gate_exp, w_up_exp, w_down_exp, *, layer,
           attn_tile, row_tile, moe_tile):
    b, s, d = x.shape
    n = b * s
    x2d = x.reshape(n, d)
    lam_init = 0.8 - 0.6 * math.exp(-0.3 * layer)

    dq_cols = (COL_DK - COL_DQ) * LANES
    col_scale = jnp.concatenate([jnp.full((1, dq_cols), DIFF_HALF ** -0.5 * LOG2E, F32),
                                 jnp.ones((1, IN_COLS - dq_cols), F32)], axis=1)
    proj = _inproj(x2d, attn_norm_g.reshape(1, d), w_in.astype(BF16), col_scale, tm=min(row_tile, n))
    proj3 = proj.reshape(b, s, IN_COLS)

    bias = _bias_tiles(rel_bias, attn_tile)
    vec = lambda a: a.reshape(1, -1)
    a_out = _diff_attention(proj3, bias, vec(lq1), vec(lk1), vec(lq2), vec(lk2), vec(diff_subln_g),
                            t=attn_tile, lam_init=lam_init)
    r_out = _retention(proj3, _retention_tables(s), vec(ret_gn_g))

    pad = LANES - N_GROUPS - N_EXPERTS
    wr = jnp.concatenate([w_group_router, jnp.transpose(w_inner_router, (1, 0, 2)).reshape(d, N_EXPERTS),
                          jnp.zeros((d, pad), F32)], axis=1).astype(BF16)
    br = jnp.concatenate([b_group, b_inner.reshape(-1), jnp.zeros((pad,), F32)]).reshape(1, LANES)
    x1, h2, route, eid = _outproj_router(a_out.reshape(n, -1), r_out.reshape(n, -1), x2d, w_out.astype(BF16),
                                         vec(ffn_norm_g), wr, br, tm=row_tile)

    plan = _moe_plan(eid.reshape(-1), moe_tile)
    ys = _moe_ffn(h2, plan, w_gate_exp, w_up_exp, w_down_exp, tm=moe_tile)
    return x1, ys, route


def kernel(x, rel_bias, attn_norm_g, w_in, lambda_q1, lambda_k1, lambda_q2, lambda_k2, diff_subln_g, ret_gn_g,
           w_out, ffn_norm_g, w_group_router, b_group, w_inner_router, b_inner, w_gate_exp, w_up_exp, w_down_exp,
           final_g):
    b, s, d = x.shape
    depth = w_in.shape[0]
    assert depth == 1, "the final norm is fused into the single layer's combine step"
    first = lambda a: a.reshape(a.shape[1:])
    x1, ys, route = _layer(x, rel_bias, attn_norm_g[0], w_in[0], lambda_q1[0], lambda_k1[0], lambda_q2[0],
                           lambda_k2[0], diff_subln_g[0], ret_gn_g[0], w_out[0], ffn_norm_g[0],
                           w_group_router[0], b_group[0], w_inner_router[0], b_inner[0], first(w_gate_exp),
                           first(w_up_exp), first(w_down_exp), layer=0, attn_tile=min(256, s), row_tile=512,
                           moe_tile=256)
    out = _final(x1, ys, route, final_g.reshape(1, d), tm=512)
    return out.reshape(b, s, d)
```

```python
import functools
import math

import numpy as np
import jax
import jax.numpy as jnp
from jax import lax
from jax.experimental import pallas as pl
from jax.experimental.pallas import tpu as pltpu

F32 = jnp.float32
BF16 = jnp.bfloat16

EPS = 1e-6
LANES = 128
N_DIFF_HEADS = 8
DIFF_HALF = 64
N_RET_HEADS = 8
RET_QK = 64
RET_CHUNK = 128
ROPE_BASE = 10000.0
N_BUCKETS = 32
MAX_DISTANCE = 128
N_GROUPS = 4
EXPERTS_PER_GROUP = 8
N_EXPERTS = N_GROUPS * EXPERTS_PER_GROUP
TOP_K = 2
MASK_VALUE = -1e30
LOG2E = math.log2(math.e)
VMEM_LIMIT = 56 * 1024 * 1024
MOE_VMEM_LIMIT = 60 * 1024 * 1024

COL_DQ, COL_DK, COL_DV = 0, 8, 16
COL_RQ, COL_RK, COL_RV, COL_RG = 24, 28, 32, 40
IN_COLS = 48 * LANES


def _params(sem, vmem=VMEM_LIMIT):
    return pltpu.CompilerParams(dimension_semantics=sem, vmem_limit_bytes=vmem)


def _pack_halves(x):
    hd = x.shape[1] // 2
    bits = lambda h: lax.bitcast_convert_type(h.astype(BF16).astype(F32), jnp.uint32)
    return lax.shift_right_logical(bits(x[:, :hd]), jnp.uint32(16)) | bits(x[:, hd:])


def _unpack_halves(w):
    return (lax.bitcast_convert_type(lax.shift_left(w, jnp.uint32(16)), F32),
            lax.bitcast_convert_type(w & jnp.uint32(0xFFFF0000), F32))


def _inproj_body(x_ref, g_ref, w_ref, cs_ref, o_ref):
    x = x_ref[...]
    acc = jnp.dot((x * g_ref[...]).astype(BF16), w_ref[...], preferred_element_type=F32)
    inv = lax.rsqrt(jnp.mean(x * x, axis=-1, keepdims=True) + EPS)
    o_ref[...] = ((acc * inv) * cs_ref[...]).astype(o_ref.dtype)


def _inproj(x2d, g, w, col_scale, *, tm):
    n, d = x2d.shape
    cols = w.shape[1]
    once = dict(pipeline_mode=pl.Buffered(1))
    return pl.pallas_call(
        _inproj_body,
        grid=(n // tm,),
        in_specs=[
            pl.BlockSpec((tm, d), lambda i: (i, 0)),
            pl.BlockSpec((1, d), lambda i: (0, 0)),
            pl.BlockSpec((d, cols), lambda i: (0, 0), **once),
            pl.BlockSpec((1, cols), lambda i: (0, 0)),
        ],
        out_specs=pl.BlockSpec((tm, cols), lambda i: (i, 0)),
        out_shape=jax.ShapeDtypeStruct((n, cols), BF16),
        compiler_params=_params(("arbitrary",)),
        name="inproj",
    )(x2d, g, w, col_scale)


def _t5_bucket_table(n):
    d = np.arange(n)
    max_exact = N_BUCKETS // 2
    log_ratio = np.log(np.maximum(d, 1).astype(np.float64) / max_exact) / math.log(MAX_DISTANCE / max_exact)
    large = np.minimum(max_exact + (log_ratio * (N_BUCKETS - max_exact)).astype(np.int64), N_BUCKETS - 1)
    return np.where(d < max_exact, d, large).astype(np.int32)


def _bias_body(rb_ref, bkt_ref, o_ref):
    h = pl.program_id(0)
    t = o_ref.shape[2]
    bkt = bkt_ref[...]
    far = rb_ref[N_BUCKETS - 1, h]
    by_dist = jnp.zeros(bkt.shape, F32)
    for b in range(N_BUCKETS - 1):
        by_dist = jnp.where(bkt == b, (rb_ref[b, h] - far) * LOG2E, by_dist)
    rows = (jnp.concatenate([jnp.full((1, t), MASK_VALUE, F32), by_dist[:, :t]], axis=1), by_dist)
    for i, row in enumerate(rows):
        shifted = pltpu.roll(jnp.broadcast_to(row, (t, 2 * t)), 0, 1, stride=1, stride_axis=0)
        o_ref[0, i] = shifted[:, t:]


def _bias_tiles(rel_bias, t):
    bkt = jnp.asarray(_t5_bucket_table(2 * t)).reshape(1, 2 * t)
    return pl.pallas_call(
        _bias_body,
        grid=(N_DIFF_HEADS,),
        in_specs=[
            pl.BlockSpec(memory_space=pltpu.SMEM),
            pl.BlockSpec((1, 2 * t), lambda h: (0, 0)),
        ],
        out_specs=pl.BlockSpec((1, 2, t, t), lambda h: (h, 0, 0, 0)),
        out_shape=jax.ShapeDtypeStruct((N_DIFF_HEADS, 2, t, t), F32),
        compiler_params=_params(("arbitrary",)),
        name="bias_tiles",
    )(rel_bias, bkt)


_NT = (((1,), (1,)), ((), ()))
_TN = (((0,), (0,)), ((), ()))


def _diffattn_body(q_ref, k_ref, v_ref, bias_ref, lq1_ref, lk1_ref, lq2_ref, lk2_ref, g_ref, o_ref, vt_ref, *,
                   t, lam_init, near_band):
    nq = k_ref.shape[1] // t
    half = t // 2
    lane = lax.broadcasted_iota(jnp.int32, (t, LANES), 1)
    lam = (jnp.exp(jnp.sum(lq1_ref[...] * lk1_ref[...], axis=-1, keepdims=True))
           - jnp.exp(jnp.sum(lq2_ref[...] * lk2_ref[...], axis=-1, keepdims=True)) + lam_init)
    vt_ref[:LANES, :] = v_ref[0].T
    vt_ref[LANES:, :] = jnp.ones((vt_ref.shape[0] - LANES, vt_ref.shape[1]), BF16)
    qmask = {}
    run_max = {}
    acc = {}

    def key_range(item):
        _, j, part = item
        return j * t + (half if part == "diag_lower" else 0), j * t + (half if part == "diag_upper" else t)

    def scores(item, c):
        qi, j, part = item
        if qi not in qmask:
            q = q_ref[0, qi * t:(qi + 1) * t, :]
            zero = jnp.zeros_like(q)
            qmask[qi] = (jnp.where(lane < DIFF_HALF, q, zero), jnp.where(lane >= DIFF_HALF, q, zero))
        k_lo, k_hi = key_range(item)
        qsel = qmask[qi][c][half:] if part == "diag_lower" else qmask[qi][c]
        s = lax.dot_general(k_ref[0, k_lo:k_hi, :], qsel, _NT, preferred_element_type=F32)
        if part == "diag_upper":
            s = s + bias_ref[0, 0, :half, :]
        elif part == "diag_lower":
            s = s + bias_ref[0, 0, half:, half:]
        elif j == qi - 1:
            if near_band >= t:
                s = s + bias_ref[0, 1]
            else:
                corner = s[t - near_band:, :near_band] + bias_ref[0, 1, t - near_band:, :near_band]
                bottom = jnp.concatenate([corner, s[t - near_band:, near_band:]], axis=1)
                s = jnp.concatenate([s[:t - near_band], bottom], axis=0)
        return s

    def softmax_step(item, c, s):
        qi, j, part = item
        smax = jnp.max(s, axis=0, keepdims=True)
        if j == 0 and part != "diag_lower":
            m_new, alpha = smax, None
            run_max[qi, c] = m_new
        else:
            m_full = run_max[qi, c]
            m_old = m_full[:, half:] if part == "diag_lower" else m_full
            m_new = jnp.maximum(m_old, smax)
            alpha = jnp.exp2(m_old - m_new)
            run_max[qi, c] = jnp.concatenate([m_full[:, :half], m_new], axis=1) if part == "diag_lower" else m_new
        return jnp.exp2(s - m_new).astype(BF16), alpha

    def values_step(item, c, p, alpha):
        qi, _, part = item
        k_lo, k_hi = key_range(item)
        r = jnp.dot(vt_ref[:, k_lo:k_hi], p, preferred_element_type=F32)
        if alpha is None:
            acc[qi, c] = r
        elif part == "diag_lower":
            acc[qi, c] = jnp.concatenate([acc[qi, c][:, :half], alpha * acc[qi, c][:, half:] + r], axis=1)
        else:
            acc[qi, c] = alpha * acc[qi, c] + r

    def finish(qi):
        outs = []
        for c in range(2):
            a = acc.pop((qi, c))
            del run_max[qi, c]
            outs.append(a[:LANES] * (1.0 / a[LANES:LANES + 1]))
        ot = (outs[0] - lam * outs[1]).T
        ms = jnp.mean(ot * ot, axis=-1, keepdims=True)
        o_ref[0, qi * t:(qi + 1) * t, :] = (
            ((ot * lax.rsqrt(ms + EPS)) * g_ref[...]) * (1.0 - lam_init)).astype(o_ref.dtype)
        del qmask[qi]

    items = []
    for qi in range(nq):
        items += [(qi, j, "full") for j in range(qi)] + [(qi, qi, "diag_upper"), (qi, qi, "diag_lower")]
    ss = [scores(items[0], c) for c in range(2)]
    pending = None
    for n, item in enumerate(items):
        nxt, cur = [], []
        for c in range(2):
            if n + 1 < len(items):
                nxt.append(scores(items[n + 1], c))
            cur.append(softmax_step(item, c, ss[c]))
            if pending is not None:
                values_step(pending[0], c, *pending[1][c])
        if pending is not None and pending[0][2] == "diag_lower":
            finish(pending[0][0])
        pending, ss = (item, cur), nxt
    for c in range(2):
        values_step(pending[0], c, *pending[1][c])
    finish(pending[0][0])


def _diff_attention(proj3, bias, lq1, lk1, lq2, lk2, subln_g, *, t, lam_init):
    b, s, _ = proj3.shape
    vec = pl.BlockSpec((1, DIFF_HALF), lambda bi, h: (0, 0))
    head = lambda col: pl.BlockSpec((1, s, LANES), lambda bi, h: (bi, 0, col + h))
    first_far = int(np.argmax(_t5_bucket_table(2 * t) == N_BUCKETS - 1))
    near_band = -(-first_far // LANES) * LANES
    ones_rows = 16
    return pl.pallas_call(
        functools.partial(_diffattn_body, t=t, lam_init=lam_init, near_band=near_band),
        grid=(b, N_DIFF_HEADS),
        in_specs=[
            head(COL_DQ), head(COL_DK), head(COL_DV),
            pl.BlockSpec((1, 2, t, t), lambda bi, h: (h, 0, 0, 0)),
            vec, vec, vec, vec,
            pl.BlockSpec((1, LANES), lambda bi, h: (0, 0)),
        ],
        out_specs=pl.BlockSpec((1, s, LANES), lambda bi, h: (bi, 0, h)),
        out_shape=jax.ShapeDtypeStruct((b, s, N_DIFF_HEADS * LANES), BF16),
        scratch_shapes=[pltpu.VMEM((LANES + ones_rows, s), BF16)],
        compiler_params=_params(("arbitrary", "arbitrary")),
        name="diff_attention",
    )(proj3, proj3, proj3, bias, lq1, lk1, lq2, lk2, subln_g)


def _retention_tables(s):
    c = RET_CHUNK
    half = RET_QK // 2
    inv_freq = ROPE_BASE ** (-jnp.arange(0, RET_QK, 2, dtype=F32) / RET_QK)
    ang = jnp.arange(s, dtype=F32)[:, None] * inv_freq[None, :]
    cos, sin = jnp.cos(ang), jnp.sin(ang)
    cos_t = jnp.tile(cos, (1, LANES // half))
    sin_t = jnp.tile(jnp.concatenate([-sin, sin], axis=1), (1, LANES // RET_QK))
    log_g = jnp.log(1.0 - jnp.exp2(-5.0 - jnp.arange(N_RET_HEADS, dtype=F32)))
    n = jnp.arange(c, dtype=F32)
    diff = n[:, None] - n[None, :]
    inner = jnp.where(diff[None] >= 0, jnp.exp(jnp.maximum(diff, 0.0)[None] * log_g[:, None, None]), 0.0)
    cross = jnp.exp((n[None] + 1.0) * log_g[:, None])
    key = jnp.exp((c - 1.0 - n[None]) * log_g[:, None])
    chunk = jnp.exp(c * log_g)
    bc = lambda a: jnp.broadcast_to(a[..., None], a.shape + (LANES,)).astype(F32)
    return cos_t, sin_t, inner.astype(F32), bc(cross), bc(key), bc(chunk[:, None])


def _retention_body(rq_ref, rk_ref, rv_ref, rg_ref, cos_ref, sin_ref, inner_ref, cross_ref, key_ref,
                    chunk_ref, gn_ref, o_ref, state_ref):
    c = RET_CHUNK
    nc = rq_ref.shape[1] // c
    nh = N_RET_HEADS
    state_ref[...] = jnp.zeros(state_ref.shape, F32)
    lane = lax.broadcasted_iota(jnp.int32, (c, LANES), 1)
    first_half = (lane % RET_QK) < (RET_QK // 2)
    head_mask = [lane < RET_QK, lane >= RET_QK]
    slab = lambda i: slice(i * LANES, (i + 1) * LANES)

    def rotary(x, cos, sin):
        rot = jnp.where(first_half, pltpu.roll(x, LANES - RET_QK // 2, 1), pltpu.roll(x, RET_QK // 2, 1))
        return x * cos + rot * sin

    def chunk_step(ci, carry):
        r = pl.multiple_of(ci * c, c)
        rows = pl.ds(r, c)
        cos = cos_ref[rows, :]
        sin = sin_ref[rows, :]
        q_pair, k_pair, kb_pair = [], [], []
        for pr in range(nh // 2):
            q_pair.append(rotary(rq_ref[0, rows, slab(pr)].astype(F32), cos, sin))
            k = rotary(rk_ref[0, rows, slab(pr)].astype(F32), cos, sin) * (RET_QK ** -0.5)
            k_pair.append(k)
            kb_pair.append(k.astype(BF16))
        qh = [jnp.where(head_mask[h % 2], q_pair[h // 2], 0.0).astype(BF16) for h in range(nh)]
        kd = [(k_pair[h // 2] * key_ref[h]).astype(BF16) for h in range(nh)]
        vh = [rv_ref[0, rows, slab(h)] for h in range(nh)]
        scores = [lax.dot_general(qh[h], kb_pair[h // 2], _NT, preferred_element_type=F32) for h in range(nh)]
        cross = [jnp.dot(qh[h], state_ref[h].astype(BF16), preferred_element_type=F32) for h in range(nh)]
        kv = [lax.dot_general(kd[h], vh[h], _TN, preferred_element_type=F32) for h in range(nh)]
        decayed = [(scores[h] * inner_ref[h]).astype(BF16) for h in range(nh)]
        for h in range(nh):
            state_ref[h] = state_ref[h] * chunk_ref[h] + kv[h]
        inner = [jnp.dot(decayed[h], vh[h], preferred_element_type=F32) for h in range(nh)]
        for h in range(nh):
            y = inner[h] + cross[h] * cross_ref[h]
            mu = jnp.mean(y, axis=-1, keepdims=True)
            yc = y - mu
            var = jnp.mean(yc * yc, axis=-1, keepdims=True)
            yn = (yc * lax.rsqrt(var + EPS)) * gn_ref[:, slab(h)]
            gate = rg_ref[0, rows, slab(h)].astype(F32)
            silu = gate * (1.0 / (1.0 + jnp.exp(-gate)))
            o_ref[0, rows, slab(h)] = (silu * yn).astype(o_ref.dtype)
        return carry

    lax.fori_loop(0, nc, chunk_step, 0, unroll=2)


def _retention(proj3, tables, gn_g):
    b, s, _ = proj3.shape
    cos_t, sin_t, inner, cross, key, chunk = tables
    c = RET_CHUNK
    nh = N_RET_HEADS
    qk_w = nh * RET_QK
    v_w = nh * LANES
    full = lambda shape: pl.BlockSpec(shape, lambda bi: (0,) * len(shape))
    return pl.pallas_call(
        _retention_body,
        grid=(b,),
        in_specs=[
            pl.BlockSpec((1, s, qk_w), lambda bi: (bi, 0, COL_RQ * LANES // qk_w)),
            pl.BlockSpec((1, s, qk_w), lambda bi: (bi, 0, COL_RK * LANES // qk_w)),
            pl.BlockSpec((1, s, v_w), lambda bi: (bi, 0, COL_RV * LANES // v_w)),
            pl.BlockSpec((1, s, v_w), lambda bi: (bi, 0, COL_RG * LANES // v_w)),
            full((s, LANES)), full((s, LANES)),
            full((nh, c, c)), full((nh, c, LANES)), full((nh, c, LANES)), full((nh, 1, LANES)),
            full((1, v_w)),
        ],
        out_specs=pl.BlockSpec((1, s, v_w), lambda bi: (bi, 0, 0)),
        out_shape=jax.ShapeDtypeStruct((b, s, v_w), BF16),
        scratch_shapes=[pltpu.VMEM((nh, LANES, LANES), F32)],
        compiler_params=_params(("arbitrary",)),
        name="retention",
    )(proj3, proj3, proj3, proj3, cos_t, sin_t, inner, cross, key, chunk, gn_g)


def _route(logits):
    lane = lax.broadcasted_iota(jnp.int32, logits.shape, 1).astype(F32)
    big = float(LANES)
    is_group = lane < N_GROUPS
    gl = jnp.where(is_group, logits, -jnp.inf)
    gmax = jnp.max(gl, axis=-1, keepdims=True)
    gidx = jnp.min(jnp.where(gl == gmax, lane, big), axis=-1, keepdims=True)
    p_group = 1.0 / jnp.sum(jnp.where(is_group, jnp.exp(gl - gmax), 0.0), axis=-1, keepdims=True)
    lo = N_GROUPS + EXPERTS_PER_GROUP * gidx
    il = jnp.where(lane >= lo, jnp.where(lane < lo + EXPERTS_PER_GROUP, logits, -jnp.inf), -jnp.inf)
    v1 = jnp.max(il, axis=-1, keepdims=True)
    i1 = jnp.min(jnp.where(il == v1, lane, big), axis=-1, keepdims=True)
    il2 = jnp.where(lane == i1, -jnp.inf, il)
    v2 = jnp.max(il2, axis=-1, keepdims=True)
    i2 = jnp.min(jnp.where(il2 == v2, lane, big), axis=-1, keepdims=True)
    e2 = jnp.exp(v2 - v1)
    inv = 1.0 / (1.0 + e2)
    g1 = p_group * inv
    g2 = p_group * (e2 * inv)
    out = jnp.where(lane == 0.0, i1 - N_GROUPS,
                    jnp.where(lane == 1.0, i2 - N_GROUPS,
                              jnp.where(lane == 2.0, g1, jnp.where(lane == 3.0, g2, 0.0))))
    return out


def _outproj_body(a_ref, r_ref, x_ref, w_ref, g_ref, wr_ref, br_ref, x1_ref, h2_ref, route_ref, eid_ref, *, sub):
    tm = x_ref.shape[0]
    half = a_ref.shape[1]
    n_sub = tm // sub

    def project(i):
        rows = slice(i * sub, (i + 1) * sub)
        acc = jnp.dot(a_ref[rows, :], w_ref[:half, :], preferred_element_type=F32)
        return acc + jnp.dot(r_ref[rows, :], w_ref[half:, :], preferred_element_type=F32)

    def epilogue(i, acc):
        rows = slice(i * sub, (i + 1) * sub)
        x1 = x_ref[rows, :] + acc
        x1_ref[rows, :] = x1
        ms = jnp.mean(x1 * x1, axis=-1, keepdims=True)
        h2 = (x1 * lax.rsqrt(ms + EPS)) * g_ref[...]
        h2_ref[rows, :] = _pack_halves(h2)
        logits = jnp.dot(h2.astype(BF16), wr_ref[...], preferred_element_type=F32) + br_ref[...]
        route = _route(logits)
        route_ref[rows, :] = route
        eid_ref[:, rows] = route.T[:TOP_K, :].astype(jnp.int32)

    acc = project(0)
    for i in range(n_sub):
        nxt = project(i + 1) if i + 1 < n_sub else None
        epilogue(i, acc)
        acc = nxt


def _outproj_router(a2d, r2d, x2d, w_out_bf16, g, wr_bf16, br, *, tm):
    n, d = x2d.shape
    half = a2d.shape[1]
    row = lambda width: pl.BlockSpec((tm, width), lambda i: (i, 0))
    full = lambda shape: pl.BlockSpec(shape, lambda i: (0, 0))
    return pl.pallas_call(
        functools.partial(_outproj_body, sub=min(tm, 2 * LANES)),
        grid=(n // tm,),
        in_specs=[row(half), row(half), row(d), full((2 * half, d)), full((1, d)), full((d, LANES)),
                  full((1, LANES))],
        out_specs=[row(d), row(d // 2), row(LANES), pl.BlockSpec((TOP_K, tm), lambda i: (0, i))],
        out_shape=[jax.ShapeDtypeStruct((n, d), F32), jax.ShapeDtypeStruct((n, d // 2), jnp.uint32),
                   jax.ShapeDtypeStruct((n, LANES), F32), jax.ShapeDtypeStruct((TOP_K, n), jnp.int32)],
        compiler_params=_params(("arbitrary",)),
        name="outproj_router",
    )(a2d, r2d, x2d, w_out_bf16, g, wr_bf16, br)


def _moe_plan(expert_id, tm):
    n_slots = expert_id.shape[0]
    n_tiles = n_slots // tm
    order = jnp.argsort(expert_id).astype(jnp.int32)
    sizes = jnp.zeros((N_EXPERTS,), jnp.int32).at[expert_id].add(1)
    ends = jnp.cumsum(sizes)
    starts = ends - sizes
    tile_row0 = jnp.arange(n_tiles, dtype=jnp.int32) * tm
    owner = lambda row: jnp.sum((ends[None, :] <= row[:, None]).astype(jnp.int32), axis=1)
    first_e = owner(tile_row0)
    last_e = owner(tile_row0 + (tm - 1))
    owners_upto = jnp.cumsum((sizes > 0).astype(jnp.int32))
    n_extra = (owners_upto[last_e] - owners_upto[first_e]).astype(jnp.int32)
    prev_last = jnp.concatenate([jnp.full((1,), -1, jnp.int32), last_e[:-1]])
    switch = (prev_last != first_e).astype(jnp.int32)
    eid = jnp.arange(N_EXPERTS, dtype=jnp.int32)
    cand = jnp.where((sizes[None, :] > 0) & (eid[None, :] > eid[:, None]), eid[None, :], N_EXPERTS)
    nxt = jnp.min(cand, axis=1)
    next_expert = jnp.where(nxt < N_EXPERTS, nxt, -1).astype(jnp.int32)
    spare = n_slots + jnp.arange(tm, dtype=jnp.int32)
    codes = jnp.concatenate([order.reshape(n_tiles, tm), spare[None, :]], axis=0).reshape(n_tiles + 1, 1, tm)
    return dict(first_e=first_e, n_extra=n_extra, switch=switch, starts=starts.astype(jnp.int32),
                ends=ends.astype(jnp.int32), next_expert=next_expert, codes=codes)


def _moe_body(fe_ref, nextra_ref, sw_ref, st_ref, en_ref, nx_ref,
              codes_hbm, h2_hbm, wg_hbm, wu_hbm, wd_hbm, ys_hbm,
              gidx, sidx, xbuf, obuf, wg_st, wu_st, wd_st, wg_rs, wu_rs, wd_rs,
              gisem, sisem, gsem, ssem, wsem, *, n_tokens, n_tiles, cast_rows, ff_chunks):
    t = pl.program_id(0)
    tm = xbuf.shape[1]
    last = n_tiles - 1
    spare_codes = n_tiles

    def gidx_copy(tile, slot):
        return pltpu.make_async_copy(codes_hbm.at[tile], gidx.at[slot], gisem.at[slot])

    def sidx_copy(tile, slot):
        return pltpu.make_async_copy(codes_hbm.at[tile], sidx.at[slot], sisem.at[slot])

    def weight_copies(ex):
        return (pltpu.make_async_copy(wg_hbm.at[ex], wg_st, wsem.at[0]),
                pltpu.make_async_copy(wu_hbm.at[ex], wu_st, wsem.at[1]),
                pltpu.make_async_copy(wd_hbm.at[ex], wd_st, wsem.at[2]))

    def gather_row(slot, r):
        code = gidx[slot, 0, r]
        tok = code - jnp.where(code >= n_tokens, n_tokens, 0)
        return pltpu.make_async_copy(h2_hbm.at[pl.ds(tok, 1)], xbuf.at[slot, pl.ds(r, 1)], gsem.at[slot])

    def scatter_row(slot, r):
        dst = sidx[slot, 0, r]
        return pltpu.make_async_copy(obuf.at[slot, pl.ds(r, 1)], ys_hbm.at[pl.ds(dst, 1)], ssem.at[slot])

    def wait_gather(slot):
        pltpu.make_async_copy(h2_hbm.at[pl.ds(0, tm)], xbuf.at[slot], gsem.at[slot]).wait()

    def wait_scatter(slot):
        pltpu.make_async_copy(obuf.at[slot], ys_hbm.at[pl.ds(0, tm)], ssem.at[slot]).wait()

    def for_rows(fn):
        def body(r, carry):
            fn(r)
            return carry
        lax.fori_loop(0, tm, body, 0, unroll=8)

    def take_over(ex, first):
        @pl.when(first)
        def _():
            for cp in weight_copies(ex):
                cp.start()

        for cp in weight_copies(ex):
            cp.wait()

        def cast_up(i, carry):
            r = pl.multiple_of(i * cast_rows, cast_rows)
            wg_rs[pl.ds(r, cast_rows), :] = wg_st[pl.ds(r, cast_rows), :].astype(BF16)
            wu_rs[pl.ds(r, cast_rows), :] = wu_st[pl.ds(r, cast_rows), :].astype(BF16)
            return carry

        def cast_down(i, carry):
            r = pl.multiple_of(i * cast_rows, cast_rows)
            wd_rs[pl.ds(r, cast_rows), :] = wd_st[pl.ds(r, cast_rows), :].astype(BF16)
            return carry

        lax.fori_loop(0, wg_st.shape[0] // cast_rows, cast_up, 0)
        lax.fori_loop(0, wd_st.shape[0] // cast_rows, cast_down, 0)
        nxt = nx_ref[ex]

        @pl.when(nxt >= 0)
        def _():
            for cp in weight_copies(nxt):
                cp.start(priority=1)

    def ffn(slot, ex, accumulate, row_dmas=None):
        x_lo, x_hi = (h.astype(BF16) for h in _unpack_halves(xbuf[slot]))
        hd = x_lo.shape[1]
        row = t * tm + lax.broadcasted_iota(jnp.int32, (tm, 1), 0)
        mine = jnp.logical_and(row >= st_ref[ex], row < en_ref[ex])
        d_ff = wg_rs.shape[1]
        n_chunks = ff_chunks if row_dmas is not None else 1
        fc = d_ff // n_chunks
        rows = tm // n_chunks
        contrib = None
        for c in range(n_chunks):
            cols = slice(c * fc, (c + 1) * fc)
            up = lambda w: (jnp.dot(x_lo, w[:hd, cols], preferred_element_type=F32)
                            + jnp.dot(x_hi, w[hd:, cols], preferred_element_type=F32))
            a = up(wg_rs)
            b = up(wu_rs)
            hmid = jnp.where(mine, (a * (1.0 / (1.0 + jnp.exp(-a)))) * b, 0.0).astype(BF16)
            part = jnp.dot(hmid, wd_rs[cols, :], preferred_element_type=F32)
            contrib = part if contrib is None else contrib + part
            if row_dmas is not None:
                row_dmas(c * rows, (c + 1) * rows)
        if accumulate:
            prev_lo, prev_hi = _unpack_halves(obuf[slot])
            contrib = jnp.concatenate([prev_lo, prev_hi], axis=1) + contrib
        obuf[slot] = _pack_halves(contrib)

    e0 = fe_ref[t]

    def step(p):
        q = 1 - p
        if p == 0:
            @pl.when(t == 0)
            def _():
                copies = (gidx_copy(0, 0), gidx_copy(min(1, last), 1), sidx_copy(spare_codes, 1))
                for cp in copies:
                    cp.start()
                for cp in copies:
                    cp.wait()
                obuf[1] = jnp.zeros(obuf.shape[1:], obuf.dtype)
                for_rows(lambda r: gather_row(0, r).start())

        gidx_copy(jnp.minimum(t + 2, last), p).start()
        sidx_copy(t, p).start()

        @pl.when(t >= 1)
        def _():
            gidx_copy(0, q).wait()
            sidx_copy(0, q).wait()
            wait_scatter(p)

        wait_gather(p)

        @pl.when(sw_ref[t] == 1)
        def _():
            take_over(e0, t == 0)

        def row_dmas(lo, hi):
            for r in range(lo, hi):
                scatter_row(q, r).start()
                gather_row(q, r).start()

        ffn(p, e0, accumulate=False, row_dmas=row_dmas)

    for parity in range(2):
        pl.when(lax.rem(t, 2) == parity)(functools.partial(step, parity))

    slot = lax.rem(t, 2)

    def further_owner(_, ex):
        nxt = nx_ref[ex]
        take_over(nxt, False)
        ffn(slot, nxt, accumulate=True)
        return nxt

    lax.fori_loop(0, nextra_ref[t], further_owner, e0)

    @pl.when(t == last)
    def _():
        gidx_copy(0, slot).wait()
        sidx_copy(0, slot).wait()
        for_rows(lambda r: scatter_row(slot, r).start())
        wait_gather(1 - slot)
        wait_scatter(1 - slot)
        wait_scatter(slot)


def _moe_ffn(h2, plan, w_gate, w_up, w_down, *, tm):
    n, hd = h2.shape
    d = 2 * hd
    n_slots = 2 * n
    n_tiles = n_slots // tm
    assert n_tiles >= 2 and n_slots % tm == 0
    d_ff = w_gate.shape[2]
    any_spec = pl.BlockSpec(memory_space=pl.ANY)
    grid_spec = pltpu.PrefetchScalarGridSpec(
        num_scalar_prefetch=6,
        grid=(n_tiles,),
        in_specs=[any_spec] * 5,
        out_specs=any_spec,
        scratch_shapes=[
            pltpu.SMEM((2, 1, tm), jnp.int32),
            pltpu.SMEM((2, 1, tm), jnp.int32),
            pltpu.VMEM((2, tm, hd), jnp.uint32),
            pltpu.VMEM((2, tm, hd), jnp.uint32),
            pltpu.VMEM((d, d_ff), F32),
            pltpu.VMEM((d, d_ff), F32),
            pltpu.VMEM((d_ff, d), F32),
            pltpu.VMEM((d, d_ff), BF16),
            pltpu.VMEM((d, d_ff), BF16),
            pltpu.VMEM((d_ff, d), BF16),
            pltpu.SemaphoreType.DMA((2,)),
            pltpu.SemaphoreType.DMA((2,)),
            pltpu.SemaphoreType.DMA((2,)),
            pltpu.SemaphoreType.DMA((2,)),
            pltpu.SemaphoreType.DMA((3,)),
        ],
    )
    return pl.pallas_call(
        functools.partial(_moe_body, n_tokens=n, n_tiles=n_tiles, cast_rows=128,
                          ff_chunks=d_ff // (2 * LANES)),
        grid_spec=grid_spec,
        out_shape=jax.ShapeDtypeStruct((n_slots + tm, hd), jnp.uint32),
        compiler_params=_params(("arbitrary",), vmem=MOE_VMEM_LIMIT),
        name="moe_ffn",
    )(plan["first_e"], plan["n_extra"], plan["switch"], plan["starts"], plan["ends"], plan["next_expert"],
      plan["codes"], h2, w_gate, w_up, w_down)


def _final_body(x1_ref, ys0_ref, ys1_ref, route_ref, g_ref, o_ref):
    d = x1_ref.shape[1]
    route = route_ref[...]
    g1, g2 = route[:, 2:3], route[:, 3:4]
    ya = _unpack_halves(ys0_ref[...])
    yb = _unpack_halves(ys1_ref[...])
    halves = (slice(0, d // 2), slice(d // 2, d))
    x2 = [x1_ref[:, cols] + (ya[i] * g1 + yb[i] * g2) for i, cols in enumerate(halves)]
    ms = sum(jnp.sum(h * h, axis=-1, keepdims=True) for h in x2) * (1.0 / d)
    inv = lax.rsqrt(ms + EPS)
    for h, cols in zip(x2, halves):
        o_ref[:, cols] = ((h * inv) * g_ref[:, cols]).astype(o_ref.dtype)


def _final(x1, ys, route, g, *, tm):
    n, d = x1.shape
    nb = n // tm
    return pl.pallas_call(
        _final_body,
        grid=(nb,),
        in_specs=[
            pl.BlockSpec((tm, d), lambda i: (i, 0)),
            pl.BlockSpec((tm, d // 2), lambda i: (i, 0)),
            pl.BlockSpec((tm, d // 2), lambda i: (nb + i, 0)),
            pl.BlockSpec((tm, LANES), lambda i: (i, 0)),
            pl.BlockSpec((1, d), lambda i: (0, 0)),
        ],
        out_specs=pl.BlockSpec((tm, d), lambda i: (i, 0)),
        out_shape=jax.ShapeDtypeStruct((n, d), F32),
        compiler_params=_params(("arbitrary",)),
        name="final_norm",
    )(x1, ys, ys, route, g)


def _layer(x, rel_bias, attn_norm_g, w_in, lq1, lk1, lq2, lk2, diff_subln_g, ret_gn_g, w_out, ffn_norm_g,
           w_group_router, b_group, w_inner_router, b_inner, w_gate_exp, w_up_exp, w_down_exp, *, layer,
           attn_tile, row_tile, moe_tile):
    b, s, d = x.shape
    n = b * s
    x2d = x.reshape(n, d)
    lam_init = 0.8 - 0.6 * math.exp(-0.3 * layer)

    dq_cols = (COL_DK - COL_DQ) * LANES
    col_scale = jnp.concatenate([jnp.full((1, dq_cols), DIFF_HALF ** -0.5 * LOG2E, F32),
                                 jnp.ones((1, IN_COLS - dq_cols), F32)], axis=1)
    proj = _inproj(x2d, attn_norm_g.reshape(1, d), w_in.astype(BF16), col_scale, tm=min(row_tile, n))
    proj3 = proj.reshape(b, s, IN_COLS)

    bias = _bias_tiles(rel_bias, attn_tile)
    vec = lambda a: a.reshape(1, -1)
    a_out = _diff_attention(proj3, bias, vec(lq1), vec(lk1), vec(lq2), vec(lk2), vec(diff_subln_g),
                            t=attn_tile, lam_init=lam_init)
    r_out = _retention(proj3, _retention_tables(s), vec(ret_gn_g))

    pad = LANES - N_GROUPS - N_EXPERTS
    wr = jnp.concatenate([w_group_router, jnp.transpose(w_inner_router, (1, 0, 2)).reshape(d, N_EXPERTS),
                          jnp.zeros((d, pad), F32)], axis=1).astype(BF16)
    br = jnp.concatenate([b_group, b_inner.reshape(-1), jnp.zeros((pad,), F32)]).reshape(1, LANES)
    x1, h2, route, eid = _outproj_router(a_out.reshape(n, -1), r_out.reshape(n, -1), x2d, w_out.astype(BF16),
                                         vec(ffn_norm_g), wr, br, tm=row_tile)

    plan = _moe_plan(eid.reshape(-1), moe_tile)
    ys = _moe_ffn(h2, plan, w_gate_exp, w_up_exp, w_down_exp, tm=moe_tile)
    return x1, ys, route


def kernel(x, rel_bias, attn_norm_g, w_in, lambda_q1, lambda_k1, lambda_q2, lambda_k2, diff_subln_g, ret_gn_g,
           w_out, ffn_norm_g, w_group_router, b_group, w_inner_router, b_inner, w_gate_exp, w_up_exp, w_down_exp,
           final_g):
    b, s, d = x.shape
    depth = w_in.shape[0]
    assert depth == 1, "the final norm is fused into the single layer's combine step"
    first = lambda a: a.reshape(a.shape[1:])
    x1, ys, route = _layer(x, rel_bias, attn_norm_g[0], w_in[0], lambda_q1[0], lambda_k1[0], lambda_q2[0],
                           lambda_k2[0], diff_subln_g[0], ret_gn_g[0], w_out[0], ffn_norm_g[0],
                           w_group_router[0], b_group[0], w_inner_router[0], b_inner[0], first(w_gate_exp),
                           first(w_up_exp), first(w_down_exp), layer=0, attn_tile=min(256, s), row_tile=512,
                           moe_tile=512)
    out = _final(x1, ys, route, final_g.reshape(1, d), tm=512)
    return out.reshape(b, s, d)
```

```python
import functools
import math

import numpy as np
import jax
import jax.numpy as jnp
from jax import lax
from jax.experimental import pallas as pl
from jax.experimental.pallas import tpu as pltpu

F32 = jnp.float32
BF16 = jnp.bfloat16

EPS = 1e-6
LANES = 128
N_DIFF_HEADS = 8
DIFF_HALF = 64
N_RET_HEADS = 8
RET_QK = 64
RET_CHUNK = 128
ROPE_BASE = 10000.0
N_BUCKETS = 32
MAX_DISTANCE = 128
N_GROUPS = 4
EXPERTS_PER_GROUP = 8
N_EXPERTS = N_GROUPS * EXPERTS_PER_GROUP
TOP_K = 2
MASK_VALUE = -1e30
LOG2E = math.log2(math.e)
VMEM_LIMIT = 56 * 1024 * 1024
MOE_VMEM_LIMIT = 60 * 1024 * 1024

COL_DQ, COL_DK, COL_DV = 0, 8, 16
COL_RQ, COL_RK, COL_RV, COL_RG = 24, 28, 32, 40
IN_COLS = 48 * LANES


def _params(sem, vmem=VMEM_LIMIT):
    return pltpu.CompilerParams(dimension_semantics=sem, vmem_limit_bytes=vmem)


def _pack_halves(x):
    hd = x.shape[1] // 2
    bits = lambda h: lax.bitcast_convert_type(h.astype(BF16).astype(F32), jnp.uint32)
    return lax.shift_right_logical(bits(x[:, :hd]), jnp.uint32(16)) | bits(x[:, hd:])


def _unpack_halves(w):
    return (lax.bitcast_convert_type(lax.shift_left(w, jnp.uint32(16)), F32),
            lax.bitcast_convert_type(w & jnp.uint32(0xFFFF0000), F32))


def _inproj_body(x_ref, g_ref, w_ref, cs_ref, o_ref):
    x = x_ref[...]
    acc = jnp.dot((x * g_ref[...]).astype(BF16), w_ref[...], preferred_element_type=F32)
    inv = lax.rsqrt(jnp.mean(x * x, axis=-1, keepdims=True) + EPS)
    o_ref[...] = ((acc * inv) * cs_ref[...]).astype(o_ref.dtype)


def _inproj(x2d, g, w, col_scale, *, tm):
    n, d = x2d.shape
    cols = w.shape[1]
    once = dict(pipeline_mode=pl.Buffered(1))
    return pl.pallas_call(
        _inproj_body,
        grid=(n // tm,),
        in_specs=[
            pl.BlockSpec((tm, d), lambda i: (i, 0)),
            pl.BlockSpec((1, d), lambda i: (0, 0)),
            pl.BlockSpec((d, cols), lambda i: (0, 0), **once),
            pl.BlockSpec((1, cols), lambda i: (0, 0)),
        ],
        out_specs=pl.BlockSpec((tm, cols), lambda i: (i, 0)),
        out_shape=jax.ShapeDtypeStruct((n, cols), BF16),
        compiler_params=_params(("arbitrary",)),
        name="inproj",
    )(x2d, g, w, col_scale)


def _t5_bucket_table(n):
    d = np.arange(n)
    max_exact = N_BUCKETS // 2
    log_ratio = np.log(np.maximum(d, 1).astype(np.float64) / max_exact) / math.log(MAX_DISTANCE / max_exact)
    large = np.minimum(max_exact + (log_ratio * (N_BUCKETS - max_exact)).astype(np.int64), N_BUCKETS - 1)
    return np.where(d < max_exact, d, large).astype(np.int32)


def _bias_body(rb_ref, bkt_ref, o_ref):
    h = pl.program_id(0)
    t = o_ref.shape[2]
    bkt = bkt_ref[...]
    far = rb_ref[N_BUCKETS - 1, h]
    by_dist = jnp.zeros(bkt.shape, F32)
    for b in range(N_BUCKETS - 1):
        by_dist = jnp.where(bkt == b, (rb_ref[b, h] - far) * LOG2E, by_dist)
    rows = (jnp.concatenate([jnp.full((1, t), MASK_VALUE, F32), by_dist[:, :t]], axis=1), by_dist)
    for i, row in enumerate(rows):
        shifted = pltpu.roll(jnp.broadcast_to(row, (t, 2 * t)), 0, 1, stride=1, stride_axis=0)
        o_ref[0, i] = shifted[:, t:]


def _bias_tiles(rel_bias, t):
    bkt = jnp.asarray(_t5_bucket_table(2 * t)).reshape(1, 2 * t)
    return pl.pallas_call(
        _bias_body,
        grid=(N_DIFF_HEADS,),
        in_specs=[
            pl.BlockSpec(memory_space=pltpu.SMEM),
            pl.BlockSpec((1, 2 * t), lambda h: (0, 0)),
        ],
        out_specs=pl.BlockSpec((1, 2, t, t), lambda h: (h, 0, 0, 0)),
        out_shape=jax.ShapeDtypeStruct((N_DIFF_HEADS, 2, t, t), F32),
        compiler_params=_params(("arbitrary",)),
        name="bias_tiles",
    )(rel_bias, bkt)


_NT = (((1,), (1,)), ((), ()))
_TN = (((0,), (0,)), ((), ()))


def _diffattn_body(q_ref, k_ref, v_ref, bias_ref, lq1_ref, lk1_ref, lq2_ref, lk2_ref, g_ref, o_ref, vt_ref, *,
                   t, lam_init, near_band):
    nq = k_ref.shape[1] // t
    half = t // 2
    lane = lax.broadcasted_iota(jnp.int32, (t, LANES), 1)
    lam = (jnp.exp(jnp.sum(lq1_ref[...] * lk1_ref[...], axis=-1, keepdims=True))
           - jnp.exp(jnp.sum(lq2_ref[...] * lk2_ref[...], axis=-1, keepdims=True)) + lam_init)
    vt_ref[:LANES, :] = v_ref[0].T
    vt_ref[LANES:, :] = jnp.ones((vt_ref.shape[0] - LANES, vt_ref.shape[1]), BF16)
    qmask = {}
    run_max = {}
    acc = {}

    def key_range(item):
        _, j, part = item
        return j * t + (half if part == "diag_lower" else 0), j * t + (half if part == "diag_upper" else t)

    def scores(item, c):
        qi, j, part = item
        if qi not in qmask:
            q = q_ref[0, qi * t:(qi + 1) * t, :]
            zero = jnp.zeros_like(q)
            qmask[qi] = (jnp.where(lane < DIFF_HALF, q, zero), jnp.where(lane >= DIFF_HALF, q, zero))
        k_lo, k_hi = key_range(item)
        qsel = qmask[qi][c][half:] if part == "diag_lower" else qmask[qi][c]
        s = lax.dot_general(k_ref[0, k_lo:k_hi, :], qsel, _NT, preferred_element_type=F32)
        if part == "diag_upper":
            s = s + bias_ref[0, 0, :half, :]
        elif part == "diag_lower":
            s = s + bias_ref[0, 0, half:, half:]
        elif j == qi - 1:
            if near_band >= t:
                s = s + bias_ref[0, 1]
            else:
                corner = s[t - near_band:, :near_band] + bias_ref[0, 1, t - near_band:, :near_band]
                bottom = jnp.concatenate([corner, s[t - near_band:, near_band:]], axis=1)
                s = jnp.concatenate([s[:t - near_band], bottom], axis=0)
        return s

    def softmax_step(item, c, s):
        qi, j, part = item
        smax = jnp.max(s, axis=0, keepdims=True)
        if j == 0 and part != "diag_lower":
            m_new, alpha = smax, None
            run_max[qi, c] = m_new
        else:
            m_full = run_max[qi, c]
            m_old = m_full[:, half:] if part == "diag_lower" else m_full
            m_new = jnp.maximum(m_old, smax)
            alpha = jnp.exp2(m_old - m_new)
            run_max[qi, c] = jnp.concatenate([m_full[:, :half], m_new], axis=1) if part == "diag_lower" else m_new
        return jnp.exp2(s - m_new).astype(BF16), alpha

    def values_step(item, c, p, alpha):
        qi, _, part = item
        k_lo, k_hi = key_range(item)
        r = jnp.dot(vt_ref[:, k_lo:k_hi], p, preferred_element_type=F32)
        if alpha is None:
            acc[qi, c] = r
        elif part == "diag_lower":
            acc[qi, c] = jnp.concatenate([acc[qi, c][:, :half], alpha * acc[qi, c][:, half:] + r], axis=1)
        else:
            acc[qi, c] = alpha * acc[qi, c] + r

    def finish(qi):
        outs = []
        for c in range(2):
            a = acc.pop((qi, c))
            del run_max[qi, c]
            outs.append(a[:LANES] * (1.0 / a[LANES:LANES + 1]))
        ot = (outs[0] - lam * outs[1]).T
        ms = jnp.mean(ot * ot, axis=-1, keepdims=True)
        o_ref[0, qi * t:(qi + 1) * t, :] = (
            ((ot * lax.rsqrt(ms + EPS)) * g_ref[...]) * (1.0 - lam_init)).astype(o_ref.dtype)
        del qmask[qi]

    items = []
    for qi in range(nq):
        items += [(qi, j, "full") for j in range(qi)] + [(qi, qi, "diag_upper"), (qi, qi, "diag_lower")]
    ss = [scores(items[0], c) for c in range(2)]
    pending = None
    for n, item in enumerate(items):
        nxt, cur = [], []
        for c in range(2):
            if n + 1 < len(items):
                nxt.append(scores(items[n + 1], c))
            cur.append(softmax_step(item, c, ss[c]))
            if pending is not None:
                values_step(pending[0], c, *pending[1][c])
        if pending is not None and pending[0][2] == "diag_lower":
            finish(pending[0][0])
        pending, ss = (item, cur), nxt
    for c in range(2):
        values_step(pending[0], c, *pending[1][c])
    finish(pending[0][0])


def _diff_attention(proj3, bias, lq1, lk1, lq2, lk2, subln_g, *, t, lam_init):
    b, s, _ = proj3.shape
    vec = pl.BlockSpec((1, DIFF_HALF), lambda bi, h: (0, 0))
    head = lambda col: pl.BlockSpec((1, s, LANES), lambda bi, h: (bi, 0, col + h))
    first_far = int(np.argmax(_t5_bucket_table(2 * t) == N_BUCKETS - 1))
    near_band = -(-first_far // LANES) * LANES
    ones_rows = 16
    return pl.pallas_call(
        functools.partial(_diffattn_body, t=t, lam_init=lam_init, near_band=near_band),
        grid=(b, N_DIFF_HEADS),
        in_specs=[
            head(COL_DQ), head(COL_DK), head(COL_DV),
            pl.BlockSpec((1, 2, t, t), lambda bi, h: (h, 0, 0, 0)),
            vec, vec, vec, vec,
            pl.BlockSpec((1, LANES), lambda bi, h: (0, 0)),
        ],
        out_specs=pl.BlockSpec((1, s, LANES), lambda bi, h: (bi, 0, h)),
        out_shape=jax.ShapeDtypeStruct((b, s, N_DIFF_HEADS * LANES), BF16),
        scratch_shapes=[pltpu.VMEM((LANES + ones_rows, s), BF16)],
        compiler_params=_params(("arbitrary", "arbitrary")),
        name="diff_attention",
    )(proj3, proj3, proj3, bias, lq1, lk1, lq2, lk2, subln_g)


def _retention_tables(s):
    c = RET_CHUNK
    half = RET_QK // 2
    inv_freq = ROPE_BASE ** (-jnp.arange(0, RET_QK, 2, dtype=F32) / RET_QK)
    ang = jnp.arange(s, dtype=F32)[:, None] * inv_freq[None, :]
    cos, sin = jnp.cos(ang), jnp.sin(ang)
    cos_t = jnp.tile(cos, (1, LANES // half))
    sin_t = jnp.tile(jnp.concatenate([-sin, sin], axis=1), (1, LANES // RET_QK))
    log_g = jnp.log(1.0 - jnp.exp2(-5.0 - jnp.arange(N_RET_HEADS, dtype=F32)))
    n = jnp.arange(c, dtype=F32)
    diff = n[:, None] - n[None, :]
    inner = jnp.where(diff[None] >= 0, jnp.exp(jnp.maximum(diff, 0.0)[None] * log_g[:, None, None]), 0.0)
    cross = jnp.exp((n[None] + 1.0) * log_g[:, None])
    key = jnp.exp((c - 1.0 - n[None]) * log_g[:, None])
    chunk = jnp.exp(c * log_g)
    bc = lambda a: jnp.broadcast_to(a[..., None], a.shape + (LANES,)).astype(F32)
    return cos_t, sin_t, inner.astype(F32), bc(cross), bc(key), bc(chunk[:, None])


def _retention_body(rq_ref, rk_ref, rv_ref, rg_ref, cos_ref, sin_ref, inner_ref, cross_ref, key_ref,
                    chunk_ref, gn_ref, o_ref, state_ref):
    c = RET_CHUNK
    nc = rq_ref.shape[1] // c
    nh = N_RET_HEADS
    heads_per_group = nh // 2
    state_ref[...] = jnp.zeros(state_ref.shape, F32)
    lane = lax.broadcasted_iota(jnp.int32, (c, LANES), 1)
    first_half = (lane % RET_QK) < (RET_QK // 2)
    head_mask = [lane < RET_QK, lane >= RET_QK]
    slab = lambda i: slice(i * LANES, (i + 1) * LANES)

    def rotary(x, cos, sin):
        rot = jnp.where(first_half, pltpu.roll(x, LANES - RET_QK // 2, 1), pltpu.roll(x, RET_QK // 2, 1))
        return x * cos + rot * sin

    def chunk_step(ci, carry):
        r = pl.multiple_of(ci * c, c)
        rows = pl.ds(r, c)
        cos = cos_ref[rows, :]
        sin = sin_ref[rows, :]
        q_pair, k_pair, kb_pair = [], [], []
        for pr in range(nh // 2):
            q_pair.append(rotary(rq_ref[0, rows, slab(pr)].astype(F32), cos, sin))
            k = rotary(rk_ref[0, rows, slab(pr)].astype(F32), cos, sin) * (RET_QK ** -0.5)
            k_pair.append(k)
            kb_pair.append(k.astype(BF16))
        for g0 in range(0, nh, heads_per_group):
            hs = range(g0, g0 + heads_per_group)
            qh = {h: jnp.where(head_mask[h % 2], q_pair[h // 2], 0.0).astype(BF16) for h in hs}
            kd = {h: (k_pair[h // 2] * key_ref[h]).astype(BF16) for h in hs}
            vh = {h: rv_ref[0, rows, slab(h)] for h in hs}
            scores = {h: lax.dot_general(qh[h], kb_pair[h // 2], _NT, preferred_element_type=F32) for h in hs}
            cross = {h: jnp.dot(qh[h], state_ref[h].astype(BF16), preferred_element_type=F32) for h in hs}
            kv = {h: lax.dot_general(kd[h], vh[h], _TN, preferred_element_type=F32) for h in hs}
            decayed = {h: (scores[h] * inner_ref[h]).astype(BF16) for h in hs}
            for h in hs:
                state_ref[h] = state_ref[h] * chunk_ref[h] + kv[h]
            inner = {h: jnp.dot(decayed[h], vh[h], preferred_element_type=F32) for h in hs}
            for h in hs:
                y = inner[h] + cross[h] * cross_ref[h]
                mu = jnp.mean(y, axis=-1, keepdims=True)
                yc = y - mu
                var = jnp.mean(yc * yc, axis=-1, keepdims=True)
                yn = (yc * lax.rsqrt(var + EPS)) * gn_ref[:, slab(h)]
                gate = rg_ref[0, rows, slab(h)].astype(F32)
                silu = gate * (1.0 / (1.0 + jnp.exp(-gate)))
                o_ref[0, rows, slab(h)] = (silu * yn).astype(o_ref.dtype)
        return carry

    lax.fori_loop(0, nc, chunk_step, 0, unroll=2)


def _retention(proj3, tables, gn_g):
    b, s, _ = proj3.shape
    cos_t, sin_t, inner, cross, key, chunk = tables
    c = RET_CHUNK
    nh = N_RET_HEADS
    qk_w = nh * RET_QK
    v_w = nh * LANES
    full = lambda shape: pl.BlockSpec(shape, lambda bi: (0,) * len(shape))
    return pl.pallas_call(
        _retention_body,
        grid=(b,),
        in_specs=[
            pl.BlockSpec((1, s, qk_w), lambda bi: (bi, 0, COL_RQ * LANES // qk_w)),
            pl.BlockSpec((1, s, qk_w), lambda bi: (bi, 0, COL_RK * LANES // qk_w)),
            pl.BlockSpec((1, s, v_w), lambda bi: (bi, 0, COL_RV * LANES // v_w)),
            pl.BlockSpec((1, s, v_w), lambda bi: (bi, 0, COL_RG * LANES // v_w)),
            full((s, LANES)), full((s, LANES)),
            full((nh, c, c)), full((nh, c, LANES)), full((nh, c, LANES)), full((nh, 1, LANES)),
            full((1, v_w)),
        ],
        out_specs=pl.BlockSpec((1, s, v_w), lambda bi: (bi, 0, 0)),
        out_shape=jax.ShapeDtypeStruct((b, s, v_w), BF16),
        scratch_shapes=[pltpu.VMEM((nh, LANES, LANES), F32)],
        compiler_params=_params(("arbitrary",)),
        name="retention",
    )(proj3, proj3, proj3, proj3, cos_t, sin_t, inner, cross, key, chunk, gn_g)


def _route(logits):
    lane = lax.broadcasted_iota(jnp.int32, logits.shape, 1).astype(F32)
    big = float(LANES)
    is_group = lane < N_GROUPS
    gl = jnp.where(is_group, logits, -jnp.inf)
    gmax = jnp.max(gl, axis=-1, keepdims=True)
    gidx = jnp.min(jnp.where(gl == gmax, lane, big), axis=-1, keepdims=True)
    p_group = 1.0 / jnp.sum(jnp.where(is_group, jnp.exp(gl - gmax), 0.0), axis=-1, keepdims=True)
    lo = N_GROUPS + EXPERTS_PER_GROUP * gidx
    il = jnp.where(lane >= lo, jnp.where(lane < lo + EXPERTS_PER_GROUP, logits, -jnp.inf), -jnp.inf)
    v1 = jnp.max(il, axis=-1, keepdims=True)
    i1 = jnp.min(jnp.where(il == v1, lane, big), axis=-1, keepdims=True)
    il2 = jnp.where(lane == i1, -jnp.inf, il)
    v2 = jnp.max(il2, axis=-1, keepdims=True)
    i2 = jnp.min(jnp.where(il2 == v2, lane, big), axis=-1, keepdims=True)
    e2 = jnp.exp(v2 - v1)
    inv = 1.0 / (1.0 + e2)
    g1 = p_group * inv
    g2 = p_group * (e2 * inv)
    out = jnp.where(lane == 0.0, i1 - N_GROUPS,
                    jnp.where(lane == 1.0, i2 - N_GROUPS,
                              jnp.where(lane == 2.0, g1, jnp.where(lane == 3.0, g2, 0.0))))
    return out


def _outproj_body(a_ref, r_ref, x_ref, w_ref, g_ref, wr_ref, br_ref, x1_ref, h2_ref, route_ref, eid_ref, *, sub):
    tm = x_ref.shape[0]
    half = a_ref.shape[1]
    n_sub = tm // sub

    def project(i):
        rows = slice(i * sub, (i + 1) * sub)
        acc = jnp.dot(a_ref[rows, :], w_ref[:half, :], preferred_element_type=F32)
        return acc + jnp.dot(r_ref[rows, :], w_ref[half:, :], preferred_element_type=F32)

    def epilogue(i, acc):
        rows = slice(i * sub, (i + 1) * sub)
        x1 = x_ref[rows, :] + acc
        x1_ref[rows, :] = x1
        ms = jnp.mean(x1 * x1, axis=-1, keepdims=True)
        h2 = (x1 * lax.rsqrt(ms + EPS)) * g_ref[...]
        h2_ref[rows, :] = _pack_halves(h2)
        logits = jnp.dot(h2.astype(BF16), wr_ref[...], preferred_element_type=F32) + br_ref[...]
        route = _route(logits)
        route_ref[rows, :] = route
        eid_ref[:, rows] = route.T[:TOP_K, :].astype(jnp.int32)

    acc = project(0)
    for i in range(n_sub):
        nxt = project(i + 1) if i + 1 < n_sub else None
        epilogue(i, acc)
        acc = nxt


def _outproj_router(a2d, r2d, x2d, w_out_bf16, g, wr_bf16, br, *, tm):
    n, d = x2d.shape
    half = a2d.shape[1]
    row = lambda width: pl.BlockSpec((tm, width), lambda i: (i, 0))
    full = lambda shape: pl.BlockSpec(shape, lambda i: (0, 0))
    return pl.pallas_call(
        functools.partial(_outproj_body, sub=min(tm, 2 * LANES)),
        grid=(n // tm,),
        in_specs=[row(half), row(half), row(d), full((2 * half, d)), full((1, d)), full((d, LANES)),
                  full((1, LANES))],
        out_specs=[row(d), row(d // 2), row(LANES), pl.BlockSpec((TOP_K, tm), lambda i: (0, i))],
        out_shape=[jax.ShapeDtypeStruct((n, d), F32), jax.ShapeDtypeStruct((n, d // 2), jnp.uint32),
                   jax.ShapeDtypeStruct((n, LANES), F32), jax.ShapeDtypeStruct((TOP_K, n), jnp.int32)],
        compiler_params=_params(("arbitrary",)),
        name="outproj_router",
    )(a2d, r2d, x2d, w_out_bf16, g, wr_bf16, br)


def _moe_plan(expert_id, tm):
    n_slots = expert_id.shape[0]
    n_tiles = n_slots // tm
    order = jnp.argsort(expert_id).astype(jnp.int32)
    sizes = jnp.zeros((N_EXPERTS,), jnp.int32).at[expert_id].add(1)
    ends = jnp.cumsum(sizes)
    starts = ends - sizes
    tile_row0 = jnp.arange(n_tiles, dtype=jnp.int32) * tm
    owner = lambda row: jnp.sum((ends[None, :] <= row[:, None]).astype(jnp.int32), axis=1)
    first_e = owner(tile_row0)
    last_e = owner(tile_row0 + (tm - 1))
    owners_upto = jnp.cumsum((sizes > 0).astype(jnp.int32))
    n_extra = (owners_upto[last_e] - owners_upto[first_e]).astype(jnp.int32)
    prev_last = jnp.concatenate([jnp.full((1,), -1, jnp.int32), last_e[:-1]])
    switch = (prev_last != first_e).astype(jnp.int32)
    eid = jnp.arange(N_EXPERTS, dtype=jnp.int32)
    cand = jnp.where((sizes[None, :] > 0) & (eid[None, :] > eid[:, None]), eid[None, :], N_EXPERTS)
    nxt = jnp.min(cand, axis=1)
    next_expert = jnp.where(nxt < N_EXPERTS, nxt, -1).astype(jnp.int32)
    spare = n_slots + jnp.arange(tm, dtype=jnp.int32)
    codes = jnp.concatenate([order.reshape(n_tiles, tm), spare[None, :]], axis=0).reshape(n_tiles + 1, 1, tm)
    return dict(first_e=first_e, n_extra=n_extra, switch=switch, starts=starts.astype(jnp.int32),
                ends=ends.astype(jnp.int32), next_expert=next_expert, codes=codes)


def _moe_body(fe_ref, nextra_ref, sw_ref, st_ref, en_ref, nx_ref,
              codes_hbm, h2_hbm, wg_hbm, wu_hbm, wd_hbm, ys_hbm,
              gidx, sidx, xbuf, obuf, wg_st, wu_st, wd_st, wg_rs, wu_rs, wd_rs,
              gisem, sisem, gsem, ssem, wsem, *, n_tokens, n_tiles, cast_rows, ff_chunks):
    t = pl.program_id(0)
    tm = xbuf.shape[1]
    last = n_tiles - 1
    spare_codes = n_tiles

    def gidx_copy(tile, slot):
        return pltpu.make_async_copy(codes_hbm.at[tile], gidx.at[slot], gisem.at[slot])

    def sidx_copy(tile, slot):
        return pltpu.make_async_copy(codes_hbm.at[tile], sidx.at[slot], sisem.at[slot])

    def weight_copies(ex):
        return (pltpu.make_async_copy(wg_hbm.at[ex], wg_st, wsem.at[0]),
                pltpu.make_async_copy(wu_hbm.at[ex], wu_st, wsem.at[1]),
                pltpu.make_async_copy(wd_hbm.at[ex], wd_st, wsem.at[2]))

    def gather_row(slot, r):
        code = gidx[slot, 0, r]
        tok = code - jnp.where(code >= n_tokens, n_tokens, 0)
        return pltpu.make_async_copy(h2_hbm.at[pl.ds(tok, 1)], xbuf.at[slot, pl.ds(r, 1)], gsem.at[slot])

    def scatter_row(slot, r):
        dst = sidx[slot, 0, r]
        return pltpu.make_async_copy(obuf.at[slot, pl.ds(r, 1)], ys_hbm.at[pl.ds(dst, 1)], ssem.at[slot])

    def wait_gather(slot):
        pltpu.make_async_copy(h2_hbm.at[pl.ds(0, tm)], xbuf.at[slot], gsem.at[slot]).wait()

    def wait_scatter(slot):
        pltpu.make_async_copy(obuf.at[slot], ys_hbm.at[pl.ds(0, tm)], ssem.at[slot]).wait()

    def for_rows(fn):
        def body(r, carry):
            fn(r)
            return carry
        lax.fori_loop(0, tm, body, 0, unroll=8)

    def take_over(ex, first):
        @pl.when(first)
        def _():
            for cp in weight_copies(ex):
                cp.start()

        for cp in weight_copies(ex):
            cp.wait()

        def cast_up(i, carry):
            r = pl.multiple_of(i * cast_rows, cast_rows)
            wg_rs[pl.ds(r, cast_rows), :] = wg_st[pl.ds(r, cast_rows), :].astype(BF16)
            wu_rs[pl.ds(r, cast_rows), :] = wu_st[pl.ds(r, cast_rows), :].astype(BF16)
            return carry

        def cast_down(i, carry):
            r = pl.multiple_of(i * cast_rows, cast_rows)
            wd_rs[pl.ds(r, cast_rows), :] = wd_st[pl.ds(r, cast_rows), :].astype(BF16)
            return carry

        lax.fori_loop(0, wg_st.shape[0] // cast_rows, cast_up, 0)
        lax.fori_loop(0, wd_st.shape[0] // cast_rows, cast_down, 0)
        nxt = nx_ref[ex]

        @pl.when(nxt >= 0)
        def _():
            for cp in weight_copies(nxt):
                cp.start(priority=1)

    def ffn(slot, ex, accumulate, row_dmas=None):
        x_lo, x_hi = (h.astype(BF16) for h in _unpack_halves(xbuf[slot]))
        hd = x_lo.shape[1]
        row = t * tm + lax.broadcasted_iota(jnp.int32, (tm, 1), 0)
        mine = jnp.logical_and(row >= st_ref[ex], row < en_ref[ex])
        d_ff = wg_rs.shape[1]
        n_chunks = ff_chunks if row_dmas is not None else 1
        fc = d_ff // n_chunks
        rows = tm // n_chunks
        contrib = None
        for c in range(n_chunks):
            cols = slice(c * fc, (c + 1) * fc)
            up = lambda w: (jnp.dot(x_lo, w[:hd, cols], preferred_element_type=F32)
                            + jnp.dot(x_hi, w[hd:, cols], preferred_element_type=F32))
            a = up(wg_rs)
            b = up(wu_rs)
            hmid = jnp.where(mine, (a * (1.0 / (1.0 + jnp.exp(-a)))) * b, 0.0).astype(BF16)
            part = jnp.dot(hmid, wd_rs[cols, :], preferred_element_type=F32)
            contrib = part if contrib is None else contrib + part
            if row_dmas is not None:
                row_dmas(c * rows, (c + 1) * rows)
        if accumulate:
            prev_lo, prev_hi = _unpack_halves(obuf[slot])
            contrib = jnp.concatenate([prev_lo, prev_hi], axis=1) + contrib
        obuf[slot] = _pack_halves(contrib)

    e0 = fe_ref[t]

    def step(p):
        q = 1 - p
        if p == 0:
            @pl.when(t == 0)
            def _():
                copies = (gidx_copy(0, 0), gidx_copy(min(1, last), 1), sidx_copy(spare_codes, 1))
                for cp in copies:
                    cp.start()
                for cp in copies:
                    cp.wait()
                obuf[1] = jnp.zeros(obuf.shape[1:], obuf.dtype)
                for_rows(lambda r: gather_row(0, r).start())

        gidx_copy(jnp.minimum(t + 2, last), p).start()
        sidx_copy(t, p).start()

        @pl.when(t >= 1)
        def _():
            gidx_copy(0, q).wait()
            sidx_copy(0, q).wait()
            wait_scatter(p)

        wait_gather(p)

        @pl.when(sw_ref[t] == 1)
        def _():
            take_over(e0, t == 0)

        def row_dmas(lo, hi):
            for r in range(lo, hi):
                scatter_row(q, r).start()
                gather_row(q, r).start()

        ffn(p, e0, accumulate=False, row_dmas=row_dmas)

    for parity in range(2):
        pl.when(lax.rem(t, 2) == parity)(functools.partial(step, parity))

    slot = lax.rem(t, 2)

    def further_owner(_, ex):
        nxt = nx_ref[ex]
        take_over(nxt, False)
        ffn(slot, nxt, accumulate=True)
        return nxt

    lax.fori_loop(0, nextra_ref[t], further_owner, e0)

    @pl.when(t == last)
    def _():
        gidx_copy(0, slot).wait()
        sidx_copy(0, slot).wait()
        for_rows(lambda r: scatter_row(slot, r).start())
        wait_gather(1 - slot)
        wait_scatter(1 - slot)
        wait_scatter(slot)


def _moe_ffn(h2, plan, w_gate, w_up, w_down, *, tm):
    n, hd = h2.shape
    d = 2 * hd
    n_slots = 2 * n
    n_tiles = n_slots // tm
    assert n_tiles >= 2 and n_slots % tm == 0
    d_ff = w_gate.shape[2]
    any_spec = pl.BlockSpec(memory_space=pl.ANY)
    grid_spec = pltpu.PrefetchScalarGridSpec(
        num_scalar_prefetch=6,
        grid=(n_tiles,),
        in_specs=[any_spec] * 5,
        out_specs=any_spec,
        scratch_shapes=[
            pltpu.SMEM((2, 1, tm), jnp.int32),
            pltpu.SMEM((2, 1, tm), jnp.int32),
            pltpu.VMEM((2, tm, hd), jnp.uint32),
            pltpu.VMEM((2, tm, hd), jnp.uint32),
            pltpu.VMEM((d, d_ff), F32),
            pltpu.VMEM((d, d_ff), F32),
            pltpu.VMEM((d_ff, d), F32),
            pltpu.VMEM((d, d_ff), BF16),
            pltpu.VMEM((d, d_ff), BF16),
            pltpu.VMEM((d_ff, d), BF16),
            pltpu.SemaphoreType.DMA((2,)),
            pltpu.SemaphoreType.DMA((2,)),
            pltpu.SemaphoreType.DMA((2,)),
            pltpu.SemaphoreType.DMA((2,)),
            pltpu.SemaphoreType.DMA((3,)),
        ],
    )
    return pl.pallas_call(
        functools.partial(_moe_body, n_tokens=n, n_tiles=n_tiles, cast_rows=128,
                          ff_chunks=d_ff // (2 * LANES)),
        grid_spec=grid_spec,
        out_shape=jax.ShapeDtypeStruct((n_slots + tm, hd), jnp.uint32),
        compiler_params=_params(("arbitrary",), vmem=MOE_VMEM_LIMIT),
        name="moe_ffn",
    )(plan["first_e"], plan["n_extra"], plan["switch"], plan["starts"], plan["ends"], plan["next_expert"],
      plan["codes"], h2, w_gate, w_up, w_down)


def _final_body(x1_ref, ys0_ref, ys1_ref, route_ref, g_ref, o_ref):
    d = x1_ref.shape[1]
    route = route_ref[...]
    g1, g2 = route[:, 2:3], route[:, 3:4]
    ya = _unpack_halves(ys0_ref[...])
    yb = _unpack_halves(ys1_ref[...])
    halves = (slice(0, d // 2), slice(d // 2, d))
    x2 = [x1_ref[:, cols] + (ya[i] * g1 + yb[i] * g2) for i, cols in enumerate(halves)]
    ms = sum(jnp.sum(h * h, axis=-1, keepdims=True) for h in x2) * (1.0 / d)
    inv = lax.rsqrt(ms + EPS)
    for h, cols in zip(x2, halves):
        o_ref[:, cols] = ((h * inv) * g_ref[:, cols]).astype(o_ref.dtype)


def _final(x1, ys, route, g, *, tm):
    n, d = x1.shape
    nb = n // tm
    return pl.pallas_call(
        _final_body,
        grid=(nb,),
        in_specs=[
            pl.BlockSpec((tm, d), lambda i: (i, 0)),
            pl.BlockSpec((tm, d // 2), lambda i: (i, 0)),
            pl.BlockSpec((tm, d // 2), lambda i: (nb + i, 0)),
            pl.BlockSpec((tm, LANES), lambda i: (i, 0)),
            pl.BlockSpec((1, d), lambda i: (0, 0)),
        ],
        out_specs=pl.BlockSpec((tm, d), lambda i: (i, 0)),
        out_shape=jax.ShapeDtypeStruct((n, d), F32),
        compiler_params=_params(("arbitrary",)),
        name="final_norm",
    )(x1, ys, ys, route, g)


def _layer(x, rel_bias, attn_norm_g, w_in, lq1, lk1, lq2, lk2, diff_subln_g, ret_gn_g, w_out, ffn_norm_g,
           w_group_router, b_group, w_inner_router, b_inner, w_gate_exp, w_up_exp, w_down_exp, *, layer,
           attn_tile, row_tile, moe_tile):
    b, s, d = x.shape
    n = b * s
    x2d = x.reshape(n, d)
    lam_init = 0.8 - 0.6 * math.exp(-0.3 * layer)

    dq_cols = (COL_DK - COL_DQ) * LANES
    col_scale = jnp.concatenate([jnp.full((1, dq_cols), DIFF_HALF ** -0.5 * LOG2E, F32),
                                 jnp.ones((1, IN_COLS - dq_cols), F32)], axis=1)
    proj = _inproj(x2d, attn_norm_g.reshape(1, d), w_in.astype(BF16), col_scale, tm=min(row_tile, n))
    proj3 = proj.reshape(b, s, IN_COLS)

    bias = _bias_tiles(rel_bias, attn_tile)
    vec = lambda a: a.reshape(1, -1)
    a_out = _diff_attention(proj3, bias, vec(lq1), vec(lk1), vec(lq2), vec(lk2), vec(diff_subln_g),
                            t=attn_tile, lam_init=lam_init)
    r_out = _retention(proj3, _retention_tables(s), vec(ret_gn_g))

    pad = LANES - N_GROUPS - N_EXPERTS
    wr = jnp.concatenate([w_group_router, jnp.transpose(w_inner_router, (1, 0, 2)).reshape(d, N_EXPERTS),
                          jnp.zeros((d, pad), F32)], axis=1).astype(BF16)
    br = jnp.concatenate([b_group, b_inner.reshape(-1), jnp.zeros((pad,), F32)]).reshape(1, LANES)
    x1, h2, route, eid = _outproj_router(a_out.reshape(n, -1), r_out.reshape(n, -1), x2d, w_out.astype(BF16),
                                         vec(ffn_norm_g), wr, br, tm=row_tile)

    plan = _moe_plan(eid.reshape(-1), moe_tile)
    ys = _moe_ffn(h2, plan, w_gate_exp, w_up_exp, w_down_exp, tm=moe_tile)
    return x1, ys, route


def kernel(x, rel_bias, attn_norm_g, w_in, lambda_q1, lambda_k1, lambda_q2, lambda_k2, diff_subln_g, ret_gn_g,
           w_out, ffn_norm_g, w_group_router, b_group, w_inner_router, b_inner, w_gate_exp, w_up_exp, w_down_exp,
           final_g):
    b, s, d = x.shape
    depth = w_in.shape[0]
    assert depth == 1, "the final norm is fused into the single layer's combine step"
    first = lambda a: a.reshape(a.shape[1:])
    x1, ys, route = _layer(x, rel_bias, attn_norm_g[0], w_in[0], lambda_q1[0], lambda_k1[0], lambda_q2[0],
                           lambda_k2[0], diff_subln_g[0], ret_gn_g[0], w_out[0], ffn_norm_g[0],
                           w_group_router[0], b_group[0], w_inner_router[0], b_inner[0], first(w_gate_exp),
                           first(w_up_exp), first(w_down_exp), layer=0, attn_tile=min(256, s), row_tile=512,
                           moe_tile=256)
    out = _final(x1, ys, route, final_g.reshape(1, d), tm=512)
    return out.reshape(b, s, d)
```

```python
import functools
import math

import numpy as np
import jax
import jax.numpy as jnp
from jax import lax
from jax.experimental import pallas as pl
from jax.experimental.pallas import tpu as pltpu

F32 = jnp.float32
BF16 = jnp.bfloat16

EPS = 1e-6
LANES = 128
N_DIFF_HEADS = 8
DIFF_HALF = 64
N_RET_HEADS = 8
RET_QK = 64
RET_CHUNK = 128
ROPE_BASE = 10000.0
N_BUCKETS = 32
MAX_DISTANCE = 128
N_GROUPS = 4
EXPERTS_PER_GROUP = 8
N_EXPERTS = N_GROUPS * EXPERTS_PER_GROUP
TOP_K = 2
MASK_VALUE = -1e30
LOG2E = math.log2(math.e)
VMEM_LIMIT = 56 * 1024 * 1024
MOE_VMEM_LIMIT = 60 * 1024 * 1024

COL_DQ, COL_DK, COL_DV = 0, 8, 16
COL_RQ, COL_RK, COL_RV, COL_RG = 24, 28, 32, 40
IN_COLS = 48 * LANES


def _params(sem, vmem=VMEM_LIMIT):
    return pltpu.CompilerParams(dimension_semantics=sem, vmem_limit_bytes=vmem)


def _pack_halves(x):
    hd = x.shape[1] // 2
    bits = lambda h: lax.bitcast_convert_type(h.astype(BF16).astype(F32), jnp.uint32)
    return lax.shift_right_logical(bits(x[:, :hd]), jnp.uint32(16)) | bits(x[:, hd:])


def _unpack_halves(w):
    return (lax.bitcast_convert_type(lax.shift_left(w, jnp.uint32(16)), F32),
            lax.bitcast_convert_type(w & jnp.uint32(0xFFFF0000), F32))


def _inproj_body(x_ref, g_ref, w_ref, cs_ref, o_ref):
    x = x_ref[...]
    acc = jnp.dot((x * g_ref[...]).astype(BF16), w_ref[...], preferred_element_type=F32)
    inv = lax.rsqrt(jnp.mean(x * x, axis=-1, keepdims=True) + EPS)
    o_ref[...] = ((acc * inv) * cs_ref[...]).astype(o_ref.dtype)


def _inproj(x2d, g, w, col_scale, *, tm):
    n, d = x2d.shape
    cols = w.shape[1]
    once = dict(pipeline_mode=pl.Buffered(1))
    return pl.pallas_call(
        _inproj_body,
        grid=(n // tm,),
        in_specs=[
            pl.BlockSpec((tm, d), lambda i: (i, 0)),
            pl.BlockSpec((1, d), lambda i: (0, 0)),
            pl.BlockSpec((d, cols), lambda i: (0, 0), **once),
            pl.BlockSpec((1, cols), lambda i: (0, 0)),
        ],
        out_specs=pl.BlockSpec((tm, cols), lambda i: (i, 0)),
        out_shape=jax.ShapeDtypeStruct((n, cols), BF16),
        compiler_params=_params(("arbitrary",)),
        name="inproj",
    )(x2d, g, w, col_scale)


def _t5_bucket_table(n):
    d = np.arange(n)
    max_exact = N_BUCKETS // 2
    log_ratio = np.log(np.maximum(d, 1).astype(np.float64) / max_exact) / math.log(MAX_DISTANCE / max_exact)
    large = np.minimum(max_exact + (log_ratio * (N_BUCKETS - max_exact)).astype(np.int64), N_BUCKETS - 1)
    return np.where(d < max_exact, d, large).astype(np.int32)


def _bias_body(rb_ref, bkt_ref, o_ref):
    h = pl.program_id(0)
    t = o_ref.shape[2]
    bkt = bkt_ref[...]
    far = rb_ref[N_BUCKETS - 1, h]
    by_dist = jnp.zeros(bkt.shape, F32)
    for b in range(N_BUCKETS - 1):
        by_dist = jnp.where(bkt == b, (rb_ref[b, h] - far) * LOG2E, by_dist)
    rows = (jnp.concatenate([jnp.full((1, t), MASK_VALUE, F32), by_dist[:, :t]], axis=1), by_dist)
    for i, row in enumerate(rows):
        shifted = pltpu.roll(jnp.broadcast_to(row, (t, 2 * t)), 0, 1, stride=1, stride_axis=0)
        o_ref[0, i] = shifted[:, t:]


def _bias_tiles(rel_bias, t):
    bkt = jnp.asarray(_t5_bucket_table(2 * t)).reshape(1, 2 * t)
    return pl.pallas_call(
        _bias_body,
        grid=(N_DIFF_HEADS,),
        in_specs=[
            pl.BlockSpec(memory_space=pltpu.SMEM),
            pl.BlockSpec((1, 2 * t), lambda h: (0, 0)),
        ],
        out_specs=pl.BlockSpec((1, 2, t, t), lambda h: (h, 0, 0, 0)),
        out_shape=jax.ShapeDtypeStruct((N_DIFF_HEADS, 2, t, t), F32),
        compiler_params=_params(("arbitrary",)),
        name="bias_tiles",
    )(rel_bias, bkt)


_NT = (((1,), (1,)), ((), ()))
_TN = (((0,), (0,)), ((), ()))


def _diffattn_body(q_ref, k_ref, v_ref, bias_ref, lq1_ref, lk1_ref, lq2_ref, lk2_ref, g_ref, o_ref, vt_ref, *,
                   t, lam_init, near_band):
    nq = k_ref.shape[1] // t
    half = t // 2
    lane = lax.broadcasted_iota(jnp.int32, (t, LANES), 1)
    lam = (jnp.exp(jnp.sum(lq1_ref[...] * lk1_ref[...], axis=-1, keepdims=True))
           - jnp.exp(jnp.sum(lq2_ref[...] * lk2_ref[...], axis=-1, keepdims=True)) + lam_init)
    vt_ref[:LANES, :] = v_ref[0].T
    vt_ref[LANES:, :] = jnp.ones((vt_ref.shape[0] - LANES, vt_ref.shape[1]), BF16)
    qmask = {}
    run_max = {}
    acc = {}

    def key_range(item):
        _, j, part = item
        return j * t + (half if part == "diag_lower" else 0), j * t + (half if part == "diag_upper" else t)

    def scores(item, c):
        qi, j, part = item
        if qi not in qmask:
            q = q_ref[0, qi * t:(qi + 1) * t, :]
            zero = jnp.zeros_like(q)
            qmask[qi] = (jnp.where(lane < DIFF_HALF, q, zero), jnp.where(lane >= DIFF_HALF, q, zero))
        k_lo, k_hi = key_range(item)
        qsel = qmask[qi][c][half:] if part == "diag_lower" else qmask[qi][c]
        s = lax.dot_general(k_ref[0, k_lo:k_hi, :], qsel, _NT, preferred_element_type=F32)
        if part == "diag_upper":
            s = s + bias_ref[0, 0, :half, :]
        elif part == "diag_lower":
            s = s + bias_ref[0, 0, half:, half:]
        elif j == qi - 1:
            if near_band >= t:
                s = s + bias_ref[0, 1]
            else:
                corner = s[t - near_band:, :near_band] + bias_ref[0, 1, t - near_band:, :near_band]
                bottom = jnp.concatenate([corner, s[t - near_band:, near_band:]], axis=1)
                s = jnp.concatenate([s[:t - near_band], bottom], axis=0)
        return s

    def softmax_step(item, c, s):
        qi, j, part = item
        smax = jnp.max(s, axis=0, keepdims=True)
        if j == 0 and part != "diag_lower":
            m_new, alpha = smax, None
            run_max[qi, c] = m_new
        else:
            m_full = run_max[qi, c]
            m_old = m_full[:, half:] if part == "diag_lower" else m_full
            m_new = jnp.maximum(m_old, smax)
            alpha = jnp.exp2(m_old - m_new)
            run_max[qi, c] = jnp.concatenate([m_full[:, :half], m_new], axis=1) if part == "diag_lower" else m_new
        return jnp.exp2(s - m_new).astype(BF16), alpha

    def values_step(item, c, p, alpha):
        qi, _, part = item
        k_lo, k_hi = key_range(item)
        r = jnp.dot(vt_ref[:, k_lo:k_hi], p, preferred_element_type=F32)
        if alpha is None:
            acc[qi, c] = r
        elif part == "diag_lower":
            acc[qi, c] = jnp.concatenate([acc[qi, c][:, :half], alpha * acc[qi, c][:, half:] + r], axis=1)
        else:
            acc[qi, c] = alpha * acc[qi, c] + r

    def finish(qi):
        outs = []
        for c in range(2):
            a = acc.pop((qi, c))
            del run_max[qi, c]
            outs.append(a[:LANES] * (1.0 / a[LANES:LANES + 1]))
        ot = (outs[0] - lam * outs[1]).T
        ms = jnp.mean(ot * ot, axis=-1, keepdims=True)
        o_ref[0, qi * t:(qi + 1) * t, :] = (
            ((ot * lax.rsqrt(ms + EPS)) * g_ref[...]) * (1.0 - lam_init)).astype(o_ref.dtype)
        del qmask[qi]

    items = []
    for qi in range(nq):
        items += [(qi, j, "full") for j in range(qi)] + [(qi, qi, "diag_upper"), (qi, qi, "diag_lower")]
    ss = [scores(items[0], c) for c in range(2)]
    pending = None
    for n, item in enumerate(items):
        nxt, cur = [], []
        for c in range(2):
            if n + 1 < len(items):
                nxt.append(scores(items[n + 1], c))
            cur.append(softmax_step(item, c, ss[c]))
            if pending is not None:
                values_step(pending[0], c, *pending[1][c])
        if pending is not None and pending[0][2] == "diag_lower":
            finish(pending[0][0])
        pending, ss = (item, cur), nxt
    for c in range(2):
        values_step(pending[0], c, *pending[1][c])
    finish(pending[0][0])


def _diff_attention(proj3, bias, lq1, lk1, lq2, lk2, subln_g, *, t, lam_init):
    b, s, _ = proj3.shape
    vec = pl.BlockSpec((1, DIFF_HALF), lambda bi, h: (0, 0))
    head = lambda col: pl.BlockSpec((1, s, LANES), lambda bi, h: (bi, 0, col + h))
    first_far = int(np.argmax(_t5_bucket_table(2 * t) == N_BUCKETS - 1))
    near_band = -(-first_far // LANES) * LANES
    ones_rows = 16
    return pl.pallas_call(
        functools.partial(_diffattn_body, t=t, lam_init=lam_init, near_band=near_band),
        grid=(b, N_DIFF_HEADS),
        in_specs=[
            head(COL_DQ), head(COL_DK), head(COL_DV),
            pl.BlockSpec((1, 2, t, t), lambda bi, h: (h, 0, 0, 0)),
            vec, vec, vec, vec,
            pl.BlockSpec((1, LANES), lambda bi, h: (0, 0)),
        ],
        out_specs=pl.BlockSpec((1, s, LANES), lambda bi, h: (bi, 0, h)),
        out_shape=jax.ShapeDtypeStruct((b, s, N_DIFF_HEADS * LANES), BF16),
        scratch_shapes=[pltpu.VMEM((LANES + ones_rows, s), BF16)],
        compiler_params=_params(("arbitrary", "arbitrary")),
        name="diff_attention",
    )(proj3, proj3, proj3, bias, lq1, lk1, lq2, lk2, subln_g)


def _retention_tables(s):
    c = RET_CHUNK
    half = RET_QK // 2
    inv_freq = ROPE_BASE ** (-jnp.arange(0, RET_QK, 2, dtype=F32) / RET_QK)
    ang = jnp.arange(s, dtype=F32)[:, None] * inv_freq[None, :]
    cos, sin = jnp.cos(ang), jnp.sin(ang)
    cos_t = jnp.tile(cos, (1, LANES // half))
    sin_t = jnp.tile(jnp.concatenate([-sin, sin], axis=1), (1, LANES // RET_QK))
    log_g = jnp.log(1.0 - jnp.exp2(-5.0 - jnp.arange(N_RET_HEADS, dtype=F32)))
    n = jnp.arange(c, dtype=F32)
    diff = n[:, None] - n[None, :]
    inner = jnp.where(diff[None] >= 0, jnp.exp(jnp.maximum(diff, 0.0)[None] * log_g[:, None, None]), 0.0)
    cross = jnp.exp((n[None] + 1.0) * log_g[:, None])
    key = jnp.exp((c - 1.0 - n[None]) * log_g[:, None])
    chunk = jnp.exp(c * log_g)
    bc = lambda a: jnp.broadcast_to(a[..., None], a.shape + (LANES,)).astype(F32)
    return cos_t, sin_t, inner.astype(F32), bc(cross), bc(key), bc(chunk[:, None])


def _retention_body(rq_ref, rk_ref, rv_ref, rg_ref, cos_ref, sin_ref, inner_ref, cross_ref, key_ref,
                    chunk_ref, gn_ref, o_ref, state_ref):
    c = RET_CHUNK
    nc = rq_ref.shape[1] // c
    nh = N_RET_HEADS
    state_ref[...] = jnp.zeros(state_ref.shape, F32)
    lane = lax.broadcasted_iota(jnp.int32, (c, LANES), 1)
    first_half = (lane % RET_QK) < (RET_QK // 2)
    head_mask = [lane < RET_QK, lane >= RET_QK]
    slab = lambda i: slice(i * LANES, (i + 1) * LANES)

    def rotary(x, cos, sin):
        rot = jnp.where(first_half, pltpu.roll(x, LANES - RET_QK // 2, 1), pltpu.roll(x, RET_QK // 2, 1))
        return x * cos + rot * sin

    def chunk_step(ci, carry):
        r = pl.multiple_of(ci * c, c)
        rows = pl.ds(r, c)
        cos = cos_ref[rows, :]
        sin = sin_ref[rows, :]
        q_pair, k_pair, kb_pair = [], [], []
        for pr in range(nh // 2):
            q_pair.append(rotary(rq_ref[0, rows, slab(pr)].astype(F32), cos, sin))
            k = rotary(rk_ref[0, rows, slab(pr)].astype(F32), cos, sin) * (RET_QK ** -0.5)
            k_pair.append(k)
            kb_pair.append(k.astype(BF16))
        qh = [jnp.where(head_mask[h % 2], q_pair[h // 2], 0.0).astype(BF16) for h in range(nh)]
        kd = [(k_pair[h // 2] * key_ref[h]).astype(BF16) for h in range(nh)]
        vh = [rv_ref[0, rows, slab(h)] for h in range(nh)]
        scores = [lax.dot_general(qh[h], kb_pair[h // 2], _NT, preferred_element_type=F32) for h in range(nh)]
        cross = [jnp.dot(qh[h], state_ref[h].astype(BF16), preferred_element_type=F32) for h in range(nh)]
        kv = [lax.dot_general(kd[h], vh[h], _TN, preferred_element_type=F32) for h in range(nh)]
        decayed = [(scores[h] * inner_ref[h]).astype(BF16) for h in range(nh)]
        for h in range(nh):
            state_ref[h] = state_ref[h] * chunk_ref[h] + kv[h]
        inner = [jnp.dot(decayed[h], vh[h], preferred_element_type=F32) for h in range(nh)]
        for h in range(nh):
            y = inner[h] + cross[h] * cross_ref[h]
            mu = jnp.mean(y, axis=-1, keepdims=True)
            yc = y - mu
            var = jnp.mean(yc * yc, axis=-1, keepdims=True)
            yn = (yc * lax.rsqrt(var + EPS)) * gn_ref[:, slab(h)]
            gate = rg_ref[0, rows, slab(h)].astype(F32)
            silu = gate * (1.0 / (1.0 + jnp.exp(-gate)))
            o_ref[0, rows, slab(h)] = (silu * yn).astype(o_ref.dtype)
        return carry

    lax.fori_loop(0, nc, chunk_step, 0, unroll=2)


def _retention(proj3, tables, gn_g):
    b, s, _ = proj3.shape
    cos_t, sin_t, inner, cross, key, chunk = tables
    c = RET_CHUNK
    nh = N_RET_HEADS
    qk_w = nh * RET_QK
    v_w = nh * LANES
    full = lambda shape: pl.BlockSpec(shape, lambda bi: (0,) * len(shape))
    return pl.pallas_call(
        _retention_body,
        grid=(b,),
        in_specs=[
            pl.BlockSpec((1, s, qk_w), lambda bi: (bi, 0, COL_RQ * LANES // qk_w)),
            pl.BlockSpec((1, s, qk_w), lambda bi: (bi, 0, COL_RK * LANES // qk_w)),
            pl.BlockSpec((1, s, v_w), lambda bi: (bi, 0, COL_RV * LANES // v_w)),
            pl.BlockSpec((1, s, v_w), lambda bi: (bi, 0, COL_RG * LANES // v_w)),
            full((s, LANES)), full((s, LANES)),
            full((nh, c, c)), full((nh, c, LANES)), full((nh, c, LANES)), full((nh, 1, LANES)),
            full((1, v_w)),
        ],
        out_specs=pl.BlockSpec((1, s, v_w), lambda bi: (bi, 0, 0)),
        out_shape=jax.ShapeDtypeStruct((b, s, v_w), BF16),
        scratch_shapes=[pltpu.VMEM((nh, LANES, LANES), F32)],
        compiler_params=_params(("arbitrary",)),
        name="retention",
    )(proj3, proj3, proj3, proj3, cos_t, sin_t, inner, cross, key, chunk, gn_g)


def _route(logits):
    lane = lax.broadcasted_iota(jnp.int32, logits.shape, 1).astype(F32)
    big = float(LANES)
    is_group = lane < N_GROUPS
    gl = jnp.where(is_group, logits, -jnp.inf)
    gmax = jnp.max(gl, axis=-1, keepdims=True)
    gidx = jnp.min(jnp.where(gl == gmax, lane, big), axis=-1, keepdims=True)
    p_group = 1.0 / jnp.sum(jnp.where(is_group, jnp.exp(gl - gmax), 0.0), axis=-1, keepdims=True)
    lo = N_GROUPS + EXPERTS_PER_GROUP * gidx
    il = jnp.where(lane >= lo, jnp.where(lane < lo + EXPERTS_PER_GROUP, logits, -jnp.inf), -jnp.inf)
    v1 = jnp.max(il, axis=-1, keepdims=True)
    i1 = jnp.min(jnp.where(il == v1, lane, big), axis=-1, keepdims=True)
    il2 = jnp.where(lane == i1, -jnp.inf, il)
    v2 = jnp.max(il2, axis=-1, keepdims=True)
    i2 = jnp.min(jnp.where(il2 == v2, lane, big), axis=-1, keepdims=True)
    e2 = jnp.exp(v2 - v1)
    inv = 1.0 / (1.0 + e2)
    g1 = p_group * inv
    g2 = p_group * (e2 * inv)
    out = jnp.where(lane == 0.0, i1 - N_GROUPS,
                    jnp.where(lane == 1.0, i2 - N_GROUPS,
                              jnp.where(lane == 2.0, g1, jnp.where(lane == 3.0, g2, 0.0))))
    return out


def _outproj_body(a_ref, r_ref, x_ref, w_ref, g_ref, wr_ref, br_ref, x1_ref, h2_ref, route_ref, eid_ref, *, sub):
    tm = x_ref.shape[0]
    half = a_ref.shape[1]
    n_sub = tm // sub

    def project(i):
        rows = slice(i * sub, (i + 1) * sub)
        acc = jnp.dot(a_ref[rows, :], w_ref[:half, :], preferred_element_type=F32)
        return acc + jnp.dot(r_ref[rows, :], w_ref[half:, :], preferred_element_type=F32)

    def epilogue(i, acc):
        rows = slice(i * sub, (i + 1) * sub)
        x1 = x_ref[rows, :] + acc
        x1_ref[rows, :] = x1
        ms = jnp.mean(x1 * x1, axis=-1, keepdims=True)
        h2 = (x1 * lax.rsqrt(ms + EPS)) * g_ref[...]
        h2_ref[rows, :] = _pack_halves(h2)
        logits = jnp.dot(h2.astype(BF16), wr_ref[...], preferred_element_type=F32) + br_ref[...]
        route = _route(logits)
        route_ref[rows, :] = route
        eid_ref[:, rows] = route.T[:TOP_K, :].astype(jnp.int32)

    acc = project(0)
    for i in range(n_sub):
        nxt = project(i + 1) if i + 1 < n_sub else None
        epilogue(i, acc)
        acc = nxt


def _outproj_router(a2d, r2d, x2d, w_out_bf16, g, wr_bf16, br, *, tm):
    n, d = x2d.shape
    half = a2d.shape[1]
    row = lambda width: pl.BlockSpec((tm, width), lambda i: (i, 0))
    full = lambda shape: pl.BlockSpec(shape, lambda i: (0, 0))
    return pl.pallas_call(
        functools.partial(_outproj_body, sub=min(tm, 2 * LANES)),
        grid=(n // tm,),
        in_specs=[row(half), row(half), row(d), full((2 * half, d)), full((1, d)), full((d, LANES)),
                  full((1, LANES))],
        out_specs=[row(d), row(d // 2), row(LANES), pl.BlockSpec((TOP_K, tm), lambda i: (0, i))],
        out_shape=[jax.ShapeDtypeStruct((n, d), F32), jax.ShapeDtypeStruct((n, d // 2), jnp.uint32),
                   jax.ShapeDtypeStruct((n, LANES), F32), jax.ShapeDtypeStruct((TOP_K, n), jnp.int32)],
        compiler_params=_params(("arbitrary",)),
        name="outproj_router",
    )(a2d, r2d, x2d, w_out_bf16, g, wr_bf16, br)


def _moe_plan(expert_id, tm):
    n_slots = expert_id.shape[0]
    n_tiles = n_slots // tm
    order = jnp.argsort(expert_id).astype(jnp.int32)
    sizes = jnp.zeros((N_EXPERTS,), jnp.int32).at[expert_id].add(1)
    ends = jnp.cumsum(sizes)
    starts = ends - sizes
    tile_row0 = jnp.arange(n_tiles, dtype=jnp.int32) * tm
    owner = lambda row: jnp.sum((ends[None, :] <= row[:, None]).astype(jnp.int32), axis=1)
    first_e = owner(tile_row0)
    last_e = owner(tile_row0 + (tm - 1))
    owners_upto = jnp.cumsum((sizes > 0).astype(jnp.int32))
    n_extra = (owners_upto[last_e] - owners_upto[first_e]).astype(jnp.int32)
    prev_last = jnp.concatenate([jnp.full((1,), -1, jnp.int32), last_e[:-1]])
    switch = (prev_last != first_e).astype(jnp.int32)
    eid = jnp.arange(N_EXPERTS, dtype=jnp.int32)
    cand = jnp.where((sizes[None, :] > 0) & (eid[None, :] > eid[:, None]), eid[None, :], N_EXPERTS)
    nxt = jnp.min(cand, axis=1)
    next_expert = jnp.where(nxt < N_EXPERTS, nxt, -1).astype(jnp.int32)
    spare = n_slots + jnp.arange(tm, dtype=jnp.int32)
    codes = jnp.concatenate([order.reshape(n_tiles, tm), spare[None, :]], axis=0).reshape(n_tiles + 1, 1, tm)
    return dict(first_e=first_e, n_extra=n_extra, switch=switch, starts=starts.astype(jnp.int32),
                ends=ends.astype(jnp.int32), next_expert=next_expert, codes=codes)


def _moe_body(fe_ref, nextra_ref, sw_ref, st_ref, en_ref, nx_ref,
              codes_hbm, h2_hbm, wg_hbm, wu_hbm, wd_hbm, ys_hbm,
              gidx, sidx, xbuf, obuf, wg_st, wu_st, wd_st, wg_rs, wu_rs, wd_rs,
              gisem, sisem, gsem, ssem, wsem, *, n_tokens, n_tiles, cast_rows, ff_chunks):
    t = pl.program_id(0)
    tm = xbuf.shape[1]
    last = n_tiles - 1
    spare_codes = n_tiles

    def gidx_copy(tile, slot):
        return pltpu.make_async_copy(codes_hbm.at[tile], gidx.at[slot], gisem.at[slot])

    def sidx_copy(tile, slot):
        return pltpu.make_async_copy(codes_hbm.at[tile], sidx.at[slot], sisem.at[slot])

    def weight_copies(ex):
        return (pltpu.make_async_copy(wg_hbm.at[ex], wg_st, wsem.at[0]),
                pltpu.make_async_copy(wu_hbm.at[ex], wu_st, wsem.at[1]),
                pltpu.make_async_copy(wd_hbm.at[ex], wd_st, wsem.at[2]))

    def gather_row(slot, r):
        code = gidx[slot, 0, r]
        tok = code - jnp.where(code >= n_tokens, n_tokens, 0)
        return pltpu.make_async_copy(h2_hbm.at[pl.ds(tok, 1)], xbuf.at[slot, pl.ds(r, 1)], gsem.at[slot])

    def scatter_row(slot, r):
        dst = sidx[slot, 0, r]
        return pltpu.make_async_copy(obuf.at[slot, pl.ds(r, 1)], ys_hbm.at[pl.ds(dst, 1)], ssem.at[slot])

    def wait_gather(slot):
        pltpu.make_async_copy(h2_hbm.at[pl.ds(0, tm)], xbuf.at[slot], gsem.at[slot]).wait()

    def wait_scatter(slot):
        pltpu.make_async_copy(obuf.at[slot], ys_hbm.at[pl.ds(0, tm)], ssem.at[slot]).wait()

    def for_rows(fn):
        def body(r, carry):
            fn(r)
            return carry
        lax.fori_loop(0, tm, body, 0, unroll=8)

    def take_over(ex, first):
        @pl.when(first)
        def _():
            for cp in weight_copies(ex):
                cp.start()

        for cp in weight_copies(ex):
            cp.wait()

        def cast_up(i, carry):
            r = pl.multiple_of(i * cast_rows, cast_rows)
            wg_rs[pl.ds(r, cast_rows), :] = wg_st[pl.ds(r, cast_rows), :].astype(BF16)
            wu_rs[pl.ds(r, cast_rows), :] = wu_st[pl.ds(r, cast_rows), :].astype(BF16)
            return carry

        def cast_down(i, carry):
            r = pl.multiple_of(i * cast_rows, cast_rows)
            wd_rs[pl.ds(r, cast_rows), :] = wd_st[pl.ds(r, cast_rows), :].astype(BF16)
            return carry

        lax.fori_loop(0, wg_st.shape[0] // cast_rows, cast_up, 0)
        lax.fori_loop(0, wd_st.shape[0] // cast_rows, cast_down, 0)
        nxt = nx_ref[ex]

        @pl.when(nxt >= 0)
        def _():
            for cp in weight_copies(nxt):
                cp.start(priority=1)

    def ffn(slot, ex, accumulate, row_dmas=None, r0=0, nr=None):
        nr = tm if nr is None else nr
        x_lo, x_hi = (h.astype(BF16) for h in _unpack_halves(xbuf[slot, r0:r0 + nr]))
        hd = x_lo.shape[1]
        row = t * tm + r0 + lax.broadcasted_iota(jnp.int32, (nr, 1), 0)
        mine = jnp.logical_and(row >= st_ref[ex], row < en_ref[ex])
        d_ff = wg_rs.shape[1]
        n_chunks = ff_chunks if row_dmas is not None else 1
        fc = d_ff // n_chunks
        rows = tm // n_chunks
        contrib = None
        for c in range(n_chunks):
            cols = slice(c * fc, (c + 1) * fc)
            up = lambda w: (jnp.dot(x_lo, w[:hd, cols], preferred_element_type=F32)
                            + jnp.dot(x_hi, w[hd:, cols], preferred_element_type=F32))
            a = up(wg_rs)
            b = up(wu_rs)
            hmid = jnp.where(mine, (a * (1.0 / (1.0 + jnp.exp(-a)))) * b, 0.0).astype(BF16)
            part = jnp.dot(hmid, wd_rs[cols, :], preferred_element_type=F32)
            contrib = part if contrib is None else contrib + part
            if row_dmas is not None:
                row_dmas(c * rows, (c + 1) * rows)
        if accumulate:
            prev_lo, prev_hi = _unpack_halves(obuf[slot, r0:r0 + nr])
            contrib = jnp.concatenate([prev_lo, prev_hi], axis=1) + contrib
        obuf[slot, r0:r0 + nr] = _pack_halves(contrib)

    e0 = fe_ref[t]

    def step(p):
        q = 1 - p
        if p == 0:
            @pl.when(t == 0)
            def _():
                copies = (gidx_copy(0, 0), gidx_copy(min(1, last), 1), sidx_copy(spare_codes, 1))
                for cp in copies:
                    cp.start()
                for cp in copies:
                    cp.wait()
                obuf[1] = jnp.zeros(obuf.shape[1:], obuf.dtype)
                for_rows(lambda r: gather_row(0, r).start())

        gidx_copy(jnp.minimum(t + 2, last), p).start()
        sidx_copy(t, p).start()

        @pl.when(t >= 1)
        def _():
            gidx_copy(0, q).wait()
            sidx_copy(0, q).wait()
            wait_scatter(p)

        wait_gather(p)

        @pl.when(sw_ref[t] == 1)
        def _():
            take_over(e0, t == 0)

        def row_dmas(lo, hi):
            for r in range(lo, hi):
                scatter_row(q, r).start()
                gather_row(q, r).start()

        ffn(p, e0, accumulate=False, row_dmas=row_dmas)

    for parity in range(2):
        pl.when(lax.rem(t, 2) == parity)(functools.partial(step, parity))

    slot = lax.rem(t, 2)

    def further_owner(_, ex):
        nxt = nx_ref[ex]
        take_over(nxt, False)
        hm = tm // 2

        @pl.when(st_ref[nxt] < t * tm + hm)
        def _():
            ffn(slot, nxt, accumulate=True, r0=0, nr=hm)

        @pl.when(en_ref[nxt] > t * tm + hm)
        def _():
            ffn(slot, nxt, accumulate=True, r0=hm, nr=hm)

        return nxt

    lax.fori_loop(0, nextra_ref[t], further_owner, e0)

    @pl.when(t == last)
    def _():
        gidx_copy(0, slot).wait()
        sidx_copy(0, slot).wait()
        for_rows(lambda r: scatter_row(slot, r).start())
        wait_gather(1 - slot)
        wait_scatter(1 - slot)
        wait_scatter(slot)


def _moe_ffn(h2, plan, w_gate, w_up, w_down, *, tm):
    n, hd = h2.shape
    d = 2 * hd
    n_slots = 2 * n
    n_tiles = n_slots // tm
    assert n_tiles >= 2 and n_slots % tm == 0
    d_ff = w_gate.shape[2]
    any_spec = pl.BlockSpec(memory_space=pl.ANY)
    grid_spec = pltpu.PrefetchScalarGridSpec(
        num_scalar_prefetch=6,
        grid=(n_tiles,),
        in_specs=[any_spec] * 5,
        out_specs=any_spec,
        scratch_shapes=[
            pltpu.SMEM((2, 1, tm), jnp.int32),
            pltpu.SMEM((2, 1, tm), jnp.int32),
            pltpu.VMEM((2, tm, hd), jnp.uint32),
            pltpu.VMEM((2, tm, hd), jnp.uint32),
            pltpu.VMEM((d, d_ff), F32),
            pltpu.VMEM((d, d_ff), F32),
            pltpu.VMEM((d_ff, d), F32),
            pltpu.VMEM((d, d_ff), BF16),
            pltpu.VMEM((d, d_ff), BF16),
            pltpu.VMEM((d_ff, d), BF16),
            pltpu.SemaphoreType.DMA((2,)),
            pltpu.SemaphoreType.DMA((2,)),
            pltpu.SemaphoreType.DMA((2,)),
            pltpu.SemaphoreType.DMA((2,)),
            pltpu.SemaphoreType.DMA((3,)),
        ],
    )
    return pl.pallas_call(
        functools.partial(_moe_body, n_tokens=n, n_tiles=n_tiles, cast_rows=128,
                          ff_chunks=d_ff // (2 * LANES)),
        grid_spec=grid_spec,
        out_shape=jax.ShapeDtypeStruct((n_slots + tm, hd), jnp.uint32),
        compiler_params=_params(("arbitrary",), vmem=MOE_VMEM_LIMIT),
        name="moe_ffn",
    )(plan["first_e"], plan["n_extra"], plan["switch"], plan["starts"], plan["ends"], plan["next_expert"],
      plan["codes"], h2, w_gate, w_up, w_down)


def _final_body(x1_ref, ys0_ref, ys1_ref, route_ref, g_ref, o_ref):
    d = x1_ref.shape[1]
    route = route_ref[...]
    g1, g2 = route[:, 2:3], route[:, 3:4]
    ya = _unpack_halves(ys0_ref[...])
    yb = _unpack_halves(ys1_ref[...])
    halves = (slice(0, d // 2), slice(d // 2, d))
    x2 = [x1_ref[:, cols] + (ya[i] * g1 + yb[i] * g2) for i, cols in enumerate(halves)]
    ms = sum(jnp.sum(h * h, axis=-1, keepdims=True) for h in x2) * (1.0 / d)
    inv = lax.rsqrt(ms + EPS)
    for h, cols in zip(x2, halves):
        o_ref[:, cols] = ((h * inv) * g_ref[:, cols]).astype(o_ref.dtype)


def _final(x1, ys, route, g, *, tm):
    n, d = x1.shape
    nb = n // tm
    return pl.pallas_call(
        _final_body,
        grid=(nb,),
        in_specs=[
            pl.BlockSpec((tm, d), lambda i: (i, 0)),
            pl.BlockSpec((tm, d // 2), lambda i: (i, 0)),
            pl.BlockSpec((tm, d // 2), lambda i: (nb + i, 0)),
            pl.BlockSpec((tm, LANES), lambda i: (i, 0)),
            pl.BlockSpec((1, d), lambda i: (0, 0)),
        ],
        out_specs=pl.BlockSpec((tm, d), lambda i: (i, 0)),
        out_shape=jax.ShapeDtypeStruct((n, d), F32),
        compiler_params=_params(("arbitrary",)),
        name="final_norm",
    )(x1, ys, ys, route, g)


def _layer(x, rel_bias, attn_norm_g, w_in, lq1, lk1, lq2, lk2, diff_subln_g, ret_gn_g, w_out, ffn_norm_g,
           w_group_router, b_group, w_inner_router, b_inner, w_gate_exp, w_up_exp, w_down_exp, *, layer,
           attn_tile, row_tile, moe_tile):
    b, s, d = x.shape
    n = b * s
    x2d = x.reshape(n, d)
    lam_init = 0.8 - 0.6 * math.exp(-0.3 * layer)

    dq_cols = (COL_DK - COL_DQ) * LANES
    col_scale = jnp.concatenate([jnp.full((1, dq_cols), DIFF_HALF ** -0.5 * LOG2E, F32),
                                 jnp.ones((1, IN_COLS - dq_cols), F32)], axis=1)
    proj = _inproj(x2d, attn_norm_g.reshape(1, d), w_in.astype(BF16), col_scale, tm=min(row_tile, n))
    proj3 = proj.reshape(b, s, IN_COLS)

    bias = _bias_tiles(rel_bias, attn_tile)
    vec = lambda a: a.reshape(1, -1)
    a_out = _diff_attention(proj3, bias, vec(lq1), vec(lk1), vec(lq2), vec(lk2), vec(diff_subln_g),
                            t=attn_tile, lam_init=lam_init)
    r_out = _retention(proj3, _retention_tables(s), vec(ret_gn_g))

    pad = LANES - N_GROUPS - N_EXPERTS
    wr = jnp.concatenate([w_group_router, jnp.transpose(w_inner_router, (1, 0, 2)).reshape(d, N_EXPERTS),
                          jnp.zeros((d, pad), F32)], axis=1).astype(BF16)
    br = jnp.concatenate([b_group, b_inner.reshape(-1), jnp.zeros((pad,), F32)]).reshape(1, LANES)
    x1, h2, route, eid = _outproj_router(a_out.reshape(n, -1), r_out.reshape(n, -1), x2d, w_out.astype(BF16),
                                         vec(ffn_norm_g), wr, br, tm=row_tile)

    plan = _moe_plan(eid.reshape(-1), moe_tile)
    ys = _moe_ffn(h2, plan, w_gate_exp, w_up_exp, w_down_exp, tm=moe_tile)
    return x1, ys, route


def kernel(x, rel_bias, attn_norm_g, w_in, lambda_q1, lambda_k1, lambda_q2, lambda_k2, diff_subln_g, ret_gn_g,
           w_out, ffn_norm_g, w_group_router, b_group, w_inner_router, b_inner, w_gate_exp, w_up_exp, w_down_exp,
           final_g):
    b, s, d = x.shape
    depth = w_in.shape[0]
    assert depth == 1, "the final norm is fused into the single layer's combine step"
    first = lambda a: a.reshape(a.shape[1:])
    x1, ys, route = _layer(x, rel_bias, attn_norm_g[0], w_in[0], lambda_q1[0], lambda_k1[0], lambda_q2[0],
                           lambda_k2[0], diff_subln_g[0], ret_gn_g[0], w_out[0], ffn_norm_g[0],
                           w_group_router[0], b_group[0], w_inner_router[0], b_inner[0], first(w_gate_exp),
                           first(w_up_exp), first(w_down_exp), layer=0, attn_tile=min(256, s), row_tile=512,
                           moe_tile=512)
    out = _final(x1, ys, route, final_g.reshape(1, d), tm=512)
    return out.reshape(b, s, d)
```
